```python
import jax
import jax.numpy as jnp
from jax import lax
import numpy as np


D_MODEL = 1024
BATCH = 32
SEQ = 2048
DEPTH = 4

GRID_W = 64
CTX_LEN = 256
HEAD_DIM = 128
N_HEADS = D_MODEL // HEAD_DIM
N_KV_HEADS = N_HEADS // 4
Q_GROUP = N_HEADS // N_KV_HEADS
Q_BLOCK = 128
ATTN_SCALE = HEAD_DIM ** -0.5
ROPE_THETA = 10000.0
ROPE_PAIRS = HEAD_DIM // 4
CONV_WIDTH = 31
CONV_CH = D_MODEL
POOL_WINDOWS = (2, 4, 8, 16)
POOL_GROUPS = len(POOL_WINDOWS)
POOL_CH = D_MODEL
POOL_GROUP_CH = POOL_CH // POOL_GROUPS
N_BRANCH = 3
D_FF = 2816
FFN_CONV_WIDTH = 3
N_MOD = 6
ATTN_W = N_HEADS * HEAD_DIM
KV_W = N_KV_HEADS * HEAD_DIM
Q_END = ATTN_W
K_END = Q_END + KV_W
V_END = K_END + KV_W
CONV_END = V_END + 2 * CONV_CH
POOL_END = CONV_END + POOL_CH
D_IN = POOL_END + N_BRANCH * D_MODEL
DEEPNORM_ALPHA = (2 * DEPTH) ** 0.25
DEEPNORM_BETA = (8 * DEPTH) ** -0.25
LN_EPS = 1e-5
RMS_EPS = 1e-6

kernel_name = "hybrid_gated_parallel_diffusion_trunk"


def layer_norm(t, g, b):
    tf = t.astype(jnp.float32)
    mu = jnp.mean(tf, axis=-1, keepdims=True)
    var = jnp.mean(jnp.square(tf - mu), axis=-1, keepdims=True)
    return ((tf - mu) * lax.rsqrt(var + LN_EPS)).astype(t.dtype) * g + b


def rms_norm(t, g):
    tf = t.astype(jnp.float32)
    ms = jnp.mean(jnp.square(tf), axis=-1, keepdims=True)
    return (tf * lax.rsqrt(ms + RMS_EPS)).astype(t.dtype) * g


def ada_mod(cond, w, b, n):
    m = jax.nn.silu(cond) @ w[:, :n * D_MODEL] + b[:n * D_MODEL]
    return m.reshape(m.shape[:-1] + (n, D_MODEL))


def modulate(h, shift, scale):
    return h * (1.0 + scale) + shift


def axial_rope_tables(n_tokens):
    n_rows = n_tokens // GRID_W
    row = jnp.repeat(jnp.arange(n_rows, dtype=jnp.float32), GRID_W)
    col = jnp.tile(jnp.arange(GRID_W, dtype=jnp.float32), n_rows)
    inv_freq = ROPE_THETA ** (-jnp.arange(ROPE_PAIRS, dtype=jnp.float32) / ROPE_PAIRS)
    ang = jnp.stack([row, col], axis=-1)[..., None] * inv_freq
    return jnp.cos(ang), jnp.sin(ang)


def apply_rope(t, cos, sin):
    b, n, h, _ = t.shape
    tf = t.astype(jnp.float32).reshape(b, n, h, 2, 2, ROPE_PAIRS)
    t1, t2 = tf[..., 0, :], tf[..., 1, :]
    c = cos[None, :, None]
    s = sin[None, :, None]
    out = jnp.stack([t1 * c - t2 * s, t1 * s + t2 * c], axis=-2)
    return out.reshape(t.shape).astype(t.dtype)


def kv_heads(z_kv, k_gain):
    lead = z_kv.shape[:-1]
    k = rms_norm(z_kv[..., :KV_W].reshape(lead + (N_KV_HEADS, HEAD_DIM)), k_gain)
    v = z_kv[..., KV_W:].reshape(lead + (N_KV_HEADS, HEAD_DIM))
    return k, v


def qkv_heads(z, q_gain, k_gain):
    q = rms_norm(z[..., :Q_END].reshape(z.shape[:-1] + (N_HEADS, HEAD_DIM)), q_gain)
    k, v = kv_heads(z[..., Q_END:V_END], k_gain)
    return q, k, v


def latent_attention(q, k_lat, v_lat, k_ctx, v_ctx):
    b, n = q.shape[0], q.shape[1]
    k_all = jnp.concatenate([k_ctx, k_lat], axis=1)
    v_all = jnp.concatenate([v_ctx, v_lat], axis=1)
    n_blk = n // Q_BLOCK
    qb = jnp.moveaxis(q.reshape(b, n_blk, Q_BLOCK, N_KV_HEADS, Q_GROUP, HEAD_DIM), 1, 0)

    def block(qi):
        s = jnp.einsum('bqhgd,bkhd->bhgqk', qi, k_all, preferred_element_type=jnp.float32) * ATTN_SCALE
        p = jax.nn.softmax(s, axis=-1).astype(v_all.dtype)
        return jnp.einsum('bhgqk,bkhd->bqhgd', p, v_all)

    o = lax.map(block, qb)
    return jnp.moveaxis(o, 0, 1).reshape(b, n, ATTN_W)


def context_attention(q, k, v):
    b, n = q.shape[0], q.shape[1]
    qg = q.reshape(b, n, N_KV_HEADS, Q_GROUP, HEAD_DIM)
    s = jnp.einsum('bqhgd,bkhd->bhgqk', qg, k, preferred_element_type=jnp.float32) * ATTN_SCALE
    p = jax.nn.softmax(s, axis=-1).astype(v.dtype)
    return jnp.einsum('bhgqk,bkhd->bqhgd', p, v).reshape(b, n, ATTN_W)


def depthwise_conv(t, w, b):
    k = w.shape[0]
    y = lax.conv_general_dilated(
        t, w[:, None, :], window_strides=(1,), padding=[((k - 1) // 2, k // 2)],
        dimension_numbers=('NWC', 'WIO', 'NWC'), feature_group_count=t.shape[-1])
    return y + b


def conformer_conv(u, dw_w, dw_b, ln_g, ln_b, pw_w, pw_b):
    a, gt = jnp.split(u, 2, axis=-1)
    h = depthwise_conv(a * jax.nn.sigmoid(gt), dw_w, dw_b)
    h = jax.nn.silu(layer_norm(h, ln_g, ln_b))
    return h @ pw_w + pw_b


def multiscale_pool(u, pool_w, pool_scale):
    b, n, _ = u.shape
    uf = u.astype(jnp.float32)
    cs = jnp.pad(jnp.cumsum(uf, axis=1), ((0, 0), (1, 0), (0, 0)))
    t = jnp.arange(n)
    outs = []
    for g, w in enumerate(POOL_WINDOWS):
        lo = jnp.clip(t - w // 2, 0, n)
        hi = jnp.clip(t - w // 2 + w, 0, n)
        sl = slice(g * POOL_GROUP_CH, (g + 1) * POOL_GROUP_CH)
        seg = cs[:, :, sl]
        win_sum = jnp.take(seg, hi, axis=1) - jnp.take(seg, lo, axis=1)
        cnt = (hi - lo).astype(jnp.float32)[None, :, None]
        outs.append(win_sum / cnt - uf[:, :, sl])
    pooled = jnp.stack(outs, axis=2).astype(u.dtype)
    mixed = jnp.einsum('blgi,gio->blgo', pooled, pool_w)
    return mixed.reshape(b, n, POOL_CH) * pool_scale


def merge_branches(z, attn_o, conv_dw_w, conv_dw_b, conv_ln_g, conv_ln_b, conv_pw_w, conv_pw_b,
                   pool_w, pool_scale, w_out, b_out):
    conv_o = conformer_conv(z[..., V_END:CONV_END], conv_dw_w, conv_dw_b, conv_ln_g, conv_ln_b,
                            conv_pw_w, conv_pw_b)
    pool_o = multiscale_pool(z[..., CONV_END:POOL_END], pool_w, pool_scale)
    gates = jax.nn.sigmoid(z[..., POOL_END:].reshape(z.shape[:-1] + (N_BRANCH, D_MODEL)))
    m = gates[..., 0, :] * attn_o + gates[..., 1, :] * conv_o + gates[..., 2, :] * pool_o
    return m @ w_out + b_out


def conv_ffn(h, w_up, dw_w, dw_b, w_down):
    a, u = jnp.split(h @ w_up, 2, axis=-1)
    a = depthwise_conv(a, dw_w, dw_b)
    return (jax.nn.silu(a) * u) @ w_down


def _fwd_setup_inputs(seed: int = 0) -> dict:
    key = jax.random.key(seed)
    ks = iter(jax.random.split(key, 32))

    def nrm(shape, scale):
        return jax.random.normal(next(ks), shape, jnp.float32) * scale

    L = DEPTH
    D = D_MODEL
    return {
        'x': nrm((BATCH, SEQ, D), 1.0),
        'c': nrm((BATCH, D), 1.0),
        'ctx': nrm((BATCH, CTX_LEN, D), 1.0),
        'c_ctx': nrm((D,), 1.0),
        'w_ada': nrm((L, D, N_MOD * D), 0.5 * D ** -0.5),
        'b_ada': nrm((L, N_MOD * D), 0.02),
        'w_in': nrm((L, D, D_IN), D ** -0.5),
        'b_in': nrm((L, D_IN), 0.02),
        'q_gain': 1.0 + nrm((L, HEAD_DIM), 0.02),
        'k_gain': 1.0 + nrm((L, HEAD_DIM), 0.02),
        'conv_dw_w': nrm((L, CONV_WIDTH, CONV_CH), CONV_WIDTH ** -0.5),
        'conv_dw_b': nrm((L, CONV_CH), 0.02),
        'conv_ln_g': 1.0 + nrm((L, CONV_CH), 0.02),
        'conv_ln_b': nrm((L, CONV_CH), 0.02),
        'conv_pw_w': nrm((L, CONV_CH, D), CONV_CH ** -0.5),
        'conv_pw_b': nrm((L, D), 0.02),
        'pool_w': nrm((L, POOL_GROUPS, POOL_GROUP_CH, POOL_GROUP_CH), POOL_GROUP_CH ** -0.5),
        'pool_scale': 1.0 + nrm((L, POOL_CH), 0.1),
        'w_out': nrm((L, D, D), DEEPNORM_BETA * D ** -0.5),
        'b_out': nrm((L, D), 0.02),
        'ln1_g': 1.0 + nrm((L, D), 0.02),
        'ln1_b': nrm((L, D), 0.02),
        'ln2_g': 1.0 + nrm((L, D), 0.02),
        'ln2_b': nrm((L, D), 0.02),
        'w_up': nrm((L, D, 2 * D_FF), D ** -0.5),
        'ffn_dw_w': nrm((L, FFN_CONV_WIDTH, D_FF), FFN_CONV_WIDTH ** -0.5),
        'ffn_dw_b': nrm((L, D_FF), 0.02),
        'w_down': nrm((L, D_FF, D), DEEPNORM_BETA * D_FF ** -0.5),
    }


def _fwd_reference(x, c, ctx, c_ctx, w_ada, b_ada, w_in, b_in, q_gain, k_gain,
              conv_dw_w, conv_dw_b, conv_ln_g, conv_ln_b, conv_pw_w, conv_pw_b,
              pool_w, pool_scale, w_out, b_out, ln1_g, ln1_b, ln2_g, ln2_b,
              w_up, ffn_dw_w, ffn_dw_b, w_down):
    cos, sin = axial_rope_tables(x.shape[1])
    for l in range(DEPTH):
        last = l == DEPTH - 1

        def mix(z, attn_o):
            return merge_branches(z, attn_o, conv_dw_w[l], conv_dw_b[l], conv_ln_g[l], conv_ln_b[l],
                                  conv_pw_w[l], conv_pw_b[l], pool_w[l], pool_scale[l],
                                  w_out[l], b_out[l])

        def ffn(h):
            return conv_ffn(h, w_up[l], ffn_dw_w[l], ffn_dw_b[l], w_down[l])

        ml = ada_mod(c, w_ada[l], b_ada[l], N_MOD)[:, None]
        mc = ada_mod(c_ctx, w_ada[l], b_ada[l], 2 if last else N_MOD)

        hl = modulate(x, ml[..., 0, :], ml[..., 1, :])
        hc = modulate(ctx, mc[0], mc[1])
        zl = hl @ w_in[l] + b_in[l]
        q_l, k_l, v_l = qkv_heads(zl, q_gain[l], k_gain[l])
        q_l = apply_rope(q_l, cos, sin)
        k_l = apply_rope(k_l, cos, sin)
        if last:
            k_c, v_c = kv_heads(hc @ w_in[l][:, Q_END:V_END] + b_in[l][Q_END:V_END], k_gain[l])
        else:
            zc = hc @ w_in[l] + b_in[l]
            q_c, k_c, v_c = qkv_heads(zc, q_gain[l], k_gain[l])
        attn_l = latent_attention(q_l, k_l, v_l, k_c, v_c)
        x = layer_norm(DEEPNORM_ALPHA * x + ml[..., 2, :] * mix(zl, attn_l), ln1_g[l], ln1_b[l])

        x = layer_norm(DEEPNORM_ALPHA * x + ml[..., 5, :] * ffn(modulate(x, ml[..., 3, :], ml[..., 4, :])),
                       ln2_g[l], ln2_b[l])

        if not last:
            attn_c = context_attention(q_c, k_c, v_c)
            ctx = layer_norm(DEEPNORM_ALPHA * ctx + mc[2] * mix(zc, attn_c), ln1_g[l], ln1_b[l])
            ctx = layer_norm(DEEPNORM_ALPHA * ctx + mc[5] * ffn(modulate(ctx, mc[3], mc[4])),
                             ln2_g[l], ln2_b[l])
    return x


import jax as _jax
import jax.numpy as _jnp

TWIN_FORMAT = 'train_step'
FWD_PARAMS = ['x', 'c', 'ctx', 'c_ctx', 'w_ada', 'b_ada', 'w_in', 'b_in', 'q_gain', 'k_gain', 'conv_dw_w', 'conv_dw_b', 'conv_ln_g', 'conv_ln_b', 'conv_pw_w', 'conv_pw_b', 'pool_w', 'pool_scale', 'w_out', 'b_out', 'ln1_g', 'ln1_b', 'ln2_g', 'ln2_b', 'w_up', 'ffn_dw_w', 'ffn_dw_b', 'w_down']
TWIN_WEIGHTS = ['c_ctx', 'w_ada', 'b_ada', 'w_in', 'b_in', 'q_gain', 'k_gain', 'conv_dw_w', 'conv_dw_b', 'conv_ln_g', 'conv_ln_b', 'conv_pw_w', 'conv_pw_b', 'pool_w', 'pool_scale', 'w_out', 'b_out', 'ln1_g', 'ln1_b', 'ln2_g', 'ln2_b', 'w_up', 'ffn_dw_w', 'ffn_dw_b', 'w_down']
TWIN_DIFF_INPUT = 'x'
TWIN_INPUTS = ['x', 'c', 'ctx', 'c_ctx', 'w_ada', 'b_ada', 'w_in', 'b_in', 'q_gain', 'k_gain', 'conv_dw_w', 'conv_dw_b', 'conv_ln_g', 'conv_ln_b', 'conv_pw_w', 'conv_pw_b', 'pool_w', 'pool_scale', 'w_out', 'b_out', 'ln1_g', 'ln1_b', 'ln2_g', 'ln2_b', 'w_up', 'ffn_dw_w', 'ffn_dw_b', 'w_down', 'loss_target', 'm_c_ctx', 'm_w_ada', 'm_b_ada', 'm_w_in', 'm_b_in', 'm_q_gain', 'm_k_gain', 'm_conv_dw_w', 'm_conv_dw_b', 'm_conv_ln_g', 'm_conv_ln_b', 'm_conv_pw_w', 'm_conv_pw_b', 'm_pool_w', 'm_pool_scale', 'm_w_out', 'm_b_out', 'm_ln1_g', 'm_ln1_b', 'm_ln2_g', 'm_ln2_b', 'm_w_up', 'm_ffn_dw_w', 'm_ffn_dw_b', 'm_w_down', 'v_c_ctx', 'v_w_ada', 'v_b_ada', 'v_w_in', 'v_b_in', 'v_q_gain', 'v_k_gain', 'v_conv_dw_w', 'v_conv_dw_b', 'v_conv_ln_g', 'v_conv_ln_b', 'v_conv_pw_w', 'v_conv_pw_b', 'v_pool_w', 'v_pool_scale', 'v_w_out', 'v_b_out', 'v_ln1_g', 'v_ln1_b', 'v_ln2_g', 'v_ln2_b', 'v_w_up', 'v_ffn_dw_w', 'v_ffn_dw_b', 'v_w_down']
TWIN_OUTPUTS = ['loss', 'grad_x', 'grad_c_ctx', 'grad_w_ada', 'grad_b_ada', 'grad_w_in', 'grad_b_in', 'grad_q_gain', 'grad_k_gain', 'grad_conv_dw_w', 'grad_conv_dw_b', 'grad_conv_ln_g', 'grad_conv_ln_b', 'grad_conv_pw_w', 'grad_conv_pw_b', 'grad_pool_w', 'grad_pool_scale', 'grad_w_out', 'grad_b_out', 'grad_ln1_g', 'grad_ln1_b', 'grad_ln2_g', 'grad_ln2_b', 'grad_w_up', 'grad_ffn_dw_w', 'grad_ffn_dw_b', 'grad_w_down', 'delta_c_ctx', 'delta_w_ada', 'delta_b_ada', 'delta_w_in', 'delta_b_in', 'delta_q_gain', 'delta_k_gain', 'delta_conv_dw_w', 'delta_conv_dw_b', 'delta_conv_ln_g', 'delta_conv_ln_b', 'delta_conv_pw_w', 'delta_conv_pw_b', 'delta_pool_w', 'delta_pool_scale', 'delta_w_out', 'delta_b_out', 'delta_ln1_g', 'delta_ln1_b', 'delta_ln2_g', 'delta_ln2_b', 'delta_w_up', 'delta_ffn_dw_w', 'delta_ffn_dw_b', 'delta_w_down', 'new_m_c_ctx', 'new_m_w_ada', 'new_m_b_ada', 'new_m_w_in', 'new_m_b_in', 'new_m_q_gain', 'new_m_k_gain', 'new_m_conv_dw_w', 'new_m_conv_dw_b', 'new_m_conv_ln_g', 'new_m_conv_ln_b', 'new_m_conv_pw_w', 'new_m_conv_pw_b', 'new_m_pool_w', 'new_m_pool_scale', 'new_m_w_out', 'new_m_b_out', 'new_m_ln1_g', 'new_m_ln1_b', 'new_m_ln2_g', 'new_m_ln2_b', 'new_m_w_up', 'new_m_ffn_dw_w', 'new_m_ffn_dw_b', 'new_m_w_down', 'new_v_c_ctx', 'new_v_w_ada', 'new_v_b_ada', 'new_v_w_in', 'new_v_b_in', 'new_v_q_gain', 'new_v_k_gain', 'new_v_conv_dw_w', 'new_v_conv_dw_b', 'new_v_conv_ln_g', 'new_v_conv_ln_b', 'new_v_conv_pw_w', 'new_v_conv_pw_b', 'new_v_pool_w', 'new_v_pool_scale', 'new_v_w_out', 'new_v_b_out', 'new_v_ln1_g', 'new_v_ln1_b', 'new_v_ln2_g', 'new_v_ln2_b', 'new_v_w_up', 'new_v_ffn_dw_w', 'new_v_ffn_dw_b', 'new_v_w_down']
TWIN_LEAF_KINDS = {'loss': 'loss', 'grad_x': 'grad_x', 'grad_c_ctx': 'grad_w', 'grad_w_ada': 'grad_w', 'grad_b_ada': 'grad_w', 'grad_w_in': 'grad_w', 'grad_b_in': 'grad_w', 'grad_q_gain': 'grad_w', 'grad_k_gain': 'grad_w', 'grad_conv_dw_w': 'grad_w', 'grad_conv_dw_b': 'grad_w', 'grad_conv_ln_g': 'grad_w', 'grad_conv_ln_b': 'grad_w', 'grad_conv_pw_w': 'grad_w', 'grad_conv_pw_b': 'grad_w', 'grad_pool_w': 'grad_w', 'grad_pool_scale': 'grad_w', 'grad_w_out': 'grad_w', 'grad_b_out': 'grad_w', 'grad_ln1_g': 'grad_w', 'grad_ln1_b': 'grad_w', 'grad_ln2_g': 'grad_w', 'grad_ln2_b': 'grad_w', 'grad_w_up': 'grad_w', 'grad_ffn_dw_w': 'grad_w', 'grad_ffn_dw_b': 'grad_w', 'grad_w_down': 'grad_w', 'delta_c_ctx': 'delta_w', 'delta_w_ada': 'delta_w', 'delta_b_ada': 'delta_w', 'delta_w_in': 'delta_w', 'delta_b_in': 'delta_w', 'delta_q_gain': 'delta_w', 'delta_k_gain': 'delta_w', 'delta_conv_dw_w': 'delta_w', 'delta_conv_dw_b': 'delta_w', 'delta_conv_ln_g': 'delta_w', 'delta_conv_ln_b': 'delta_w', 'delta_conv_pw_w': 'delta_w', 'delta_conv_pw_b': 'delta_w', 'delta_pool_w': 'delta_w', 'delta_pool_scale': 'delta_w', 'delta_w_out': 'delta_w', 'delta_b_out': 'delta_w', 'delta_ln1_g': 'delta_w', 'delta_ln1_b': 'delta_w', 'delta_ln2_g': 'delta_w', 'delta_ln2_b': 'delta_w', 'delta_w_up': 'delta_w', 'delta_ffn_dw_w': 'delta_w', 'delta_ffn_dw_b': 'delta_w', 'delta_w_down': 'delta_w', 'new_m_c_ctx': 'new_m', 'new_m_w_ada': 'new_m', 'new_m_b_ada': 'new_m', 'new_m_w_in': 'new_m', 'new_m_b_in': 'new_m', 'new_m_q_gain': 'new_m', 'new_m_k_gain': 'new_m', 'new_m_conv_dw_w': 'new_m', 'new_m_conv_dw_b': 'new_m', 'new_m_conv_ln_g': 'new_m', 'new_m_conv_ln_b': 'new_m', 'new_m_conv_pw_w': 'new_m', 'new_m_conv_pw_b': 'new_m', 'new_m_pool_w': 'new_m', 'new_m_pool_scale': 'new_m', 'new_m_w_out': 'new_m', 'new_m_b_out': 'new_m', 'new_m_ln1_g': 'new_m', 'new_m_ln1_b': 'new_m', 'new_m_ln2_g': 'new_m', 'new_m_ln2_b': 'new_m', 'new_m_w_up': 'new_m', 'new_m_ffn_dw_w': 'new_m', 'new_m_ffn_dw_b': 'new_m', 'new_m_w_down': 'new_m', 'new_v_c_ctx': 'new_v', 'new_v_w_ada': 'new_v', 'new_v_b_ada': 'new_v', 'new_v_w_in': 'new_v', 'new_v_b_in': 'new_v', 'new_v_q_gain': 'new_v', 'new_v_k_gain': 'new_v', 'new_v_conv_dw_w': 'new_v', 'new_v_conv_dw_b': 'new_v', 'new_v_conv_ln_g': 'new_v', 'new_v_conv_ln_b': 'new_v', 'new_v_conv_pw_w': 'new_v', 'new_v_conv_pw_b': 'new_v', 'new_v_pool_w': 'new_v', 'new_v_pool_scale': 'new_v', 'new_v_w_out': 'new_v', 'new_v_b_out': 'new_v', 'new_v_ln1_g': 'new_v', 'new_v_ln1_b': 'new_v', 'new_v_ln2_g': 'new_v', 'new_v_ln2_b': 'new_v', 'new_v_w_up': 'new_v', 'new_v_ffn_dw_w': 'new_v', 'new_v_ffn_dw_b': 'new_v', 'new_v_w_down': 'new_v'}


def _forward(args):
    return _fwd_reference(*[args[k] for k in FWD_PARAMS])


def _output_shape():
    out = _jax.eval_shape(lambda: _forward(_fwd_setup_inputs(0)))
    return out.shape, out.dtype

N_MICROBATCH = 1
ADAM_LR = 0.001
ADAM_B1 = 0.9
ADAM_B2 = 0.999
ADAM_EPS = 1e-08
ADAM_WD = 0.01
ADAM_STEP = 10
PER_EXAMPLE_BATCH_AXIS = {'x': 0, 'c': 0, 'ctx': 0, 'loss_target': 0}
SHARED_INPUTS = []
_WEIGHT_DTYPES = {'c_ctx': _jnp.float32, 'w_ada': _jnp.float32, 'b_ada': _jnp.float32, 'w_in': _jnp.float32, 'b_in': _jnp.float32, 'q_gain': _jnp.float32, 'k_gain': _jnp.float32, 'conv_dw_w': _jnp.float32, 'conv_dw_b': _jnp.float32, 'conv_ln_g': _jnp.float32, 'conv_ln_b': _jnp.float32, 'conv_pw_w': _jnp.float32, 'conv_pw_b': _jnp.float32, 'pool_w': _jnp.float32, 'pool_scale': _jnp.float32, 'w_out': _jnp.float32, 'b_out': _jnp.float32, 'ln1_g': _jnp.float32, 'ln1_b': _jnp.float32, 'ln2_g': _jnp.float32, 'ln2_b': _jnp.float32, 'w_up': _jnp.float32, 'ffn_dw_w': _jnp.float32, 'ffn_dw_b': _jnp.float32, 'w_down': _jnp.float32}
MOMENT_SCALE = {'c_ctx': 3.720202e-03, 'w_ada': 1.830992e-02, 'b_ada': 3.003490e-02, 'w_in': 5.034052e-03, 'b_in': 6.804310e-03, 'q_gain': 2.633546e-03, 'k_gain': 2.676997e-03, 'conv_dw_w': 6.670323e-03, 'conv_dw_b': 1.408109e-02, 'conv_ln_g': 8.661581e-03, 'conv_ln_b': 8.612641e-03, 'conv_pw_w': 6.882030e-03, 'conv_pw_b': 1.594394e-02, 'pool_w': 9.736360e-03, 'pool_scale': 9.710608e-03, 'w_out': 3.006534e-02, 'b_out': 7.749395e-02, 'ln1_g': 1.743347e+00, 'ln1_b': 7.461439e-01, 'ln2_g': 3.214594e+01, 'ln2_b': 1.502043e+00, 'w_up': 8.902429e-03, 'ffn_dw_w': 8.968796e-03, 'ffn_dw_b': 8.150935e-03, 'w_down': 3.468331e-02}


def _to_microbatches(a, axis):
    t = _jnp.moveaxis(a, axis, 0)
    t = t.reshape((N_MICROBATCH, t.shape[0] // N_MICROBATCH) + t.shape[1:])
    return _jnp.moveaxis(t, 1, axis + 1)


def setup_inputs(seed: int = 0) -> dict:
    inp = _fwd_setup_inputs(seed)
    key = _jax.random.fold_in(_jax.random.key(seed), 7919)
    shape, _ = _output_shape()
    out = dict(inp)
    out["loss_target"] = _jax.random.normal(_jax.random.fold_in(key, 0), shape, _jnp.float32)
    for i, name in enumerate(TWIN_WEIGHTS):
        w = inp[name].astype(_jnp.float32)
        if MOMENT_SCALE is None:
            s = _jnp.sqrt(_jnp.mean(_jnp.square(w)) + 1e-30)
        else:
            s = MOMENT_SCALE[name]
        km, kv = _jax.random.split(_jax.random.fold_in(key, i + 1))
        out[name] = w
        out["m_" + name] = s * _jax.random.normal(km, w.shape, _jnp.float32)
        out["v_" + name] = (s * s) * _jax.random.uniform(kv, w.shape, _jnp.float32, 0.5, 1.5)
    if N_MICROBATCH > 1:
        for name, axis in PER_EXAMPLE_BATCH_AXIS.items():
            out[name] = _to_microbatches(out[name], axis)
    return {'x': out['x'], 'c': out['c'], 'ctx': out['ctx'], 'c_ctx': out['c_ctx'], 'w_ada': out['w_ada'], 'b_ada': out['b_ada'], 'w_in': out['w_in'], 'b_in': out['b_in'], 'q_gain': out['q_gain'], 'k_gain': out['k_gain'], 'conv_dw_w': out['conv_dw_w'], 'conv_dw_b': out['conv_dw_b'], 'conv_ln_g': out['conv_ln_g'], 'conv_ln_b': out['conv_ln_b'], 'conv_pw_w': out['conv_pw_w'], 'conv_pw_b': out['conv_pw_b'], 'pool_w': out['pool_w'], 'pool_scale': out['pool_scale'], 'w_out': out['w_out'], 'b_out': out['b_out'], 'ln1_g': out['ln1_g'], 'ln1_b': out['ln1_b'], 'ln2_g': out['ln2_g'], 'ln2_b': out['ln2_b'], 'w_up': out['w_up'], 'ffn_dw_w': out['ffn_dw_w'], 'ffn_dw_b': out['ffn_dw_b'], 'w_down': out['w_down'], 'loss_target': out['loss_target'], 'm_c_ctx': out['m_c_ctx'], 'm_w_ada': out['m_w_ada'], 'm_b_ada': out['m_b_ada'], 'm_w_in': out['m_w_in'], 'm_b_in': out['m_b_in'], 'm_q_gain': out['m_q_gain'], 'm_k_gain': out['m_k_gain'], 'm_conv_dw_w': out['m_conv_dw_w'], 'm_conv_dw_b': out['m_conv_dw_b'], 'm_conv_ln_g': out['m_conv_ln_g'], 'm_conv_ln_b': out['m_conv_ln_b'], 'm_conv_pw_w': out['m_conv_pw_w'], 'm_conv_pw_b': out['m_conv_pw_b'], 'm_pool_w': out['m_pool_w'], 'm_pool_scale': out['m_pool_scale'], 'm_w_out': out['m_w_out'], 'm_b_out': out['m_b_out'], 'm_ln1_g': out['m_ln1_g'], 'm_ln1_b': out['m_ln1_b'], 'm_ln2_g': out['m_ln2_g'], 'm_ln2_b': out['m_ln2_b'], 'm_w_up': out['m_w_up'], 'm_ffn_dw_w': out['m_ffn_dw_w'], 'm_ffn_dw_b': out['m_ffn_dw_b'], 'm_w_down': out['m_w_down'], 'v_c_ctx': out['v_c_ctx'], 'v_w_ada': out['v_w_ada'], 'v_b_ada': out['v_b_ada'], 'v_w_in': out['v_w_in'], 'v_b_in': out['v_b_in'], 'v_q_gain': out['v_q_gain'], 'v_k_gain': out['v_k_gain'], 'v_conv_dw_w': out['v_conv_dw_w'], 'v_conv_dw_b': out['v_conv_dw_b'], 'v_conv_ln_g': out['v_conv_ln_g'], 'v_conv_ln_b': out['v_conv_ln_b'], 'v_conv_pw_w': out['v_conv_pw_w'], 'v_conv_pw_b': out['v_conv_pw_b'], 'v_pool_w': out['v_pool_w'], 'v_pool_scale': out['v_pool_scale'], 'v_w_out': out['v_w_out'], 'v_b_out': out['v_b_out'], 'v_ln1_g': out['v_ln1_g'], 'v_ln1_b': out['v_ln1_b'], 'v_ln2_g': out['v_ln2_g'], 'v_ln2_b': out['v_ln2_b'], 'v_w_up': out['v_w_up'], 'v_ffn_dw_w': out['v_ffn_dw_w'], 'v_ffn_dw_b': out['v_ffn_dw_b'], 'v_w_down': out['v_w_down']}


def _loss(weights, diff, rest, loss_target):
    with _jax.named_scope("forward"):
        args = {**rest, TWIN_DIFF_INPUT: diff, **{k: w.astype(_WEIGHT_DTYPES[k]) for k, w in weights.items()}}
        y = _forward(args)
    with _jax.named_scope("loss_head"):
        err = _jnp.square(y.astype(_jnp.float32) - loss_target)
        return 0.5 * _jnp.sum(_jnp.mean(err, axis=-1)) if err.ndim else 0.5 * err


def _adamw(w, g, m, v):
    m = ADAM_B1 * m + (1.0 - ADAM_B1) * g
    v = ADAM_B2 * v + (1.0 - ADAM_B2) * _jnp.square(g)
    m_hat = m / (1.0 - ADAM_B1 ** ADAM_STEP)
    v_hat = v / (1.0 - ADAM_B2 ** ADAM_STEP)
    delta = -ADAM_LR * (m_hat / (_jnp.sqrt(v_hat) + ADAM_EPS) + ADAM_WD * w)
    return delta, m, v


def reference(x, c, ctx, c_ctx, w_ada, b_ada, w_in, b_in, q_gain, k_gain, conv_dw_w, conv_dw_b, conv_ln_g, conv_ln_b, conv_pw_w, conv_pw_b, pool_w, pool_scale, w_out, b_out, ln1_g, ln1_b, ln2_g, ln2_b, w_up, ffn_dw_w, ffn_dw_b, w_down, loss_target, m_c_ctx, m_w_ada, m_b_ada, m_w_in, m_b_in, m_q_gain, m_k_gain, m_conv_dw_w, m_conv_dw_b, m_conv_ln_g, m_conv_ln_b, m_conv_pw_w, m_conv_pw_b, m_pool_w, m_pool_scale, m_w_out, m_b_out, m_ln1_g, m_ln1_b, m_ln2_g, m_ln2_b, m_w_up, m_ffn_dw_w, m_ffn_dw_b, m_w_down, v_c_ctx, v_w_ada, v_b_ada, v_w_in, v_b_in, v_q_gain, v_k_gain, v_conv_dw_w, v_conv_dw_b, v_conv_ln_g, v_conv_ln_b, v_conv_pw_w, v_conv_pw_b, v_pool_w, v_pool_scale, v_w_out, v_b_out, v_ln1_g, v_ln1_b, v_ln2_g, v_ln2_b, v_w_up, v_ffn_dw_w, v_ffn_dw_b, v_w_down):
    given = dict(x=x, c=c, ctx=ctx, c_ctx=c_ctx, w_ada=w_ada, b_ada=b_ada, w_in=w_in, b_in=b_in, q_gain=q_gain, k_gain=k_gain, conv_dw_w=conv_dw_w, conv_dw_b=conv_dw_b, conv_ln_g=conv_ln_g, conv_ln_b=conv_ln_b, conv_pw_w=conv_pw_w, conv_pw_b=conv_pw_b, pool_w=pool_w, pool_scale=pool_scale, w_out=w_out, b_out=b_out, ln1_g=ln1_g, ln1_b=ln1_b, ln2_g=ln2_g, ln2_b=ln2_b, w_up=w_up, ffn_dw_w=ffn_dw_w, ffn_dw_b=ffn_dw_b, w_down=w_down, loss_target=loss_target, m_c_ctx=m_c_ctx, m_w_ada=m_w_ada, m_b_ada=m_b_ada, m_w_in=m_w_in, m_b_in=m_b_in, m_q_gain=m_q_gain, m_k_gain=m_k_gain, m_conv_dw_w=m_conv_dw_w, m_conv_dw_b=m_conv_dw_b, m_conv_ln_g=m_conv_ln_g, m_conv_ln_b=m_conv_ln_b, m_conv_pw_w=m_conv_pw_w, m_conv_pw_b=m_conv_pw_b, m_pool_w=m_pool_w, m_pool_scale=m_pool_scale, m_w_out=m_w_out, m_b_out=m_b_out, m_ln1_g=m_ln1_g, m_ln1_b=m_ln1_b, m_ln2_g=m_ln2_g, m_ln2_b=m_ln2_b, m_w_up=m_w_up, m_ffn_dw_w=m_ffn_dw_w, m_ffn_dw_b=m_ffn_dw_b, m_w_down=m_w_down, v_c_ctx=v_c_ctx, v_w_ada=v_w_ada, v_b_ada=v_b_ada, v_w_in=v_w_in, v_b_in=v_b_in, v_q_gain=v_q_gain, v_k_gain=v_k_gain, v_conv_dw_w=v_conv_dw_w, v_conv_dw_b=v_conv_dw_b, v_conv_ln_g=v_conv_ln_g, v_conv_ln_b=v_conv_ln_b, v_conv_pw_w=v_conv_pw_w, v_conv_pw_b=v_conv_pw_b, v_pool_w=v_pool_w, v_pool_scale=v_pool_scale, v_w_out=v_w_out, v_b_out=v_b_out, v_ln1_g=v_ln1_g, v_ln1_b=v_ln1_b, v_ln2_g=v_ln2_g, v_ln2_b=v_ln2_b, v_w_up=v_w_up, v_ffn_dw_w=v_ffn_dw_w, v_ffn_dw_b=v_ffn_dw_b, v_w_down=v_w_down)
    weights = {n: given[n] for n in TWIN_WEIGHTS}
    shared = {n: given[n] for n in SHARED_INPUTS}
    per_example = {n: given[n] for n in ['x', 'c', 'ctx']}
    grad_fn = _jax.value_and_grad(_loss, argnums=(0, 1))

    def one_microbatch(ex, loss_target):
        ex = dict(ex)
        diff = ex.pop(TWIN_DIFF_INPUT)
        return grad_fn(weights, diff, {**shared, **ex}, loss_target)

    if N_MICROBATCH == 1:
        loss, (grad_w, grad_x) = one_microbatch(per_example, given["loss_target"])
    else:
        def body(carry, xs):
            loss_sum, grad_sum = carry
            l_k, (gw_k, gx_k) = one_microbatch(xs[0], xs[1])
            with _jax.named_scope("update"):
                return (loss_sum + l_k, _jax.tree.map(_jnp.add, grad_sum, gw_k)), gx_k

        init = (_jnp.zeros((), _jnp.float32), _jax.tree.map(_jnp.zeros_like, weights))
        (loss, grad_w), grad_x = _jax.lax.scan(body, init, (per_example, given["loss_target"]))
    with _jax.named_scope("update"):
        delta_w, new_m, new_v = {}, {}, {}
        for n in TWIN_WEIGHTS:
            delta_w[n], new_m[n], new_v[n] = _adamw(weights[n], grad_w[n], given["m_" + n], given["v_" + n])
    return (loss, grad_x, *[grad_w[n] for n in TWIN_WEIGHTS], *[delta_w[n] for n in TWIN_WEIGHTS],
            *[new_m[n] for n in TWIN_WEIGHTS], *[new_v[n] for n in TWIN_WEIGHTS])
```

```python
import functools

import jax
import jax.numpy as jnp
import numpy as np
from jax import lax
from jax.experimental import pallas as pl
from jax.experimental.pallas import tpu as pltpu

F32 = jnp.float32
MXU = jnp.bfloat16
WIRE = jnp.bfloat16

D = 1024
HD = 128
NH = 8
NKV = 2
QG = NH // NKV
KV_W = NKV * HD
QKV_W = NH * HD + 2 * KV_W
CONV_CH = D
POOL_CH = D
POOL_WINDOWS = (2, 4, 8, 16)
POOL_GCH = POOL_CH // len(POOL_WINDOWS)
N_GATE = 3 * D
D_IN = QKV_W + 2 * CONV_CH + POOL_CH + N_GATE
D_FF = 2816
N_MOD = 6
DEPTH = 4
CONV_K = 31
FFN_K = 3
GRID_W = 64
ROPE_THETA = 10000.0
ROPE_PAIRS = HD // 4
ALPHA = (2 * DEPTH) ** 0.25
LN_EPS = 1e-5
RMS_EPS = 1e-6
ATTN_SCALE = HD ** -0.5
ADAM_LR, ADAM_B1, ADAM_B2, ADAM_EPS, ADAM_WD, ADAM_STEP = 0.001, 0.9, 0.999, 1e-08, 0.01, 10

N_DEV = 8
TM = 256
GAP = 16
LANES = 1024
VMEM_MB = 48

NN = (((1,), (0,)), ((), ()))
NT = (((1,), (1,)), ((), ()))
TN = (((0,), (0,)), ((), ()))

_pcall = pl.pallas_call


def _call(body, *, name, grid, in_specs, out_specs, out_shape, scratch=(), aliases=None, vmem=VMEM_MB):
    return _pcall(
        body, name=name, grid=grid, in_specs=in_specs, out_specs=out_specs, out_shape=out_shape,
        scratch_shapes=list(scratch), input_output_aliases=aliases or {},
        compiler_params=pltpu.CompilerParams(dimension_semantics=("arbitrary",) * len(grid), vmem_limit_bytes=vmem * 2 ** 20),
    )


def _sds(shape, dtype):
    return jax.ShapeDtypeStruct(tuple(shape), dtype)


def _pick(n, cap, mult):
    best = None
    for t in range(mult, min(n, cap) + 1, mult):
        if n % t == 0:
            best = t
    return best if best is not None else n


def _dot(a, b, dims):
    return lax.dot_general(a, b, dims, preferred_element_type=F32)


def _sigmoid(x):
    return 1.0 / (1.0 + jnp.exp(-x))


def _matmul(a, b, mode, *, name, bias=None, acc_in=None, out_dtype=F32, tm=1024, tn=1024, tk=None):
    if mode == "nn":
        (M, K), (K2, N) = a.shape, b.shape
    elif mode == "nt":
        (M, K), (N, K2) = a.shape, b.shape
    else:
        (K, M), (K2, N) = a.shape, b.shape
    assert K == K2, (a.shape, b.shape, mode)
    tm = _pick(M, tm, 16)
    tn = _pick(N, tn, 128)
    tk = K if tk is None else _pick(K, tk, 128 if mode != "tn" else 16)
    gk = K // tk
    dims = {"nn": NN, "nt": NT, "tn": TN}[mode]
    a_spec = pl.BlockSpec((tk, tm), lambda j, i, k: (k, i)) if mode == "tn" else pl.BlockSpec((tm, tk), lambda j, i, k: (i, k))
    b_spec = pl.BlockSpec((tn, tk), lambda j, i, k: (j, k)) if mode == "nt" else pl.BlockSpec((tk, tn), lambda j, i, k: (k, j))
    in_specs, args = [a_spec, b_spec], [a, b]
    if bias is not None:
        in_specs.append(pl.BlockSpec((1, tn), lambda j, i, k: (0, j)))
        args.append(bias)
    aliases = {}
    if acc_in is not None:
        aliases = {len(args): 0}
        in_specs.append(pl.BlockSpec((tm, tn), lambda j, i, k: (i, j)))
        args.append(acc_in)

    def body(*refs):
        a_ref, b_ref = refs[0], refs[1]
        pos = 2
        bias_ref = acc_in_ref = None
        if bias is not None:
            bias_ref = refs[pos]
            pos += 1
        if acc_in is not None:
            acc_in_ref = refs[pos]
            pos += 1
        o_ref = refs[pos]
        part = _dot(a_ref[...].astype(MXU), b_ref[...].astype(MXU), dims)

        def finish(acc):
            if bias_ref is not None:
                acc = acc + bias_ref[...]
            if acc_in_ref is not None:
                acc = acc + acc_in_ref[...]
            o_ref[...] = acc.astype(out_dtype)

        if gk == 1:
            finish(part)
        else:
            acc_ref = refs[pos + 1]
            k = pl.program_id(2)

            @pl.when(k == 0)
            def _():
                acc_ref[...] = part

            @pl.when(k > 0)
            def _():
                acc_ref[...] += part

            @pl.when(k == gk - 1)
            def _():
                finish(acc_ref[...])

    return _call(
        body, name=name, grid=(N // tn, M // tm, gk), in_specs=in_specs,
        out_specs=pl.BlockSpec((tm, tn), lambda j, i, k: (i, j)), out_shape=_sds((M, N), out_dtype),
        scratch=[pltpu.VMEM((tm, tn), F32)] if gk > 1 else [], aliases=aliases,
    )(*args)


def _rt(w, cb=0):
    return pl.BlockSpec((TM, w), lambda i: (i, cb))


def _vec(w):
    return pl.BlockSpec((1, w), lambda i: (0, 0))


def _part(w):
    return pl.BlockSpec((1, 1, w), lambda i: (i, 0, 0))


def _mod(ref, k):
    return ref[0, :, k * D:(k + 1) * D]


def _colsum(x):
    return jnp.sum(x, axis=0, keepdims=True)


def _ln_stats(s):
    mu = jnp.mean(s, axis=1, keepdims=True)
    cen = s - mu
    var = jnp.mean(cen * cen, axis=1, keepdims=True)
    rstd = lax.rsqrt(var + LN_EPS)
    return cen * rstd, rstd


def _modulate_cast(x, modt, k_shift):
    T = x.shape[0]
    nt = T // TM

    def body(x_ref, mod_ref, h_ref):
        h_ref[...] = (x_ref[...] * (1.0 + _mod(mod_ref, k_shift + 1)) + _mod(mod_ref, k_shift)).astype(MXU)

    return _call(body, name="modulate", grid=(nt,), in_specs=[_rt(D), _part(N_MOD * D)], out_specs=_rt(D),
                 out_shape=_sds((T, D), MXU))(x, modt)


def _resid_ln(x, br, modt, k_gate, g, b, mod_next=None, k_shift_next=0):
    T = x.shape[0]
    nt = T // TM
    with_h = mod_next is not None

    def body(*refs):
        x_ref, br_ref, mod_ref, g_ref, b_ref = refs[:5]
        s = ALPHA * x_ref[...] + _mod(mod_ref, k_gate) * br_ref[...]
        xhat, _ = _ln_stats(s)
        y = xhat * g_ref[...] + b_ref[...]
        if with_h:
            modn_ref, y_ref, h_ref = refs[5:]
            y_ref[...] = y
            h_ref[...] = (y * (1.0 + _mod(modn_ref, k_shift_next + 1)) + _mod(modn_ref, k_shift_next)).astype(MXU)
        else:
            refs[5][...] = y

    in_specs = [_rt(D), _rt(D), _part(N_MOD * D), _vec(D), _vec(D)]
    args = [x, br, modt, g, b]
    if with_h:
        in_specs.append(_part(N_MOD * D))
        args.append(mod_next)
        return _call(body, name="resid_ln_mod", grid=(nt,), in_specs=in_specs, out_specs=[_rt(D), _rt(D)],
                     out_shape=[_sds((T, D), F32), _sds((T, D), MXU)])(*args)
    return _call(body, name="resid_ln", grid=(nt,), in_specs=in_specs, out_specs=_rt(D), out_shape=_sds((T, D), F32))(*args)


def _ln_bwd(dy_part, x, br, modt, k_gate, g, dh=None, y=None, mod_next=None, k_shift_next=0):
    T = x.shape[0]
    nt = T // TM
    with_h = dh is not None

    def body(*refs):
        if with_h:
            dyp_ref, x_ref, br_ref, mod_ref, g_ref, dh_ref, y_ref, modn_ref = refs[:8]
            outs = refs[8:]
        else:
            dyp_ref, x_ref, br_ref, mod_ref, g_ref = refs[:5]
            outs = refs[5:]
        dx_ref, dbr_ref, dgate_ref, dlg_ref, dlb_ref, dbsum_ref = outs[:6]
        i = pl.program_id(0)

        @pl.when(i == 0)
        def _():
            dlg_ref[...] = jnp.zeros_like(dlg_ref)
            dlb_ref[...] = jnp.zeros_like(dlb_ref)
            dbsum_ref[...] = jnp.zeros_like(dbsum_ref)

        dy = dyp_ref[...]
        if with_h:
            dshift_ref, dscale_ref = outs[6:]
            dhv = dh_ref[...]
            dy = dy + dhv * (1.0 + _mod(modn_ref, k_shift_next + 1))
            dshift_ref[0] = _colsum(dhv)
            dscale_ref[0] = _colsum(dhv * y_ref[...])
        gate = _mod(mod_ref, k_gate)
        brv = br_ref[...]
        s = ALPHA * x_ref[...] + gate * brv
        xhat, rstd = _ln_stats(s)
        dlg_ref[...] += _colsum(dy * xhat)
        dlb_ref[...] += _colsum(dy)
        dyg = dy * g_ref[...]
        m1 = jnp.mean(dyg, axis=1, keepdims=True)
        m2 = jnp.mean(dyg * xhat, axis=1, keepdims=True)
        ds = rstd * (dyg - m1 - xhat * m2)
        dx_ref[...] = ALPHA * ds
        dbr = gate * ds
        dbr_ref[...] = dbr.astype(MXU)
        dbsum_ref[...] += _colsum(dbr)
        dgate_ref[0] = _colsum(ds * brv)

    in_specs = [_rt(D), _rt(D), _rt(D), _part(N_MOD * D), _vec(D)]
    args = [dy_part, x, br, modt, g]
    out_specs = [_rt(D), _rt(D), _part(D), _vec(D), _vec(D), _vec(D)]
    out_shape = [_sds((T, D), F32), _sds((T, D), MXU), _sds((nt, 1, D), F32), _sds((1, D), F32), _sds((1, D), F32), _sds((1, D), F32)]
    if with_h:
        in_specs += [_rt(D), _rt(D), _part(N_MOD * D)]
        args += [dh, y, mod_next]
        out_specs += [_part(D), _part(D)]
        out_shape += [_sds((nt, 1, D), F32), _sds((nt, 1, D), F32)]
    return _call(body, name="ln_bwd_mod" if with_h else "ln_bwd", grid=(nt,), in_specs=in_specs, out_specs=out_specs,
                 out_shape=out_shape)(*args)


def _mod_bwd(dx_part, dh, x, modt, k_shift):
    T = x.shape[0]
    nt = T // TM

    def body(dxp_ref, dh_ref, x_ref, mod_ref, dx_ref, dshift_ref, dscale_ref):
        dhv = dh_ref[...]
        dx_ref[...] = dxp_ref[...] + dhv * (1.0 + _mod(mod_ref, k_shift + 1))
        dshift_ref[0] = _colsum(dhv)
        dscale_ref[0] = _colsum(dhv * x_ref[...])

    return _call(body, name="mod_bwd", grid=(nt,), in_specs=[_rt(D), _rt(D), _rt(D), _part(N_MOD * D)],
                 out_specs=[_rt(D), _part(D), _part(D)],
                 out_shape=[_sds((T, D), F32), _sds((nt, 1, D), F32), _sds((nt, 1, D), F32)])(dx_part, dh, x, modt)


def _loss_grad(y, target, tpe, ncq):
    T = y.shape[0]
    nt = T // TM
    nl = tpe - ncq

    def body(y_ref, t_ref, dy_ref, loss_ref):
        i = pl.program_id(0)

        @pl.when(i == 0)
        def _():
            loss_ref[...] = jnp.zeros_like(loss_ref)

        @pl.when(i % tpe < ncq)
        def _():
            dy_ref[...] = jnp.zeros_like(dy_ref)

        @pl.when(i % tpe >= ncq)
        def _():
            err = y_ref[...] - t_ref[...]
            dy_ref[...] = err * (1.0 / D)
            loss_ref[...] += (0.5 / D) * jnp.sum(_colsum(err * err), axis=1, keepdims=True)

    tgt_spec = pl.BlockSpec((TM, D), lambda i: ((i // tpe) * nl + jnp.maximum(i % tpe - ncq, 0), 0))
    return _call(body, name="loss_grad", grid=(nt,), in_specs=[_rt(D), tgt_spec], out_specs=[_rt(D), _vec(128)],
                 out_shape=[_sds((T, D), F32), _sds((1, 128), F32)])(y, target)


def _rope_partner(x):
    lane = lax.broadcasted_iota(jnp.int32, x.shape, 1)
    first = (lane % (2 * ROPE_PAIRS)) < ROPE_PAIRS
    return jnp.where(first, pltpu.roll(x, HD - ROPE_PAIRS, 1), pltpu.roll(x, ROPE_PAIRS, 1))


def _qk_prep(z_qkv, cos_t, sin_t, q_gain, k_gain, tpe):
    T = z_qkv.shape[0]
    nt = T // TM

    def body(z_ref, cos_ref, sin_ref, qg_ref, kg_ref, o_ref):
        cos, sin = cos_ref[...], sin_ref[...]
        for h in range(NH + NKV):
            sl = slice(h * HD, (h + 1) * HD)
            t = z_ref[:, sl]
            gain = qg_ref[...] if h < NH else kg_ref[...]
            n = t * lax.rsqrt(jnp.mean(t * t, axis=1, keepdims=True) + RMS_EPS) * gain
            o_ref[:, sl] = (n * cos + _rope_partner(n) * sin).astype(MXU)
        o_ref[:, (NH + NKV) * HD:] = z_ref[:, (NH + NKV) * HD:].astype(MXU)

    tab = pl.BlockSpec((TM, HD), lambda i: (i % tpe, 0))
    return _call(body, name="qk_prep", grid=(nt,), in_specs=[_rt(QKV_W), tab, tab, _vec(HD), _vec(HD)], out_specs=_rt(QKV_W),
                 out_shape=_sds((T, QKV_W), MXU))(z_qkv, cos_t, sin_t, q_gain, k_gain)


def _qk_bwd(dq, dk, dv, z_qkv, cos_t, sin_t, q_gain, k_gain, tpe):
    T = z_qkv.shape[0]
    nt = T // TM

    def body(dq_ref, dk_ref, dv_ref, z_ref, cos_ref, sin_ref, qg_ref, kg_ref, dz_ref, dqg_ref, dkg_ref, bsum_ref):
        i = pl.program_id(0)

        @pl.when(i == 0)
        def _():
            dqg_ref[...] = jnp.zeros_like(dqg_ref)
            dkg_ref[...] = jnp.zeros_like(dkg_ref)
            bsum_ref[...] = jnp.zeros_like(bsum_ref)

        cos, sin = cos_ref[...], sin_ref[...]
        for h in range(NH + NKV):
            sl = slice(h * HD, (h + 1) * HD)
            dr = dq_ref[:, sl] if h < NH else dk_ref[:, (h - NH) * HD:(h - NH + 1) * HD]
            gain = qg_ref[...] if h < NH else kg_ref[...]
            dn = dr * cos + _rope_partner(dr * sin)
            t = z_ref[:, sl]
            rstd = lax.rsqrt(jnp.mean(t * t, axis=1, keepdims=True) + RMS_EPS)
            that = t * rstd
            dgain = _colsum(dn * that)
            if h < NH:
                dqg_ref[...] += dgain
            else:
                dkg_ref[...] += dgain
            dthat = dn * gain
            dt = rstd * (dthat - that * jnp.mean(dthat * that, axis=1, keepdims=True))
            dz_ref[:, sl] = dt.astype(MXU)
            bsum_ref[:, sl] += _colsum(dt)
        dvv = dv_ref[...]
        dz_ref[:, (NH + NKV) * HD:] = dvv.astype(MXU)
        bsum_ref[:, (NH + NKV) * HD:] += _colsum(dvv)

    tab = pl.BlockSpec((TM, HD), lambda i: (i % tpe, 0))
    return _call(body, name="qk_bwd", grid=(nt,),
                 in_specs=[_rt(NH * HD), _rt(KV_W), _rt(KV_W), _rt(QKV_W), tab, tab, _vec(HD), _vec(HD)],
                 out_specs=[_rt(QKV_W), _vec(HD), _vec(HD), _vec(QKV_W)],
                 out_shape=[_sds((T, QKV_W), MXU), _sds((1, HD), F32), _sds((1, HD), F32), _sds((1, QKV_W), F32)],
                 )(dq, dk, dv, z_qkv, cos_t, sin_t, q_gain, k_gain)


def _ln_silu(hc, g, b):
    T = hc.shape[0]

    def body(h_ref, g_ref, b_ref, o_ref):
        xhat, _ = _ln_stats(h_ref[...])
        n = xhat * g_ref[...] + b_ref[...]
        o_ref[...] = (n * _sigmoid(n)).astype(MXU)

    return _call(body, name="ln_silu", grid=(T // TM,), in_specs=[_rt(D), _vec(D), _vec(D)], out_specs=_rt(D),
                 out_shape=_sds((T, D), MXU))(hc, g, b)


def _ln_silu_bwd(dsw, hc, g, b):
    T = hc.shape[0]

    def body(d_ref, h_ref, g_ref, b_ref, dh_ref, dg_ref, db_ref, dcb_ref):
        i = pl.program_id(0)

        @pl.when(i == 0)
        def _():
            dg_ref[...] = jnp.zeros_like(dg_ref)
            db_ref[...] = jnp.zeros_like(db_ref)
            dcb_ref[...] = jnp.zeros_like(dcb_ref)

        xhat, rstd = _ln_stats(h_ref[...])
        n = xhat * g_ref[...] + b_ref[...]
        sg = _sigmoid(n)
        dn = d_ref[...] * (sg * (1.0 + n * (1.0 - sg)))
        dg_ref[...] += _colsum(dn * xhat)
        db_ref[...] += _colsum(dn)
        dng = dn * g_ref[...]
        m1 = jnp.mean(dng, axis=1, keepdims=True)
        m2 = jnp.mean(dng * xhat, axis=1, keepdims=True)
        dh = rstd * (dng - m1 - xhat * m2)
        dh_ref[...] = dh
        dcb_ref[...] += _colsum(dh)

    return _call(body, name="ln_silu_bwd", grid=(T // TM,), in_specs=[_rt(D), _rt(D), _vec(D), _vec(D)],
                 out_specs=[_rt(D), _vec(D), _vec(D), _vec(D)],
                 out_shape=[_sds((T, D), F32)] + [_sds((1, D), F32)] * 3)(dsw, hc, g, b)


def _merge(attn, conv_o, pool_o, z_gate):
    T = attn.shape[0]

    def body(a_ref, c_ref, p_ref, zg_ref, m_ref):
        m = (_sigmoid(zg_ref[:, 0:D]) * a_ref[...] + _sigmoid(zg_ref[:, D:2 * D]) * c_ref[...]
             + _sigmoid(zg_ref[:, 2 * D:3 * D]) * p_ref[...])
        m_ref[...] = m.astype(MXU)

    return _call(body, name="merge", grid=(T // TM,), in_specs=[_rt(D), _rt(D), _rt(D), _rt(N_GATE)], out_specs=_rt(D),
                 out_shape=_sds((T, D), MXU))(attn, conv_o, pool_o, z_gate)


def _merge_bwd(dm, attn, conv_o, pool_o, z_gate):
    T = attn.shape[0]

    def body(dm_ref, a_ref, c_ref, p_ref, zg_ref, da_ref, dc_ref, dp_ref, dzg_ref, dcsum_ref, gsum_ref):
        i = pl.program_id(0)

        @pl.when(i == 0)
        def _():
            dcsum_ref[...] = jnp.zeros_like(dcsum_ref)
            gsum_ref[...] = jnp.zeros_like(gsum_ref)

        dmv = dm_ref[...]
        for k, (br_ref, out_ref) in enumerate(((a_ref, da_ref), (c_ref, dc_ref), (p_ref, dp_ref))):
            gk = _sigmoid(zg_ref[:, k * D:(k + 1) * D])
            dbr = dmv * gk
            out_ref[...] = dbr.astype(out_ref.dtype)
            if k == 1:
                dcsum_ref[...] += _colsum(dbr)
            dzg = dmv * br_ref[...] * gk * (1.0 - gk)
            dzg_ref[:, k * D:(k + 1) * D] = dzg.astype(MXU)
            gsum_ref[:, k * D:(k + 1) * D] += _colsum(dzg)

    return _call(body, name="merge_bwd", grid=(T // TM,), in_specs=[_rt(D), _rt(D), _rt(D), _rt(D), _rt(N_GATE)],
                 out_specs=[_rt(D), _rt(D), _rt(D), _rt(N_GATE), _vec(D), _vec(N_GATE)],
                 out_shape=[_sds((T, D), F32), _sds((T, D), MXU), _sds((T, D), F32), _sds((T, N_GATE), MXU),
                            _sds((1, D), F32), _sds((1, N_GATE), F32)])(dm, attn, conv_o, pool_o, z_gate)


def _softmax_rows(q, k):
    s = _dot(q, k, NT) * ATTN_SCALE
    p = jnp.exp(s - jnp.max(s, axis=1, keepdims=True))
    return p * (1.0 / jnp.sum(p, axis=1, keepdims=True))


def _attn_specs(nq):
    q_spec = pl.BlockSpec((TM, QG * HD), lambda b, h, q: (b * nq + q, h))
    k_spec = pl.BlockSpec((nq * TM, HD), lambda b, h, q: (b, NH + h))
    v_spec = pl.BlockSpec((nq * TM, HD), lambda b, h, q: (b, NH + NKV + h))
    return q_spec, k_spec, v_spec


def _attn_fwd(qkv, B, C, R):
    nq, ncq = R // TM, C // TM

    def body(q_ref, k_ref, v_ref, o_ref):
        def attend(L):
            k, v = k_ref[0:L, :], v_ref[0:L, :]
            for i in range(QG):
                sl = slice(i * HD, (i + 1) * HD)
                p = _softmax_rows(q_ref[:, sl], k)
                o_ref[:, sl] = _dot(p.astype(MXU), v, NN)

        qi = pl.program_id(2)
        pl.when(qi < ncq)(functools.partial(attend, C))
        pl.when(qi >= ncq)(functools.partial(attend, R))

    q_spec, k_spec, v_spec = _attn_specs(nq)
    return _call(body, name="attn_fwd", grid=(B, NKV, nq), in_specs=[q_spec, k_spec, v_spec], out_specs=q_spec,
                 out_shape=_sds((B * R, NH * HD), F32))(qkv, qkv, qkv)


def _attn_bwd(qkv, o, do, B, C, R):
    nq, ncq = R // TM, C // TM

    def body(q_ref, k_ref, v_ref, o_ref, do_ref, dq_ref, dk_ref, dv_ref):
        qi = pl.program_id(2)

        @pl.when(qi == 0)
        def _():
            dk_ref[...] = jnp.zeros_like(dk_ref)
            dv_ref[...] = jnp.zeros_like(dv_ref)

        def bwd(L):
            k, v = k_ref[0:L, :], v_ref[0:L, :]
            for i in range(QG):
                sl = slice(i * HD, (i + 1) * HD)
                q = q_ref[:, sl]
                p = _softmax_rows(q, k)
                dov = do_ref[:, sl]
                dob = dov.astype(MXU)
                dp = _dot(dob, v, NT)
                dl = jnp.sum(dov * o_ref[:, sl], axis=1, keepdims=True)
                ds = (p * (dp - dl) * ATTN_SCALE).astype(MXU)
                dq_ref[:, sl] = _dot(ds, k, NN)
                dk_ref[0:L, :] += _dot(ds, q, TN)
                dv_ref[0:L, :] += _dot(p.astype(MXU), dob, TN)

        pl.when(qi < ncq)(functools.partial(bwd, C))
        pl.when(qi >= ncq)(functools.partial(bwd, R))

    q_spec, k_spec, v_spec = _attn_specs(nq)
    kv_out = pl.BlockSpec((R, HD), lambda b, h, q: (b, h))
    return _call(body, name="attn_bwd", grid=(B, NKV, nq), in_specs=[q_spec, k_spec, v_spec, q_spec, q_spec],
                 out_specs=[q_spec, kv_out, kv_out],
                 out_shape=[_sds((B * R, NH * HD), F32), _sds((B * R, KV_W), F32), _sds((B * R, KV_W), F32)],
                 )(qkv, qkv, qkv, o, do)


def _segments(C, S):
    return ((0, GAP, C), (C, 2 * GAP + C, S))


def _padded_rows(C, S):
    return 3 * GAP + C + S


def _zero_gaps(pad_ref, C, S):
    for off in (0, GAP + C, 2 * GAP + C + S):
        pad_ref[off:off + GAP, :] = jnp.zeros((GAP, pad_ref.shape[1]), pad_ref.dtype)


def _chunks(n, ch, fn):
    def step(i, carry):
        fn(pl.multiple_of(i * ch, ch))
        return carry

    lax.fori_loop(0, n // ch, step, 0)


def _window(pad_ref, row, ch):
    return pad_ref[pl.ds(row - GAP, ch + 2 * GAP), :]


def _shifted(win, off, ch):
    return win[GAP + off:GAP + off + ch, :]


def _taps(pad_ref, w, row, ch, n_taps, flip=False):
    half = (n_taps - 1) // 2
    win = _window(pad_ref, row, ch)
    acc = None
    for k in range(n_taps):
        term = w[k:k + 1, :] * _shifted(win, (half - k) if flip else (k - half), ch)
        acc = term if acc is None else acc + term
    return acc


def _tap_grads(dw_ref, d, pad_ref, row, ch, n_taps):
    half = (n_taps - 1) // 2
    win = _window(pad_ref, row, ch)
    for k in range(n_taps):
        prod = d * _shifted(win, k - half, ch)
        dw_ref[k] += jnp.sum(prod.reshape(ch // 8, 8, prod.shape[1]), axis=0)


def _conv_fwd(z_conv, w, bias, B, C, S, cw=256, ch=64):
    R = C + S
    nj = CONV_CH // cw
    segs = _segments(C, S)

    def body(a_ref, g_ref, w_ref, b_ref, o_ref, pad):
        _zero_gaps(pad, C, S)
        wv = w_ref[...]
        for so, po, n in segs:
            def fill(r, so=so, po=po):
                pad[pl.ds(po + r, ch), :] = a_ref[pl.ds(so + r, ch), :] * _sigmoid(g_ref[pl.ds(so + r, ch), :])

            _chunks(n, ch, fill)
        for so, po, n in segs:
            def conv(r, so=so, po=po):
                o_ref[pl.ds(so + r, ch), :] = _taps(pad, wv, po + r, ch, CONV_K) + b_ref[...]

            _chunks(n, ch, conv)

    return _call(
        body, name="conv_fwd", grid=(nj, B),
        in_specs=[pl.BlockSpec((R, cw), lambda j, b: (b, j)), pl.BlockSpec((R, cw), lambda j, b: (b, nj + j)),
                  pl.BlockSpec((32, cw), lambda j, b: (0, j)), pl.BlockSpec((1, cw), lambda j, b: (0, j))],
        out_specs=pl.BlockSpec((R, cw), lambda j, b: (b, j)), out_shape=_sds((B * R, CONV_CH), F32),
        scratch=[pltpu.VMEM((_padded_rows(C, S), cw), F32)])(z_conv, z_conv, w, bias)


def _conv_bwd(dhc, z_conv, w, B, C, S, cw=256, ch=64):
    R = C + S
    nj = CONV_CH // cw
    segs = _segments(C, S)

    def body(d_ref, a_ref, g_ref, w_ref, da_ref, dg_ref, dw_ref, sa_ref, sg_ref, gpad, dpad, dwacc):
        b = pl.program_id(1)

        @pl.when(b == 0)
        def _():
            dw_ref[...] = jnp.zeros_like(dw_ref)
            sa_ref[...] = jnp.zeros_like(sa_ref)
            sg_ref[...] = jnp.zeros_like(sg_ref)

        _zero_gaps(gpad, C, S)
        _zero_gaps(dpad, C, S)
        dwacc[...] = jnp.zeros_like(dwacc)
        wv = w_ref[...]
        for so, po, n in segs:
            def fill(r, so=so, po=po):
                gpad[pl.ds(po + r, ch), :] = a_ref[pl.ds(so + r, ch), :] * _sigmoid(g_ref[pl.ds(so + r, ch), :])
                dpad[pl.ds(po + r, ch), :] = d_ref[pl.ds(so + r, ch), :]

            _chunks(n, ch, fill)
        for so, po, n in segs:
            def step(r, so=so, po=po):
                _tap_grads(dwacc, dpad[pl.ds(po + r, ch), :], gpad, po + r, ch, CONV_K)
                dglu = _taps(dpad, wv, po + r, ch, CONV_K, flip=True)
                av = a_ref[pl.ds(so + r, ch), :]
                sg = _sigmoid(g_ref[pl.ds(so + r, ch), :])
                da = dglu * sg
                dg = dglu * av * sg * (1.0 - sg)
                da_ref[pl.ds(so + r, ch), :] = da.astype(MXU)
                dg_ref[pl.ds(so + r, ch), :] = dg.astype(MXU)
                sa_ref[...] += _colsum(da)
                sg_ref[...] += _colsum(dg)

            _chunks(n, ch, step)
        for k in range(CONV_K):
            dw_ref[k:k + 1, :] += _colsum(dwacc[k])

    blk = pl.BlockSpec((R, cw), lambda j, b: (b, j))
    acc1 = pl.BlockSpec((1, cw), lambda j, b: (0, j))
    return _call(
        body, name="conv_bwd", grid=(nj, B),
        in_specs=[blk, blk, pl.BlockSpec((R, cw), lambda j, b: (b, nj + j)), pl.BlockSpec((32, cw), lambda j, b: (0, j))],
        out_specs=[blk, blk, pl.BlockSpec((32, cw), lambda j, b: (0, j)), acc1, acc1],
        out_shape=[_sds((B * R, CONV_CH), MXU), _sds((B * R, CONV_CH), MXU), _sds((32, CONV_CH), F32),
                   _sds((1, CONV_CH), F32), _sds((1, CONV_CH), F32)],
        scratch=[pltpu.VMEM((_padded_rows(C, S), cw), F32), pltpu.VMEM((_padded_rows(C, S), cw), F32),
                 pltpu.VMEM((32, 8, cw), F32)])(dhc, z_conv, z_conv, w)


def _ffn_mid(up, w, bias, B, C, S, cw=256, ch=64):
    R = C + S
    nj = D_FF // cw
    segs = _segments(C, S)

    def body(a_ref, u_ref, w_ref, b_ref, f_ref, pad):
        _zero_gaps(pad, C, S)
        wv = w_ref[...]
        for so, po, n in segs:
            def fill(r, so=so, po=po):
                pad[pl.ds(po + r, ch), :] = a_ref[pl.ds(so + r, ch), :]

            _chunks(n, ch, fill)
        for so, po, n in segs:
            def conv(r, so=so, po=po):
                ac = _taps(pad, wv, po + r, ch, FFN_K) + b_ref[...]
                f_ref[pl.ds(so + r, ch), :] = (ac * _sigmoid(ac) * u_ref[pl.ds(so + r, ch), :]).astype(MXU)

            _chunks(n, ch, conv)

    return _call(
        body, name="ffn_mid", grid=(nj, B),
        in_specs=[pl.BlockSpec((R, cw), lambda j, b: (b, j)), pl.BlockSpec((R, cw), lambda j, b: (b, nj + j)),
                  pl.BlockSpec((8, cw), lambda j, b: (0, j)), pl.BlockSpec((1, cw), lambda j, b: (0, j))],
        out_specs=pl.BlockSpec((R, cw), lambda j, b: (b, j)), out_shape=_sds((B * R, D_FF), MXU),
        scratch=[pltpu.VMEM((_padded_rows(C, S), cw), F32)])(up, up, w, bias)


def _ffn_mid_bwd(df, up, w, bias, B, C, S, cw=256, ch=64):
    R = C + S
    nj = D_FF // cw
    segs = _segments(C, S)

    def body(d_ref, a_ref, u_ref, w_ref, b_ref, da_ref, du_ref, dw_ref, db_ref, apad, dpad, dwacc):
        b = pl.program_id(1)

        @pl.when(b == 0)
        def _():
            dw_ref[...] = jnp.zeros_like(dw_ref)
            db_ref[...] = jnp.zeros_like(db_ref)

        _zero_gaps(apad, C, S)
        _zero_gaps(dpad, C, S)
        dwacc[...] = jnp.zeros_like(dwacc)
        wv = w_ref[...]
        for so, po, n in segs:
            def fill(r, so=so, po=po):
                apad[pl.ds(po + r, ch), :] = a_ref[pl.ds(so + r, ch), :]

            _chunks(n, ch, fill)
        for so, po, n in segs:
            def first(r, so=so, po=po):
                ac = _taps(apad, wv, po + r, ch, FFN_K) + b_ref[...]
                sg = _sigmoid(ac)
                dfv = d_ref[pl.ds(so + r, ch), :]
                du_ref[pl.ds(so + r, ch), :] = (dfv * ac * sg).astype(MXU)
                dac = dfv * u_ref[pl.ds(so + r, ch), :] * (sg * (1.0 + ac * (1.0 - sg)))
                dpad[pl.ds(po + r, ch), :] = dac
                db_ref[...] += _colsum(dac)

            _chunks(n, ch, first)
        for so, po, n in segs:
            def second(r, so=so, po=po):
                _tap_grads(dwacc, dpad[pl.ds(po + r, ch), :], apad, po + r, ch, FFN_K)
                da_ref[pl.ds(so + r, ch), :] = _taps(dpad, wv, po + r, ch, FFN_K, flip=True).astype(MXU)

            _chunks(n, ch, second)
        for k in range(FFN_K):
            dw_ref[k:k + 1, :] += _colsum(dwacc[k])

    blk = pl.BlockSpec((R, cw), lambda j, b: (b, j))
    return _call(
        body, name="ffn_mid_bwd", grid=(nj, B),
        in_specs=[blk, blk, pl.BlockSpec((R, cw), lambda j, b: (b, nj + j)), pl.BlockSpec((8, cw), lambda j, b: (0, j)),
                  pl.BlockSpec((1, cw), lambda j, b: (0, j))],
        out_specs=[blk, blk, pl.BlockSpec((8, cw), lambda j, b: (0, j)), pl.BlockSpec((1, cw), lambda j, b: (0, j))],
        out_shape=[_sds((B * R, D_FF), MXU), _sds((B * R, D_FF), MXU), _sds((8, D_FF), F32), _sds((1, D_FF), F32)],
        scratch=[pltpu.VMEM((_padded_rows(C, S), cw), F32), pltpu.VMEM((_padded_rows(C, S), cw), F32),
                 pltpu.VMEM((8, 8, cw), F32)])(df, up, up, w, bias)


def _window_count(r, ch, n, w):
    t = r + lax.broadcasted_iota(jnp.int32, (ch, 1), 0)
    return (jnp.minimum(t + w // 2, n) - jnp.maximum(t - w // 2, 0)).astype(F32)


def _pool_fwd(z_pool, pool_w, pool_scale, B, C, S, ch=128):
    R = C + S
    gch = POOL_GCH
    segs = _segments(C, S)

    def body(u_ref, pw_ref, sc_ref, pooled_ref, po_ref, pad):
        g = pl.program_id(1)
        _zero_gaps(pad, C, S)
        for so, po, n in segs:
            def fill(r, so=so, po=po):
                pad[pl.ds(po + r, ch), :] = u_ref[pl.ds(so + r, ch), :]

            _chunks(n, ch, fill)
        for gi, w in enumerate(POOL_WINDOWS):
            @pl.when(g == gi)
            def _(w=w):
                for so, po, n in segs:
                    def step(r, so=so, po=po, n=n):
                        win = _window(pad, po + r, ch)
                        acc = _shifted(win, -(w // 2), ch)
                        for o in range(1 - w // 2, w // 2):
                            acc = acc + _shifted(win, o, ch)
                        pooled = (acc / _window_count(r, ch, n, w) - _shifted(win, 0, ch)).astype(MXU)
                        pooled_ref[pl.ds(so + r, ch), :] = pooled
                        po_ref[pl.ds(so + r, ch), :] = _dot(pooled, pw_ref[0], NN) * sc_ref[...]

                    _chunks(n, ch, step)

    blk = pl.BlockSpec((R, gch), lambda b, g: (b, g))
    return _call(
        body, name="pool_fwd", grid=(B, len(POOL_WINDOWS)),
        in_specs=[blk, pl.BlockSpec((1, gch, gch), lambda b, g: (g, 0, 0)), pl.BlockSpec((1, gch), lambda b, g: (0, g))],
        out_specs=[blk, blk], out_shape=[_sds((B * R, POOL_CH), MXU), _sds((B * R, POOL_CH), F32)],
        scratch=[pltpu.VMEM((_padded_rows(C, S), gch), F32)])(z_pool, pool_w, pool_scale)


def _pool_bwd(dpo, pooled, pool_w, pool_scale, B, C, S, ch=128):
    R = C + S
    gch = POOL_GCH
    segs = _segments(C, S)

    def body(d_ref, p_ref, pw_ref, sc_ref, du_ref, dpw_ref, dsc_ref, su_ref, qpad, dpl):
        g, b = pl.program_id(0), pl.program_id(1)

        @pl.when(b == 0)
        def _():
            dpw_ref[...] = jnp.zeros_like(dpw_ref)
            dsc_ref[...] = jnp.zeros_like(dsc_ref)
            su_ref[...] = jnp.zeros_like(su_ref)

        _zero_gaps(qpad, C, S)
        pw = pw_ref[0]
        for gi, w in enumerate(POOL_WINDOWS):
            @pl.when(g == gi)
            def _(w=w):
                for so, po, n in segs:
                    def first(r, so=so, po=po, n=n):
                        pv = p_ref[pl.ds(so + r, ch), :]
                        dv = d_ref[pl.ds(so + r, ch), :]
                        dsc_ref[...] += _colsum(dv * _dot(pv, pw, NN))
                        dmx = (dv * sc_ref[...]).astype(MXU)
                        dpw_ref[0] += _dot(pv, dmx, TN)
                        dp = _dot(dmx, pw, NT)
                        dpl[pl.ds(so + r, ch), :] = dp
                        qpad[pl.ds(po + r, ch), :] = dp / _window_count(r, ch, n, w)

                    _chunks(n, ch, first)
                for so, po, n in segs:
                    def second(r, so=so, po=po):
                        win = _window(qpad, po + r, ch)
                        acc = _shifted(win, 1 - w // 2, ch)
                        for o in range(2 - w // 2, w // 2 + 1):
                            acc = acc + _shifted(win, o, ch)
                        du = acc - dpl[pl.ds(so + r, ch), :]
                        du_ref[pl.ds(so + r, ch), :] = du.astype(MXU)
                        su_ref[...] += _colsum(du)

                    _chunks(n, ch, second)

    blk = pl.BlockSpec((R, gch), lambda g, b: (b, g))
    vec = pl.BlockSpec((1, gch), lambda g, b: (0, g))
    wblk = pl.BlockSpec((1, gch, gch), lambda g, b: (g, 0, 0))
    return _call(
        body, name="pool_bwd", grid=(len(POOL_WINDOWS), B), in_specs=[blk, blk, wblk, vec], out_specs=[blk, wblk, vec, vec],
        out_shape=[_sds((B * R, POOL_CH), MXU), _sds((len(POOL_WINDOWS), gch, gch), F32), _sds((1, POOL_CH), F32),
                   _sds((1, POOL_CH), F32)],
        scratch=[pltpu.VMEM((_padded_rows(C, S), gch), F32), pltpu.VMEM((R, gch), F32)])(dpo, pooled, pool_w, pool_scale)


def _silu_rows(cond):
    def body(c_ref, s_ref, d_ref):
        c = c_ref[...]
        sg = _sigmoid(c)
        s_ref[...] = (c * sg).astype(MXU)
        d_ref[...] = sg * (1.0 + c * (1.0 - sg))

    full = pl.BlockSpec(cond.shape, lambda i: (0, 0))
    return _call(body, name="silu_rows", grid=(1,), in_specs=[full], out_specs=[full, full],
                 out_shape=[_sds(cond.shape, MXU), _sds(cond.shape, F32)])(cond)


def _adamw(parts, w, m, v, *, name, tr):
    n_parts, rows, lanes = parts.shape
    c1 = 1.0 - ADAM_B1 ** ADAM_STEP
    c2 = 1.0 - ADAM_B2 ** ADAM_STEP

    def body(p_ref, w_ref, m_ref, v_ref, g_ref, d_ref, nm_ref, nv_ref):
        g = p_ref[0].astype(F32)
        for k in range(1, n_parts):
            g = g + p_ref[k].astype(F32)
        nm = ADAM_B1 * m_ref[...] + (1.0 - ADAM_B1) * g
        nv = ADAM_B2 * v_ref[...] + (1.0 - ADAM_B2) * (g * g)
        g_ref[...] = g
        nm_ref[...] = nm
        nv_ref[...] = nv
        d_ref[...] = -ADAM_LR * ((nm / c1) / (jnp.sqrt(nv / c2) + ADAM_EPS) + ADAM_WD * w_ref[...])

    blk = pl.BlockSpec((tr, lanes), lambda i: (i, 0))
    return _call(body, name=name, grid=(rows // tr,), in_specs=[pl.BlockSpec((n_parts, tr, lanes), lambda i: (0, i, 0)), blk, blk, blk],
                 out_specs=[blk] * 4, out_shape=[_sds((rows, lanes), F32)] * 4)(parts, w, m, v)


def _pair_sum(gp, r1, *, tr):
    _, rows, lanes = gp.shape
    c = lax.axis_index("c")
    gp4 = gp.reshape(4, 2, rows, lanes)

    def body(c_ref, g_ref, r_ref, o_ref):
        o_ref[...] = (g_ref[:, 0] + r_ref[...]).astype(WIRE)

    return _pcall(
        body, name="pair_sum", out_shape=_sds((4, rows, lanes), WIRE),
        grid_spec=pltpu.PrefetchScalarGridSpec(
            num_scalar_prefetch=1, grid=(rows // tr,),
            in_specs=[pl.BlockSpec((4, 1, tr, lanes), lambda i, c_ref: (0, c_ref[0], i, 0)),
                      pl.BlockSpec((4, tr, lanes), lambda i, c_ref: (0, i, 0))],
            out_specs=pl.BlockSpec((4, tr, lanes), lambda i, c_ref: (0, i, 0))),
        compiler_params=pltpu.CompilerParams(dimension_semantics=("arbitrary",), vmem_limit_bytes=VMEM_MB * 2 ** 20),
    )(jnp.reshape(c, (1,)).astype(jnp.int32), gp4, r1)


MESH = pl.DeviceIdType.MESH
ANY = pl.BlockSpec(memory_space=pl.ANY)


def _all_gather(shard, *, name):
    rows, lanes = shard.shape

    def body(x_ref, out_ref, send_sems, recv_sems, local_sem):
        x, y, c = lax.axis_index("x"), lax.axis_index("y"), lax.axis_index("c")
        me, sibling = (x, y, c), (x, y, 1 - c)
        chips = [(1 - x, y), (x, 1 - y), (1 - x, 1 - y)]

        def block(px, py, pc):
            return out_ref.at[4 * px + 2 * py + pc]

        def copy(k, blk, to, src=None):
            return pltpu.make_async_remote_copy(
                src_ref=block(*blk) if src is None else src, dst_ref=block(*blk), send_sem=send_sems.at[k],
                recv_sem=recv_sems.at[k], device_id=to, device_id_type=MESH)

        mine = pltpu.make_async_copy(x_ref, block(*me), local_sem)
        mine.start()
        first = [copy(0, me, sibling, src=x_ref)] + [copy(1 + j, me, (*chip, c), src=x_ref) for j, chip in enumerate(chips)]
        for cp in first:
            cp.start()
        passed = [copy(4 + j, (*chip, c), sibling) for j, chip in enumerate(chips)]
        for j, chip in enumerate(chips):
            copy(1 + j, (*chip, c), me).wait_recv()
            passed[j].start()
        copy(0, sibling, me).wait_recv()
        for j, chip in enumerate(chips):
            copy(4 + j, (*chip, 1 - c), me).wait_recv()
        for cp in first + passed:
            cp.wait_send()
        mine.wait()

    return _pcall(
        body, name=name, out_shape=_sds((N_DEV, rows, lanes), shard.dtype), in_specs=[ANY], out_specs=ANY,
        scratch_shapes=[pltpu.SemaphoreType.DMA((7,)), pltpu.SemaphoreType.DMA((7,)), pltpu.SemaphoreType.DMA],
    )(shard)


def _sibling_exchange(gp):
    _, rows, lanes = gp.shape
    gp4 = gp.reshape(4, 2, rows, lanes)

    def body(g_ref, r_ref, send_sems, recv_sems):
        x, y, c = lax.axis_index("x"), lax.axis_index("y"), lax.axis_index("c")
        copies = [pltpu.make_async_remote_copy(
            src_ref=g_ref.at[j, 1 - c], dst_ref=r_ref.at[j], send_sem=send_sems.at[j], recv_sem=recv_sems.at[j],
            device_id=(x, y, 1 - c), device_id_type=MESH) for j in range(4)]
        for cp in copies:
            cp.start()
        for cp in copies:
            cp.wait_recv()
        for cp in copies:
            cp.wait_send()

    return _pcall(
        body, name="sibling_exchange", out_shape=_sds((4, rows, lanes), gp.dtype), in_specs=[ANY], out_specs=ANY,
        scratch_shapes=[pltpu.SemaphoreType.DMA((4,)), pltpu.SemaphoreType.DMA((4,))],
    )(gp4)


def _chip_exchange(p):
    _, rows, lanes = p.shape

    def body(p_ref, r_ref, send_sems, recv_sems, local_sem):
        x, y, c = lax.axis_index("x"), lax.axis_index("y"), lax.axis_index("c")
        mine = 2 * x + y
        chips = [(1 - x, y), (x, 1 - y), (1 - x, 1 - y)]
        own = pltpu.make_async_copy(p_ref.at[mine], r_ref.at[mine], local_sem)
        own.start()
        sends = [pltpu.make_async_remote_copy(
            src_ref=p_ref.at[2 * px + py], dst_ref=r_ref.at[mine], send_sem=send_sems.at[j], recv_sem=recv_sems.at[j],
            device_id=(px, py, c), device_id_type=MESH) for j, (px, py) in enumerate(chips)]
        for cp in sends:
            cp.start()
        for j, (px, py) in enumerate(chips):
            pltpu.make_async_remote_copy(
                src_ref=p_ref.at[mine], dst_ref=r_ref.at[2 * px + py], send_sem=send_sems.at[j], recv_sem=recv_sems.at[j],
                device_id=(px, py, c), device_id_type=MESH).wait_recv()
        for cp in sends:
            cp.wait_send()
        own.wait()

    return _pcall(
        body, name="chip_exchange", out_shape=_sds((4, rows, lanes), p.dtype), in_specs=[ANY], out_specs=ANY,
        scratch_shapes=[pltpu.SemaphoreType.DMA((3,)), pltpu.SemaphoreType.DMA((3,)), pltpu.SemaphoreType.DMA],
    )(p)


BIG = (("w_ada", (D, N_MOD * D // N_DEV), 1), ("w_in", (D, D_IN // N_DEV), 1), ("conv_pw_w", (CONV_CH // N_DEV, D), 0),
       ("pool_w", (len(POOL_WINDOWS), POOL_GCH // N_DEV, POOL_GCH), 1), ("w_out", (D // N_DEV, D), 0),
       ("w_up", (D, 2 * D_FF // N_DEV), 1), ("w_down", (D_FF // N_DEV, D), 0))
TAPS = (("conv_dw_w", (CONV_K, CONV_CH // N_DEV), 1), ("ffn_dw_w", (FFN_K, D_FF // N_DEV), 1))
SHARDED = BIG + TAPS
REPLICATED = (("b_ada", N_MOD * D), ("b_in", D_IN), ("q_gain", HD), ("k_gain", HD), ("conv_dw_b", CONV_CH), ("conv_ln_g", CONV_CH),
              ("conv_ln_b", CONV_CH), ("conv_pw_b", D), ("pool_scale", POOL_CH), ("b_out", D), ("ln1_g", D), ("ln1_b", D),
              ("ln2_g", D), ("ln2_b", D), ("ffn_dw_b", D_FF))


def _rows_of(shape):
    n = int(np.prod(shape))
    return -(-n // LANES)


def _flat_rows(a, lead=()):
    flat = a.reshape(lead + (-1,))
    n = flat.shape[-1]
    padn = _rows_of((n,)) * LANES - n
    if padn:
        flat = jnp.pad(flat, [(0, 0)] * len(lead) + [(0, padn)])
    return flat.reshape(lead + (-1, LANES))


def _pack_rows(pieces, rows):
    buf = jnp.concatenate(pieces, axis=-2)
    extra = rows - buf.shape[-2]
    if extra:
        buf = jnp.pad(buf, [(0, 0)] * (buf.ndim - 2) + [(0, extra), (0, 0)])
    return buf


def _unpack_rows(buf, specs, lead=()):
    out, r = {}, 0
    for name, shape, _ in specs:
        nr = _rows_of(shape)
        n = int(np.prod(shape))
        out[name] = buf[..., r:r + nr, :].reshape(lead + (nr * LANES,))[..., :n].reshape(lead + tuple(shape))
        r += nr
    return out


def _full_from_blocks(blocks, axis):
    moved = jnp.moveaxis(blocks, 0, axis)
    shape = list(moved.shape)
    shape[axis:axis + 2] = [shape[axis] * shape[axis + 1]]
    return moved.reshape(shape)


def _blocks_from_full(full, axis):
    shape = list(full.shape)
    shape[axis:axis + 1] = [N_DEV, shape[axis] // N_DEV]
    return jnp.moveaxis(full.reshape(shape), axis, 0)


W_ROWS = sum(_rows_of(s) for _, s, _ in BIG)
G_ROWS = -(-(W_ROWS + sum(_rows_of(s) for _, s, _ in TAPS)) // 448) * 448
TAP_ROWS = -(-DEPTH * sum(_rows_of(s) for _, s, _ in TAPS) // 8) * 8
SMALL_N = DEPTH * sum(n for _, n in REPLICATED) + D
SMALL_ROWS = -(-(SMALL_N + 1) // (8 * LANES)) * 8


def _rope_tables(C, S):
    t = np.arange(S)
    inv_freq = ROPE_THETA ** (-np.arange(ROPE_PAIRS, dtype=np.float32) / ROPE_PAIRS)
    row = jnp.asarray((t // GRID_W).astype(np.float32))[:, None] * jnp.asarray(inv_freq, F32)
    col = jnp.asarray((t % GRID_W).astype(np.float32))[:, None] * jnp.asarray(inv_freq, F32)
    cos = jnp.concatenate([jnp.cos(row), jnp.cos(row), jnp.cos(col), jnp.cos(col)], axis=1)
    sin = jnp.concatenate([-jnp.sin(row), jnp.sin(row), -jnp.sin(col), jnp.sin(col)], axis=1)
    cos = jnp.concatenate([jnp.ones((C, HD), F32), cos], axis=0)
    sin = jnp.concatenate([jnp.zeros((C, HD), F32), sin], axis=0)
    return cos, sin


def _segment_sums(parts, B, tpe, ncq):
    p = parts.reshape(B, tpe, D)
    return jnp.concatenate([jnp.sum(p[:, ncq:], axis=1), jnp.sum(p[:, :ncq], axis=(0, 1))[None]], axis=0)


def kernel(x, c, ctx, c_ctx, w_ada, b_ada, w_in, b_in, q_gain, k_gain, conv_dw_w, conv_dw_b, conv_ln_g, conv_ln_b, conv_pw_w, conv_pw_b, pool_w, pool_scale, w_out, b_out, ln1_g, ln1_b, ln2_g, ln2_b, w_up, ffn_dw_w, ffn_dw_b, w_down, loss_target, m_c_ctx, m_w_ada, m_b_ada, m_w_in, m_b_in, m_q_gain, m_k_gain, m_conv_dw_w, m_conv_dw_b, m_conv_ln_g, m_conv_ln_b, m_conv_pw_w, m_conv_pw_b, m_pool_w, m_pool_scale, m_w_out, m_b_out, m_ln1_g, m_ln1_b, m_ln2_g, m_ln2_b, m_w_up, m_ffn_dw_w, m_ffn_dw_b, m_w_down, v_c_ctx, v_w_ada, v_b_ada, v_w_in, v_b_in, v_q_gain, v_k_gain, v_conv_dw_w, v_conv_dw_b, v_conv_ln_g, v_conv_ln_b, v_conv_pw_w, v_conv_pw_b, v_pool_w, v_pool_scale, v_w_out, v_b_out, v_ln1_g, v_ln1_b, v_ln2_g, v_ln2_b, v_w_up, v_ffn_dw_w, v_ffn_dw_b, v_w_down):
    given = dict(locals())
    B, S, _ = x.shape
    C = ctx.shape[1]
    R = C + S
    T = B * R
    tpe, ncq = R // TM, C // TM
    nt = T // TM
    assert S % TM == 0 and C % TM == 0 and B + 1 <= 16

    packs = [_all_gather(_pack_rows([_flat_rows(given[n][l]) for n, _, _ in BIG], W_ROWS).astype(MXU), name="gather_weights")
             for l in range(DEPTH)]
    tap_pack = _pack_rows([_flat_rows(given[n][l]) for l in range(DEPTH) for n, _, _ in TAPS], TAP_ROWS)
    tap_blocks = _unpack_rows(_all_gather(tap_pack, name="gather_taps"),
                              tuple((f"{n}{l}", s, a) for l in range(DEPTH) for n, s, a in TAPS), lead=(N_DEV,))
    W = []
    for l in range(DEPTH):
        blocks = _unpack_rows(packs[l], BIG, lead=(N_DEV,))
        wl = {n: _full_from_blocks(blocks[n], a) for n, _, a in BIG}
        for n, _, a in TAPS:
            wl[n] = _full_from_blocks(tap_blocks[f"{n}{l}"], a)
        W.append(wl)

    xu = jnp.concatenate([ctx, x], axis=1).reshape(T, D)
    cond = jnp.zeros((16, D), F32).at[:B].set(c).at[B].set(c_ctx)
    s_cond, ds_cond = _silu_rows(cond)
    tile_row = np.array([B if (t % tpe) < ncq else t // tpe for t in range(nt)])
    cos_t, sin_t = _rope_tables(C, S)
    row = lambda v: v.reshape(1, -1)

    mods = [_matmul(s_cond, W[l]["w_ada"], "nn", name="ada", bias=row(b_ada[l]), tm=16, tn=1024) for l in range(DEPTH)]
    modt = [m[tile_row].reshape(nt, 1, N_MOD * D) for m in mods]

    saved = []
    h1 = _modulate_cast(xu, modt[0], 0)
    xin = xu
    for l in range(DEPTH):
        wl = W[l]
        w_inl, b_inl = wl["w_in"], b_in[l]
        z_qkv = _matmul(h1, w_inl[:, :QKV_W], "nn", name="z_qkv", bias=row(b_inl[:QKV_W]), tn=768)
        c0, p0, g0 = QKV_W, QKV_W + 2 * CONV_CH, QKV_W + 2 * CONV_CH + POOL_CH
        z_conv = _matmul(h1, w_inl[:, c0:p0], "nn", name="z_conv", bias=row(b_inl[c0:p0]))
        z_pool = _matmul(h1, w_inl[:, p0:g0], "nn", name="z_pool", bias=row(b_inl[p0:g0]))
        z_gate = _matmul(h1, w_inl[:, g0:], "nn", name="z_gate", bias=row(b_inl[g0:]))
        qkv = _qk_prep(z_qkv, cos_t, sin_t, row(q_gain[l]), row(k_gain[l]), tpe)
        attn = _attn_fwd(qkv, B, C, R)
        dw32 = jnp.pad(wl["conv_dw_w"], ((0, 32 - CONV_K), (0, 0)))
        hc = _conv_fwd(z_conv, dw32, row(conv_dw_b[l]), B, C, S)
        sw = _ln_silu(hc, row(conv_ln_g[l]), row(conv_ln_b[l]))
        conv_o = _matmul(sw, wl["conv_pw_w"], "nn", name="conv_pw", bias=row(conv_pw_b[l]))
        pooled, pool_o = _pool_fwd(z_pool, wl["pool_w"], row(pool_scale[l]), B, C, S)
        m = _merge(attn, conv_o, pool_o, z_gate)
        mo = _matmul(m, wl["w_out"], "nn", name="w_out", bias=row(b_out[l]))
        y1, h2 = _resid_ln(xin, mo, modt[l], 2, row(ln1_g[l]), row(ln1_b[l]), modt[l], 3)
        up = _matmul(h2, wl["w_up"], "nn", name="w_up", tn=1408)
        fw8 = jnp.pad(wl["ffn_dw_w"], ((0, 8 - FFN_K), (0, 0)))
        f = _ffn_mid(up, fw8, row(ffn_dw_b[l]), B, C, S)
        fo = _matmul(f, wl["w_down"], "nn", name="w_down", tm=512)
        if l + 1 < DEPTH:
            y2, h_next = _resid_ln(y1, fo, modt[l], 5, row(ln2_g[l]), row(ln2_b[l]), modt[l + 1], 0)
        else:
            y2, h_next = _resid_ln(y1, fo, modt[l], 5, row(ln2_g[l]), row(ln2_b[l])), None
        saved.append(dict(xin=xin, h1=h1, z_qkv=z_qkv, z_conv=z_conv, z_gate=z_gate, qkv=qkv, attn=attn, hc=hc, sw=sw,
                          conv_o=conv_o, pooled=pooled, pool_o=pool_o, m=m, mo=mo, y1=y1, h2=h2, up=up, f=f, fo=fo, dw32=dw32,
                          fw8=fw8))
        xin, h1 = y2, h_next

    dy, loss_part = _loss_grad(xin, loss_target.reshape(B * S, D), tpe, ncq)

    small = {n: [None] * DEPTH for n, _ in REPLICATED}
    d_c_ctx = jnp.zeros((D,), F32)
    dmods_t = [[None] * N_MOD for _ in range(DEPTH)]
    layer_grads = [None] * DEPTH
    dh1 = None
    for l in reversed(range(DEPTH)):
        wl, sv = W[l], saved[l]
        dmod = dmods_t[l]
        if dh1 is None:
            dy1p, dfo, dgate2, dg, db, _ = _ln_bwd(dy, sv["y1"], sv["fo"], modt[l], 5, row(ln2_g[l]))
        else:
            dy1p, dfo, dgate2, dg, db, _, dsh, dsc = _ln_bwd(dy, sv["y1"], sv["fo"], modt[l], 5, row(ln2_g[l]), dh=dh1,
                                                             y=saved[l + 1]["xin"], mod_next=modt[l + 1], k_shift_next=0)
            dmods_t[l + 1][0], dmods_t[l + 1][1] = dsh, dsc
        small["ln2_g"][l], small["ln2_b"][l] = dg[0], db[0]
        dmod[5] = dgate2
        df = _matmul(dfo, wl["w_down"], "nt", name="d_f", tn=1408)
        g_w_down = _matmul(sv["f"], dfo, "tn", name="g_w_down", tm=1408, tk=512)
        da2, du2, g_fdw, g_fdb = _ffn_mid_bwd(df, sv["up"], sv["fw8"], row(ffn_dw_b[l]), B, C, S)
        small["ffn_dw_b"][l] = g_fdb[0]
        dh2 = _matmul(da2, wl["w_up"][:, :D_FF], "nt", name="d_h2a", tm=512)
        dh2 = _matmul(du2, wl["w_up"][:, D_FF:], "nt", name="d_h2u", tm=512, acc_in=dh2)
        g_w_up = jnp.concatenate([_matmul(sv["h2"], da2, "tn", name="g_w_up_a", tn=1408, tk=512),
                                  _matmul(sv["h2"], du2, "tn", name="g_w_up_u", tn=1408, tk=512)], axis=1)
        dxp, dmo, dgate1, dg, db, dbo, dsh, dsc = _ln_bwd(dy1p, sv["xin"], sv["mo"], modt[l], 2, row(ln1_g[l]), dh=dh2,
                                                          y=sv["y1"], mod_next=modt[l], k_shift_next=3)
        small["ln1_g"][l], small["ln1_b"][l], small["b_out"][l] = dg[0], db[0], dbo[0]
        dmod[2], dmod[3], dmod[4] = dgate1, dsh, dsc
        dm = _matmul(dmo, wl["w_out"], "nt", name="d_m")
        g_w_out = _matmul(sv["m"], dmo, "tn", name="g_w_out", tk=512)
        dattn, dconv_o, dpool_o, dzg, g_pwb, gsum_gate = _merge_bwd(dm, sv["attn"], sv["conv_o"], sv["pool_o"], sv["z_gate"])
        small["conv_pw_b"][l] = g_pwb[0]
        du, g_pool_w, g_pool_sc, gsum_pool = _pool_bwd(dpool_o, sv["pooled"], wl["pool_w"], row(pool_scale[l]), B, C, S)
        small["pool_scale"][l] = g_pool_sc[0]
        dsw = _matmul(dconv_o, wl["conv_pw_w"], "nt", name="d_sw")
        g_pw = _matmul(sv["sw"], dconv_o, "tn", name="g_conv_pw", tk=512)
        dhc, g_cg, g_cb, g_cdb = _ln_silu_bwd(dsw, sv["hc"], row(conv_ln_g[l]), row(conv_ln_b[l]))
        small["conv_ln_g"][l], small["conv_ln_b"][l], small["conv_dw_b"][l] = g_cg[0], g_cb[0], g_cdb[0]
        da, dgt, g_cdw, gsum_a, gsum_gt = _conv_bwd(dhc, sv["z_conv"], sv["dw32"], B, C, S)
        dq, dk, dv = _attn_bwd(sv["qkv"], sv["attn"], dattn, B, C, R)
        dz_qkv, g_qg, g_kg, gsum_qkv = _qk_bwd(dq, dk, dv, sv["z_qkv"], cos_t, sin_t, row(q_gain[l]), row(k_gain[l]), tpe)
        small["q_gain"][l], small["k_gain"][l] = g_qg[0], g_kg[0]
        small["b_in"][l] = jnp.concatenate([gsum_qkv[0], gsum_a[0], gsum_gt[0], gsum_pool[0], gsum_gate[0]])
        w_inl = wl["w_in"]
        pieces = ((dz_qkv, 0, QKV_W), (da, QKV_W, CONV_CH), (dgt, QKV_W + CONV_CH, CONV_CH), (du, QKV_W + 2 * CONV_CH, POOL_CH),
                  (dzg, QKV_W + 2 * CONV_CH + POOL_CH, N_GATE))
        dh1 = None
        g_w_in = []
        for k, (dz, c0, wd) in enumerate(pieces):
            dh1 = _matmul(dz, w_inl[:, c0:c0 + wd], "nt", name=f"d_h1_{k}", tm=512, acc_in=dh1)
            g_w_in.append(_matmul(sv["h1"], dz, "tn", name=f"g_w_in_{k}", tk=512))
        g_w_in = jnp.concatenate(g_w_in, axis=1)

        layer_grads[l] = {"w_in": g_w_in, "conv_pw_w": g_pw, "pool_w": g_pool_w, "w_out": g_w_out, "w_up": g_w_up,
                          "w_down": g_w_down, "conv_dw_w": g_cdw[:CONV_K], "ffn_dw_w": g_fdw[:FFN_K]}
        dy = dxp
    gx_u, dmods_t[0][0], dmods_t[0][1] = _mod_bwd(dy, dh1, saved[0]["xin"], modt[0], 0)
    grad_x = gx_u.reshape(B, R, D)[:, C:]

    grad_packs = []
    for l in range(DEPTH):
        dmods = jnp.concatenate([_segment_sums(p, B, tpe, ncq) for p in dmods_t[l]], axis=1)
        small["b_ada"][l] = jnp.sum(dmods, axis=0)
        dm16 = jnp.zeros((16, N_MOD * D), F32).at[:B + 1].set(dmods).astype(MXU)
        layer_grads[l]["w_ada"] = _matmul(s_cond, dm16, "tn", name="g_w_ada", tm=1024, tn=1024)
        dcond = _matmul(dm16, W[l]["w_ada"], "nt", name="d_cond", tm=16, tn=1024, tk=2048)
        d_c_ctx = d_c_ctx + dcond[B] * ds_cond[B]
        blocks = [_flat_rows(_blocks_from_full(layer_grads[l][n], a), lead=(N_DEV,)) for n, _, a in SHARDED]
        grad_packs.append(_pack_rows(blocks, G_ROWS))

    outs = {}
    for l in range(DEPTH):
        gp = grad_packs[l]
        p = _pair_sum(gp, _sibling_exchange(gp), tr=224)
        r2 = _chip_exchange(p)
        wp, mp, vp = (_pack_rows([_flat_rows(given[pre + n][l]) for n, _, _ in SHARDED], G_ROWS) for pre in ("", "m_", "v_"))
        res = _adamw(r2, wp, mp, vp, name="adamw_sharded", tr=224)
        for kind, buf in zip(("grad_", "delta_", "new_m_", "new_v_"), res):
            for n, a in _unpack_rows(buf, SHARDED).items():
                outs.setdefault(kind + n, []).append(a)
    for k in list(outs):
        outs[k] = jnp.stack(outs[k])

    def small_pack(get, extra):
        vecs = [get(n)[l] for l in range(DEPTH) for n, _ in REPLICATED] + [get("c_ctx"), extra]
        flat = jnp.concatenate([v.reshape(-1) for v in vecs])
        return jnp.pad(flat, (0, SMALL_ROWS * LANES - flat.shape[0])).reshape(SMALL_ROWS, LANES)

    small["c_ctx"] = d_c_ctx
    zero1 = jnp.zeros((1,), F32)
    g_small = _all_gather(small_pack(lambda n: small[n], loss_part[0, :1]), name="gather_small")
    res = _adamw(g_small, small_pack(lambda n: given[n], zero1), small_pack(lambda n: given["m_" + n], zero1),
                 small_pack(lambda n: given["v_" + n], zero1), name="adamw_small", tr=SMALL_ROWS)
    for kind, buf in zip(("grad_", "delta_", "new_m_", "new_v_"), res):
        flat = buf.reshape(-1)
        off = 0
        per_layer = {n: [] for n, _ in REPLICATED}
        for l in range(DEPTH):
            for n, sz in REPLICATED:
                per_layer[n].append(flat[off:off + sz])
                off += sz
        for n, _ in REPLICATED:
            outs[kind + n] = jnp.stack(per_layer[n])
        outs[kind + "c_ctx"] = flat[off:off + D]
        if kind == "grad_":
            loss = flat[off + D]

    names = ["c_ctx", "w_ada", "b_ada", "w_in", "b_in", "q_gain", "k_gain", "conv_dw_w", "conv_dw_b", "conv_ln_g", "conv_ln_b",
             "conv_pw_w", "conv_pw_b", "pool_w", "pool_scale", "w_out", "b_out", "ln1_g", "ln1_b", "ln2_g", "ln2_b", "w_up",
             "ffn_dw_w", "ffn_dw_b", "w_down"]
    return (loss, grad_x, *[outs[k + n] for k in ("grad_", "delta_", "new_m_", "new_v_") for n in names])
```

```python
import functools

import jax
import jax.numpy as jnp
import numpy as np
from jax import lax
from jax.experimental import pallas as pl
from jax.experimental.pallas import tpu as pltpu

F32 = jnp.float32
MXU = jnp.bfloat16
WIRE = jnp.bfloat16

D = 1024
HD = 128
NH = 8
NKV = 2
QG = NH // NKV
KV_W = NKV * HD
QKV_W = NH * HD + 2 * KV_W
CONV_CH = D
POOL_CH = D
POOL_WINDOWS = (2, 4, 8, 16)
POOL_GCH = POOL_CH // len(POOL_WINDOWS)
N_GATE = 3 * D
D_IN = QKV_W + 2 * CONV_CH + POOL_CH + N_GATE
D_FF = 2816
N_MOD = 6
DEPTH = 4
CONV_K = 31
FFN_K = 3
GRID_W = 64
ROPE_THETA = 10000.0
ROPE_PAIRS = HD // 4
ALPHA = (2 * DEPTH) ** 0.25
LN_EPS = 1e-5
RMS_EPS = 1e-6
ATTN_SCALE = HD ** -0.5
ADAM_LR, ADAM_B1, ADAM_B2, ADAM_EPS, ADAM_WD, ADAM_STEP = 0.001, 0.9, 0.999, 1e-08, 0.01, 10

N_DEV = 8
TM = 256
GAP = 16
LANES = 1024
VMEM_MB = 48

NN = (((1,), (0,)), ((), ()))
NT = (((1,), (1,)), ((), ()))
TN = (((0,), (0,)), ((), ()))

_pcall = pl.pallas_call


def _call(body, *, name, grid, in_specs, out_specs, out_shape, scratch=(), aliases=None, vmem=VMEM_MB):
    return _pcall(
        body, name=name, grid=grid, in_specs=in_specs, out_specs=out_specs, out_shape=out_shape,
        scratch_shapes=list(scratch), input_output_aliases=aliases or {},
        compiler_params=pltpu.CompilerParams(dimension_semantics=("arbitrary",) * len(grid), vmem_limit_bytes=vmem * 2 ** 20),
    )


def _sds(shape, dtype):
    return jax.ShapeDtypeStruct(tuple(shape), dtype)


def _pick(n, cap, mult):
    best = None
    for t in range(mult, min(n, cap) + 1, mult):
        if n % t == 0:
            best = t
    return best if best is not None else n


def _dot(a, b, dims):
    return lax.dot_general(a, b, dims, preferred_element_type=F32)


def _sigmoid(x):
    return 1.0 / (1.0 + jnp.exp(-x))


def _matmul(a, b, mode, *, name, bias=None, acc_in=None, into=None, out_dtype=F32, tm=1024, tn=1024, tk=None):
    if mode == "nn":
        (M, K), (K2, N) = a.shape, b.shape
    elif mode == "nt":
        (M, K), (N, K2) = a.shape, b.shape
    else:
        (K, M), (K2, N) = a.shape, b.shape
    assert K == K2, (a.shape, b.shape, mode)
    tm = _pick(M, tm, 16)
    tn = _pick(N, tn, 128)
    tk = K if tk is None else _pick(K, tk, 128 if mode != "tn" else 16)
    gk = K // tk
    dims = {"nn": NN, "nt": NT, "tn": TN}[mode]
    a_spec = pl.BlockSpec((tk, tm), lambda j, i, k: (k, i)) if mode == "tn" else pl.BlockSpec((tm, tk), lambda j, i, k: (i, k))
    b_spec = pl.BlockSpec((tn, tk), lambda j, i, k: (j, k)) if mode == "nt" else pl.BlockSpec((tk, tn), lambda j, i, k: (k, j))
    in_specs, args = [a_spec, b_spec], [a, b]
    if bias is not None:
        in_specs.append(pl.BlockSpec((1, tn), lambda j, i, k: (0, j)))
        args.append(bias)
    aliases = {}
    if acc_in is not None:
        aliases = {len(args): 0}
        in_specs.append(pl.BlockSpec((tm, tn), lambda j, i, k: (i, j)))
        args.append(acc_in)
    n_total, col0, prev = (N, 0, None) if into is None else into
    assert col0 % tn == 0
    jb = col0 // tn
    if prev is not None:
        aliases = {len(args): 0}
        in_specs.append(pl.BlockSpec(memory_space=pl.ANY))
        args.append(prev)
    n_in = len(args)

    def body(*refs):
        a_ref, b_ref = refs[0], refs[1]
        pos = 2
        bias_ref = acc_in_ref = None
        if bias is not None:
            bias_ref = refs[pos]
            pos += 1
        if acc_in is not None:
            acc_in_ref = refs[pos]
        pos = n_in
        o_ref = refs[pos]
        part = _dot(a_ref[...].astype(MXU), b_ref[...].astype(MXU), dims)

        def finish(acc):
            if bias_ref is not None:
                acc = acc + bias_ref[...]
            if acc_in_ref is not None:
                acc = acc + acc_in_ref[...]
            o_ref[...] = acc.astype(out_dtype)

        if gk == 1:
            finish(part)
        else:
            acc_ref = refs[pos + 1]
            k = pl.program_id(2)

            @pl.when(k == 0)
            def _():
                acc_ref[...] = part

            @pl.when(k > 0)
            def _():
                acc_ref[...] += part

            @pl.when(k == gk - 1)
            def _():
                finish(acc_ref[...])

    return _call(
        body, name=name, grid=(N // tn, M // tm, gk), in_specs=in_specs,
        out_specs=pl.BlockSpec((tm, tn), lambda j, i, k: (i, j + jb)), out_shape=_sds((M, n_total), out_dtype),
        scratch=[pltpu.VMEM((tm, tn), F32)] if gk > 1 else [], aliases=aliases,
    )(*args)


def _rt(w, cb=0):
    return pl.BlockSpec((TM, w), lambda i: (i, cb))


def _vec(w):
    return pl.BlockSpec((1, w), lambda i: (0, 0))


def _part(w):
    return pl.BlockSpec((1, 1, w), lambda i: (i, 0, 0))


def _mod(ref, k):
    return ref[0, :, k * D:(k + 1) * D]


def _colsum(x):
    return jnp.sum(x, axis=0, keepdims=True)


def _ln_stats(s):
    mu = jnp.mean(s, axis=1, keepdims=True)
    cen = s - mu
    var = jnp.mean(cen * cen, axis=1, keepdims=True)
    rstd = lax.rsqrt(var + LN_EPS)
    return cen * rstd, rstd


def _modulate_cast(x, modt, k_shift):
    T = x.shape[0]
    nt = T // TM

    def body(x_ref, mod_ref, h_ref):
        h_ref[...] = (x_ref[...] * (1.0 + _mod(mod_ref, k_shift + 1)) + _mod(mod_ref, k_shift)).astype(MXU)

    return _call(body, name="modulate", grid=(nt,), in_specs=[_rt(D), _part(N_MOD * D)], out_specs=_rt(D),
                 out_shape=_sds((T, D), MXU))(x, modt)


def _resid_ln(x, br, modt, k_gate, g, b, mod_next=None, k_shift_next=0):
    T = x.shape[0]
    nt = T // TM
    with_h = mod_next is not None

    def body(*refs):
        x_ref, br_ref, mod_ref, g_ref, b_ref = refs[:5]
        s = ALPHA * x_ref[...] + _mod(mod_ref, k_gate) * br_ref[...]
        xhat, _ = _ln_stats(s)
        y = xhat * g_ref[...] + b_ref[...]
        if with_h:
            modn_ref, y_ref, h_ref = refs[5:]
            y_ref[...] = y
            h_ref[...] = (y * (1.0 + _mod(modn_ref, k_shift_next + 1)) + _mod(modn_ref, k_shift_next)).astype(MXU)
        else:
            refs[5][...] = y

    in_specs = [_rt(D), _rt(D), _part(N_MOD * D), _vec(D), _vec(D)]
    args = [x, br, modt, g, b]
    if with_h:
        in_specs.append(_part(N_MOD * D))
        args.append(mod_next)
        return _call(body, name="resid_ln_mod", grid=(nt,), in_specs=in_specs, out_specs=[_rt(D), _rt(D)],
                     out_shape=[_sds((T, D), F32), _sds((T, D), MXU)])(*args)
    return _call(body, name="resid_ln", grid=(nt,), in_specs=in_specs, out_specs=_rt(D), out_shape=_sds((T, D), F32))(*args)


def _ln_bwd(dy_part, x, br, modt, k_gate, g, dh=None, y=None, mod_next=None, k_shift_next=0):
    T = x.shape[0]
    nt = T // TM
    with_h = dh is not None

    def body(*refs):
        if with_h:
            dyp_ref, x_ref, br_ref, mod_ref, g_ref, dh_ref, y_ref, modn_ref = refs[:8]
            outs = refs[8:]
        else:
            dyp_ref, x_ref, br_ref, mod_ref, g_ref = refs[:5]
            outs = refs[5:]
        dx_ref, dbr_ref, dgate_ref, dlg_ref, dlb_ref, dbsum_ref = outs[:6]
        i = pl.program_id(0)

        @pl.when(i == 0)
        def _():
            dlg_ref[...] = jnp.zeros_like(dlg_ref)
            dlb_ref[...] = jnp.zeros_like(dlb_ref)
            dbsum_ref[...] = jnp.zeros_like(dbsum_ref)

        dy = dyp_ref[...]
        if with_h:
            dshift_ref, dscale_ref = outs[6:]
            dhv = dh_ref[...]
            dy = dy + dhv * (1.0 + _mod(modn_ref, k_shift_next + 1))
            dshift_ref[0] = _colsum(dhv)
            dscale_ref[0] = _colsum(dhv * y_ref[...])
        gate = _mod(mod_ref, k_gate)
        brv = br_ref[...]
        s = ALPHA * x_ref[...] + gate * brv
        xhat, rstd = _ln_stats(s)
        dlg_ref[...] += _colsum(dy * xhat)
        dlb_ref[...] += _colsum(dy)
        dyg = dy * g_ref[...]
        m1 = jnp.mean(dyg, axis=1, keepdims=True)
        m2 = jnp.mean(dyg * xhat, axis=1, keepdims=True)
        ds = rstd * (dyg - m1 - xhat * m2)
        dx_ref[...] = ALPHA * ds
        dbr = gate * ds
        dbr_ref[...] = dbr.astype(MXU)
        dbsum_ref[...] += _colsum(dbr)
        dgate_ref[0] = _colsum(ds * brv)

    in_specs = [_rt(D), _rt(D), _rt(D), _part(N_MOD * D), _vec(D)]
    args = [dy_part, x, br, modt, g]
    out_specs = [_rt(D), _rt(D), _part(D), _vec(D), _vec(D), _vec(D)]
    out_shape = [_sds((T, D), F32), _sds((T, D), MXU), _sds((nt, 1, D), F32), _sds((1, D), F32), _sds((1, D), F32), _sds((1, D), F32)]
    if with_h:
        in_specs += [_rt(D), _rt(D), _part(N_MOD * D)]
        args += [dh, y, mod_next]
        out_specs += [_part(D), _part(D)]
        out_shape += [_sds((nt, 1, D), F32), _sds((nt, 1, D), F32)]
    return _call(body, name="ln_bwd_mod" if with_h else "ln_bwd", grid=(nt,), in_specs=in_specs, out_specs=out_specs,
                 out_shape=out_shape)(*args)


def _mod_bwd(dx_part, dh, x, modt, k_shift):
    T = x.shape[0]
    nt = T // TM

    def body(dxp_ref, dh_ref, x_ref, mod_ref, dx_ref, dshift_ref, dscale_ref):
        dhv = dh_ref[...]
        dx_ref[...] = dxp_ref[...] + dhv * (1.0 + _mod(mod_ref, k_shift + 1))
        dshift_ref[0] = _colsum(dhv)
        dscale_ref[0] = _colsum(dhv * x_ref[...])

    return _call(body, name="mod_bwd", grid=(nt,), in_specs=[_rt(D), _rt(D), _rt(D), _part(N_MOD * D)],
                 out_specs=[_rt(D), _part(D), _part(D)],
                 out_shape=[_sds((T, D), F32), _sds((nt, 1, D), F32), _sds((nt, 1, D), F32)])(dx_part, dh, x, modt)


def _loss_grad(y, target, tpe, ncq):
    T = y.shape[0]
    nt = T // TM
    nl = tpe - ncq

    def body(y_ref, t_ref, dy_ref, loss_ref):
        i = pl.program_id(0)

        @pl.when(i == 0)
        def _():
            loss_ref[...] = jnp.zeros_like(loss_ref)

        @pl.when(i % tpe < ncq)
        def _():
            dy_ref[...] = jnp.zeros_like(dy_ref)

        @pl.when(i % tpe >= ncq)
        def _():
            err = y_ref[...] - t_ref[...]
            dy_ref[...] = err * (1.0 / D)
            loss_ref[...] += (0.5 / D) * jnp.sum(_colsum(err * err), axis=1, keepdims=True)

    tgt_spec = pl.BlockSpec((TM, D), lambda i: ((i // tpe) * nl + jnp.maximum(i % tpe - ncq, 0), 0))
    return _call(body, name="loss_grad", grid=(nt,), in_specs=[_rt(D), tgt_spec], out_specs=[_rt(D), _vec(128)],
                 out_shape=[_sds((T, D), F32), _sds((1, 128), F32)])(y, target)


def _rope_partner(x):
    lane = lax.broadcasted_iota(jnp.int32, x.shape, 1)
    first = (lane % (2 * ROPE_PAIRS)) < ROPE_PAIRS
    return jnp.where(first, pltpu.roll(x, HD - ROPE_PAIRS, 1), pltpu.roll(x, ROPE_PAIRS, 1))


def _qk_prep(z_qkv, cos_t, sin_t, q_gain, k_gain, tpe):
    T = z_qkv.shape[0]
    nt = T // TM

    def body(z_ref, cos_ref, sin_ref, qg_ref, kg_ref, o_ref):
        cos, sin = cos_ref[...], sin_ref[...]
        for h in range(NH + NKV):
            sl = slice(h * HD, (h + 1) * HD)
            t = z_ref[:, sl]
            gain = qg_ref[...] if h < NH else kg_ref[...]
            n = t * lax.rsqrt(jnp.mean(t * t, axis=1, keepdims=True) + RMS_EPS) * gain
            o_ref[:, sl] = (n * cos + _rope_partner(n) * sin).astype(MXU)
        o_ref[:, (NH + NKV) * HD:] = z_ref[:, (NH + NKV) * HD:].astype(MXU)

    tab = pl.BlockSpec((TM, HD), lambda i: (i % tpe, 0))
    return _call(body, name="qk_prep", grid=(nt,), in_specs=[_rt(QKV_W), tab, tab, _vec(HD), _vec(HD)], out_specs=_rt(QKV_W),
                 out_shape=_sds((T, QKV_W), MXU))(z_qkv, cos_t, sin_t, q_gain, k_gain)


def _qk_bwd(dq, dk, dv, z_qkv, cos_t, sin_t, q_gain, k_gain, tpe):
    T = z_qkv.shape[0]
    nt = T // TM

    def body(dq_ref, dk_ref, dv_ref, z_ref, cos_ref, sin_ref, qg_ref, kg_ref, dz_ref, dqg_ref, dkg_ref, bsum_ref):
        i = pl.program_id(0)

        @pl.when(i == 0)
        def _():
            dqg_ref[...] = jnp.zeros_like(dqg_ref)
            dkg_ref[...] = jnp.zeros_like(dkg_ref)
            bsum_ref[...] = jnp.zeros_like(bsum_ref)

        cos, sin = cos_ref[...], sin_ref[...]
        for h in range(NH + NKV):
            sl = slice(h * HD, (h + 1) * HD)
            dr = dq_ref[:, sl] if h < NH else dk_ref[:, (h - NH) * HD:(h - NH + 1) * HD]
            gain = qg_ref[...] if h < NH else kg_ref[...]
            dn = dr * cos + _rope_partner(dr * sin)
            t = z_ref[:, sl]
            rstd = lax.rsqrt(jnp.mean(t * t, axis=1, keepdims=True) + RMS_EPS)
            that = t * rstd
            dgain = _colsum(dn * that)
            if h < NH:
                dqg_ref[...] += dgain
            else:
                dkg_ref[...] += dgain
            dthat = dn * gain
            dt = rstd * (dthat - that * jnp.mean(dthat * that, axis=1, keepdims=True))
            dz_ref[:, sl] = dt.astype(MXU)
            bsum_ref[:, sl] += _colsum(dt)
        dvv = dv_ref[...]
        dz_ref[:, (NH + NKV) * HD:] = dvv.astype(MXU)
        bsum_ref[:, (NH + NKV) * HD:] += _colsum(dvv)

    tab = pl.BlockSpec((TM, HD), lambda i: (i % tpe, 0))
    return _call(body, name="qk_bwd", grid=(nt,),
                 in_specs=[_rt(NH * HD), _rt(KV_W), _rt(KV_W), _rt(QKV_W), tab, tab, _vec(HD), _vec(HD)],
                 out_specs=[_rt(QKV_W), _vec(HD), _vec(HD), _vec(QKV_W)],
                 out_shape=[_sds((T, QKV_W), MXU), _sds((1, HD), F32), _sds((1, HD), F32), _sds((1, QKV_W), F32)],
                 )(dq, dk, dv, z_qkv, cos_t, sin_t, q_gain, k_gain)


def _ln_silu(hc, g, b):
    T = hc.shape[0]

    def body(h_ref, g_ref, b_ref, o_ref):
        xhat, _ = _ln_stats(h_ref[...])
        n = xhat * g_ref[...] + b_ref[...]
        o_ref[...] = (n * _sigmoid(n)).astype(MXU)

    return _call(body, name="ln_silu", grid=(T // TM,), in_specs=[_rt(D), _vec(D), _vec(D)], out_specs=_rt(D),
                 out_shape=_sds((T, D), MXU))(hc, g, b)


def _ln_silu_bwd(dsw, hc, g, b):
    T = hc.shape[0]

    def body(d_ref, h_ref, g_ref, b_ref, dh_ref, dg_ref, db_ref, dcb_ref):
        i = pl.program_id(0)

        @pl.when(i == 0)
        def _():
            dg_ref[...] = jnp.zeros_like(dg_ref)
            db_ref[...] = jnp.zeros_like(db_ref)
            dcb_ref[...] = jnp.zeros_like(dcb_ref)

        xhat, rstd = _ln_stats(h_ref[...])
        n = xhat * g_ref[...] + b_ref[...]
        sg = _sigmoid(n)
        dn = d_ref[...] * (sg * (1.0 + n * (1.0 - sg)))
        dg_ref[...] += _colsum(dn * xhat)
        db_ref[...] += _colsum(dn)
        dng = dn * g_ref[...]
        m1 = jnp.mean(dng, axis=1, keepdims=True)
        m2 = jnp.mean(dng * xhat, axis=1, keepdims=True)
        dh = rstd * (dng - m1 - xhat * m2)
        dh_ref[...] = dh
        dcb_ref[...] += _colsum(dh)

    return _call(body, name="ln_silu_bwd", grid=(T // TM,), in_specs=[_rt(D), _rt(D), _vec(D), _vec(D)],
                 out_specs=[_rt(D), _vec(D), _vec(D), _vec(D)],
                 out_shape=[_sds((T, D), F32)] + [_sds((1, D), F32)] * 3)(dsw, hc, g, b)


def _merge(attn, conv_o, pool_o, z_gate):
    T = attn.shape[0]

    def body(a_ref, c_ref, p_ref, zg_ref, m_ref):
        m = (_sigmoid(zg_ref[:, 0:D]) * a_ref[...] + _sigmoid(zg_ref[:, D:2 * D]) * c_ref[...]
             + _sigmoid(zg_ref[:, 2 * D:3 * D]) * p_ref[...])
        m_ref[...] = m.astype(MXU)

    return _call(body, name="merge", grid=(T // TM,), in_specs=[_rt(D), _rt(D), _rt(D), _rt(N_GATE)], out_specs=_rt(D),
                 out_shape=_sds((T, D), MXU))(attn, conv_o, pool_o, z_gate)


def _merge_bwd(dm, attn, conv_o, pool_o, z_gate):
    T = attn.shape[0]

    def body(dm_ref, a_ref, c_ref, p_ref, zg_ref, da_ref, dc_ref, dp_ref, dzg_ref, dcsum_ref, gsum_ref):
        i = pl.program_id(0)

        @pl.when(i == 0)
        def _():
            dcsum_ref[...] = jnp.zeros_like(dcsum_ref)
            gsum_ref[...] = jnp.zeros_like(gsum_ref)

        dmv = dm_ref[...]
        for k, (br_ref, out_ref) in enumerate(((a_ref, da_ref), (c_ref, dc_ref), (p_ref, dp_ref))):
            gk = _sigmoid(zg_ref[:, k * D:(k + 1) * D])
            dbr = dmv * gk
            out_ref[...] = dbr.astype(out_ref.dtype)
            if k == 1:
                dcsum_ref[...] += _colsum(dbr)
            dzg = dmv * br_ref[...] * gk * (1.0 - gk)
            dzg_ref[:, k * D:(k + 1) * D] = dzg.astype(MXU)
            gsum_ref[:, k * D:(k + 1) * D] += _colsum(dzg)

    return _call(body, name="merge_bwd", grid=(T // TM,), in_specs=[_rt(D), _rt(D), _rt(D), _rt(D), _rt(N_GATE)],
                 out_specs=[_rt(D), _rt(D), _rt(D), _rt(N_GATE), _vec(D), _vec(N_GATE)],
                 out_shape=[_sds((T, D), F32), _sds((T, D), MXU), _sds((T, D), F32), _sds((T, N_GATE), MXU),
                            _sds((1, D), F32), _sds((1, N_GATE), F32)])(dm, attn, conv_o, pool_o, z_gate)


def _softmax_rows(q, k):
    s = _dot(q, k, NT) * ATTN_SCALE
    p = jnp.exp(s - jnp.max(s, axis=1, keepdims=True))
    return p * (1.0 / jnp.sum(p, axis=1, keepdims=True))


def _attn_specs(nq):
    q_spec = pl.BlockSpec((TM, QG * HD), lambda b, h, q: (b * nq + q, h))
    k_spec = pl.BlockSpec((nq * TM, HD), lambda b, h, q: (b, NH + h))
    v_spec = pl.BlockSpec((nq * TM, HD), lambda b, h, q: (b, NH + NKV + h))
    return q_spec, k_spec, v_spec


def _attn_fwd(qkv, B, C, R):
    nq, ncq = R // TM, C // TM

    def body(q_ref, k_ref, v_ref, o_ref):
        def attend(L):
            k, v = k_ref[0:L, :], v_ref[0:L, :]
            for i in range(QG):
                sl = slice(i * HD, (i + 1) * HD)
                p = _softmax_rows(q_ref[:, sl], k)
                o_ref[:, sl] = _dot(p.astype(MXU), v, NN)

        qi = pl.program_id(2)
        pl.when(qi < ncq)(functools.partial(attend, C))
        pl.when(qi >= ncq)(functools.partial(attend, R))

    q_spec, k_spec, v_spec = _attn_specs(nq)
    return _call(body, name="attn_fwd", grid=(B, NKV, nq), in_specs=[q_spec, k_spec, v_spec], out_specs=q_spec,
                 out_shape=_sds((B * R, NH * HD), F32))(qkv, qkv, qkv)


def _attn_bwd(qkv, o, do, B, C, R):
    nq, ncq = R // TM, C // TM

    def body(q_ref, k_ref, v_ref, o_ref, do_ref, dq_ref, dk_ref, dv_ref):
        qi = pl.program_id(2)

        @pl.when(qi == 0)
        def _():
            dk_ref[...] = jnp.zeros_like(dk_ref)
            dv_ref[...] = jnp.zeros_like(dv_ref)

        def bwd(L):
            k, v = k_ref[0:L, :], v_ref[0:L, :]
            for i in range(QG):
                sl = slice(i * HD, (i + 1) * HD)
                q = q_ref[:, sl]
                p = _softmax_rows(q, k)
                dov = do_ref[:, sl]
                dob = dov.astype(MXU)
                dp = _dot(dob, v, NT)
                dl = jnp.sum(dov * o_ref[:, sl], axis=1, keepdims=True)
                ds = (p * (dp - dl) * ATTN_SCALE).astype(MXU)
                dq_ref[:, sl] = _dot(ds, k, NN)
                dk_ref[0:L, :] += _dot(ds, q, TN)
                dv_ref[0:L, :] += _dot(p.astype(MXU), dob, TN)

        pl.when(qi < ncq)(functools.partial(bwd, C))
        pl.when(qi >= ncq)(functools.partial(bwd, R))

    q_spec, k_spec, v_spec = _attn_specs(nq)
    kv_out = pl.BlockSpec((R, HD), lambda b, h, q: (b, h))
    return _call(body, name="attn_bwd", grid=(B, NKV, nq), in_specs=[q_spec, k_spec, v_spec, q_spec, q_spec],
                 out_specs=[q_spec, kv_out, kv_out],
                 out_shape=[_sds((B * R, NH * HD), F32), _sds((B * R, KV_W), F32), _sds((B * R, KV_W), F32)],
                 )(qkv, qkv, qkv, o, do)


def _segments(C, S):
    return ((0, GAP, C), (C, 2 * GAP + C, S))


def _padded_rows(C, S):
    return 3 * GAP + C + S


def _zero_gaps(pad_ref, C, S):
    for off in (0, GAP + C, 2 * GAP + C + S):
        pad_ref[off:off + GAP, :] = jnp.zeros((GAP, pad_ref.shape[1]), pad_ref.dtype)


def _chunks(n, ch, fn):
    def step(i, carry):
        fn(pl.multiple_of(i * ch, ch))
        return carry

    lax.fori_loop(0, n // ch, step, 0)


def _window(pad_ref, row, ch):
    return pad_ref[pl.ds(row - GAP, ch + 2 * GAP), :]


def _shifted(win, off, ch):
    return win[GAP + off:GAP + off + ch, :]


def _taps(pad_ref, w, row, ch, n_taps, flip=False):
    half = (n_taps - 1) // 2
    win = _window(pad_ref, row, ch)
    acc = None
    for k in range(n_taps):
        term = w[k:k + 1, :] * _shifted(win, (half - k) if flip else (k - half), ch)
        acc = term if acc is None else acc + term
    return acc


def _tap_grads(dw_ref, d, pad_ref, row, ch, n_taps):
    half = (n_taps - 1) // 2
    win = _window(pad_ref, row, ch)
    for k in range(n_taps):
        prod = d * _shifted(win, k - half, ch)
        dw_ref[k] += jnp.sum(prod.reshape(ch // 8, 8, prod.shape[1]), axis=0)


def _conv_fwd(z_conv, w, bias, B, C, S, cw=256, ch=64):
    R = C + S
    nj = CONV_CH // cw
    segs = _segments(C, S)

    def body(a_ref, g_ref, w_ref, b_ref, o_ref, pad):
        _zero_gaps(pad, C, S)
        wv = w_ref[...]
        for so, po, n in segs:
            def fill(r, so=so, po=po):
                pad[pl.ds(po + r, ch), :] = a_ref[pl.ds(so + r, ch), :] * _sigmoid(g_ref[pl.ds(so + r, ch), :])

            _chunks(n, ch, fill)
        for so, po, n in segs:
            def conv(r, so=so, po=po):
                o_ref[pl.ds(so + r, ch), :] = _taps(pad, wv, po + r, ch, CONV_K) + b_ref[...]

            _chunks(n, ch, conv)

    return _call(
        body, name="conv_fwd", grid=(nj, B),
        in_specs=[pl.BlockSpec((R, cw), lambda j, b: (b, j)), pl.BlockSpec((R, cw), lambda j, b: (b, nj + j)),
                  pl.BlockSpec((32, cw), lambda j, b: (0, j)), pl.BlockSpec((1, cw), lambda j, b: (0, j))],
        out_specs=pl.BlockSpec((R, cw), lambda j, b: (b, j)), out_shape=_sds((B * R, CONV_CH), F32),
        scratch=[pltpu.VMEM((_padded_rows(C, S), cw), F32)])(z_conv, z_conv, w, bias)


def _conv_bwd(dhc, z_conv, w, B, C, S, cw=256, ch=64):
    R = C + S
    nj = CONV_CH // cw
    segs = _segments(C, S)

    def body(d_ref, a_ref, g_ref, w_ref, da_ref, dg_ref, dw_ref, sa_ref, sg_ref, gpad, dpad, dwacc):
        b = pl.program_id(1)

        @pl.when(b == 0)
        def _():
            dw_ref[...] = jnp.zeros_like(dw_ref)
            sa_ref[...] = jnp.zeros_like(sa_ref)
            sg_ref[...] = jnp.zeros_like(sg_ref)

        _zero_gaps(gpad, C, S)
        _zero_gaps(dpad, C, S)
        dwacc[...] = jnp.zeros_like(dwacc)
        wv = w_ref[...]
        for so, po, n in segs:
            def fill(r, so=so, po=po):
                gpad[pl.ds(po + r, ch), :] = a_ref[pl.ds(so + r, ch), :] * _sigmoid(g_ref[pl.ds(so + r, ch), :])
                dpad[pl.ds(po + r, ch), :] = d_ref[pl.ds(so + r, ch), :]

            _chunks(n, ch, fill)
        for so, po, n in segs:
            def step(r, so=so, po=po):
                _tap_grads(dwacc, dpad[pl.ds(po + r, ch), :], gpad, po + r, ch, CONV_K)
                dglu = _taps(dpad, wv, po + r, ch, CONV_K, flip=True)
                av = a_ref[pl.ds(so + r, ch), :]
                sg = _sigmoid(g_ref[pl.ds(so + r, ch), :])
                da = dglu * sg
                dg = dglu * av * sg * (1.0 - sg)
                da_ref[pl.ds(so + r, ch), :] = da.astype(MXU)
                dg_ref[pl.ds(so + r, ch), :] = dg.astype(MXU)
                sa_ref[...] += _colsum(da)
                sg_ref[...] += _colsum(dg)

            _chunks(n, ch, step)
        for k in range(CONV_K):
            dw_ref[k:k + 1, :] += _colsum(dwacc[k])

    blk = pl.BlockSpec((R, cw), lambda j, b: (b, j))
    acc1 = pl.BlockSpec((1, cw), lambda j, b: (0, j))
    return _call(
        body, name="conv_bwd", grid=(nj, B),
        in_specs=[blk, blk, pl.BlockSpec((R, cw), lambda j, b: (b, nj + j)), pl.BlockSpec((32, cw), lambda j, b: (0, j))],
        out_specs=[blk, blk, pl.BlockSpec((32, cw), lambda j, b: (0, j)), acc1, acc1],
        out_shape=[_sds((B * R, CONV_CH), MXU), _sds((B * R, CONV_CH), MXU), _sds((32, CONV_CH), F32),
                   _sds((1, CONV_CH), F32), _sds((1, CONV_CH), F32)],
        scratch=[pltpu.VMEM((_padded_rows(C, S), cw), F32), pltpu.VMEM((_padded_rows(C, S), cw), F32),
                 pltpu.VMEM((32, 8, cw), F32)])(dhc, z_conv, z_conv, w)


def _ffn_mid(up, w, bias, B, C, S, cw=256, ch=64):
    R = C + S
    nj = D_FF // cw
    segs = _segments(C, S)

    def body(a_ref, u_ref, w_ref, b_ref, f_ref, pad):
        _zero_gaps(pad, C, S)
        wv = w_ref[...]
        for so, po, n in segs:
            def fill(r, so=so, po=po):
                pad[pl.ds(po + r, ch), :] = a_ref[pl.ds(so + r, ch), :]

            _chunks(n, ch, fill)
        for so, po, n in segs:
            def conv(r, so=so, po=po):
                ac = _taps(pad, wv, po + r, ch, FFN_K) + b_ref[...]
                f_ref[pl.ds(so + r, ch), :] = (ac * _sigmoid(ac) * u_ref[pl.ds(so + r, ch), :]).astype(MXU)

            _chunks(n, ch, conv)

    return _call(
        body, name="ffn_mid", grid=(nj, B),
        in_specs=[pl.BlockSpec((R, cw), lambda j, b: (b, j)), pl.BlockSpec((R, cw), lambda j, b: (b, nj + j)),
                  pl.BlockSpec((8, cw), lambda j, b: (0, j)), pl.BlockSpec((1, cw), lambda j, b: (0, j))],
        out_specs=pl.BlockSpec((R, cw), lambda j, b: (b, j)), out_shape=_sds((B * R, D_FF), MXU),
        scratch=[pltpu.VMEM((_padded_rows(C, S), cw), F32)])(up, up, w, bias)


def _ffn_mid_bwd(df, up, w, bias, B, C, S, cw=256, ch=64):
    R = C + S
    nj = D_FF // cw
    segs = _segments(C, S)

    def body(d_ref, a_ref, u_ref, w_ref, b_ref, da_ref, du_ref, dw_ref, db_ref, apad, dpad, dwacc):
        b = pl.program_id(1)

        @pl.when(b == 0)
        def _():
            dw_ref[...] = jnp.zeros_like(dw_ref)
            db_ref[...] = jnp.zeros_like(db_ref)

        _zero_gaps(apad, C, S)
        _zero_gaps(dpad, C, S)
        dwacc[...] = jnp.zeros_like(dwacc)
        wv = w_ref[...]
        for so, po, n in segs:
            def fill(r, so=so, po=po):
                apad[pl.ds(po + r, ch), :] = a_ref[pl.ds(so + r, ch), :]

            _chunks(n, ch, fill)
        for so, po, n in segs:
            def first(r, so=so, po=po):
                ac = _taps(apad, wv, po + r, ch, FFN_K) + b_ref[...]
                sg = _sigmoid(ac)
                dfv = d_ref[pl.ds(so + r, ch), :]
                du_ref[pl.ds(so + r, ch), :] = (dfv * ac * sg).astype(MXU)
                dac = dfv * u_ref[pl.ds(so + r, ch), :] * (sg * (1.0 + ac * (1.0 - sg)))
                dpad[pl.ds(po + r, ch), :] = dac
                db_ref[...] += _colsum(dac)

            _chunks(n, ch, first)
        for so, po, n in segs:
            def second(r, so=so, po=po):
                _tap_grads(dwacc, dpad[pl.ds(po + r, ch), :], apad, po + r, ch, FFN_K)
                da_ref[pl.ds(so + r, ch), :] = _taps(dpad, wv, po + r, ch, FFN_K, flip=True).astype(MXU)

            _chunks(n, ch, second)
        for k in range(FFN_K):
            dw_ref[k:k + 1, :] += _colsum(dwacc[k])

    blk = pl.BlockSpec((R, cw), lambda j, b: (b, j))
    return _call(
        body, name="ffn_mid_bwd", grid=(nj, B),
        in_specs=[blk, blk, pl.BlockSpec((R, cw), lambda j, b: (b, nj + j)), pl.BlockSpec((8, cw), lambda j, b: (0, j)),
                  pl.BlockSpec((1, cw), lambda j, b: (0, j))],
        out_specs=[blk, blk, pl.BlockSpec((8, cw), lambda j, b: (0, j)), pl.BlockSpec((1, cw), lambda j, b: (0, j))],
        out_shape=[_sds((B * R, D_FF), MXU), _sds((B * R, D_FF), MXU), _sds((8, D_FF), F32), _sds((1, D_FF), F32)],
        scratch=[pltpu.VMEM((_padded_rows(C, S), cw), F32), pltpu.VMEM((_padded_rows(C, S), cw), F32),
                 pltpu.VMEM((8, 8, cw), F32)])(df, up, up, w, bias)


def _window_count(r, ch, n, w):
    t = r + lax.broadcasted_iota(jnp.int32, (ch, 1), 0)
    return (jnp.minimum(t + w // 2, n) - jnp.maximum(t - w // 2, 0)).astype(F32)


def _pool_fwd(z_pool, pool_w, pool_scale, B, C, S, ch=128):
    R = C + S
    gch = POOL_GCH
    segs = _segments(C, S)

    def body(u_ref, pw_ref, sc_ref, pooled_ref, po_ref, pad):
        g = pl.program_id(1)
        _zero_gaps(pad, C, S)
        for so, po, n in segs:
            def fill(r, so=so, po=po):
                pad[pl.ds(po + r, ch), :] = u_ref[pl.ds(so + r, ch), :]

            _chunks(n, ch, fill)
        for gi, w in enumerate(POOL_WINDOWS):
            @pl.when(g == gi)
            def _(w=w):
                for so, po, n in segs:
                    def step(r, so=so, po=po, n=n):
                        win = _window(pad, po + r, ch)
                        acc = _shifted(win, -(w // 2), ch)
                        for o in range(1 - w // 2, w // 2):
                            acc = acc + _shifted(win, o, ch)
                        pooled = (acc / _window_count(r, ch, n, w) - _shifted(win, 0, ch)).astype(MXU)
                        pooled_ref[pl.ds(so + r, ch), :] = pooled
                        po_ref[pl.ds(so + r, ch), :] = _dot(pooled, pw_ref[0], NN) * sc_ref[...]

                    _chunks(n, ch, step)

    blk = pl.BlockSpec((R, gch), lambda b, g: (b, g))
    return _call(
        body, name="pool_fwd", grid=(B, len(POOL_WINDOWS)),
        in_specs=[blk, pl.BlockSpec((1, gch, gch), lambda b, g: (g, 0, 0)), pl.BlockSpec((1, gch), lambda b, g: (0, g))],
        out_specs=[blk, blk], out_shape=[_sds((B * R, POOL_CH), MXU), _sds((B * R, POOL_CH), F32)],
        scratch=[pltpu.VMEM((_padded_rows(C, S), gch), F32)])(z_pool, pool_w, pool_scale)


def _pool_bwd(dpo, pooled, pool_w, pool_scale, B, C, S, ch=128):
    R = C + S
    gch = POOL_GCH
    segs = _segments(C, S)

    def body(d_ref, p_ref, pw_ref, sc_ref, du_ref, dpw_ref, dsc_ref, su_ref, qpad, dpl):
        g, b = pl.program_id(0), pl.program_id(1)

        @pl.when(b == 0)
        def _():
            dpw_ref[...] = jnp.zeros_like(dpw_ref)
            dsc_ref[...] = jnp.zeros_like(dsc_ref)
            su_ref[...] = jnp.zeros_like(su_ref)

        _zero_gaps(qpad, C, S)
        pw = pw_ref[0]
        for gi, w in enumerate(POOL_WINDOWS):
            @pl.when(g == gi)
            def _(w=w):
                for so, po, n in segs:
                    def first(r, so=so, po=po, n=n):
                        pv = p_ref[pl.ds(so + r, ch), :]
                        dv = d_ref[pl.ds(so + r, ch), :]
                        dsc_ref[...] += _colsum(dv * _dot(pv, pw, NN))
                        dmx = (dv * sc_ref[...]).astype(MXU)
                        dpw_ref[0] += _dot(pv, dmx, TN)
                        dp = _dot(dmx, pw, NT)
                        dpl[pl.ds(so + r, ch), :] = dp
                        qpad[pl.ds(po + r, ch), :] = dp / _window_count(r, ch, n, w)

                    _chunks(n, ch, first)
                for so, po, n in segs:
                    def second(r, so=so, po=po):
                        win = _window(qpad, po + r, ch)
                        acc = _shifted(win, 1 - w // 2, ch)
                        for o in range(2 - w // 2, w // 2 + 1):
                            acc = acc + _shifted(win, o, ch)
                        du = acc - dpl[pl.ds(so + r, ch), :]
                        du_ref[pl.ds(so + r, ch), :] = du.astype(MXU)
                        su_ref[...] += _colsum(du)

                    _chunks(n, ch, second)

    blk = pl.BlockSpec((R, gch), lambda g, b: (b, g))
    vec = pl.BlockSpec((1, gch), lambda g, b: (0, g))
    wblk = pl.BlockSpec((1, gch, gch), lambda g, b: (g, 0, 0))
    return _call(
        body, name="pool_bwd", grid=(len(POOL_WINDOWS), B), in_specs=[blk, blk, wblk, vec], out_specs=[blk, wblk, vec, vec],
        out_shape=[_sds((B * R, POOL_CH), MXU), _sds((len(POOL_WINDOWS), gch, gch), F32), _sds((1, POOL_CH), F32),
                   _sds((1, POOL_CH), F32)],
        scratch=[pltpu.VMEM((_padded_rows(C, S), gch), F32), pltpu.VMEM((R, gch), F32)])(dpo, pooled, pool_w, pool_scale)


def _silu_rows(cond):
    def body(c_ref, s_ref, d_ref):
        c = c_ref[...]
        sg = _sigmoid(c)
        s_ref[...] = (c * sg).astype(MXU)
        d_ref[...] = sg * (1.0 + c * (1.0 - sg))

    full = pl.BlockSpec(cond.shape, lambda i: (0, 0))
    return _call(body, name="silu_rows", grid=(1,), in_specs=[full], out_specs=[full, full],
                 out_shape=[_sds(cond.shape, MXU), _sds(cond.shape, F32)])(cond)


def _row_tile(rows, cols, n_bufs):
    cap = max(16, (16 * 2 ** 20) // (4 * n_bufs * max(cols, 128)))
    return rows if rows <= cap else _pick(rows, cap, 16)


def _adamw(parts, w, m, v, layer, prev, *, name):
    n_parts, rows, cols = parts.shape
    layers = w.shape[0]
    c1 = 1.0 - ADAM_B1 ** ADAM_STEP
    c2 = 1.0 - ADAM_B2 ** ADAM_STEP
    tr = _row_tile(rows, cols, n_parts + 7)

    def body(p_ref, w_ref, m_ref, v_ref, *rest):
        g_ref, d_ref, nm_ref, nv_ref = rest[-4:]
        g = p_ref[0].astype(F32)
        for k in range(1, n_parts):
            g = g + p_ref[k].astype(F32)
        nm = ADAM_B1 * m_ref[...] + (1.0 - ADAM_B1) * g
        nv = ADAM_B2 * v_ref[...] + (1.0 - ADAM_B2) * (g * g)
        g_ref[...] = g
        nm_ref[...] = nm
        nv_ref[...] = nv
        d_ref[...] = -ADAM_LR * ((nm / c1) / (jnp.sqrt(nv / c2) + ADAM_EPS) + ADAM_WD * w_ref[...])

    blk = pl.BlockSpec((None, tr, cols), lambda i: (layer, i, 0))
    in_specs = [pl.BlockSpec((n_parts, tr, cols), lambda i: (0, i, 0)), blk, blk, blk]
    args = [parts, w, m, v]
    aliases = {}
    if prev is not None:
        in_specs += [ANY] * 4
        aliases = {4 + k: k for k in range(4)}
        args += list(prev)
    return _call(body, name=name, grid=(rows // tr,), in_specs=in_specs, out_specs=[blk] * 4,
                 out_shape=[_sds((layers, rows, cols), F32)] * 4, aliases=aliases)(*args)


def _pair_sum(g, r1):
    _, rows, cols = g.shape
    c = lax.axis_index("c")
    g4 = g.reshape(4, 2, rows, cols)
    tr = _row_tile(rows, cols, 10)

    def body(c_ref, g_ref, r_ref, o_ref):
        o_ref[...] = (g_ref[...] + r_ref[...]).astype(WIRE)

    return _pcall(
        body, name="pair_sum", out_shape=_sds((4, rows, cols), WIRE),
        grid_spec=pltpu.PrefetchScalarGridSpec(
            num_scalar_prefetch=1, grid=(rows // tr,),
            in_specs=[pl.BlockSpec((4, None, tr, cols), lambda i, c_ref: (0, c_ref[0], i, 0)),
                      pl.BlockSpec((4, tr, cols), lambda i, c_ref: (0, i, 0))],
            out_specs=pl.BlockSpec((4, tr, cols), lambda i, c_ref: (0, i, 0))),
        compiler_params=pltpu.CompilerParams(dimension_semantics=("arbitrary",), vmem_limit_bytes=VMEM_MB * 2 ** 20),
    )(jnp.reshape(c, (1,)).astype(jnp.int32), g4, r1)


MESH = pl.DeviceIdType.MESH
ANY = pl.BlockSpec(memory_space=pl.ANY)


def _all_gather(shards, *, name, layer=None):
    n_t = len(shards)
    blks = [s.shape if layer is None else s.shape[1:] for s in shards]

    def body(*refs):
        x_refs, out_refs = refs[:n_t], refs[n_t:2 * n_t]
        send_sems, recv_sems, local_sems = refs[2 * n_t:]
        x, y, c = lax.axis_index("x"), lax.axis_index("y"), lax.axis_index("c")
        me, sibling = (x, y, c), (x, y, 1 - c)
        chips = [(1 - x, y), (x, 1 - y), (1 - x, 1 - y)]

        def copy(n, k, blk, to, own=False):
            dst = out_refs[n].at[4 * blk[0] + 2 * blk[1] + blk[2]]
            src = dst if not own else (x_refs[n] if layer is None else x_refs[n].at[layer])
            return pltpu.make_async_remote_copy(src_ref=src, dst_ref=dst, send_sem=send_sems.at[n * 7 + k],
                                                recv_sem=recv_sems.at[n * 7 + k], device_id=to, device_id_type=MESH)

        mine = [pltpu.make_async_copy(x_refs[n] if layer is None else x_refs[n].at[layer], out_refs[n].at[4 * x + 2 * y + c],
                                      local_sems.at[n]) for n in range(n_t)]
        for cp in mine:
            cp.start()
        first = [copy(n, 1 + j, me, (*chip, c), own=True) for j, chip in enumerate(chips) for n in range(n_t)]
        first += [copy(n, 0, me, sibling, own=True) for n in range(n_t)]
        for cp in first:
            cp.start()
        passed = []
        for j, chip in enumerate(chips):
            for n in range(n_t):
                copy(n, 1 + j, (*chip, c), me).wait_recv()
                passed.append(copy(n, 4 + j, (*chip, c), sibling))
                passed[-1].start()
        for n in range(n_t):
            copy(n, 0, sibling, me).wait_recv()
        for j, chip in enumerate(chips):
            for n in range(n_t):
                copy(n, 4 + j, (*chip, 1 - c), me).wait_recv()
        for cp in first + passed:
            cp.wait_send()
        for cp in mine:
            cp.wait()

    return _pcall(
        body, name=name, out_shape=[_sds((N_DEV,) + tuple(b), s.dtype) for b, s in zip(blks, shards)],
        in_specs=[ANY] * n_t, out_specs=[ANY] * n_t,
        scratch_shapes=[pltpu.SemaphoreType.DMA((7 * n_t,)), pltpu.SemaphoreType.DMA((7 * n_t,)), pltpu.SemaphoreType.DMA((n_t,))],
    )(*shards)


def _sibling_exchange(gs):
    n_t = len(gs)

    def body(*refs):
        g_refs, r_refs = refs[:n_t], refs[n_t:2 * n_t]
        send_sems, recv_sems = refs[2 * n_t:]
        x, y, c = lax.axis_index("x"), lax.axis_index("y"), lax.axis_index("c")
        copies = [pltpu.make_async_remote_copy(
            src_ref=g_refs[n].at[:, 1 - c], dst_ref=r_refs[n], send_sem=send_sems.at[n], recv_sem=recv_sems.at[n],
            device_id=(x, y, 1 - c), device_id_type=MESH) for n in range(n_t)]
        for cp in copies:
            cp.start()
        for cp in copies:
            cp.wait_recv()
        for cp in copies:
            cp.wait_send()

    return _pcall(
        body, name="sibling_exchange", out_shape=[_sds((4,) + g.shape[1:], g.dtype) for g in gs],
        in_specs=[ANY] * n_t, out_specs=[ANY] * n_t,
        scratch_shapes=[pltpu.SemaphoreType.DMA((n_t,)), pltpu.SemaphoreType.DMA((n_t,))],
    )(*[g.reshape((4, 2) + g.shape[1:]) for g in gs])


def _chip_exchange(ps):
    n_t = len(ps)

    def body(*refs):
        p_refs, r_refs = refs[:n_t], refs[n_t:2 * n_t]
        send_sems, recv_sems, local_sems = refs[2 * n_t:]
        x, y, c = lax.axis_index("x"), lax.axis_index("y"), lax.axis_index("c")
        mine = 2 * x + y
        chips = [(1 - x, y), (x, 1 - y), (1 - x, 1 - y)]
        own = [pltpu.make_async_copy(p_refs[n].at[mine], r_refs[n].at[mine], local_sems.at[n]) for n in range(n_t)]
        for cp in own:
            cp.start()

        def copy(n, j, src_chip, dst_slot):
            px, py = chips[j]
            return pltpu.make_async_remote_copy(
                src_ref=p_refs[n].at[src_chip], dst_ref=r_refs[n].at[dst_slot], send_sem=send_sems.at[n * 3 + j],
                recv_sem=recv_sems.at[n * 3 + j], device_id=(px, py, c), device_id_type=MESH)

        sends = [copy(n, j, 2 * px + py, mine) for j, (px, py) in enumerate(chips) for n in range(n_t)]
        for cp in sends:
            cp.start()
        for j, (px, py) in enumerate(chips):
            for n in range(n_t):
                copy(n, j, mine, 2 * px + py).wait_recv()
        for cp in sends:
            cp.wait_send()
        for cp in own:
            cp.wait()

    return _pcall(
        body, name="chip_exchange", out_shape=[_sds(p.shape, p.dtype) for p in ps], in_specs=[ANY] * n_t, out_specs=[ANY] * n_t,
        scratch_shapes=[pltpu.SemaphoreType.DMA((3 * n_t,)), pltpu.SemaphoreType.DMA((3 * n_t,)), pltpu.SemaphoreType.DMA((n_t,))],
    )(*ps)


BIG = (("w_ada", (D, N_MOD * D // N_DEV), 1), ("w_in", (D, D_IN // N_DEV), 1), ("conv_pw_w", (CONV_CH // N_DEV, D), 0),
       ("pool_w", (len(POOL_WINDOWS), POOL_GCH // N_DEV, POOL_GCH), 1), ("w_out", (D // N_DEV, D), 0),
       ("w_up", (D, 2 * D_FF // N_DEV), 1), ("w_down", (D_FF // N_DEV, D), 0))
TAPS = (("conv_dw_w", (CONV_K, CONV_CH // N_DEV), 1), ("ffn_dw_w", (FFN_K, D_FF // N_DEV), 1))
SHARDED = BIG + TAPS
REPLICATED = (("b_ada", N_MOD * D), ("b_in", D_IN), ("q_gain", HD), ("k_gain", HD), ("conv_dw_b", CONV_CH), ("conv_ln_g", CONV_CH),
              ("conv_ln_b", CONV_CH), ("conv_pw_b", D), ("pool_scale", POOL_CH), ("b_out", D), ("ln1_g", D), ("ln1_b", D),
              ("ln2_g", D), ("ln2_b", D), ("ffn_dw_b", D_FF))


def _as_rows(shape):
    return (int(np.prod(shape[:-1])), shape[-1])


def _full_from_blocks(blocks, axis):
    moved = jnp.moveaxis(blocks, 0, axis)
    shape = list(moved.shape)
    shape[axis:axis + 2] = [shape[axis] * shape[axis + 1]]
    return moved.reshape(shape)


def _blocks_from_full(full, axis):
    shape = list(full.shape)
    shape[axis:axis + 1] = [N_DEV, shape[axis] // N_DEV]
    return jnp.moveaxis(full.reshape(shape), axis, 0)


SMALL_N = DEPTH * sum(n for _, n in REPLICATED) + D
SMALL_ROWS = -(-(SMALL_N + 1) // (8 * LANES)) * 8


def _rope_tables(C, S):
    t = np.arange(S)
    inv_freq = ROPE_THETA ** (-np.arange(ROPE_PAIRS, dtype=np.float32) / ROPE_PAIRS)
    row = jnp.asarray((t // GRID_W).astype(np.float32))[:, None] * jnp.asarray(inv_freq, F32)
    col = jnp.asarray((t % GRID_W).astype(np.float32))[:, None] * jnp.asarray(inv_freq, F32)
    cos = jnp.concatenate([jnp.cos(row), jnp.cos(row), jnp.cos(col), jnp.cos(col)], axis=1)
    sin = jnp.concatenate([-jnp.sin(row), jnp.sin(row), -jnp.sin(col), jnp.sin(col)], axis=1)
    cos = jnp.concatenate([jnp.ones((C, HD), F32), cos], axis=0)
    sin = jnp.concatenate([jnp.zeros((C, HD), F32), sin], axis=0)
    return cos, sin


def _segment_sums(parts, B, tpe, ncq):
    p = parts.reshape(B, tpe, D)
    return jnp.concatenate([jnp.sum(p[:, ncq:], axis=1), jnp.sum(p[:, :ncq], axis=(0, 1))[None]], axis=0)


def kernel(x, c, ctx, c_ctx, w_ada, b_ada, w_in, b_in, q_gain, k_gain, conv_dw_w, conv_dw_b, conv_ln_g, conv_ln_b, conv_pw_w, conv_pw_b, pool_w, pool_scale, w_out, b_out, ln1_g, ln1_b, ln2_g, ln2_b, w_up, ffn_dw_w, ffn_dw_b, w_down, loss_target, m_c_ctx, m_w_ada, m_b_ada, m_w_in, m_b_in, m_q_gain, m_k_gain, m_conv_dw_w, m_conv_dw_b, m_conv_ln_g, m_conv_ln_b, m_conv_pw_w, m_conv_pw_b, m_pool_w, m_pool_scale, m_w_out, m_b_out, m_ln1_g, m_ln1_b, m_ln2_g, m_ln2_b, m_w_up, m_ffn_dw_w, m_ffn_dw_b, m_w_down, v_c_ctx, v_w_ada, v_b_ada, v_w_in, v_b_in, v_q_gain, v_k_gain, v_conv_dw_w, v_conv_dw_b, v_conv_ln_g, v_conv_ln_b, v_conv_pw_w, v_conv_pw_b, v_pool_w, v_pool_scale, v_w_out, v_b_out, v_ln1_g, v_ln1_b, v_ln2_g, v_ln2_b, v_w_up, v_ffn_dw_w, v_ffn_dw_b, v_w_down):
    given = dict(locals())
    B, S, _ = x.shape
    C = ctx.shape[1]
    R = C + S
    T = B * R
    tpe, ncq = R // TM, C // TM
    nt = T // TM
    assert S % TM == 0 and C % TM == 0 and B + 1 <= 16

    operands = [given[n].astype(MXU) for n, _, _ in BIG]
    taps = _all_gather([given[n] for n, _, _ in TAPS], name="gather_taps")
    W = []
    for l in range(DEPTH):
        blocks = _all_gather(operands, name="gather_weights", layer=l)
        wl = {n: _full_from_blocks(blk, a) for (n, _, a), blk in zip(BIG, blocks)}
        for (n, _, a), blk in zip(TAPS, taps):
            wl[n] = _full_from_blocks(blk[:, l], a)
        W.append(wl)

    xu = jnp.concatenate([ctx, x], axis=1).reshape(T, D)
    cond = jnp.concatenate([c, c_ctx[None], jnp.zeros((16 - B - 1, D), F32)], axis=0)
    s_cond, ds_cond = _silu_rows(cond)
    ctx_tile = jnp.asarray((np.arange(tpe) < ncq)[None, :, None])
    cos_t, sin_t = _rope_tables(C, S)
    row = lambda v: v.reshape(1, -1)

    mods = [_matmul(s_cond, W[l]["w_ada"], "nn", name="ada", bias=row(b_ada[l]), tm=16, tn=1024) for l in range(DEPTH)]
    modt = [jnp.where(ctx_tile, m[B][None, None, :], m[:B][:, None, :]).reshape(nt, 1, N_MOD * D) for m in mods]

    saved = []
    h1 = _modulate_cast(xu, modt[0], 0)
    xin = xu
    for l in range(DEPTH):
        wl = W[l]
        w_inl, b_inl = wl["w_in"], b_in[l]
        z_qkv = _matmul(h1, w_inl[:, :QKV_W], "nn", name="z_qkv", bias=row(b_inl[:QKV_W]), tn=768)
        c0, p0, g0 = QKV_W, QKV_W + 2 * CONV_CH, QKV_W + 2 * CONV_CH + POOL_CH
        z_conv = _matmul(h1, w_inl[:, c0:p0], "nn", name="z_conv", bias=row(b_inl[c0:p0]))
        z_pool = _matmul(h1, w_inl[:, p0:g0], "nn", name="z_pool", bias=row(b_inl[p0:g0]))
        z_gate = _matmul(h1, w_inl[:, g0:], "nn", name="z_gate", bias=row(b_inl[g0:]))
        qkv = _qk_prep(z_qkv, cos_t, sin_t, row(q_gain[l]), row(k_gain[l]), tpe)
        attn = _attn_fwd(qkv, B, C, R)
        dw32 = jnp.pad(wl["conv_dw_w"], ((0, 32 - CONV_K), (0, 0)))
        hc = _conv_fwd(z_conv, dw32, row(conv_dw_b[l]), B, C, S)
        sw = _ln_silu(hc, row(conv_ln_g[l]), row(conv_ln_b[l]))
        conv_o = _matmul(sw, wl["conv_pw_w"], "nn", name="conv_pw", bias=row(conv_pw_b[l]))
        pooled, pool_o = _pool_fwd(z_pool, wl["pool_w"], row(pool_scale[l]), B, C, S)
        m = _merge(attn, conv_o, pool_o, z_gate)
        mo = _matmul(m, wl["w_out"], "nn", name="w_out", bias=row(b_out[l]))
        y1, h2 = _resid_ln(xin, mo, modt[l], 2, row(ln1_g[l]), row(ln1_b[l]), modt[l], 3)
        up = _matmul(h2, wl["w_up"], "nn", name="w_up", tn=1408)
        fw8 = jnp.pad(wl["ffn_dw_w"], ((0, 8 - FFN_K), (0, 0)))
        f = _ffn_mid(up, fw8, row(ffn_dw_b[l]), B, C, S)
        fo = _matmul(f, wl["w_down"], "nn", name="w_down", tm=512)
        if l + 1 < DEPTH:
            y2, h_next = _resid_ln(y1, fo, modt[l], 5, row(ln2_g[l]), row(ln2_b[l]), modt[l + 1], 0)
        else:
            y2, h_next = _resid_ln(y1, fo, modt[l], 5, row(ln2_g[l]), row(ln2_b[l])), None
        saved.append(dict(xin=xin, h1=h1, z_qkv=z_qkv, z_conv=z_conv, z_gate=z_gate, qkv=qkv, attn=attn, hc=hc, sw=sw,
                          conv_o=conv_o, pooled=pooled, pool_o=pool_o, m=m, mo=mo, y1=y1, h2=h2, up=up, f=f, fo=fo, dw32=dw32,
                          fw8=fw8))
        xin, h1 = y2, h_next

    dy, loss_part = _loss_grad(xin, loss_target.reshape(B * S, D), tpe, ncq)

    small = {n: [None] * DEPTH for n, _ in REPLICATED}
    d_c_ctx = jnp.zeros((D,), F32)
    dmods_t = [[None] * N_MOD for _ in range(DEPTH)]
    layer_grads = [None] * DEPTH
    dh1 = None
    for l in reversed(range(DEPTH)):
        wl, sv = W[l], saved[l]
        dmod = dmods_t[l]
        if dh1 is None:
            dy1p, dfo, dgate2, dg, db, _ = _ln_bwd(dy, sv["y1"], sv["fo"], modt[l], 5, row(ln2_g[l]))
        else:
            dy1p, dfo, dgate2, dg, db, _, dsh, dsc = _ln_bwd(dy, sv["y1"], sv["fo"], modt[l], 5, row(ln2_g[l]), dh=dh1,
                                                             y=saved[l + 1]["xin"], mod_next=modt[l + 1], k_shift_next=0)
            dmods_t[l + 1][0], dmods_t[l + 1][1] = dsh, dsc
        small["ln2_g"][l], small["ln2_b"][l] = dg[0], db[0]
        dmod[5] = dgate2
        df = _matmul(dfo, wl["w_down"], "nt", name="d_f", tn=1408)
        g_w_down = _matmul(sv["f"], dfo, "tn", name="g_w_down", tm=1408, tk=512)
        da2, du2, g_fdw, g_fdb = _ffn_mid_bwd(df, sv["up"], sv["fw8"], row(ffn_dw_b[l]), B, C, S)
        small["ffn_dw_b"][l] = g_fdb[0]
        dh2 = _matmul(da2, wl["w_up"][:, :D_FF], "nt", name="d_h2a", tm=512)
        dh2 = _matmul(du2, wl["w_up"][:, D_FF:], "nt", name="d_h2u", tm=512, acc_in=dh2)
        g_w_up = _matmul(sv["h2"], da2, "tn", name="g_w_up_a", tn=1408, tk=512, into=(2 * D_FF, 0, None))
        g_w_up = _matmul(sv["h2"], du2, "tn", name="g_w_up_u", tn=1408, tk=512, into=(2 * D_FF, D_FF, g_w_up))
        dxp, dmo, dgate1, dg, db, dbo, dsh, dsc = _ln_bwd(dy1p, sv["xin"], sv["mo"], modt[l], 2, row(ln1_g[l]), dh=dh2,
                                                          y=sv["y1"], mod_next=modt[l], k_shift_next=3)
        small["ln1_g"][l], small["ln1_b"][l], small["b_out"][l] = dg[0], db[0], dbo[0]
        dmod[2], dmod[3], dmod[4] = dgate1, dsh, dsc
        dm = _matmul(dmo, wl["w_out"], "nt", name="d_m")
        g_w_out = _matmul(sv["m"], dmo, "tn", name="g_w_out", tk=512)
        dattn, dconv_o, dpool_o, dzg, g_pwb, gsum_gate = _merge_bwd(dm, sv["attn"], sv["conv_o"], sv["pool_o"], sv["z_gate"])
        small["conv_pw_b"][l] = g_pwb[0]
        du, g_pool_w, g_pool_sc, gsum_pool = _pool_bwd(dpool_o, sv["pooled"], wl["pool_w"], row(pool_scale[l]), B, C, S)
        small["pool_scale"][l] = g_pool_sc[0]
        dsw = _matmul(dconv_o, wl["conv_pw_w"], "nt", name="d_sw")
        g_pw = _matmul(sv["sw"], dconv_o, "tn", name="g_conv_pw", tk=512)
        dhc, g_cg, g_cb, g_cdb = _ln_silu_bwd(dsw, sv["hc"], row(conv_ln_g[l]), row(conv_ln_b[l]))
        small["conv_ln_g"][l], small["conv_ln_b"][l], small["conv_dw_b"][l] = g_cg[0], g_cb[0], g_cdb[0]
        da, dgt, g_cdw, gsum_a, gsum_gt = _conv_bwd(dhc, sv["z_conv"], sv["dw32"], B, C, S)
        dq, dk, dv = _attn_bwd(sv["qkv"], sv["attn"], dattn, B, C, R)
        dz_qkv, g_qg, g_kg, gsum_qkv = _qk_bwd(dq, dk, dv, sv["z_qkv"], cos_t, sin_t, row(q_gain[l]), row(k_gain[l]), tpe)
        small["q_gain"][l], small["k_gain"][l] = g_qg[0], g_kg[0]
        small["b_in"][l] = jnp.concatenate([gsum_qkv[0], gsum_a[0], gsum_gt[0], gsum_pool[0], gsum_gate[0]])
        w_inl = wl["w_in"]
        pieces = ((dz_qkv, 0, QKV_W), (da, QKV_W, CONV_CH), (dgt, QKV_W + CONV_CH, CONV_CH), (du, QKV_W + 2 * CONV_CH, POOL_CH),
                  (dzg, QKV_W + 2 * CONV_CH + POOL_CH, N_GATE))
        dh1 = g_w_in = None
        for k, (dz, c0, wd) in enumerate(pieces):
            dh1 = _matmul(dz, w_inl[:, c0:c0 + wd], "nt", name=f"d_h1_{k}", tm=512, acc_in=dh1)
            g_w_in = _matmul(sv["h1"], dz, "tn", name=f"g_w_in_{k}", tn=512, tk=512, into=(D_IN, c0, g_w_in))

        layer_grads[l] = {"w_in": g_w_in, "conv_pw_w": g_pw, "pool_w": g_pool_w, "w_out": g_w_out, "w_up": g_w_up,
                          "w_down": g_w_down, "conv_dw_w": g_cdw[:CONV_K], "ffn_dw_w": g_fdw[:FFN_K]}
        dy = dxp
    gx_u, dmods_t[0][0], dmods_t[0][1] = _mod_bwd(dy, dh1, saved[0]["xin"], modt[0], 0)
    grad_x = gx_u.reshape(B, R, D)[:, C:]

    kinds = ("grad_", "delta_", "new_m_", "new_v_")
    stacks = {n: [given[pre + n].reshape((DEPTH,) + _as_rows(s)) for pre in ("", "m_", "v_")] for n, s, _ in SHARDED}
    results = {n: None for n, _, _ in SHARDED}
    for l in range(DEPTH):
        dmods = jnp.concatenate([_segment_sums(p, B, tpe, ncq) for p in dmods_t[l]], axis=1)
        small["b_ada"][l] = jnp.sum(dmods, axis=0)
        dm16 = jnp.concatenate([dmods, jnp.zeros((16 - B - 1, N_MOD * D), F32)], axis=0).astype(MXU)
        layer_grads[l]["w_ada"] = _matmul(s_cond, dm16, "tn", name="g_w_ada", tm=1024, tn=1024)
        dcond = _matmul(dm16, W[l]["w_ada"], "nt", name="d_cond", tm=16, tn=1024, tk=2048)
        d_c_ctx = d_c_ctx + dcond[B] * ds_cond[B]
        gs = [_blocks_from_full(layer_grads[l][n], a).reshape((N_DEV,) + _as_rows(s)) for n, s, a in SHARDED]
        r1 = _sibling_exchange(gs)
        r2 = _chip_exchange([_pair_sum(g, r) for g, r in zip(gs, r1)])
        for (n, _, _), parts in zip(SHARDED, r2):
            results[n] = _adamw(parts, *stacks[n], l, results[n], name="adamw_sharded")
    outs = {}
    for n, s, _ in SHARDED:
        for kind, buf in zip(kinds, results[n]):
            outs[kind + n] = buf.reshape((DEPTH,) + tuple(s))

    def small_pack(pieces):
        flat = jnp.concatenate([p.reshape(-1) for p in pieces])
        return jnp.pad(flat, (0, SMALL_ROWS * LANES - flat.shape[0])).reshape(SMALL_ROWS, LANES)

    zero1 = jnp.zeros((1,), F32)
    g_pack = small_pack([small[n][l] for n, _ in REPLICATED for l in range(DEPTH)] + [d_c_ctx, loss_part[0, :1]])
    g_small, = _all_gather([g_pack], name="gather_small")
    wmv = [small_pack([given[pre + n] for n, _ in REPLICATED] + [given[pre + "c_ctx"], zero1])[None] for pre in ("", "m_", "v_")]
    res = _adamw(g_small, *wmv, 0, None, name="adamw_small")
    for kind, buf in zip(kinds, res):
        flat = buf.reshape(-1)
        off = 0
        for n, sz in REPLICATED:
            outs[kind + n] = flat[off:off + DEPTH * sz].reshape(DEPTH, sz)
            off += DEPTH * sz
        outs[kind + "c_ctx"] = flat[off:off + D]
        if kind == "grad_":
            loss = flat[off + D]

    names = ["c_ctx", "w_ada", "b_ada", "w_in", "b_in", "q_gain", "k_gain", "conv_dw_w", "conv_dw_b", "conv_ln_g", "conv_ln_b",
             "conv_pw_w", "conv_pw_b", "pool_w", "pool_scale", "w_out", "b_out", "ln1_g", "ln1_b", "ln2_g", "ln2_b", "w_up",
             "ffn_dw_w", "ffn_dw_b", "w_down"]
    return (loss, grad_x, *[outs[k + n] for k in ("grad_", "delta_", "new_m_", "new_v_") for n in names])
```

```python
import functools

import jax
import jax.numpy as jnp
import numpy as np
from jax import lax
from jax.experimental import pallas as pl
from jax.experimental.pallas import tpu as pltpu

F32 = jnp.float32
MXU = jnp.bfloat16
WIRE = jnp.bfloat16

D = 1024
HD = 128
NH = 8
NKV = 2
QG = NH // NKV
KV_W = NKV * HD
QKV_W = NH * HD + 2 * KV_W
CONV_CH = D
POOL_CH = D
POOL_WINDOWS = (2, 4, 8, 16)
POOL_GCH = POOL_CH // len(POOL_WINDOWS)
N_GATE = 3 * D
D_IN = QKV_W + 2 * CONV_CH + POOL_CH + N_GATE
D_FF = 2816
N_MOD = 6
DEPTH = 4
CONV_K = 31
FFN_K = 3
GRID_W = 64
ROPE_THETA = 10000.0
ROPE_PAIRS = HD // 4
ALPHA = (2 * DEPTH) ** 0.25
LN_EPS = 1e-5
RMS_EPS = 1e-6
ATTN_SCALE = HD ** -0.5
LOG2_E = 1.4426950408889634
ADAM_LR, ADAM_B1, ADAM_B2, ADAM_EPS, ADAM_WD, ADAM_STEP = 0.001, 0.9, 0.999, 1e-08, 0.01, 10

N_DEV = 8
TM = 256
GAP = 16
LANES = 1024
VMEM_MB = 48

NN = (((1,), (0,)), ((), ()))
NT = (((1,), (1,)), ((), ()))
TN = (((0,), (0,)), ((), ()))

_pcall = pl.pallas_call


def _call(body, *, name, grid, in_specs, out_specs, out_shape, scratch=(), aliases=None, vmem=VMEM_MB):
    return _pcall(
        body, name=name, grid=grid, in_specs=in_specs, out_specs=out_specs, out_shape=out_shape,
        scratch_shapes=list(scratch), input_output_aliases=aliases or {},
        compiler_params=pltpu.CompilerParams(dimension_semantics=("arbitrary",) * len(grid), vmem_limit_bytes=vmem * 2 ** 20),
    )


def _sds(shape, dtype):
    return jax.ShapeDtypeStruct(tuple(shape), dtype)


def _pick(n, cap, mult):
    best = None
    for t in range(mult, min(n, cap) + 1, mult):
        if n % t == 0:
            best = t
    return best if best is not None else n


def _dot(a, b, dims):
    return lax.dot_general(a, b, dims, preferred_element_type=F32)


def _sigmoid(x):
    return 1.0 / (1.0 + jnp.exp(-x))


def _matmul(a, b, mode, *, name, bias=None, acc_in=None, into=None, out_dtype=F32, tm=1024, tn=1024, tk=None):
    if mode == "nn":
        (M, K), (K2, N) = a.shape, b.shape
    elif mode == "nt":
        (M, K), (N, K2) = a.shape, b.shape
    else:
        (K, M), (K2, N) = a.shape, b.shape
    assert K == K2, (a.shape, b.shape, mode)
    tm = _pick(M, tm, 16)
    tn = _pick(N, tn, 128)
    tk = K if tk is None else _pick(K, tk, 128 if mode != "tn" else 16)
    gk = K // tk
    dims = {"nn": NN, "nt": NT, "tn": TN}[mode]
    a_spec = pl.BlockSpec((tk, tm), lambda j, i, k: (k, i)) if mode == "tn" else pl.BlockSpec((tm, tk), lambda j, i, k: (i, k))
    b_spec = pl.BlockSpec((tn, tk), lambda j, i, k: (j, k)) if mode == "nt" else pl.BlockSpec((tk, tn), lambda j, i, k: (k, j))
    in_specs, args = [a_spec, b_spec], [a, b]
    if bias is not None:
        in_specs.append(pl.BlockSpec((1, tn), lambda j, i, k: (0, j)))
        args.append(bias)
    aliases = {}
    if acc_in is not None:
        aliases = {len(args): 0}
        in_specs.append(pl.BlockSpec((tm, tn), lambda j, i, k: (i, j)))
        args.append(acc_in)
    n_total, col0, prev = (N, 0, None) if into is None else into
    assert col0 % tn == 0
    jb = col0 // tn
    if prev is not None:
        aliases = {len(args): 0}
        in_specs.append(pl.BlockSpec(memory_space=pl.ANY))
        args.append(prev)
    n_in = len(args)

    def body(*refs):
        a_ref, b_ref = refs[0], refs[1]
        pos = 2
        bias_ref = acc_in_ref = None
        if bias is not None:
            bias_ref = refs[pos]
            pos += 1
        if acc_in is not None:
            acc_in_ref = refs[pos]
        pos = n_in
        o_ref = refs[pos]
        part = _dot(a_ref[...].astype(MXU), b_ref[...].astype(MXU), dims)

        def finish(acc):
            if bias_ref is not None:
                acc = acc + bias_ref[...]
            if acc_in_ref is not None:
                acc = acc + acc_in_ref[...]
            o_ref[...] = acc.astype(out_dtype)

        if gk == 1:
            finish(part)
        else:
            acc_ref = refs[pos + 1]
            k = pl.program_id(2)

            @pl.when(k == 0)
            def _():
                acc_ref[...] = part

            @pl.when(k > 0)
            def _():
                acc_ref[...] += part

            @pl.when(k == gk - 1)
            def _():
                finish(acc_ref[...])

    return _call(
        body, name=name, grid=(N // tn, M // tm, gk), in_specs=in_specs,
        out_specs=pl.BlockSpec((tm, tn), lambda j, i, k: (i, j + jb)), out_shape=_sds((M, n_total), out_dtype),
        scratch=[pltpu.VMEM((tm, tn), F32)] if gk > 1 else [], aliases=aliases,
    )(*args)


def _rt(w, cb=0):
    return pl.BlockSpec((TM, w), lambda i: (i, cb))


def _ct(w):
    return pl.BlockSpec((w, TM), lambda i: (0, i))


def _vec(w):
    return pl.BlockSpec((1, w), lambda i: (0, 0))


def _part(w):
    return pl.BlockSpec((1, 1, w), lambda i: (i, 0, 0))


def _mod(ref, k):
    return ref[0, :, k * D:(k + 1) * D]


def _colsum(x):
    return jnp.sum(x, axis=0, keepdims=True)


def _ln_stats(s):
    mu = jnp.mean(s, axis=1, keepdims=True)
    cen = s - mu
    var = jnp.mean(cen * cen, axis=1, keepdims=True)
    rstd = lax.rsqrt(var + LN_EPS)
    return cen * rstd, rstd


def _modulate_cast(x, modt, k_shift):
    T = x.shape[0]
    nt = T // TM

    def body(x_ref, mod_ref, h_ref, ht_ref):
        h = (x_ref[...] * (1.0 + _mod(mod_ref, k_shift + 1)) + _mod(mod_ref, k_shift)).astype(MXU)
        h_ref[...] = h
        ht_ref[...] = h.T

    return _call(body, name="modulate", grid=(nt,), in_specs=[_rt(D), _part(N_MOD * D)], out_specs=[_rt(D), _ct(D)],
                 out_shape=[_sds((T, D), MXU), _sds((D, T), MXU)])(x, modt)


def _resid_ln(x, br, modt, k_gate, g, b, mod_next=None, k_shift_next=0):
    T = x.shape[0]
    nt = T // TM
    with_h = mod_next is not None

    def body(*refs):
        x_ref, br_ref, mod_ref, g_ref, b_ref = refs[:5]
        s = ALPHA * x_ref[...] + _mod(mod_ref, k_gate) * br_ref[...]
        xhat, _ = _ln_stats(s)
        y = xhat * g_ref[...] + b_ref[...]
        if with_h:
            modn_ref, y_ref, h_ref, ht_ref = refs[5:]
            y_ref[...] = y
            h = (y * (1.0 + _mod(modn_ref, k_shift_next + 1)) + _mod(modn_ref, k_shift_next)).astype(MXU)
            h_ref[...] = h
            ht_ref[...] = h.T
        else:
            refs[5][...] = y

    in_specs = [_rt(D), _rt(D), _part(N_MOD * D), _vec(D), _vec(D)]
    args = [x, br, modt, g, b]
    if with_h:
        in_specs.append(_part(N_MOD * D))
        args.append(mod_next)
        return _call(body, name="resid_ln_mod", grid=(nt,), in_specs=in_specs, out_specs=[_rt(D), _rt(D), _ct(D)],
                     out_shape=[_sds((T, D), F32), _sds((T, D), MXU), _sds((D, T), MXU)])(*args)
    return _call(body, name="resid_ln", grid=(nt,), in_specs=in_specs, out_specs=_rt(D), out_shape=_sds((T, D), F32))(*args)


def _ln_bwd(dy_part, x, br, modt, k_gate, g, dh=None, y=None, mod_next=None, k_shift_next=0):
    T = x.shape[0]
    nt = T // TM
    with_h = dh is not None

    def body(*refs):
        if with_h:
            dyp_ref, x_ref, br_ref, mod_ref, g_ref, dh_ref, y_ref, modn_ref = refs[:8]
            outs = refs[8:]
        else:
            dyp_ref, x_ref, br_ref, mod_ref, g_ref = refs[:5]
            outs = refs[5:]
        dx_ref, dbr_ref, dgate_ref, dlg_ref, dlb_ref, dbsum_ref = outs[:6]
        i = pl.program_id(0)

        @pl.when(i == 0)
        def _():
            dlg_ref[...] = jnp.zeros_like(dlg_ref)
            dlb_ref[...] = jnp.zeros_like(dlb_ref)
            dbsum_ref[...] = jnp.zeros_like(dbsum_ref)

        dy = dyp_ref[...]
        if with_h:
            dshift_ref, dscale_ref = outs[6:]
            dhv = dh_ref[...]
            dy = dy + dhv * (1.0 + _mod(modn_ref, k_shift_next + 1))
            dshift_ref[0] = _colsum(dhv)
            dscale_ref[0] = _colsum(dhv * y_ref[...])
        gate = _mod(mod_ref, k_gate)
        brv = br_ref[...]
        s = ALPHA * x_ref[...] + gate * brv
        xhat, rstd = _ln_stats(s)
        dlg_ref[...] += _colsum(dy * xhat)
        dlb_ref[...] += _colsum(dy)
        dyg = dy * g_ref[...]
        m1 = jnp.mean(dyg, axis=1, keepdims=True)
        m2 = jnp.mean(dyg * xhat, axis=1, keepdims=True)
        ds = rstd * (dyg - m1 - xhat * m2)
        dx_ref[...] = ALPHA * ds
        dbr = gate * ds
        dbr_ref[...] = dbr.astype(MXU)
        dbsum_ref[...] += _colsum(dbr)
        dgate_ref[0] = _colsum(ds * brv)

    in_specs = [_rt(D), _rt(D), _rt(D), _part(N_MOD * D), _vec(D)]
    args = [dy_part, x, br, modt, g]
    out_specs = [_rt(D), _rt(D), _part(D), _vec(D), _vec(D), _vec(D)]
    out_shape = [_sds((T, D), F32), _sds((T, D), MXU), _sds((nt, 1, D), F32), _sds((1, D), F32), _sds((1, D), F32), _sds((1, D), F32)]
    if with_h:
        in_specs += [_rt(D), _rt(D), _part(N_MOD * D)]
        args += [dh, y, mod_next]
        out_specs += [_part(D), _part(D)]
        out_shape += [_sds((nt, 1, D), F32), _sds((nt, 1, D), F32)]
    return _call(body, name="ln_bwd_mod" if with_h else "ln_bwd", grid=(nt,), in_specs=in_specs, out_specs=out_specs,
                 out_shape=out_shape)(*args)


def _mod_bwd(dx_part, dh, x, modt, k_shift):
    T = x.shape[0]
    nt = T // TM

    def body(dxp_ref, dh_ref, x_ref, mod_ref, dx_ref, dshift_ref, dscale_ref):
        dhv = dh_ref[...]
        dx_ref[...] = dxp_ref[...] + dhv * (1.0 + _mod(mod_ref, k_shift + 1))
        dshift_ref[0] = _colsum(dhv)
        dscale_ref[0] = _colsum(dhv * x_ref[...])

    return _call(body, name="mod_bwd", grid=(nt,), in_specs=[_rt(D), _rt(D), _rt(D), _part(N_MOD * D)],
                 out_specs=[_rt(D), _part(D), _part(D)],
                 out_shape=[_sds((T, D), F32), _sds((nt, 1, D), F32), _sds((nt, 1, D), F32)])(dx_part, dh, x, modt)


def _loss_grad(y, target, tpe, ncq):
    T = y.shape[0]
    nt = T // TM
    nl = tpe - ncq

    def body(y_ref, t_ref, dy_ref, loss_ref):
        i = pl.program_id(0)

        @pl.when(i == 0)
        def _():
            loss_ref[...] = jnp.zeros_like(loss_ref)

        @pl.when(i % tpe < ncq)
        def _():
            dy_ref[...] = jnp.zeros_like(dy_ref)

        @pl.when(i % tpe >= ncq)
        def _():
            err = y_ref[...] - t_ref[...]
            dy_ref[...] = err * (1.0 / D)
            loss_ref[...] += (0.5 / D) * jnp.sum(_colsum(err * err), axis=1, keepdims=True)

    tgt_spec = pl.BlockSpec((TM, D), lambda i: ((i // tpe) * nl + jnp.maximum(i % tpe - ncq, 0), 0))
    return _call(body, name="loss_grad", grid=(nt,), in_specs=[_rt(D), tgt_spec], out_specs=[_rt(D), _vec(128)],
                 out_shape=[_sds((T, D), F32), _sds((1, 128), F32)])(y, target)


def _rope_partner(x):
    lane = lax.broadcasted_iota(jnp.int32, x.shape, 1)
    first = (lane % (2 * ROPE_PAIRS)) < ROPE_PAIRS
    return jnp.where(first, pltpu.roll(x, HD - ROPE_PAIRS, 1), pltpu.roll(x, ROPE_PAIRS, 1))


def _qk_prep(z_qkv, cos_t, sin_t, q_gain, k_gain, tpe):
    T = z_qkv.shape[0]
    nt = T // TM

    def body(z_ref, cos_ref, sin_ref, qg_ref, kg_ref, o_ref):
        cos, sin = cos_ref[...], sin_ref[...]
        for h in range(NH + NKV):
            sl = slice(h * HD, (h + 1) * HD)
            t = z_ref[:, sl]
            gain = qg_ref[...] if h < NH else kg_ref[...]
            n = t * lax.rsqrt(jnp.mean(t * t, axis=1, keepdims=True) + RMS_EPS) * gain
            o_ref[:, sl] = (n * cos + _rope_partner(n) * sin).astype(MXU)
        o_ref[:, (NH + NKV) * HD:] = z_ref[:, (NH + NKV) * HD:].astype(MXU)

    tab = pl.BlockSpec((TM, HD), lambda i: (i % tpe, 0))
    return _call(body, name="qk_prep", grid=(nt,), in_specs=[_rt(QKV_W), tab, tab, _vec(HD), _vec(HD)], out_specs=_rt(QKV_W),
                 out_shape=_sds((T, QKV_W), MXU))(z_qkv, cos_t, sin_t, q_gain, k_gain)


def _qk_bwd(dq, dk, dv, z_qkv, cos_t, sin_t, q_gain, k_gain, tpe):
    T = z_qkv.shape[0]
    nt = T // TM

    def body(dq_ref, dk_ref, dv_ref, z_ref, cos_ref, sin_ref, qg_ref, kg_ref, dz_ref, dqg_ref, dkg_ref, bsum_ref):
        i = pl.program_id(0)

        @pl.when(i == 0)
        def _():
            dqg_ref[...] = jnp.zeros_like(dqg_ref)
            dkg_ref[...] = jnp.zeros_like(dkg_ref)
            bsum_ref[...] = jnp.zeros_like(bsum_ref)

        cos, sin = cos_ref[...], sin_ref[...]
        for h in range(NH + NKV):
            sl = slice(h * HD, (h + 1) * HD)
            dr = dq_ref[:, sl] if h < NH else dk_ref[:, (h - NH) * HD:(h - NH + 1) * HD]
            gain = qg_ref[...] if h < NH else kg_ref[...]
            dn = dr * cos + _rope_partner(dr * sin)
            t = z_ref[:, sl]
            rstd = lax.rsqrt(jnp.mean(t * t, axis=1, keepdims=True) + RMS_EPS)
            that = t * rstd
            dgain = _colsum(dn * that)
            if h < NH:
                dqg_ref[...] += dgain
            else:
                dkg_ref[...] += dgain
            dthat = dn * gain
            dt = rstd * (dthat - that * jnp.mean(dthat * that, axis=1, keepdims=True))
            dz_ref[:, sl] = dt.astype(MXU)
            bsum_ref[:, sl] += _colsum(dt)
        dvv = dv_ref[...]
        dz_ref[:, (NH + NKV) * HD:] = dvv.astype(MXU)
        bsum_ref[:, (NH + NKV) * HD:] += _colsum(dvv)

    tab = pl.BlockSpec((TM, HD), lambda i: (i % tpe, 0))
    return _call(body, name="qk_bwd", grid=(nt,),
                 in_specs=[_rt(NH * HD), _rt(KV_W), _rt(KV_W), _rt(QKV_W), tab, tab, _vec(HD), _vec(HD)],
                 out_specs=[_rt(QKV_W), _vec(HD), _vec(HD), _vec(QKV_W)],
                 out_shape=[_sds((T, QKV_W), MXU), _sds((1, HD), F32), _sds((1, HD), F32), _sds((1, QKV_W), F32)],
                 )(dq, dk, dv, z_qkv, cos_t, sin_t, q_gain, k_gain)


def _ln_silu(hc, g, b):
    T = hc.shape[0]

    def body(h_ref, g_ref, b_ref, o_ref, ot_ref):
        xhat, _ = _ln_stats(h_ref[...])
        n = xhat * g_ref[...] + b_ref[...]
        sw = (n * _sigmoid(n)).astype(MXU)
        o_ref[...] = sw
        ot_ref[...] = sw.T

    return _call(body, name="ln_silu", grid=(T // TM,), in_specs=[_rt(D), _vec(D), _vec(D)], out_specs=[_rt(D), _ct(D)],
                 out_shape=[_sds((T, D), MXU), _sds((D, T), MXU)])(hc, g, b)


def _ln_silu_bwd(dsw, hc, g, b):
    T = hc.shape[0]

    def body(d_ref, h_ref, g_ref, b_ref, dh_ref, dg_ref, db_ref, dcb_ref):
        i = pl.program_id(0)

        @pl.when(i == 0)
        def _():
            dg_ref[...] = jnp.zeros_like(dg_ref)
            db_ref[...] = jnp.zeros_like(db_ref)
            dcb_ref[...] = jnp.zeros_like(dcb_ref)

        xhat, rstd = _ln_stats(h_ref[...])
        n = xhat * g_ref[...] + b_ref[...]
        sg = _sigmoid(n)
        dn = d_ref[...] * (sg * (1.0 + n * (1.0 - sg)))
        dg_ref[...] += _colsum(dn * xhat)
        db_ref[...] += _colsum(dn)
        dng = dn * g_ref[...]
        m1 = jnp.mean(dng, axis=1, keepdims=True)
        m2 = jnp.mean(dng * xhat, axis=1, keepdims=True)
        dh = rstd * (dng - m1 - xhat * m2)
        dh_ref[...] = dh
        dcb_ref[...] += _colsum(dh)

    return _call(body, name="ln_silu_bwd", grid=(T // TM,), in_specs=[_rt(D), _rt(D), _vec(D), _vec(D)],
                 out_specs=[_rt(D), _vec(D), _vec(D), _vec(D)],
                 out_shape=[_sds((T, D), F32)] + [_sds((1, D), F32)] * 3)(dsw, hc, g, b)


def _merge(attn, conv_o, pool_o, z_gate):
    T = attn.shape[0]

    def body(a_ref, c_ref, p_ref, zg_ref, m_ref, mt_ref):
        m = (_sigmoid(zg_ref[:, 0:D]) * a_ref[...] + _sigmoid(zg_ref[:, D:2 * D]) * c_ref[...]
             + _sigmoid(zg_ref[:, 2 * D:3 * D]) * p_ref[...]).astype(MXU)
        m_ref[...] = m
        mt_ref[...] = m.T

    return _call(body, name="merge", grid=(T // TM,), in_specs=[_rt(D), _rt(D), _rt(D), _rt(N_GATE)], out_specs=[_rt(D), _ct(D)],
                 out_shape=[_sds((T, D), MXU), _sds((D, T), MXU)])(attn, conv_o, pool_o, z_gate)


def _merge_bwd(dm, attn, conv_o, pool_o, z_gate):
    T = attn.shape[0]

    def body(dm_ref, a_ref, c_ref, p_ref, zg_ref, da_ref, dc_ref, dp_ref, dzg_ref, dcsum_ref, gsum_ref):
        i = pl.program_id(0)

        @pl.when(i == 0)
        def _():
            dcsum_ref[...] = jnp.zeros_like(dcsum_ref)
            gsum_ref[...] = jnp.zeros_like(gsum_ref)

        dmv = dm_ref[...]
        for k, (br_ref, out_ref) in enumerate(((a_ref, da_ref), (c_ref, dc_ref), (p_ref, dp_ref))):
            gk = _sigmoid(zg_ref[:, k * D:(k + 1) * D])
            dbr = dmv * gk
            out_ref[...] = dbr.astype(out_ref.dtype)
            if k == 1:
                dcsum_ref[...] += _colsum(dbr)
            dzg = dmv * br_ref[...] * gk * (1.0 - gk)
            dzg_ref[:, k * D:(k + 1) * D] = dzg.astype(MXU)
            gsum_ref[:, k * D:(k + 1) * D] += _colsum(dzg)

    return _call(body, name="merge_bwd", grid=(T // TM,), in_specs=[_rt(D), _rt(D), _rt(D), _rt(D), _rt(N_GATE)],
                 out_specs=[_rt(D), _rt(D), _rt(D), _rt(N_GATE), _vec(D), _vec(N_GATE)],
                 out_shape=[_sds((T, D), F32), _sds((T, D), MXU), _sds((T, D), F32), _sds((T, N_GATE), MXU),
                            _sds((1, D), F32), _sds((1, N_GATE), F32)])(dm, attn, conv_o, pool_o, z_gate)


def _softmax_parts(q, k):
    s = _dot(q, k, NT)
    p = jnp.exp2((s - jnp.max(s, axis=1, keepdims=True)) * (ATTN_SCALE * LOG2_E))
    return p, 1.0 / jnp.sum(p, axis=1, keepdims=True)


def _attn_specs(nq):
    q_spec = pl.BlockSpec((TM, QG * HD), lambda b, h, q: (b * nq + q, h))
    k_spec = pl.BlockSpec((nq * TM, HD), lambda b, h, q: (b, NH + h))
    v_spec = pl.BlockSpec((nq * TM, HD), lambda b, h, q: (b, NH + NKV + h))
    return q_spec, k_spec, v_spec


def _attn_fwd(qkv, B, C, R):
    nq, ncq = R // TM, C // TM

    def body(q_ref, k_ref, v_ref, o_ref):
        def attend(L):
            k, v = k_ref[0:L, :], v_ref[0:L, :]
            for i in range(QG):
                sl = slice(i * HD, (i + 1) * HD)
                p, inv_l = _softmax_parts(q_ref[:, sl], k)
                o_ref[:, sl] = _dot(p.astype(MXU), v, NN) * inv_l

        qi = pl.program_id(2)
        pl.when(qi < ncq)(functools.partial(attend, C))
        pl.when(qi >= ncq)(functools.partial(attend, R))

    q_spec, k_spec, v_spec = _attn_specs(nq)
    return _call(body, name="attn_fwd", grid=(B, NKV, nq), in_specs=[q_spec, k_spec, v_spec], out_specs=q_spec,
                 out_shape=_sds((B * R, NH * HD), F32))(qkv, qkv, qkv)


def _attn_bwd(qkv, o, do, B, C, R):
    nq, ncq = R // TM, C // TM

    def body(q_ref, k_ref, v_ref, o_ref, do_ref, dq_ref, dk_ref, dv_ref, dkt, dvt):
        qi = pl.program_id(2)

        @pl.when(qi == 0)
        def _():
            dkt[...] = jnp.zeros_like(dkt)
            dvt[...] = jnp.zeros_like(dvt)

        def bwd(L):
            k, v = k_ref[0:L, :], v_ref[0:L, :]
            for i in range(QG):
                sl = slice(i * HD, (i + 1) * HD)
                q = q_ref[:, sl]
                p, inv_l = _softmax_parts(q, k)
                dov = do_ref[:, sl]
                dp = _dot(dov.astype(MXU), v, NT)
                dl = jnp.sum(dov * o_ref[:, sl], axis=1, keepdims=True)
                ds = (p * ((dp - dl) * (inv_l * ATTN_SCALE))).astype(MXU)
                dq_ref[:, sl] = _dot(ds, k, NN)
                dkt[:, 0:L] += _dot(q, ds, TN)
                dvt[:, 0:L] += _dot((dov * inv_l).astype(MXU), p.astype(MXU), TN)

        pl.when(qi < ncq)(functools.partial(bwd, C))
        pl.when(qi >= ncq)(functools.partial(bwd, R))

        @pl.when(qi == nq - 1)
        def _():
            dk_ref[...] = dkt[...].T
            dv_ref[...] = dvt[...].T

    q_spec, k_spec, v_spec = _attn_specs(nq)
    kv_out = pl.BlockSpec((R, HD), lambda b, h, q: (b, h))
    return _call(body, name="attn_bwd", grid=(B, NKV, nq), in_specs=[q_spec, k_spec, v_spec, q_spec, q_spec],
                 out_specs=[q_spec, kv_out, kv_out],
                 out_shape=[_sds((B * R, NH * HD), F32), _sds((B * R, KV_W), F32), _sds((B * R, KV_W), F32)],
                 scratch=[pltpu.VMEM((HD, R), F32), pltpu.VMEM((HD, R), F32)])(qkv, qkv, qkv, o, do)


def _segments(C, S):
    return ((0, GAP, C), (C, 2 * GAP + C, S))


def _padded_rows(C, S):
    return 3 * GAP + C + S


def _zero_gaps(pad_ref, C, S):
    for off in (0, GAP + C, 2 * GAP + C + S):
        pad_ref[off:off + GAP, :] = jnp.zeros((GAP, pad_ref.shape[1]), pad_ref.dtype)


def _chunks(n, ch, fn):
    def step(i, carry):
        fn(pl.multiple_of(i * ch, ch))
        return carry

    lax.fori_loop(0, n // ch, step, 0)


class _Window:
    def __init__(self, pad_ref, row, ch):
        self.ch = ch
        self.win = pad_ref[pl.ds(row - GAP, ch + 2 * GAP), :]
        self.moved = {}

    def at(self, off):
        s, q = off % 8, off // 8
        if s not in self.moved:
            self.moved[s] = self.win[s:s + self.ch + 2 * GAP - 8, :]
        return self.moved[s][GAP + 8 * q:GAP + 8 * q + self.ch, :]


def _window(pad_ref, row, ch):
    return _Window(pad_ref, row, ch)


def _shifted(win, off, ch):
    return win.at(off)


def _taps(pad_ref, w, row, ch, n_taps, flip=False):
    half = (n_taps - 1) // 2
    win = _window(pad_ref, row, ch)
    acc = None
    for k in range(n_taps):
        term = w[k:k + 1, :] * _shifted(win, (half - k) if flip else (k - half), ch)
        acc = term if acc is None else acc + term
    return acc


def _tap_grads(dw_ref, d, pad_ref, row, ch, n_taps):
    half = (n_taps - 1) // 2
    win = _window(pad_ref, row, ch)
    for k in range(n_taps):
        prod = d * _shifted(win, k - half, ch)
        dw_ref[k] += jnp.sum(prod.reshape(ch // 8, 8, prod.shape[1]), axis=0)


def _conv_fwd(z_conv, w, bias, B, C, S, cw=256, ch=64):
    R = C + S
    nj = CONV_CH // cw
    segs = _segments(C, S)

    def body(a_ref, g_ref, w_ref, b_ref, o_ref, pad):
        _zero_gaps(pad, C, S)
        wv = w_ref[...]
        for so, po, n in segs:
            def fill(r, so=so, po=po):
                pad[pl.ds(po + r, ch), :] = a_ref[pl.ds(so + r, ch), :] * _sigmoid(g_ref[pl.ds(so + r, ch), :])

            _chunks(n, ch, fill)
        for so, po, n in segs:
            def conv(r, so=so, po=po):
                o_ref[pl.ds(so + r, ch), :] = _taps(pad, wv, po + r, ch, CONV_K) + b_ref[...]

            _chunks(n, ch, conv)

    return _call(
        body, name="conv_fwd", grid=(nj, B),
        in_specs=[pl.BlockSpec((R, cw), lambda j, b: (b, j)), pl.BlockSpec((R, cw), lambda j, b: (b, nj + j)),
                  pl.BlockSpec((32, cw), lambda j, b: (0, j)), pl.BlockSpec((1, cw), lambda j, b: (0, j))],
        out_specs=pl.BlockSpec((R, cw), lambda j, b: (b, j)), out_shape=_sds((B * R, CONV_CH), F32),
        scratch=[pltpu.VMEM((_padded_rows(C, S), cw), F32)])(z_conv, z_conv, w, bias)


def _conv_bwd(dhc, z_conv, w, B, C, S, cw=256, ch=64):
    R = C + S
    nj = CONV_CH // cw
    segs = _segments(C, S)

    def body(d_ref, a_ref, g_ref, w_ref, da_ref, dg_ref, dw_ref, sa_ref, sg_ref, gpad, dpad, dwacc):
        b = pl.program_id(1)

        @pl.when(b == 0)
        def _():
            dw_ref[...] = jnp.zeros_like(dw_ref)
            sa_ref[...] = jnp.zeros_like(sa_ref)
            sg_ref[...] = jnp.zeros_like(sg_ref)

        _zero_gaps(gpad, C, S)
        _zero_gaps(dpad, C, S)
        dwacc[...] = jnp.zeros_like(dwacc)
        wv = w_ref[...]
        for so, po, n in segs:
            def fill(r, so=so, po=po):
                gpad[pl.ds(po + r, ch), :] = a_ref[pl.ds(so + r, ch), :] * _sigmoid(g_ref[pl.ds(so + r, ch), :])
                dpad[pl.ds(po + r, ch), :] = d_ref[pl.ds(so + r, ch), :]

            _chunks(n, ch, fill)
        for so, po, n in segs:
            def step(r, so=so, po=po):
                _tap_grads(dwacc, dpad[pl.ds(po + r, ch), :], gpad, po + r, ch, CONV_K)
                dglu = _taps(dpad, wv, po + r, ch, CONV_K, flip=True)
                av = a_ref[pl.ds(so + r, ch), :]
                sg = _sigmoid(g_ref[pl.ds(so + r, ch), :])
                da = dglu * sg
                dg = dglu * av * sg * (1.0 - sg)
                da_ref[pl.ds(so + r, ch), :] = da.astype(MXU)
                dg_ref[pl.ds(so + r, ch), :] = dg.astype(MXU)
                sa_ref[...] += _colsum(da)
                sg_ref[...] += _colsum(dg)

            _chunks(n, ch, step)
        for k in range(CONV_K):
            dw_ref[k:k + 1, :] += _colsum(dwacc[k])

    blk = pl.BlockSpec((R, cw), lambda j, b: (b, j))
    acc1 = pl.BlockSpec((1, cw), lambda j, b: (0, j))
    return _call(
        body, name="conv_bwd", grid=(nj, B),
        in_specs=[blk, blk, pl.BlockSpec((R, cw), lambda j, b: (b, nj + j)), pl.BlockSpec((32, cw), lambda j, b: (0, j))],
        out_specs=[blk, blk, pl.BlockSpec((32, cw), lambda j, b: (0, j)), acc1, acc1],
        out_shape=[_sds((B * R, CONV_CH), MXU), _sds((B * R, CONV_CH), MXU), _sds((32, CONV_CH), F32),
                   _sds((1, CONV_CH), F32), _sds((1, CONV_CH), F32)],
        scratch=[pltpu.VMEM((_padded_rows(C, S), cw), F32), pltpu.VMEM((_padded_rows(C, S), cw), F32),
                 pltpu.VMEM((32, 8, cw), F32)])(dhc, z_conv, z_conv, w)


def _ffn_mid(up, w, bias, B, C, S, cw=256, ch=64):
    R = C + S
    nj = D_FF // cw
    segs = _segments(C, S)

    def body(a_ref, u_ref, w_ref, b_ref, f_ref, ft_ref, pad):
        _zero_gaps(pad, C, S)
        wv = w_ref[...]
        for so, po, n in segs:
            def fill(r, so=so, po=po):
                pad[pl.ds(po + r, ch), :] = a_ref[pl.ds(so + r, ch), :]

            _chunks(n, ch, fill)
        for so, po, n in segs:
            def conv(r, so=so, po=po):
                ac = _taps(pad, wv, po + r, ch, FFN_K) + b_ref[...]
                f_ref[pl.ds(so + r, ch), :] = (ac * _sigmoid(ac) * u_ref[pl.ds(so + r, ch), :]).astype(MXU)

            _chunks(n, ch, conv)
        ft_ref[...] = f_ref[...].T

    return _call(
        body, name="ffn_mid", grid=(nj, B),
        in_specs=[pl.BlockSpec((R, cw), lambda j, b: (b, j)), pl.BlockSpec((R, cw), lambda j, b: (b, nj + j)),
                  pl.BlockSpec((8, cw), lambda j, b: (0, j)), pl.BlockSpec((1, cw), lambda j, b: (0, j))],
        out_specs=[pl.BlockSpec((R, cw), lambda j, b: (b, j)), pl.BlockSpec((cw, R), lambda j, b: (j, b))],
        out_shape=[_sds((B * R, D_FF), MXU), _sds((D_FF, B * R), MXU)],
        scratch=[pltpu.VMEM((_padded_rows(C, S), cw), F32)])(up, up, w, bias)


def _ffn_mid_bwd(df, up, w, bias, B, C, S, cw=256, ch=64):
    R = C + S
    nj = D_FF // cw
    segs = _segments(C, S)

    def body(d_ref, a_ref, u_ref, w_ref, b_ref, da_ref, du_ref, dw_ref, db_ref, apad, dpad, dwacc):
        b = pl.program_id(1)

        @pl.when(b == 0)
        def _():
            dw_ref[...] = jnp.zeros_like(dw_ref)
            db_ref[...] = jnp.zeros_like(db_ref)

        _zero_gaps(apad, C, S)
        _zero_gaps(dpad, C, S)
        dwacc[...] = jnp.zeros_like(dwacc)
        wv = w_ref[...]
        for so, po, n in segs:
            def fill(r, so=so, po=po):
                apad[pl.ds(po + r, ch), :] = a_ref[pl.ds(so + r, ch), :]

            _chunks(n, ch, fill)
        for so, po, n in segs:
            def first(r, so=so, po=po):
                ac = _taps(apad, wv, po + r, ch, FFN_K) + b_ref[...]
                sg = _sigmoid(ac)
                dfv = d_ref[pl.ds(so + r, ch), :]
                du_ref[pl.ds(so + r, ch), :] = (dfv * ac * sg).astype(MXU)
                dac = dfv * u_ref[pl.ds(so + r, ch), :] * (sg * (1.0 + ac * (1.0 - sg)))
                dpad[pl.ds(po + r, ch), :] = dac
                db_ref[...] += _colsum(dac)

            _chunks(n, ch, first)
        for so, po, n in segs:
            def second(r, so=so, po=po):
                _tap_grads(dwacc, dpad[pl.ds(po + r, ch), :], apad, po + r, ch, FFN_K)
                da_ref[pl.ds(so + r, ch), :] = _taps(dpad, wv, po + r, ch, FFN_K, flip=True).astype(MXU)

            _chunks(n, ch, second)
        for k in range(FFN_K):
            dw_ref[k:k + 1, :] += _colsum(dwacc[k])

    blk = pl.BlockSpec((R, cw), lambda j, b: (b, j))
    return _call(
        body, name="ffn_mid_bwd", grid=(nj, B),
        in_specs=[blk, blk, pl.BlockSpec((R, cw), lambda j, b: (b, nj + j)), pl.BlockSpec((8, cw), lambda j, b: (0, j)),
                  pl.BlockSpec((1, cw), lambda j, b: (0, j))],
        out_specs=[blk, blk, pl.BlockSpec((8, cw), lambda j, b: (0, j)), pl.BlockSpec((1, cw), lambda j, b: (0, j))],
        out_shape=[_sds((B * R, D_FF), MXU), _sds((B * R, D_FF), MXU), _sds((8, D_FF), F32), _sds((1, D_FF), F32)],
        scratch=[pltpu.VMEM((_padded_rows(C, S), cw), F32), pltpu.VMEM((_padded_rows(C, S), cw), F32),
                 pltpu.VMEM((8, 8, cw), F32)])(df, up, up, w, bias)


def _window_count(r, ch, n, w):
    t = r + lax.broadcasted_iota(jnp.int32, (ch, 1), 0)
    return (jnp.minimum(t + w // 2, n) - jnp.maximum(t - w // 2, 0)).astype(F32)


def _pool_fwd(z_pool, pool_w, pool_scale, B, C, S, ch=128):
    R = C + S
    gch = POOL_GCH
    segs = _segments(C, S)

    def body(u_ref, pw_ref, sc_ref, pooled_ref, po_ref, pad):
        g = pl.program_id(1)
        _zero_gaps(pad, C, S)
        for so, po, n in segs:
            def fill(r, so=so, po=po):
                pad[pl.ds(po + r, ch), :] = u_ref[pl.ds(so + r, ch), :]

            _chunks(n, ch, fill)
        for gi, w in enumerate(POOL_WINDOWS):
            @pl.when(g == gi)
            def _(w=w):
                for so, po, n in segs:
                    def step(r, so=so, po=po, n=n):
                        win = _window(pad, po + r, ch)
                        acc = _shifted(win, -(w // 2), ch)
                        for o in range(1 - w // 2, w // 2):
                            acc = acc + _shifted(win, o, ch)
                        pooled = (acc / _window_count(r, ch, n, w) - _shifted(win, 0, ch)).astype(MXU)
                        pooled_ref[pl.ds(so + r, ch), :] = pooled
                        po_ref[pl.ds(so + r, ch), :] = _dot(pooled, pw_ref[0], NN) * sc_ref[...]

                    _chunks(n, ch, step)

    blk = pl.BlockSpec((R, gch), lambda b, g: (b, g))
    return _call(
        body, name="pool_fwd", grid=(B, len(POOL_WINDOWS)),
        in_specs=[blk, pl.BlockSpec((1, gch, gch), lambda b, g: (g, 0, 0)), pl.BlockSpec((1, gch), lambda b, g: (0, g))],
        out_specs=[blk, blk], out_shape=[_sds((B * R, POOL_CH), MXU), _sds((B * R, POOL_CH), F32)],
        scratch=[pltpu.VMEM((_padded_rows(C, S), gch), F32)])(z_pool, pool_w, pool_scale)


def _pool_bwd(dpo, pooled, pool_w, pool_scale, B, C, S, ch=128):
    R = C + S
    gch = POOL_GCH
    segs = _segments(C, S)

    def body(d_ref, p_ref, pw_ref, sc_ref, du_ref, dpw_ref, dsc_ref, su_ref, qpad, dpl):
        g, b = pl.program_id(0), pl.program_id(1)

        @pl.when(b == 0)
        def _():
            dpw_ref[...] = jnp.zeros_like(dpw_ref)
            dsc_ref[...] = jnp.zeros_like(dsc_ref)
            su_ref[...] = jnp.zeros_like(su_ref)

        _zero_gaps(qpad, C, S)
        pw = pw_ref[0]
        for gi, w in enumerate(POOL_WINDOWS):
            @pl.when(g == gi)
            def _(w=w):
                for so, po, n in segs:
                    def first(r, so=so, po=po, n=n):
                        pv = p_ref[pl.ds(so + r, ch), :]
                        dv = d_ref[pl.ds(so + r, ch), :]
                        dsc_ref[...] += _colsum(dv * _dot(pv, pw, NN))
                        dmx = (dv * sc_ref[...]).astype(MXU)
                        dpw_ref[0] += _dot(pv, dmx, TN)
                        dp = _dot(dmx, pw, NT)
                        dpl[pl.ds(so + r, ch), :] = dp
                        qpad[pl.ds(po + r, ch), :] = dp / _window_count(r, ch, n, w)

                    _chunks(n, ch, first)
                for so, po, n in segs:
                    def second(r, so=so, po=po):
                        win = _window(qpad, po + r, ch)
                        acc = _shifted(win, 1 - w // 2, ch)
                        for o in range(2 - w // 2, w // 2 + 1):
                            acc = acc + _shifted(win, o, ch)
                        du = acc - dpl[pl.ds(so + r, ch), :]
                        du_ref[pl.ds(so + r, ch), :] = du.astype(MXU)
                        su_ref[...] += _colsum(du)

                    _chunks(n, ch, second)

    blk = pl.BlockSpec((R, gch), lambda g, b: (b, g))
    vec = pl.BlockSpec((1, gch), lambda g, b: (0, g))
    wblk = pl.BlockSpec((1, gch, gch), lambda g, b: (g, 0, 0))
    return _call(
        body, name="pool_bwd", grid=(len(POOL_WINDOWS), B), in_specs=[blk, blk, wblk, vec], out_specs=[blk, wblk, vec, vec],
        out_shape=[_sds((B * R, POOL_CH), MXU), _sds((len(POOL_WINDOWS), gch, gch), F32), _sds((1, POOL_CH), F32),
                   _sds((1, POOL_CH), F32)],
        scratch=[pltpu.VMEM((_padded_rows(C, S), gch), F32), pltpu.VMEM((R, gch), F32)])(dpo, pooled, pool_w, pool_scale)


def _silu_rows(cond):
    def body(c_ref, s_ref, d_ref):
        c = c_ref[...]
        sg = _sigmoid(c)
        s_ref[...] = (c * sg).astype(MXU)
        d_ref[...] = sg * (1.0 + c * (1.0 - sg))

    full = pl.BlockSpec(cond.shape, lambda i: (0, 0))
    return _call(body, name="silu_rows", grid=(1,), in_specs=[full], out_specs=[full, full],
                 out_shape=[_sds(cond.shape, MXU), _sds(cond.shape, F32)])(cond)


def _row_tile(rows, cols, n_bufs):
    cap = max(16, (16 * 2 ** 20) // (4 * n_bufs * max(cols, 128)))
    return rows if rows <= cap else _pick(rows, cap, 16)


def _adamw(parts, w, m, v, layer, prev, *, name):
    n_parts, rows, cols = parts.shape
    layers = w.shape[0]
    c1 = 1.0 - ADAM_B1 ** ADAM_STEP
    c2 = 1.0 - ADAM_B2 ** ADAM_STEP
    tr = _row_tile(rows, cols, n_parts + 7)

    def body(p_ref, w_ref, m_ref, v_ref, *rest):
        g_ref, d_ref, nm_ref, nv_ref = rest[-4:]
        g = p_ref[0].astype(F32)
        for k in range(1, n_parts):
            g = g + p_ref[k].astype(F32)
        nm = ADAM_B1 * m_ref[...] + (1.0 - ADAM_B1) * g
        nv = ADAM_B2 * v_ref[...] + (1.0 - ADAM_B2) * (g * g)
        g_ref[...] = g
        nm_ref[...] = nm
        nv_ref[...] = nv
        d_ref[...] = -ADAM_LR * ((nm / c1) / (jnp.sqrt(nv / c2) + ADAM_EPS) + ADAM_WD * w_ref[...])

    blk = pl.BlockSpec((None, tr, cols), lambda i: (layer, i, 0))
    in_specs = [pl.BlockSpec((n_parts, tr, cols), lambda i: (0, i, 0)), blk, blk, blk]
    args = [parts, w, m, v]
    aliases = {}
    if prev is not None:
        in_specs += [ANY] * 4
        aliases = {4 + k: k for k in range(4)}
        args += list(prev)
    return _call(body, name=name, grid=(rows // tr,), in_specs=in_specs, out_specs=[blk] * 4,
                 out_shape=[_sds((layers, rows, cols), F32)] * 4, aliases=aliases)(*args)


def _pair_sum(g, r1):
    _, rows, cols = g.shape
    c = lax.axis_index("c")
    g4 = g.reshape(4, 2, rows, cols)
    tr = _row_tile(rows, cols, 10)

    def body(c_ref, g_ref, r_ref, o_ref):
        o_ref[...] = (g_ref[...] + r_ref[...]).astype(WIRE)

    return _pcall(
        body, name="pair_sum", out_shape=_sds((4, rows, cols), WIRE),
        grid_spec=pltpu.PrefetchScalarGridSpec(
            num_scalar_prefetch=1, grid=(rows // tr,),
            in_specs=[pl.BlockSpec((4, None, tr, cols), lambda i, c_ref: (0, c_ref[0], i, 0)),
                      pl.BlockSpec((4, tr, cols), lambda i, c_ref: (0, i, 0))],
            out_specs=pl.BlockSpec((4, tr, cols), lambda i, c_ref: (0, i, 0))),
        compiler_params=pltpu.CompilerParams(dimension_semantics=("arbitrary",), vmem_limit_bytes=VMEM_MB * 2 ** 20),
    )(jnp.reshape(c, (1,)).astype(jnp.int32), g4, r1)


MESH = pl.DeviceIdType.MESH
ANY = pl.BlockSpec(memory_space=pl.ANY)


def _all_gather(shards, *, name, layer=None):
    n_t = len(shards)
    blks = [s.shape if layer is None else s.shape[1:] for s in shards]

    def body(*refs):
        x_refs, out_refs = refs[:n_t], refs[n_t:2 * n_t]
        send_sems, recv_sems, local_sems = refs[2 * n_t:]
        x, y, c = lax.axis_index("x"), lax.axis_index("y"), lax.axis_index("c")
        me, sibling = (x, y, c), (x, y, 1 - c)
        chips = [(1 - x, y), (x, 1 - y), (1 - x, 1 - y)]

        def copy(n, k, blk, to, own=False):
            dst = out_refs[n].at[4 * blk[0] + 2 * blk[1] + blk[2]]
            src = dst if not own else (x_refs[n] if layer is None else x_refs[n].at[layer])
            return pltpu.make_async_remote_copy(src_ref=src, dst_ref=dst, send_sem=send_sems.at[n * 7 + k],
                                                recv_sem=recv_sems.at[n * 7 + k], device_id=to, device_id_type=MESH)

        mine = [pltpu.make_async_copy(x_refs[n] if layer is None else x_refs[n].at[layer], out_refs[n].at[4 * x + 2 * y + c],
                                      local_sems.at[n]) for n in range(n_t)]
        for cp in mine:
            cp.start()
        first = [copy(n, 1 + j, me, (*chip, c), own=True) for j, chip in enumerate(chips) for n in range(n_t)]
        first += [copy(n, 0, me, sibling, own=True) for n in range(n_t)]
        for cp in first:
            cp.start()
        passed = []
        for j, chip in enumerate(chips):
            for n in range(n_t):
                copy(n, 1 + j, (*chip, c), me).wait_recv()
                passed.append(copy(n, 4 + j, (*chip, c), sibling))
                passed[-1].start()
        for n in range(n_t):
            copy(n, 0, sibling, me).wait_recv()
        for j, chip in enumerate(chips):
            for n in range(n_t):
                copy(n, 4 + j, (*chip, 1 - c), me).wait_recv()
        for cp in first + passed:
            cp.wait_send()
        for cp in mine:
            cp.wait()

    return _pcall(
        body, name=name, out_shape=[_sds((N_DEV,) + tuple(b), s.dtype) for b, s in zip(blks, shards)],
        in_specs=[ANY] * n_t, out_specs=[ANY] * n_t,
        scratch_shapes=[pltpu.SemaphoreType.DMA((7 * n_t,)), pltpu.SemaphoreType.DMA((7 * n_t,)), pltpu.SemaphoreType.DMA((n_t,))],
    )(*shards)


def _sibling_exchange(gs):
    n_t = len(gs)

    def body(*refs):
        g_refs, r_refs = refs[:n_t], refs[n_t:2 * n_t]
        send_sems, recv_sems = refs[2 * n_t:]
        x, y, c = lax.axis_index("x"), lax.axis_index("y"), lax.axis_index("c")
        copies = [pltpu.make_async_remote_copy(
            src_ref=g_refs[n].at[:, 1 - c], dst_ref=r_refs[n], send_sem=send_sems.at[n], recv_sem=recv_sems.at[n],
            device_id=(x, y, 1 - c), device_id_type=MESH) for n in range(n_t)]
        for cp in copies:
            cp.start()
        for cp in copies:
            cp.wait_recv()
        for cp in copies:
            cp.wait_send()

    return _pcall(
        body, name="sibling_exchange", out_shape=[_sds((4,) + g.shape[1:], g.dtype) for g in gs],
        in_specs=[ANY] * n_t, out_specs=[ANY] * n_t,
        scratch_shapes=[pltpu.SemaphoreType.DMA((n_t,)), pltpu.SemaphoreType.DMA((n_t,))],
    )(*[g.reshape((4, 2) + g.shape[1:]) for g in gs])


def _chip_exchange(ps):
    n_t = len(ps)

    def body(*refs):
        p_refs, r_refs = refs[:n_t], refs[n_t:2 * n_t]
        send_sems, recv_sems, local_sems = refs[2 * n_t:]
        x, y, c = lax.axis_index("x"), lax.axis_index("y"), lax.axis_index("c")
        mine = 2 * x + y
        chips = [(1 - x, y), (x, 1 - y), (1 - x, 1 - y)]
        own = [pltpu.make_async_copy(p_refs[n].at[mine], r_refs[n].at[mine], local_sems.at[n]) for n in range(n_t)]
        for cp in own:
            cp.start()

        def copy(n, j, src_chip, dst_slot):
            px, py = chips[j]
            return pltpu.make_async_remote_copy(
                src_ref=p_refs[n].at[src_chip], dst_ref=r_refs[n].at[dst_slot], send_sem=send_sems.at[n * 3 + j],
                recv_sem=recv_sems.at[n * 3 + j], device_id=(px, py, c), device_id_type=MESH)

        sends = [copy(n, j, 2 * px + py, mine) for j, (px, py) in enumerate(chips) for n in range(n_t)]
        for cp in sends:
            cp.start()
        for j, (px, py) in enumerate(chips):
            for n in range(n_t):
                copy(n, j, mine, 2 * px + py).wait_recv()
        for cp in sends:
            cp.wait_send()
        for cp in own:
            cp.wait()

    return _pcall(
        body, name="chip_exchange", out_shape=[_sds(p.shape, p.dtype) for p in ps], in_specs=[ANY] * n_t, out_specs=[ANY] * n_t,
        scratch_shapes=[pltpu.SemaphoreType.DMA((3 * n_t,)), pltpu.SemaphoreType.DMA((3 * n_t,)), pltpu.SemaphoreType.DMA((n_t,))],
    )(*ps)


BIG = (("w_ada", (D, N_MOD * D // N_DEV), 1), ("w_in", (D, D_IN // N_DEV), 1), ("conv_pw_w", (CONV_CH // N_DEV, D), 0),
       ("pool_w", (len(POOL_WINDOWS), POOL_GCH // N_DEV, POOL_GCH), 1), ("w_out", (D // N_DEV, D), 0),
       ("w_up", (D, 2 * D_FF // N_DEV), 1), ("w_down", (D_FF // N_DEV, D), 0))
TAPS = (("conv_dw_w", (CONV_K, CONV_CH // N_DEV), 1), ("ffn_dw_w", (FFN_K, D_FF // N_DEV), 1))
SHARDED = BIG + TAPS
REPLICATED = (("b_ada", N_MOD * D), ("b_in", D_IN), ("q_gain", HD), ("k_gain", HD), ("conv_dw_b", CONV_CH), ("conv_ln_g", CONV_CH),
              ("conv_ln_b", CONV_CH), ("conv_pw_b", D), ("pool_scale", POOL_CH), ("b_out", D), ("ln1_g", D), ("ln1_b", D),
              ("ln2_g", D), ("ln2_b", D), ("ffn_dw_b", D_FF))


def _as_rows(shape):
    return (int(np.prod(shape[:-1])), shape[-1])


def _full_from_blocks(blocks, axis):
    moved = jnp.moveaxis(blocks, 0, axis)
    shape = list(moved.shape)
    shape[axis:axis + 2] = [shape[axis] * shape[axis + 1]]
    return moved.reshape(shape)


def _blocks_from_full(full, axis):
    shape = list(full.shape)
    shape[axis:axis + 1] = [N_DEV, shape[axis] // N_DEV]
    return jnp.moveaxis(full.reshape(shape), axis, 0)


SMALL_N = DEPTH * sum(n for _, n in REPLICATED) + D
SMALL_ROWS = -(-(SMALL_N + 1) // (8 * LANES)) * 8


def _rope_tables(C, S):
    t = np.arange(S)
    inv_freq = ROPE_THETA ** (-np.arange(ROPE_PAIRS, dtype=np.float32) / ROPE_PAIRS)
    row = jnp.asarray((t // GRID_W).astype(np.float32))[:, None] * jnp.asarray(inv_freq, F32)
    col = jnp.asarray((t % GRID_W).astype(np.float32))[:, None] * jnp.asarray(inv_freq, F32)
    cos = jnp.concatenate([jnp.cos(row), jnp.cos(row), jnp.cos(col), jnp.cos(col)], axis=1)
    sin = jnp.concatenate([-jnp.sin(row), jnp.sin(row), -jnp.sin(col), jnp.sin(col)], axis=1)
    cos = jnp.concatenate([jnp.ones((C, HD), F32), cos], axis=0)
    sin = jnp.concatenate([jnp.zeros((C, HD), F32), sin], axis=0)
    return cos, sin


def _segment_sums(parts, B, tpe, ncq):
    p = parts.reshape(B, tpe, D)
    return jnp.concatenate([jnp.sum(p[:, ncq:], axis=1), jnp.sum(p[:, :ncq], axis=(0, 1))[None]], axis=0)


def kernel(x, c, ctx, c_ctx, w_ada, b_ada, w_in, b_in, q_gain, k_gain, conv_dw_w, conv_dw_b, conv_ln_g, conv_ln_b, conv_pw_w, conv_pw_b, pool_w, pool_scale, w_out, b_out, ln1_g, ln1_b, ln2_g, ln2_b, w_up, ffn_dw_w, ffn_dw_b, w_down, loss_target, m_c_ctx, m_w_ada, m_b_ada, m_w_in, m_b_in, m_q_gain, m_k_gain, m_conv_dw_w, m_conv_dw_b, m_conv_ln_g, m_conv_ln_b, m_conv_pw_w, m_conv_pw_b, m_pool_w, m_pool_scale, m_w_out, m_b_out, m_ln1_g, m_ln1_b, m_ln2_g, m_ln2_b, m_w_up, m_ffn_dw_w, m_ffn_dw_b, m_w_down, v_c_ctx, v_w_ada, v_b_ada, v_w_in, v_b_in, v_q_gain, v_k_gain, v_conv_dw_w, v_conv_dw_b, v_conv_ln_g, v_conv_ln_b, v_conv_pw_w, v_conv_pw_b, v_pool_w, v_pool_scale, v_w_out, v_b_out, v_ln1_g, v_ln1_b, v_ln2_g, v_ln2_b, v_w_up, v_ffn_dw_w, v_ffn_dw_b, v_w_down):
    given = dict(locals())
    B, S, _ = x.shape
    C = ctx.shape[1]
    R = C + S
    T = B * R
    tpe, ncq = R // TM, C // TM
    nt = T // TM
    assert S % TM == 0 and C % TM == 0 and B + 1 <= 16

    operands = [given[n].astype(MXU) for n, _, _ in BIG]
    taps = _all_gather([given[n] for n, _, _ in TAPS], name="gather_taps")
    W = []
    for l in range(DEPTH):
        blocks = _all_gather(operands, name="gather_weights", layer=l)
        wl = {n: _full_from_blocks(blk, a) for (n, _, a), blk in zip(BIG, blocks)}
        for (n, _, a), blk in zip(TAPS, taps):
            wl[n] = _full_from_blocks(blk[:, l], a)
        W.append(wl)

    xu = jnp.concatenate([ctx, x], axis=1).reshape(T, D)
    cond = jnp.concatenate([c, c_ctx[None], jnp.zeros((16 - B - 1, D), F32)], axis=0)
    s_cond, ds_cond = _silu_rows(cond)
    ctx_tile = jnp.asarray((np.arange(tpe) < ncq)[None, :, None])
    cos_t, sin_t = _rope_tables(C, S)
    row = lambda v: v.reshape(1, -1)

    mods = [_matmul(s_cond, W[l]["w_ada"], "nn", name="ada", bias=row(b_ada[l]), tm=16, tn=1024) for l in range(DEPTH)]
    modt = [jnp.where(ctx_tile, m[B][None, None, :], m[:B][:, None, :]).reshape(nt, 1, N_MOD * D) for m in mods]

    saved = []
    h1, h1t = _modulate_cast(xu, modt[0], 0)
    xin = xu
    for l in range(DEPTH):
        wl = W[l]
        w_inl, b_inl = wl["w_in"], b_in[l]
        z_qkv = _matmul(h1, w_inl[:, :QKV_W], "nn", name="z_qkv", bias=row(b_inl[:QKV_W]), tn=768)
        c0, p0, g0 = QKV_W, QKV_W + 2 * CONV_CH, QKV_W + 2 * CONV_CH + POOL_CH
        z_conv = _matmul(h1, w_inl[:, c0:p0], "nn", name="z_conv", bias=row(b_inl[c0:p0]))
        z_pool = _matmul(h1, w_inl[:, p0:g0], "nn", name="z_pool", bias=row(b_inl[p0:g0]))
        z_gate = _matmul(h1, w_inl[:, g0:], "nn", name="z_gate", bias=row(b_inl[g0:]))
        qkv = _qk_prep(z_qkv, cos_t, sin_t, row(q_gain[l]), row(k_gain[l]), tpe)
        attn = _attn_fwd(qkv, B, C, R)
        dw32 = jnp.pad(wl["conv_dw_w"], ((0, 32 - CONV_K), (0, 0)))
        hc = _conv_fwd(z_conv, dw32, row(conv_dw_b[l]), B, C, S)
        sw, swt = _ln_silu(hc, row(conv_ln_g[l]), row(conv_ln_b[l]))
        conv_o = _matmul(sw, wl["conv_pw_w"], "nn", name="conv_pw", bias=row(conv_pw_b[l]))
        pooled, pool_o = _pool_fwd(z_pool, wl["pool_w"], row(pool_scale[l]), B, C, S)
        m, mt = _merge(attn, conv_o, pool_o, z_gate)
        mo = _matmul(m, wl["w_out"], "nn", name="w_out", bias=row(b_out[l]))
        y1, h2, h2t = _resid_ln(xin, mo, modt[l], 2, row(ln1_g[l]), row(ln1_b[l]), modt[l], 3)
        up = _matmul(h2, wl["w_up"], "nn", name="w_up", tn=1408)
        fw8 = jnp.pad(wl["ffn_dw_w"], ((0, 8 - FFN_K), (0, 0)))
        f, ft = _ffn_mid(up, fw8, row(ffn_dw_b[l]), B, C, S)
        fo = _matmul(f, wl["w_down"], "nn", name="w_down", tm=512)
        if l + 1 < DEPTH:
            y2, h_next, ht_next = _resid_ln(y1, fo, modt[l], 5, row(ln2_g[l]), row(ln2_b[l]), modt[l + 1], 0)
        else:
            y2, h_next, ht_next = _resid_ln(y1, fo, modt[l], 5, row(ln2_g[l]), row(ln2_b[l])), None, None
        saved.append(dict(xin=xin, h1t=h1t, z_qkv=z_qkv, z_conv=z_conv, z_gate=z_gate, qkv=qkv, attn=attn, hc=hc, swt=swt,
                          conv_o=conv_o, pooled=pooled, pool_o=pool_o, mt=mt, mo=mo, y1=y1, h2t=h2t, up=up, ft=ft, fo=fo,
                          dw32=dw32, fw8=fw8))
        xin, h1, h1t = y2, h_next, ht_next

    dy, loss_part = _loss_grad(xin, loss_target.reshape(B * S, D), tpe, ncq)

    small = {n: [None] * DEPTH for n, _ in REPLICATED}
    d_c_ctx = jnp.zeros((D,), F32)
    dmods_t = [[None] * N_MOD for _ in range(DEPTH)]
    layer_grads = [None] * DEPTH
    dh1 = None
    for l in reversed(range(DEPTH)):
        wl, sv = W[l], saved[l]
        dmod = dmods_t[l]
        if dh1 is None:
            dy1p, dfo, dgate2, dg, db, _ = _ln_bwd(dy, sv["y1"], sv["fo"], modt[l], 5, row(ln2_g[l]))
        else:
            dy1p, dfo, dgate2, dg, db, _, dsh, dsc = _ln_bwd(dy, sv["y1"], sv["fo"], modt[l], 5, row(ln2_g[l]), dh=dh1,
                                                             y=saved[l + 1]["xin"], mod_next=modt[l + 1], k_shift_next=0)
            dmods_t[l + 1][0], dmods_t[l + 1][1] = dsh, dsc
        small["ln2_g"][l], small["ln2_b"][l] = dg[0], db[0]
        dmod[5] = dgate2
        df = _matmul(dfo, wl["w_down"], "nt", name="d_f", tn=1408)
        g_w_down = _matmul(sv["ft"], dfo, "nn", name="g_w_down", tm=1408, tk=512)
        da2, du2, g_fdw, g_fdb = _ffn_mid_bwd(df, sv["up"], sv["fw8"], row(ffn_dw_b[l]), B, C, S)
        small["ffn_dw_b"][l] = g_fdb[0]
        dh2 = _matmul(da2, wl["w_up"][:, :D_FF], "nt", name="d_h2a", tm=512)
        dh2 = _matmul(du2, wl["w_up"][:, D_FF:], "nt", name="d_h2u", tm=512, acc_in=dh2)
        g_w_up = _matmul(sv["h2t"], da2, "nn", name="g_w_up_a", tn=1408, tk=512, into=(2 * D_FF, 0, None))
        g_w_up = _matmul(sv["h2t"], du2, "nn", name="g_w_up_u", tn=1408, tk=512, into=(2 * D_FF, D_FF, g_w_up))
        dxp, dmo, dgate1, dg, db, dbo, dsh, dsc = _ln_bwd(dy1p, sv["xin"], sv["mo"], modt[l], 2, row(ln1_g[l]), dh=dh2,
                                                          y=sv["y1"], mod_next=modt[l], k_shift_next=3)
        small["ln1_g"][l], small["ln1_b"][l], small["b_out"][l] = dg[0], db[0], dbo[0]
        dmod[2], dmod[3], dmod[4] = dgate1, dsh, dsc
        dm = _matmul(dmo, wl["w_out"], "nt", name="d_m")
        g_w_out = _matmul(sv["mt"], dmo, "nn", name="g_w_out", tk=512)
        dattn, dconv_o, dpool_o, dzg, g_pwb, gsum_gate = _merge_bwd(dm, sv["attn"], sv["conv_o"], sv["pool_o"], sv["z_gate"])
        small["conv_pw_b"][l] = g_pwb[0]
        du, g_pool_w, g_pool_sc, gsum_pool = _pool_bwd(dpool_o, sv["pooled"], wl["pool_w"], row(pool_scale[l]), B, C, S)
        small["pool_scale"][l] = g_pool_sc[0]
        dsw = _matmul(dconv_o, wl["conv_pw_w"], "nt", name="d_sw")
        g_pw = _matmul(sv["swt"], dconv_o, "nn", name="g_conv_pw", tk=512)
        dhc, g_cg, g_cb, g_cdb = _ln_silu_bwd(dsw, sv["hc"], row(conv_ln_g[l]), row(conv_ln_b[l]))
        small["conv_ln_g"][l], small["conv_ln_b"][l], small["conv_dw_b"][l] = g_cg[0], g_cb[0], g_cdb[0]
        da, dgt, g_cdw, gsum_a, gsum_gt = _conv_bwd(dhc, sv["z_conv"], sv["dw32"], B, C, S)
        dq, dk, dv = _attn_bwd(sv["qkv"], sv["attn"], dattn, B, C, R)
        dz_qkv, g_qg, g_kg, gsum_qkv = _qk_bwd(dq, dk, dv, sv["z_qkv"], cos_t, sin_t, row(q_gain[l]), row(k_gain[l]), tpe)
        small["q_gain"][l], small["k_gain"][l] = g_qg[0], g_kg[0]
        small["b_in"][l] = jnp.concatenate([gsum_qkv[0], gsum_a[0], gsum_gt[0], gsum_pool[0], gsum_gate[0]])
        w_inl = wl["w_in"]
        g0 = QKV_W + 2 * CONV_CH + POOL_CH
        pieces = ((dzg, g0, N_GATE), (da, QKV_W, CONV_CH), (dgt, QKV_W + CONV_CH, CONV_CH), (du, QKV_W + 2 * CONV_CH, POOL_CH),
                  (dz_qkv, 0, QKV_W))
        dh1 = g_w_in = None
        at = 0
        for k, (dz, c0, wd) in enumerate(pieces):
            dh1 = _matmul(dz, w_inl[:, c0:c0 + wd], "nt", name=f"d_h1_{k}", tm=512, acc_in=dh1)
            g_w_in = _matmul(sv["h1t"], dz, "nn", name=f"g_w_in_{k}", tk=512, into=(D_IN, at, g_w_in))
            at += wd
        g_w_in = jnp.concatenate([g_w_in[:, N_GATE + 2 * CONV_CH + POOL_CH:], g_w_in[:, N_GATE:N_GATE + 2 * CONV_CH + POOL_CH],
                                  g_w_in[:, :N_GATE]], axis=1)

        layer_grads[l] = {"w_in": g_w_in, "conv_pw_w": g_pw, "pool_w": g_pool_w, "w_out": g_w_out, "w_up": g_w_up,
                          "w_down": g_w_down, "conv_dw_w": g_cdw[:CONV_K], "ffn_dw_w": g_fdw[:FFN_K]}
        dy = dxp
    gx_u, dmods_t[0][0], dmods_t[0][1] = _mod_bwd(dy, dh1, saved[0]["xin"], modt[0], 0)
    grad_x = gx_u.reshape(B, R, D)[:, C:]

    kinds = ("grad_", "delta_", "new_m_", "new_v_")
    stacks = {n: [given[pre + n].reshape((DEPTH,) + _as_rows(s)) for pre in ("", "m_", "v_")] for n, s, _ in SHARDED}
    results = {n: None for n, _, _ in SHARDED}
    for l in range(DEPTH):
        dmods = jnp.concatenate([_segment_sums(p, B, tpe, ncq) for p in dmods_t[l]], axis=1)
        small["b_ada"][l] = jnp.sum(dmods, axis=0)
        dm16 = jnp.concatenate([dmods, jnp.zeros((16 - B - 1, N_MOD * D), F32)], axis=0).astype(MXU)
        layer_grads[l]["w_ada"] = _matmul(s_cond, dm16, "tn", name="g_w_ada", tm=1024, tn=1024)
        dcond = _matmul(dm16, W[l]["w_ada"], "nt", name="d_cond", tm=16, tn=1024, tk=2048)
        d_c_ctx = d_c_ctx + dcond[B] * ds_cond[B]
        gs = [_blocks_from_full(layer_grads[l][n], a).reshape((N_DEV,) + _as_rows(s)) for n, s, a in SHARDED]
        r1 = _sibling_exchange(gs)
        r2 = _chip_exchange([_pair_sum(g, r) for g, r in zip(gs, r1)])
        for (n, _, _), parts in zip(SHARDED, r2):
            results[n] = _adamw(parts, *stacks[n], l, results[n], name="adamw_sharded")
    outs = {}
    for n, s, _ in SHARDED:
        for kind, buf in zip(kinds, results[n]):
            outs[kind + n] = buf.reshape((DEPTH,) + tuple(s))

    def small_pack(pieces):
        flat = jnp.concatenate([p.reshape(-1) for p in pieces])
        return jnp.pad(flat, (0, SMALL_ROWS * LANES - flat.shape[0])).reshape(SMALL_ROWS, LANES)

    zero1 = jnp.zeros((1,), F32)
    g_pack = small_pack([small[n][l] for n, _ in REPLICATED for l in range(DEPTH)] + [d_c_ctx, loss_part[0, :1]])
    g_small, = _all_gather([g_pack], name="gather_small")
    wmv = [small_pack([given[pre + n] for n, _ in REPLICATED] + [given[pre + "c_ctx"], zero1])[None] for pre in ("", "m_", "v_")]
    res = _adamw(g_small, *wmv, 0, None, name="adamw_small")
    for kind, buf in zip(kinds, res):
        flat = buf.reshape(-1)
        off = 0
        for n, sz in REPLICATED:
            outs[kind + n] = flat[off:off + DEPTH * sz].reshape(DEPTH, sz)
            off += DEPTH * sz
        outs[kind + "c_ctx"] = flat[off:off + D]
        if kind == "grad_":
            loss = flat[off + D]

    names = ["c_ctx", "w_ada", "b_ada", "w_in", "b_in", "q_gain", "k_gain", "conv_dw_w", "conv_dw_b", "conv_ln_g", "conv_ln_b",
             "conv_pw_w", "conv_pw_b", "pool_w", "pool_scale", "w_out", "b_out", "ln1_g", "ln1_b", "ln2_g", "ln2_b", "w_up",
             "ffn_dw_w", "ffn_dw_b", "w_down"]
    return (loss, grad_x, *[outs[k + n] for k in ("grad_", "delta_", "new_m_", "new_v_") for n in names])
```

```python
import functools

import jax
import jax.numpy as jnp
import numpy as np
from jax import lax
from jax.experimental import pallas as pl
from jax.experimental.pallas import tpu as pltpu

F32 = jnp.float32
MXU = jnp.bfloat16
WIRE = jnp.bfloat16

D = 1024
HD = 128
NH = 8
NKV = 2
QG = NH // NKV
KV_W = NKV * HD
QKV_W = NH * HD + 2 * KV_W
CONV_CH = D
POOL_CH = D
POOL_WINDOWS = (2, 4, 8, 16)
POOL_GCH = POOL_CH // len(POOL_WINDOWS)
N_GATE = 3 * D
D_IN = QKV_W + 2 * CONV_CH + POOL_CH + N_GATE
D_FF = 2816
N_MOD = 6
DEPTH = 4
CONV_K = 31
FFN_K = 3
GRID_W = 64
ROPE_THETA = 10000.0
ROPE_PAIRS = HD // 4
ALPHA = (2 * DEPTH) ** 0.25
LN_EPS = 1e-5
RMS_EPS = 1e-6
ATTN_SCALE = HD ** -0.5
LOG2_E = 1.4426950408889634
ADAM_LR, ADAM_B1, ADAM_B2, ADAM_EPS, ADAM_WD, ADAM_STEP = 0.001, 0.9, 0.999, 1e-08, 0.01, 10

N_DEV = 8
TM = 256
GAP = 16
LANES = 1024
VMEM_MB = 48

NN = (((1,), (0,)), ((), ()))
NT = (((1,), (1,)), ((), ()))
TN = (((0,), (0,)), ((), ()))

_pcall = pl.pallas_call


def _call(body, *, name, grid, in_specs, out_specs, out_shape, scratch=(), aliases=None, vmem=VMEM_MB):
    return _pcall(
        body, name=name, grid=grid, in_specs=in_specs, out_specs=out_specs, out_shape=out_shape,
        scratch_shapes=list(scratch), input_output_aliases=aliases or {},
        compiler_params=pltpu.CompilerParams(dimension_semantics=("arbitrary",) * len(grid), vmem_limit_bytes=vmem * 2 ** 20),
    )


def _sds(shape, dtype):
    return jax.ShapeDtypeStruct(tuple(shape), dtype)


def _pick(n, cap, mult):
    best = None
    for t in range(mult, min(n, cap) + 1, mult):
        if n % t == 0:
            best = t
    return best if best is not None else n


def _dot(a, b, dims):
    return lax.dot_general(a, b, dims, preferred_element_type=F32)


def _sigmoid(x):
    return 1.0 / (1.0 + jnp.exp(-x))


def _matmul(a, b, mode, *, name, bias=None, acc_in=None, into=None, out_dtype=F32, tm=1024, tn=1024, tk=None):
    if mode == "nn":
        (M, K), (K2, N) = a.shape, b.shape
    elif mode == "nt":
        (M, K), (N, K2) = a.shape, b.shape
    else:
        (K, M), (K2, N) = a.shape, b.shape
    assert K == K2, (a.shape, b.shape, mode)
    tm = _pick(M, tm, 16)
    tn = _pick(N, tn, 128)
    tk = K if tk is None else _pick(K, tk, 128 if mode != "tn" else 16)
    gk = K // tk
    dims = {"nn": NN, "nt": NT, "tn": TN}[mode]
    a_spec = pl.BlockSpec((tk, tm), lambda j, i, k: (k, i)) if mode == "tn" else pl.BlockSpec((tm, tk), lambda j, i, k: (i, k))
    b_spec = pl.BlockSpec((tn, tk), lambda j, i, k: (j, k)) if mode == "nt" else pl.BlockSpec((tk, tn), lambda j, i, k: (k, j))
    in_specs, args = [a_spec, b_spec], [a, b]
    if bias is not None:
        in_specs.append(pl.BlockSpec((1, tn), lambda j, i, k: (0, j)))
        args.append(bias)
    aliases = {}
    if acc_in is not None:
        aliases = {len(args): 0}
        in_specs.append(pl.BlockSpec((tm, tn), lambda j, i, k: (i, j)))
        args.append(acc_in)
    n_total, col0, prev = (N, 0, None) if into is None else into
    assert col0 % tn == 0
    jb = col0 // tn
    if prev is not None:
        aliases = {len(args): 0}
        in_specs.append(pl.BlockSpec(memory_space=pl.ANY))
        args.append(prev)
    n_in = len(args)

    def body(*refs):
        a_ref, b_ref = refs[0], refs[1]
        pos = 2
        bias_ref = acc_in_ref = None
        if bias is not None:
            bias_ref = refs[pos]
            pos += 1
        if acc_in is not None:
            acc_in_ref = refs[pos]
        pos = n_in
        o_ref = refs[pos]
        part = _dot(a_ref[...].astype(MXU), b_ref[...].astype(MXU), dims)

        def finish(acc):
            if bias_ref is not None:
                acc = acc + bias_ref[...]
            if acc_in_ref is not None:
                acc = acc + acc_in_ref[...]
            o_ref[...] = acc.astype(out_dtype)

        if gk == 1:
            finish(part)
        else:
            acc_ref = refs[pos + 1]
            k = pl.program_id(2)

            @pl.when(k == 0)
            def _():
                acc_ref[...] = part

            @pl.when(k > 0)
            def _():
                acc_ref[...] += part

            @pl.when(k == gk - 1)
            def _():
                finish(acc_ref[...])

    return _call(
        body, name=name, grid=(N // tn, M // tm, gk), in_specs=in_specs,
        out_specs=pl.BlockSpec((tm, tn), lambda j, i, k: (i, j + jb)), out_shape=_sds((M, n_total), out_dtype),
        scratch=[pltpu.VMEM((tm, tn), F32)] if gk > 1 else [], aliases=aliases,
    )(*args)


def _rt(w, cb=0):
    return pl.BlockSpec((TM, w), lambda i: (i, cb))


def _ct(w):
    return pl.BlockSpec((w, TM), lambda i: (0, i))


def _vec(w):
    return pl.BlockSpec((1, w), lambda i: (0, 0))


def _part(w):
    return pl.BlockSpec((1, 1, w), lambda i: (i, 0, 0))


def _mod(ref, k):
    return ref[0, :, k * D:(k + 1) * D]


def _colsum(x):
    return jnp.sum(x, axis=0, keepdims=True)


def _ln_stats(s):
    mu = jnp.mean(s, axis=1, keepdims=True)
    cen = s - mu
    var = jnp.mean(cen * cen, axis=1, keepdims=True)
    rstd = lax.rsqrt(var + LN_EPS)
    return cen * rstd, rstd


def _modulate_cast(x, modt, k_shift):
    T = x.shape[0]
    nt = T // TM

    def body(x_ref, mod_ref, h_ref, ht_ref):
        h = (x_ref[...] * (1.0 + _mod(mod_ref, k_shift + 1)) + _mod(mod_ref, k_shift)).astype(MXU)
        h_ref[...] = h
        ht_ref[...] = h.T

    return _call(body, name="modulate", grid=(nt,), in_specs=[_rt(D), _part(N_MOD * D)], out_specs=[_rt(D), _ct(D)],
                 out_shape=[_sds((T, D), MXU), _sds((D, T), MXU)])(x, modt)


def _resid_ln(x, br, modt, k_gate, g, b, mod_next=None, k_shift_next=0):
    T = x.shape[0]
    nt = T // TM
    with_h = mod_next is not None

    def body(*refs):
        x_ref, br_ref, mod_ref, g_ref, b_ref = refs[:5]
        s = ALPHA * x_ref[...] + _mod(mod_ref, k_gate) * br_ref[...]
        xhat, _ = _ln_stats(s)
        y = xhat * g_ref[...] + b_ref[...]
        if with_h:
            modn_ref, y_ref, h_ref, ht_ref = refs[5:]
            y_ref[...] = y
            h = (y * (1.0 + _mod(modn_ref, k_shift_next + 1)) + _mod(modn_ref, k_shift_next)).astype(MXU)
            h_ref[...] = h
            ht_ref[...] = h.T
        else:
            refs[5][...] = y

    in_specs = [_rt(D), _rt(D), _part(N_MOD * D), _vec(D), _vec(D)]
    args = [x, br, modt, g, b]
    if with_h:
        in_specs.append(_part(N_MOD * D))
        args.append(mod_next)
        return _call(body, name="resid_ln_mod", grid=(nt,), in_specs=in_specs, out_specs=[_rt(D), _rt(D), _ct(D)],
                     out_shape=[_sds((T, D), F32), _sds((T, D), MXU), _sds((D, T), MXU)])(*args)
    return _call(body, name="resid_ln", grid=(nt,), in_specs=in_specs, out_specs=_rt(D), out_shape=_sds((T, D), F32))(*args)


def _ln_bwd(dy_part, x, br, modt, k_gate, g, dh=None, y=None, mod_next=None, k_shift_next=0):
    T = x.shape[0]
    nt = T // TM
    with_h = dh is not None

    def body(*refs):
        if with_h:
            dyp_ref, x_ref, br_ref, mod_ref, g_ref, dh_ref, y_ref, modn_ref = refs[:8]
            outs = refs[8:]
        else:
            dyp_ref, x_ref, br_ref, mod_ref, g_ref = refs[:5]
            outs = refs[5:]
        dx_ref, dbr_ref, dgate_ref, dlg_ref, dlb_ref, dbsum_ref = outs[:6]
        i = pl.program_id(0)

        @pl.when(i == 0)
        def _():
            dlg_ref[...] = jnp.zeros_like(dlg_ref)
            dlb_ref[...] = jnp.zeros_like(dlb_ref)
            dbsum_ref[...] = jnp.zeros_like(dbsum_ref)

        dy = dyp_ref[...]
        if with_h:
            dshift_ref, dscale_ref = outs[6:]
            dhv = dh_ref[...]
            dy = dy + dhv * (1.0 + _mod(modn_ref, k_shift_next + 1))
            dshift_ref[0] = _colsum(dhv)
            dscale_ref[0] = _colsum(dhv * y_ref[...])
        gate = _mod(mod_ref, k_gate)
        brv = br_ref[...]
        s = ALPHA * x_ref[...] + gate * brv
        xhat, rstd = _ln_stats(s)
        dlg_ref[...] += _colsum(dy * xhat)
        dlb_ref[...] += _colsum(dy)
        dyg = dy * g_ref[...]
        m1 = jnp.mean(dyg, axis=1, keepdims=True)
        m2 = jnp.mean(dyg * xhat, axis=1, keepdims=True)
        ds = rstd * (dyg - m1 - xhat * m2)
        dx_ref[...] = ALPHA * ds
        dbr = gate * ds
        dbr_ref[...] = dbr.astype(MXU)
        dbsum_ref[...] += _colsum(dbr)
        dgate_ref[0] = _colsum(ds * brv)

    in_specs = [_rt(D), _rt(D), _rt(D), _part(N_MOD * D), _vec(D)]
    args = [dy_part, x, br, modt, g]
    out_specs = [_rt(D), _rt(D), _part(D), _vec(D), _vec(D), _vec(D)]
    out_shape = [_sds((T, D), F32), _sds((T, D), MXU), _sds((nt, 1, D), F32), _sds((1, D), F32), _sds((1, D), F32), _sds((1, D), F32)]
    if with_h:
        in_specs += [_rt(D), _rt(D), _part(N_MOD * D)]
        args += [dh, y, mod_next]
        out_specs += [_part(D), _part(D)]
        out_shape += [_sds((nt, 1, D), F32), _sds((nt, 1, D), F32)]
    return _call(body, name="ln_bwd_mod" if with_h else "ln_bwd", grid=(nt,), in_specs=in_specs, out_specs=out_specs,
                 out_shape=out_shape)(*args)


def _mod_bwd(dx_part, dh, x, modt, k_shift):
    T = x.shape[0]
    nt = T // TM

    def body(dxp_ref, dh_ref, x_ref, mod_ref, dx_ref, dshift_ref, dscale_ref):
        dhv = dh_ref[...]
        dx_ref[...] = dxp_ref[...] + dhv * (1.0 + _mod(mod_ref, k_shift + 1))
        dshift_ref[0] = _colsum(dhv)
        dscale_ref[0] = _colsum(dhv * x_ref[...])

    return _call(body, name="mod_bwd", grid=(nt,), in_specs=[_rt(D), _rt(D), _rt(D), _part(N_MOD * D)],
                 out_specs=[_rt(D), _part(D), _part(D)],
                 out_shape=[_sds((T, D), F32), _sds((nt, 1, D), F32), _sds((nt, 1, D), F32)])(dx_part, dh, x, modt)


def _loss_grad(y, target, tpe, ncq):
    T = y.shape[0]
    nt = T // TM
    nl = tpe - ncq

    def body(y_ref, t_ref, dy_ref, loss_ref):
        i = pl.program_id(0)

        @pl.when(i == 0)
        def _():
            loss_ref[...] = jnp.zeros_like(loss_ref)

        @pl.when(i % tpe < ncq)
        def _():
            dy_ref[...] = jnp.zeros_like(dy_ref)

        @pl.when(i % tpe >= ncq)
        def _():
            err = y_ref[...] - t_ref[...]
            dy_ref[...] = err * (1.0 / D)
            loss_ref[...] += (0.5 / D) * jnp.sum(_colsum(err * err), axis=1, keepdims=True)

    tgt_spec = pl.BlockSpec((TM, D), lambda i: ((i // tpe) * nl + jnp.maximum(i % tpe - ncq, 0), 0))
    return _call(body, name="loss_grad", grid=(nt,), in_specs=[_rt(D), tgt_spec], out_specs=[_rt(D), _vec(128)],
                 out_shape=[_sds((T, D), F32), _sds((1, 128), F32)])(y, target)


def _rope_partner(x):
    lane = lax.broadcasted_iota(jnp.int32, x.shape, 1)
    first = (lane % (2 * ROPE_PAIRS)) < ROPE_PAIRS
    return jnp.where(first, pltpu.roll(x, HD - ROPE_PAIRS, 1), pltpu.roll(x, ROPE_PAIRS, 1))


def _qk_prep(z_qkv, cos_t, sin_t, q_gain, k_gain, tpe):
    T = z_qkv.shape[0]
    nt = T // TM

    def body(z_ref, cos_ref, sin_ref, qg_ref, kg_ref, o_ref):
        cos, sin = cos_ref[...], sin_ref[...]
        for h in range(NH + NKV):
            sl = slice(h * HD, (h + 1) * HD)
            t = z_ref[:, sl]
            gain = qg_ref[...] if h < NH else kg_ref[...]
            n = t * lax.rsqrt(jnp.mean(t * t, axis=1, keepdims=True) + RMS_EPS) * gain
            o_ref[:, sl] = (n * cos + _rope_partner(n) * sin).astype(MXU)
        o_ref[:, (NH + NKV) * HD:] = z_ref[:, (NH + NKV) * HD:].astype(MXU)

    tab = pl.BlockSpec((TM, HD), lambda i: (i % tpe, 0))
    return _call(body, name="qk_prep", grid=(nt,), in_specs=[_rt(QKV_W), tab, tab, _vec(HD), _vec(HD)], out_specs=_rt(QKV_W),
                 out_shape=_sds((T, QKV_W), MXU))(z_qkv, cos_t, sin_t, q_gain, k_gain)


def _qk_bwd(dq, dk, dv, z_qkv, cos_t, sin_t, q_gain, k_gain, tpe):
    T = z_qkv.shape[0]
    nt = T // TM

    def body(dq_ref, dk_ref, dv_ref, z_ref, cos_ref, sin_ref, qg_ref, kg_ref, dz_ref, dqg_ref, dkg_ref, bsum_ref):
        i = pl.program_id(0)

        @pl.when(i == 0)
        def _():
            dqg_ref[...] = jnp.zeros_like(dqg_ref)
            dkg_ref[...] = jnp.zeros_like(dkg_ref)
            bsum_ref[...] = jnp.zeros_like(bsum_ref)

        cos, sin = cos_ref[...], sin_ref[...]
        for h in range(NH + NKV):
            sl = slice(h * HD, (h + 1) * HD)
            dr = dq_ref[:, sl] if h < NH else dk_ref[:, (h - NH) * HD:(h - NH + 1) * HD]
            gain = qg_ref[...] if h < NH else kg_ref[...]
            dn = dr * cos + _rope_partner(dr * sin)
            t = z_ref[:, sl]
            rstd = lax.rsqrt(jnp.mean(t * t, axis=1, keepdims=True) + RMS_EPS)
            that = t * rstd
            dgain = _colsum(dn * that)
            if h < NH:
                dqg_ref[...] += dgain
            else:
                dkg_ref[...] += dgain
            dthat = dn * gain
            dt = rstd * (dthat - that * jnp.mean(dthat * that, axis=1, keepdims=True))
            dz_ref[:, sl] = dt.astype(MXU)
            bsum_ref[:, sl] += _colsum(dt)
        dvv = dv_ref[...]
        dz_ref[:, (NH + NKV) * HD:] = dvv.astype(MXU)
        bsum_ref[:, (NH + NKV) * HD:] += _colsum(dvv)

    tab = pl.BlockSpec((TM, HD), lambda i: (i % tpe, 0))
    return _call(body, name="qk_bwd", grid=(nt,),
                 in_specs=[_rt(NH * HD), _rt(KV_W), _rt(KV_W), _rt(QKV_W), tab, tab, _vec(HD), _vec(HD)],
                 out_specs=[_rt(QKV_W), _vec(HD), _vec(HD), _vec(QKV_W)],
                 out_shape=[_sds((T, QKV_W), MXU), _sds((1, HD), F32), _sds((1, HD), F32), _sds((1, QKV_W), F32)],
                 )(dq, dk, dv, z_qkv, cos_t, sin_t, q_gain, k_gain)


def _ln_silu(hc, g, b):
    T = hc.shape[0]

    def body(h_ref, g_ref, b_ref, o_ref, ot_ref):
        xhat, _ = _ln_stats(h_ref[...])
        n = xhat * g_ref[...] + b_ref[...]
        sw = (n * _sigmoid(n)).astype(MXU)
        o_ref[...] = sw
        ot_ref[...] = sw.T

    return _call(body, name="ln_silu", grid=(T // TM,), in_specs=[_rt(D), _vec(D), _vec(D)], out_specs=[_rt(D), _ct(D)],
                 out_shape=[_sds((T, D), MXU), _sds((D, T), MXU)])(hc, g, b)


def _ln_silu_bwd(dsw, hc, g, b):
    T = hc.shape[0]

    def body(d_ref, h_ref, g_ref, b_ref, dh_ref, dg_ref, db_ref, dcb_ref):
        i = pl.program_id(0)

        @pl.when(i == 0)
        def _():
            dg_ref[...] = jnp.zeros_like(dg_ref)
            db_ref[...] = jnp.zeros_like(db_ref)
            dcb_ref[...] = jnp.zeros_like(dcb_ref)

        xhat, rstd = _ln_stats(h_ref[...])
        n = xhat * g_ref[...] + b_ref[...]
        sg = _sigmoid(n)
        dn = d_ref[...] * (sg * (1.0 + n * (1.0 - sg)))
        dg_ref[...] += _colsum(dn * xhat)
        db_ref[...] += _colsum(dn)
        dng = dn * g_ref[...]
        m1 = jnp.mean(dng, axis=1, keepdims=True)
        m2 = jnp.mean(dng * xhat, axis=1, keepdims=True)
        dh = rstd * (dng - m1 - xhat * m2)
        dh_ref[...] = dh
        dcb_ref[...] += _colsum(dh)

    return _call(body, name="ln_silu_bwd", grid=(T // TM,), in_specs=[_rt(D), _rt(D), _vec(D), _vec(D)],
                 out_specs=[_rt(D), _vec(D), _vec(D), _vec(D)],
                 out_shape=[_sds((T, D), F32)] + [_sds((1, D), F32)] * 3)(dsw, hc, g, b)


def _merge(attn, conv_o, pool_o, z_gate):
    T = attn.shape[0]

    def body(a_ref, c_ref, p_ref, zg_ref, m_ref, mt_ref):
        m = (_sigmoid(zg_ref[:, 0:D]) * a_ref[...] + _sigmoid(zg_ref[:, D:2 * D]) * c_ref[...]
             + _sigmoid(zg_ref[:, 2 * D:3 * D]) * p_ref[...]).astype(MXU)
        m_ref[...] = m
        mt_ref[...] = m.T

    return _call(body, name="merge", grid=(T // TM,), in_specs=[_rt(D), _rt(D), _rt(D), _rt(N_GATE)], out_specs=[_rt(D), _ct(D)],
                 out_shape=[_sds((T, D), MXU), _sds((D, T), MXU)])(attn, conv_o, pool_o, z_gate)


def _merge_bwd(dm, attn, conv_o, pool_o, z_gate):
    T = attn.shape[0]

    def body(dm_ref, a_ref, c_ref, p_ref, zg_ref, da_ref, dc_ref, dp_ref, dzg_ref, dcsum_ref, gsum_ref):
        i = pl.program_id(0)

        @pl.when(i == 0)
        def _():
            dcsum_ref[...] = jnp.zeros_like(dcsum_ref)
            gsum_ref[...] = jnp.zeros_like(gsum_ref)

        dmv = dm_ref[...]
        for k, (br_ref, out_ref) in enumerate(((a_ref, da_ref), (c_ref, dc_ref), (p_ref, dp_ref))):
            gk = _sigmoid(zg_ref[:, k * D:(k + 1) * D])
            dbr = dmv * gk
            out_ref[...] = dbr.astype(out_ref.dtype)
            if k == 1:
                dcsum_ref[...] += _colsum(dbr)
            dzg = dmv * br_ref[...] * gk * (1.0 - gk)
            dzg_ref[:, k * D:(k + 1) * D] = dzg.astype(MXU)
            gsum_ref[:, k * D:(k + 1) * D] += _colsum(dzg)

    return _call(body, name="merge_bwd", grid=(T // TM,), in_specs=[_rt(D), _rt(D), _rt(D), _rt(D), _rt(N_GATE)],
                 out_specs=[_rt(D), _rt(D), _rt(D), _rt(N_GATE), _vec(D), _vec(N_GATE)],
                 out_shape=[_sds((T, D), F32), _sds((T, D), MXU), _sds((T, D), F32), _sds((T, N_GATE), MXU),
                            _sds((1, D), F32), _sds((1, N_GATE), F32)])(dm, attn, conv_o, pool_o, z_gate)


def _softmax_parts(q, k):
    s = _dot(q, k, NT)
    p = jnp.exp2((s - jnp.max(s, axis=1, keepdims=True)) * (ATTN_SCALE * LOG2_E))
    return p, 1.0 / jnp.sum(p, axis=1, keepdims=True)


def _attn_specs(nq):
    q_spec = pl.BlockSpec((TM, QG * HD), lambda b, h, q: (b * nq + q, h))
    k_spec = pl.BlockSpec((nq * TM, HD), lambda b, h, q: (b, NH + h))
    v_spec = pl.BlockSpec((nq * TM, HD), lambda b, h, q: (b, NH + NKV + h))
    return q_spec, k_spec, v_spec


def _attn_fwd(qkv, B, C, R):
    nq, ncq = R // TM, C // TM

    def body(q_ref, k_ref, v_ref, o_ref):
        def attend(L):
            k, v = k_ref[0:L, :], v_ref[0:L, :]
            for i in range(QG):
                sl = slice(i * HD, (i + 1) * HD)
                p, inv_l = _softmax_parts(q_ref[:, sl], k)
                o_ref[:, sl] = _dot(p.astype(MXU), v, NN) * inv_l

        qi = pl.program_id(2)
        pl.when(qi < ncq)(functools.partial(attend, C))
        pl.when(qi >= ncq)(functools.partial(attend, R))

    q_spec, k_spec, v_spec = _attn_specs(nq)
    return _call(body, name="attn_fwd", grid=(B, NKV, nq), in_specs=[q_spec, k_spec, v_spec], out_specs=q_spec,
                 out_shape=_sds((B * R, NH * HD), F32))(qkv, qkv, qkv)


def _attn_bwd(qkv, o, do, B, C, R):
    nq, ncq = R // TM, C // TM

    def body(q_ref, k_ref, v_ref, o_ref, do_ref, dq_ref, dk_ref, dv_ref, dkt, dvt):
        qi = pl.program_id(2)

        @pl.when(qi == 0)
        def _():
            dkt[...] = jnp.zeros_like(dkt)
            dvt[...] = jnp.zeros_like(dvt)

        def bwd(L):
            k, v = k_ref[0:L, :], v_ref[0:L, :]
            for i in range(QG):
                sl = slice(i * HD, (i + 1) * HD)
                q = q_ref[:, sl]
                p, inv_l = _softmax_parts(q, k)
                dov = do_ref[:, sl]
                dp = _dot(dov.astype(MXU), v, NT)
                dl = jnp.sum(dov * o_ref[:, sl], axis=1, keepdims=True)
                ds = (p * ((dp - dl) * (inv_l * ATTN_SCALE))).astype(MXU)
                dq_ref[:, sl] = _dot(ds, k, NN)
                dkt[:, 0:L] += _dot(q, ds, TN)
                dvt[:, 0:L] += _dot((dov * inv_l).astype(MXU), p.astype(MXU), TN)

        pl.when(qi < ncq)(functools.partial(bwd, C))
        pl.when(qi >= ncq)(functools.partial(bwd, R))

        @pl.when(qi == nq - 1)
        def _():
            dk_ref[...] = dkt[...].T
            dv_ref[...] = dvt[...].T

    q_spec, k_spec, v_spec = _attn_specs(nq)
    kv_out = pl.BlockSpec((R, HD), lambda b, h, q: (b, h))
    return _call(body, name="attn_bwd", grid=(B, NKV, nq), in_specs=[q_spec, k_spec, v_spec, q_spec, q_spec],
                 out_specs=[q_spec, kv_out, kv_out],
                 out_shape=[_sds((B * R, NH * HD), F32), _sds((B * R, KV_W), F32), _sds((B * R, KV_W), F32)],
                 scratch=[pltpu.VMEM((HD, R), F32), pltpu.VMEM((HD, R), F32)])(qkv, qkv, qkv, o, do)


def _segments(C, S):
    return ((0, GAP, C), (C, 2 * GAP + C, S))


def _padded_rows(C, S):
    return 3 * GAP + C + S


def _zero_gaps(pad_ref, C, S):
    for off in (0, GAP + C, 2 * GAP + C + S):
        pad_ref[off:off + GAP, :] = jnp.zeros((GAP, pad_ref.shape[1]), pad_ref.dtype)


def _chunks(n, ch, fn):
    def step(i, carry):
        fn(pl.multiple_of(i * ch, ch))
        return carry

    lax.fori_loop(0, n // ch, step, 0)


class _Window:
    def __init__(self, pad_ref, row, ch, moved_ref=None):
        self.pad_ref, self.row, self.ch, self.moved_ref = pad_ref, row, ch, moved_ref
        self.win = pad_ref[pl.ds(row - GAP, ch + 2 * GAP), :]
        self.done = set()

    def at(self, off):
        if self.moved_ref is None:
            return self.win[GAP + off:GAP + off + self.ch, :]
        s, q = off % 8, off // 8
        if s == 0:
            return self.pad_ref[pl.ds(pl.multiple_of(self.row + off, 8), self.ch), :]
        if s not in self.done:
            self.moved_ref[s] = self.win[s:s + self.ch + 2 * GAP - 8, :]
            self.done.add(s)
        return self.moved_ref[s, GAP + 8 * q:GAP + 8 * q + self.ch, :]


def _taps(pad_ref, w, row, ch, n_taps, flip=False, moved_ref=None):
    half = (n_taps - 1) // 2
    win = _Window(pad_ref, row, ch, moved_ref)
    acc = None
    for k in range(n_taps):
        term = w[k:k + 1, :] * win.at((half - k) if flip else (k - half))
        acc = term if acc is None else acc + term
    return acc


def _tap_grads(dw_ref, d, pad_ref, row, ch, n_taps, moved_ref=None):
    half = (n_taps - 1) // 2
    win = _Window(pad_ref, row, ch, moved_ref)
    for k in range(n_taps):
        prod = d * win.at(k - half)
        dw_ref[k] += jnp.sum(prod.reshape(ch // 8, 8, prod.shape[1]), axis=0)


def _moved_scratch(ch, width):
    return pltpu.VMEM((8, ch + 2 * GAP - 8, width), F32)


def _conv_fwd(z_conv, w, bias, B, C, S, cw=128, ch=128):
    R = C + S
    nj = CONV_CH // cw
    segs = _segments(C, S)

    def body(a_ref, g_ref, w_ref, b_ref, o_ref, pad, moved):
        _zero_gaps(pad, C, S)
        wv = w_ref[...]
        for so, po, n in segs:
            def fill(r, so=so, po=po):
                pad[pl.ds(po + r, ch), :] = a_ref[pl.ds(so + r, ch), :] * _sigmoid(g_ref[pl.ds(so + r, ch), :])

            _chunks(n, ch, fill)
        for so, po, n in segs:
            def conv(r, so=so, po=po):
                o_ref[pl.ds(so + r, ch), :] = _taps(pad, wv, po + r, ch, CONV_K, moved_ref=moved) + b_ref[...]

            _chunks(n, ch, conv)

    return _call(
        body, name="conv_fwd", grid=(nj, B),
        in_specs=[pl.BlockSpec((R, cw), lambda j, b: (b, j)), pl.BlockSpec((R, cw), lambda j, b: (b, nj + j)),
                  pl.BlockSpec((32, cw), lambda j, b: (0, j)), pl.BlockSpec((1, cw), lambda j, b: (0, j))],
        out_specs=pl.BlockSpec((R, cw), lambda j, b: (b, j)), out_shape=_sds((B * R, CONV_CH), F32),
        scratch=[pltpu.VMEM((_padded_rows(C, S), cw), F32), _moved_scratch(ch, cw)])(z_conv, z_conv, w, bias)


def _conv_bwd(dhc, z_conv, w, B, C, S, cw=128, ch=128):
    R = C + S
    nj = CONV_CH // cw
    segs = _segments(C, S)

    def body(d_ref, a_ref, g_ref, w_ref, da_ref, dg_ref, dw_ref, sa_ref, sg_ref, gpad, dpad, dwacc, moved):
        b = pl.program_id(1)

        @pl.when(b == 0)
        def _():
            dw_ref[...] = jnp.zeros_like(dw_ref)
            sa_ref[...] = jnp.zeros_like(sa_ref)
            sg_ref[...] = jnp.zeros_like(sg_ref)

        _zero_gaps(gpad, C, S)
        _zero_gaps(dpad, C, S)
        dwacc[...] = jnp.zeros_like(dwacc)
        wv = w_ref[...]
        for so, po, n in segs:
            def fill(r, so=so, po=po):
                gpad[pl.ds(po + r, ch), :] = a_ref[pl.ds(so + r, ch), :] * _sigmoid(g_ref[pl.ds(so + r, ch), :])
                dpad[pl.ds(po + r, ch), :] = d_ref[pl.ds(so + r, ch), :]

            _chunks(n, ch, fill)
        for so, po, n in segs:
            def step(r, so=so, po=po):
                _tap_grads(dwacc, dpad[pl.ds(po + r, ch), :], gpad, po + r, ch, CONV_K, moved_ref=moved)
                dglu = _taps(dpad, wv, po + r, ch, CONV_K, flip=True, moved_ref=moved)
                av = a_ref[pl.ds(so + r, ch), :]
                sg = _sigmoid(g_ref[pl.ds(so + r, ch), :])
                da = dglu * sg
                dg = dglu * av * sg * (1.0 - sg)
                da_ref[pl.ds(so + r, ch), :] = da.astype(MXU)
                dg_ref[pl.ds(so + r, ch), :] = dg.astype(MXU)
                sa_ref[...] += _colsum(da)
                sg_ref[...] += _colsum(dg)

            _chunks(n, ch, step)
        for k in range(CONV_K):
            dw_ref[k:k + 1, :] += _colsum(dwacc[k])

    blk = pl.BlockSpec((R, cw), lambda j, b: (b, j))
    acc1 = pl.BlockSpec((1, cw), lambda j, b: (0, j))
    return _call(
        body, name="conv_bwd", grid=(nj, B),
        in_specs=[blk, blk, pl.BlockSpec((R, cw), lambda j, b: (b, nj + j)), pl.BlockSpec((32, cw), lambda j, b: (0, j))],
        out_specs=[blk, blk, pl.BlockSpec((32, cw), lambda j, b: (0, j)), acc1, acc1],
        out_shape=[_sds((B * R, CONV_CH), MXU), _sds((B * R, CONV_CH), MXU), _sds((32, CONV_CH), F32),
                   _sds((1, CONV_CH), F32), _sds((1, CONV_CH), F32)],
        scratch=[pltpu.VMEM((_padded_rows(C, S), cw), F32), pltpu.VMEM((_padded_rows(C, S), cw), F32),
                 pltpu.VMEM((32, 8, cw), F32), _moved_scratch(ch, cw)])(dhc, z_conv, z_conv, w)


def _ffn_mid(up, w, bias, B, C, S, cw=128, ch=128):
    R = C + S
    nj = D_FF // cw
    segs = _segments(C, S)

    def body(a_ref, u_ref, w_ref, b_ref, f_ref, ft_ref, pad):
        _zero_gaps(pad, C, S)
        wv = w_ref[...]
        for so, po, n in segs:
            def fill(r, so=so, po=po):
                pad[pl.ds(po + r, ch), :] = a_ref[pl.ds(so + r, ch), :]

            _chunks(n, ch, fill)
        for so, po, n in segs:
            def conv(r, so=so, po=po):
                ac = _taps(pad, wv, po + r, ch, FFN_K) + b_ref[...]
                f_ref[pl.ds(so + r, ch), :] = (ac * _sigmoid(ac) * u_ref[pl.ds(so + r, ch), :]).astype(MXU)

            _chunks(n, ch, conv)
        ft_ref[...] = f_ref[...].T

    return _call(
        body, name="ffn_mid", grid=(nj, B),
        in_specs=[pl.BlockSpec((R, cw), lambda j, b: (b, j)), pl.BlockSpec((R, cw), lambda j, b: (b, nj + j)),
                  pl.BlockSpec((8, cw), lambda j, b: (0, j)), pl.BlockSpec((1, cw), lambda j, b: (0, j))],
        out_specs=[pl.BlockSpec((R, cw), lambda j, b: (b, j)), pl.BlockSpec((cw, R), lambda j, b: (j, b))],
        out_shape=[_sds((B * R, D_FF), MXU), _sds((D_FF, B * R), MXU)],
        scratch=[pltpu.VMEM((_padded_rows(C, S), cw), F32)])(up, up, w, bias)


def _ffn_mid_bwd(df, up, w, bias, B, C, S, cw=128, ch=128):
    R = C + S
    nj = D_FF // cw
    segs = _segments(C, S)

    def body(d_ref, a_ref, u_ref, w_ref, b_ref, da_ref, du_ref, dw_ref, db_ref, apad, dpad, dwacc):
        b = pl.program_id(1)

        @pl.when(b == 0)
        def _():
            dw_ref[...] = jnp.zeros_like(dw_ref)
            db_ref[...] = jnp.zeros_like(db_ref)

        _zero_gaps(apad, C, S)
        _zero_gaps(dpad, C, S)
        dwacc[...] = jnp.zeros_like(dwacc)
        wv = w_ref[...]
        for so, po, n in segs:
            def fill(r, so=so, po=po):
                apad[pl.ds(po + r, ch), :] = a_ref[pl.ds(so + r, ch), :]

            _chunks(n, ch, fill)
        for so, po, n in segs:
            def first(r, so=so, po=po):
                ac = _taps(apad, wv, po + r, ch, FFN_K) + b_ref[...]
                sg = _sigmoid(ac)
                dfv = d_ref[pl.ds(so + r, ch), :]
                du_ref[pl.ds(so + r, ch), :] = (dfv * ac * sg).astype(MXU)
                dac = dfv * u_ref[pl.ds(so + r, ch), :] * (sg * (1.0 + ac * (1.0 - sg)))
                dpad[pl.ds(po + r, ch), :] = dac
                db_ref[...] += _colsum(dac)

            _chunks(n, ch, first)
        for so, po, n in segs:
            def second(r, so=so, po=po):
                _tap_grads(dwacc, dpad[pl.ds(po + r, ch), :], apad, po + r, ch, FFN_K)
                da_ref[pl.ds(so + r, ch), :] = _taps(dpad, wv, po + r, ch, FFN_K, flip=True).astype(MXU)

            _chunks(n, ch, second)
        for k in range(FFN_K):
            dw_ref[k:k + 1, :] += _colsum(dwacc[k])

    blk = pl.BlockSpec((R, cw), lambda j, b: (b, j))
    return _call(
        body, name="ffn_mid_bwd", grid=(nj, B),
        in_specs=[blk, blk, pl.BlockSpec((R, cw), lambda j, b: (b, nj + j)), pl.BlockSpec((8, cw), lambda j, b: (0, j)),
                  pl.BlockSpec((1, cw), lambda j, b: (0, j))],
        out_specs=[blk, blk, pl.BlockSpec((8, cw), lambda j, b: (0, j)), pl.BlockSpec((1, cw), lambda j, b: (0, j))],
        out_shape=[_sds((B * R, D_FF), MXU), _sds((B * R, D_FF), MXU), _sds((8, D_FF), F32), _sds((1, D_FF), F32)],
        scratch=[pltpu.VMEM((_padded_rows(C, S), cw), F32), pltpu.VMEM((_padded_rows(C, S), cw), F32),
                 pltpu.VMEM((8, 8, cw), F32)])(df, up, up, w, bias)


def _window_count(r, ch, n, w):
    t = r + lax.broadcasted_iota(jnp.int32, (ch, 1), 0)
    return (jnp.minimum(t + w // 2, n) - jnp.maximum(t - w // 2, 0)).astype(F32)


def _pool_fwd(z_pool, pool_w, pool_scale, B, C, S, ch=128):
    R = C + S
    gch = POOL_GCH
    segs = _segments(C, S)

    def body(u_ref, pw_ref, sc_ref, pooled_ref, po_ref, pad):
        g = pl.program_id(1)
        _zero_gaps(pad, C, S)
        for so, po, n in segs:
            def fill(r, so=so, po=po):
                pad[pl.ds(po + r, ch), :] = u_ref[pl.ds(so + r, ch), :]

            _chunks(n, ch, fill)
        for gi, w in enumerate(POOL_WINDOWS):
            @pl.when(g == gi)
            def _(w=w):
                for so, po, n in segs:
                    def step(r, so=so, po=po, n=n):
                        win = _Window(pad, po + r, ch)
                        acc = win.at(-(w // 2))
                        for o in range(1 - w // 2, w // 2):
                            acc = acc + win.at(o)
                        pooled = (acc / _window_count(r, ch, n, w) - win.at(0)).astype(MXU)
                        pooled_ref[pl.ds(so + r, ch), :] = pooled
                        po_ref[pl.ds(so + r, ch), :] = _dot(pooled, pw_ref[0], NN) * sc_ref[...]

                    _chunks(n, ch, step)

    blk = pl.BlockSpec((R, gch), lambda b, g: (b, g))
    return _call(
        body, name="pool_fwd", grid=(B, len(POOL_WINDOWS)),
        in_specs=[blk, pl.BlockSpec((1, gch, gch), lambda b, g: (g, 0, 0)), pl.BlockSpec((1, gch), lambda b, g: (0, g))],
        out_specs=[blk, blk], out_shape=[_sds((B * R, POOL_CH), MXU), _sds((B * R, POOL_CH), F32)],
        scratch=[pltpu.VMEM((_padded_rows(C, S), gch), F32)])(z_pool, pool_w, pool_scale)


def _pool_bwd(dpo, pooled, pool_w, pool_scale, B, C, S, ch=128):
    R = C + S
    gch = POOL_GCH
    segs = _segments(C, S)

    def body(d_ref, p_ref, pw_ref, sc_ref, du_ref, dpw_ref, dsc_ref, su_ref, qpad, dpl):
        g, b = pl.program_id(0), pl.program_id(1)

        @pl.when(b == 0)
        def _():
            dpw_ref[...] = jnp.zeros_like(dpw_ref)
            dsc_ref[...] = jnp.zeros_like(dsc_ref)
            su_ref[...] = jnp.zeros_like(su_ref)

        _zero_gaps(qpad, C, S)
        pw = pw_ref[0]
        for gi, w in enumerate(POOL_WINDOWS):
            @pl.when(g == gi)
            def _(w=w):
                for so, po, n in segs:
                    def first(r, so=so, po=po, n=n):
                        pv = p_ref[pl.ds(so + r, ch), :]
                        dv = d_ref[pl.ds(so + r, ch), :]
                        dsc_ref[...] += _colsum(dv * _dot(pv, pw, NN))
                        dmx = (dv * sc_ref[...]).astype(MXU)
                        dpw_ref[0] += _dot(pv, dmx, TN)
                        dp = _dot(dmx, pw, NT)
                        dpl[pl.ds(so + r, ch), :] = dp
                        qpad[pl.ds(po + r, ch), :] = dp / _window_count(r, ch, n, w)

                    _chunks(n, ch, first)
                for so, po, n in segs:
                    def second(r, so=so, po=po):
                        win = _Window(qpad, po + r, ch)
                        acc = win.at(1 - w // 2)
                        for o in range(2 - w // 2, w // 2 + 1):
                            acc = acc + win.at(o)
                        du = acc - dpl[pl.ds(so + r, ch), :]
                        du_ref[pl.ds(so + r, ch), :] = du.astype(MXU)
                        su_ref[...] += _colsum(du)

                    _chunks(n, ch, second)

    blk = pl.BlockSpec((R, gch), lambda g, b: (b, g))
    vec = pl.BlockSpec((1, gch), lambda g, b: (0, g))
    wblk = pl.BlockSpec((1, gch, gch), lambda g, b: (g, 0, 0))
    return _call(
        body, name="pool_bwd", grid=(len(POOL_WINDOWS), B), in_specs=[blk, blk, wblk, vec], out_specs=[blk, wblk, vec, vec],
        out_shape=[_sds((B * R, POOL_CH), MXU), _sds((len(POOL_WINDOWS), gch, gch), F32), _sds((1, POOL_CH), F32),
                   _sds((1, POOL_CH), F32)],
        scratch=[pltpu.VMEM((_padded_rows(C, S), gch), F32), pltpu.VMEM((R, gch), F32)])(dpo, pooled, pool_w, pool_scale)


def _silu_rows(cond):
    def body(c_ref, s_ref, d_ref):
        c = c_ref[...]
        sg = _sigmoid(c)
        s_ref[...] = (c * sg).astype(MXU)
        d_ref[...] = sg * (1.0 + c * (1.0 - sg))

    full = pl.BlockSpec(cond.shape, lambda i: (0, 0))
    return _call(body, name="silu_rows", grid=(1,), in_specs=[full], out_specs=[full, full],
                 out_shape=[_sds(cond.shape, MXU), _sds(cond.shape, F32)])(cond)


def _row_tile(rows, cols, n_bufs):
    cap = max(16, (16 * 2 ** 20) // (4 * n_bufs * max(cols, 128)))
    return rows if rows <= cap else _pick(rows, cap, 16)


def _adamw(parts, w, m, v, layer, prev, *, name):
    n_parts, rows, cols = parts.shape
    layers = w.shape[0]
    c1 = 1.0 - ADAM_B1 ** ADAM_STEP
    c2 = 1.0 - ADAM_B2 ** ADAM_STEP
    tr = _row_tile(rows, cols, n_parts + 7)

    def body(p_ref, w_ref, m_ref, v_ref, *rest):
        g_ref, d_ref, nm_ref, nv_ref = rest[-4:]
        g = p_ref[0].astype(F32)
        for k in range(1, n_parts):
            g = g + p_ref[k].astype(F32)
        nm = ADAM_B1 * m_ref[...] + (1.0 - ADAM_B1) * g
        nv = ADAM_B2 * v_ref[...] + (1.0 - ADAM_B2) * (g * g)
        g_ref[...] = g
        nm_ref[...] = nm
        nv_ref[...] = nv
        d_ref[...] = -ADAM_LR * ((nm / c1) / (jnp.sqrt(nv / c2) + ADAM_EPS) + ADAM_WD * w_ref[...])

    blk = pl.BlockSpec((None, tr, cols), lambda i: (layer, i, 0))
    in_specs = [pl.BlockSpec((n_parts, tr, cols), lambda i: (0, i, 0)), blk, blk, blk]
    args = [parts, w, m, v]
    aliases = {}
    if prev is not None:
        in_specs += [ANY] * 4
        aliases = {4 + k: k for k in range(4)}
        args += list(prev)
    return _call(body, name=name, grid=(rows // tr,), in_specs=in_specs, out_specs=[blk] * 4,
                 out_shape=[_sds((layers, rows, cols), F32)] * 4, aliases=aliases)(*args)


def _pair_sum(g, r1):
    _, rows, cols = g.shape
    c = lax.axis_index("c")
    g4 = g.reshape(4, 2, rows, cols)
    tr = _row_tile(rows, cols, 10)

    def body(c_ref, g_ref, r_ref, o_ref):
        o_ref[...] = (g_ref[...] + r_ref[...]).astype(WIRE)

    return _pcall(
        body, name="pair_sum", out_shape=_sds((4, rows, cols), WIRE),
        grid_spec=pltpu.PrefetchScalarGridSpec(
            num_scalar_prefetch=1, grid=(rows // tr,),
            in_specs=[pl.BlockSpec((4, None, tr, cols), lambda i, c_ref: (0, c_ref[0], i, 0)),
                      pl.BlockSpec((4, tr, cols), lambda i, c_ref: (0, i, 0))],
            out_specs=pl.BlockSpec((4, tr, cols), lambda i, c_ref: (0, i, 0))),
        compiler_params=pltpu.CompilerParams(dimension_semantics=("arbitrary",), vmem_limit_bytes=VMEM_MB * 2 ** 20),
    )(jnp.reshape(c, (1,)).astype(jnp.int32), g4, r1)


MESH = pl.DeviceIdType.MESH
ANY = pl.BlockSpec(memory_space=pl.ANY)


def _all_gather(shards, *, name, layer=None):
    n_t = len(shards)
    blks = [s.shape if layer is None else s.shape[1:] for s in shards]

    def body(*refs):
        x_refs, out_refs = refs[:n_t], refs[n_t:2 * n_t]
        send_sems, recv_sems, local_sems = refs[2 * n_t:]
        x, y, c = lax.axis_index("x"), lax.axis_index("y"), lax.axis_index("c")
        me, sibling = (x, y, c), (x, y, 1 - c)
        chips = [(1 - x, y), (x, 1 - y), (1 - x, 1 - y)]

        def copy(n, k, blk, to, own=False):
            dst = out_refs[n].at[4 * blk[0] + 2 * blk[1] + blk[2]]
            src = dst if not own else (x_refs[n] if layer is None else x_refs[n].at[layer])
            return pltpu.make_async_remote_copy(src_ref=src, dst_ref=dst, send_sem=send_sems.at[n * 7 + k],
                                                recv_sem=recv_sems.at[n * 7 + k], device_id=to, device_id_type=MESH)

        mine = [pltpu.make_async_copy(x_refs[n] if layer is None else x_refs[n].at[layer], out_refs[n].at[4 * x + 2 * y + c],
                                      local_sems.at[n]) for n in range(n_t)]
        for cp in mine:
            cp.start()
        first = [copy(n, 1 + j, me, (*chip, c), own=True) for j, chip in enumerate(chips) for n in range(n_t)]
        first += [copy(n, 0, me, sibling, own=True) for n in range(n_t)]
        for cp in first:
            cp.start()
        passed = []
        for j, chip in enumerate(chips):
            for n in range(n_t):
                copy(n, 1 + j, (*chip, c), me).wait_recv()
                passed.append(copy(n, 4 + j, (*chip, c), sibling))
                passed[-1].start()
        for n in range(n_t):
            copy(n, 0, sibling, me).wait_recv()
        for j, chip in enumerate(chips):
            for n in range(n_t):
                copy(n, 4 + j, (*chip, 1 - c), me).wait_recv()
        for cp in first + passed:
            cp.wait_send()
        for cp in mine:
            cp.wait()

    return _pcall(
        body, name=name, out_shape=[_sds((N_DEV,) + tuple(b), s.dtype) for b, s in zip(blks, shards)],
        in_specs=[ANY] * n_t, out_specs=[ANY] * n_t,
        scratch_shapes=[pltpu.SemaphoreType.DMA((7 * n_t,)), pltpu.SemaphoreType.DMA((7 * n_t,)), pltpu.SemaphoreType.DMA((n_t,))],
    )(*shards)


def _sibling_exchange(gs):
    n_t = len(gs)

    def body(*refs):
        g_refs, r_refs = refs[:n_t], refs[n_t:2 * n_t]
        send_sems, recv_sems = refs[2 * n_t:]
        x, y, c = lax.axis_index("x"), lax.axis_index("y"), lax.axis_index("c")
        copies = [pltpu.make_async_remote_copy(
            src_ref=g_refs[n].at[:, 1 - c], dst_ref=r_refs[n], send_sem=send_sems.at[n], recv_sem=recv_sems.at[n],
            device_id=(x, y, 1 - c), device_id_type=MESH) for n in range(n_t)]
        for cp in copies:
            cp.start()
        for cp in copies:
            cp.wait_recv()
        for cp in copies:
            cp.wait_send()

    return _pcall(
        body, name="sibling_exchange", out_shape=[_sds((4,) + g.shape[1:], g.dtype) for g in gs],
        in_specs=[ANY] * n_t, out_specs=[ANY] * n_t,
        scratch_shapes=[pltpu.SemaphoreType.DMA((n_t,)), pltpu.SemaphoreType.DMA((n_t,))],
    )(*[g.reshape((4, 2) + g.shape[1:]) for g in gs])


def _chip_exchange(ps):
    n_t = len(ps)

    def body(*refs):
        p_refs, r_refs = refs[:n_t], refs[n_t:2 * n_t]
        send_sems, recv_sems, local_sems = refs[2 * n_t:]
        x, y, c = lax.axis_index("x"), lax.axis_index("y"), lax.axis_index("c")
        mine = 2 * x + y
        chips = [(1 - x, y), (x, 1 - y), (1 - x, 1 - y)]
        own = [pltpu.make_async_copy(p_refs[n].at[mine], r_refs[n].at[mine], local_sems.at[n]) for n in range(n_t)]
        for cp in own:
            cp.start()

        def copy(n, j, src_chip, dst_slot):
            px, py = chips[j]
            return pltpu.make_async_remote_copy(
                src_ref=p_refs[n].at[src_chip], dst_ref=r_refs[n].at[dst_slot], send_sem=send_sems.at[n * 3 + j],
                recv_sem=recv_sems.at[n * 3 + j], device_id=(px, py, c), device_id_type=MESH)

        sends = [copy(n, j, 2 * px + py, mine) for j, (px, py) in enumerate(chips) for n in range(n_t)]
        for cp in sends:
            cp.start()
        for j, (px, py) in enumerate(chips):
            for n in range(n_t):
                copy(n, j, mine, 2 * px + py).wait_recv()
        for cp in sends:
            cp.wait_send()
        for cp in own:
            cp.wait()

    return _pcall(
        body, name="chip_exchange", out_shape=[_sds(p.shape, p.dtype) for p in ps], in_specs=[ANY] * n_t, out_specs=[ANY] * n_t,
        scratch_shapes=[pltpu.SemaphoreType.DMA((3 * n_t,)), pltpu.SemaphoreType.DMA((3 * n_t,)), pltpu.SemaphoreType.DMA((n_t,))],
    )(*ps)


BIG = (("w_ada", (D, N_MOD * D // N_DEV), 1), ("w_in", (D, D_IN // N_DEV), 1), ("conv_pw_w", (CONV_CH // N_DEV, D), 0),
       ("pool_w", (len(POOL_WINDOWS), POOL_GCH // N_DEV, POOL_GCH), 1), ("w_out", (D // N_DEV, D), 0),
       ("w_up", (D, 2 * D_FF // N_DEV), 1), ("w_down", (D_FF // N_DEV, D), 0))
TAPS = (("conv_dw_w", (CONV_K, CONV_CH // N_DEV), 1), ("ffn_dw_w", (FFN_K, D_FF // N_DEV), 1))
SHARDED = BIG + TAPS
REPLICATED = (("b_ada", N_MOD * D), ("b_in", D_IN), ("q_gain", HD), ("k_gain", HD), ("conv_dw_b", CONV_CH), ("conv_ln_g", CONV_CH),
              ("conv_ln_b", CONV_CH), ("conv_pw_b", D), ("pool_scale", POOL_CH), ("b_out", D), ("ln1_g", D), ("ln1_b", D),
              ("ln2_g", D), ("ln2_b", D), ("ffn_dw_b", D_FF))


def _as_rows(shape):
    return (int(np.prod(shape[:-1])), shape[-1])


def _full_from_blocks(blocks, axis):
    moved = jnp.moveaxis(blocks, 0, axis)
    shape = list(moved.shape)
    shape[axis:axis + 2] = [shape[axis] * shape[axis + 1]]
    return moved.reshape(shape)


def _blocks_from_full(full, axis):
    shape = list(full.shape)
    shape[axis:axis + 1] = [N_DEV, shape[axis] // N_DEV]
    return jnp.moveaxis(full.reshape(shape), axis, 0)


SMALL_N = DEPTH * sum(n for _, n in REPLICATED) + D
SMALL_ROWS = -(-(SMALL_N + 1) // (8 * LANES)) * 8


def _rope_tables(C, S):
    t = np.arange(S)
    inv_freq = ROPE_THETA ** (-np.arange(ROPE_PAIRS, dtype=np.float32) / ROPE_PAIRS)
    row = jnp.asarray((t // GRID_W).astype(np.float32))[:, None] * jnp.asarray(inv_freq, F32)
    col = jnp.asarray((t % GRID_W).astype(np.float32))[:, None] * jnp.asarray(inv_freq, F32)
    cos = jnp.concatenate([jnp.cos(row), jnp.cos(row), jnp.cos(col), jnp.cos(col)], axis=1)
    sin = jnp.concatenate([-jnp.sin(row), jnp.sin(row), -jnp.sin(col), jnp.sin(col)], axis=1)
    cos = jnp.concatenate([jnp.ones((C, HD), F32), cos], axis=0)
    sin = jnp.concatenate([jnp.zeros((C, HD), F32), sin], axis=0)
    return cos, sin


def _segment_sums(parts, B, tpe, ncq):
    p = parts.reshape(B, tpe, D)
    return jnp.concatenate([jnp.sum(p[:, ncq:], axis=1), jnp.sum(p[:, :ncq], axis=(0, 1))[None]], axis=0)


def kernel(x, c, ctx, c_ctx, w_ada, b_ada, w_in, b_in, q_gain, k_gain, conv_dw_w, conv_dw_b, conv_ln_g, conv_ln_b, conv_pw_w, conv_pw_b, pool_w, pool_scale, w_out, b_out, ln1_g, ln1_b, ln2_g, ln2_b, w_up, ffn_dw_w, ffn_dw_b, w_down, loss_target, m_c_ctx, m_w_ada, m_b_ada, m_w_in, m_b_in, m_q_gain, m_k_gain, m_conv_dw_w, m_conv_dw_b, m_conv_ln_g, m_conv_ln_b, m_conv_pw_w, m_conv_pw_b, m_pool_w, m_pool_scale, m_w_out, m_b_out, m_ln1_g, m_ln1_b, m_ln2_g, m_ln2_b, m_w_up, m_ffn_dw_w, m_ffn_dw_b, m_w_down, v_c_ctx, v_w_ada, v_b_ada, v_w_in, v_b_in, v_q_gain, v_k_gain, v_conv_dw_w, v_conv_dw_b, v_conv_ln_g, v_conv_ln_b, v_conv_pw_w, v_conv_pw_b, v_pool_w, v_pool_scale, v_w_out, v_b_out, v_ln1_g, v_ln1_b, v_ln2_g, v_ln2_b, v_w_up, v_ffn_dw_w, v_ffn_dw_b, v_w_down):
    given = dict(locals())
    B, S, _ = x.shape
    C = ctx.shape[1]
    R = C + S
    T = B * R
    tpe, ncq = R // TM, C // TM
    nt = T // TM
    assert S % TM == 0 and C % TM == 0 and B + 1 <= 16

    operands = [given[n].astype(MXU) for n, _, _ in BIG]
    taps = _all_gather([given[n] for n, _, _ in TAPS], name="gather_taps")
    W = []
    for l in range(DEPTH):
        blocks = _all_gather(operands, name="gather_weights", layer=l)
        wl = {n: _full_from_blocks(blk, a) for (n, _, a), blk in zip(BIG, blocks)}
        for (n, _, a), blk in zip(TAPS, taps):
            wl[n] = _full_from_blocks(blk[:, l], a)
        W.append(wl)

    xu = jnp.concatenate([ctx, x], axis=1).reshape(T, D)
    cond = jnp.concatenate([c, c_ctx[None], jnp.zeros((16 - B - 1, D), F32)], axis=0)
    s_cond, ds_cond = _silu_rows(cond)
    ctx_tile = jnp.asarray((np.arange(tpe) < ncq)[None, :, None])
    cos_t, sin_t = _rope_tables(C, S)
    row = lambda v: v.reshape(1, -1)

    mods = [_matmul(s_cond, W[l]["w_ada"], "nn", name="ada", bias=row(b_ada[l]), tm=16, tn=1024) for l in range(DEPTH)]
    modt = [jnp.where(ctx_tile, m[B][None, None, :], m[:B][:, None, :]).reshape(nt, 1, N_MOD * D) for m in mods]

    saved = []
    h1, h1t = _modulate_cast(xu, modt[0], 0)
    xin = xu
    for l in range(DEPTH):
        wl = W[l]
        w_inl, b_inl = wl["w_in"], b_in[l]
        z_qkv = _matmul(h1, w_inl[:, :QKV_W], "nn", name="z_qkv", bias=row(b_inl[:QKV_W]), tn=768)
        c0, p0, g0 = QKV_W, QKV_W + 2 * CONV_CH, QKV_W + 2 * CONV_CH + POOL_CH
        z_conv = _matmul(h1, w_inl[:, c0:p0], "nn", name="z_conv", bias=row(b_inl[c0:p0]))
        z_pool = _matmul(h1, w_inl[:, p0:g0], "nn", name="z_pool", bias=row(b_inl[p0:g0]))
        z_gate = _matmul(h1, w_inl[:, g0:], "nn", name="z_gate", bias=row(b_inl[g0:]))
        qkv = _qk_prep(z_qkv, cos_t, sin_t, row(q_gain[l]), row(k_gain[l]), tpe)
        attn = _attn_fwd(qkv, B, C, R)
        dw32 = jnp.pad(wl["conv_dw_w"], ((0, 32 - CONV_K), (0, 0)))
        hc = _conv_fwd(z_conv, dw32, row(conv_dw_b[l]), B, C, S)
        sw, swt = _ln_silu(hc, row(conv_ln_g[l]), row(conv_ln_b[l]))
        conv_o = _matmul(sw, wl["conv_pw_w"], "nn", name="conv_pw", bias=row(conv_pw_b[l]))
        pooled, pool_o = _pool_fwd(z_pool, wl["pool_w"], row(pool_scale[l]), B, C, S)
        m, mt = _merge(attn, conv_o, pool_o, z_gate)
        mo = _matmul(m, wl["w_out"], "nn", name="w_out", bias=row(b_out[l]))
        y1, h2, h2t = _resid_ln(xin, mo, modt[l], 2, row(ln1_g[l]), row(ln1_b[l]), modt[l], 3)
        up = _matmul(h2, wl["w_up"], "nn", name="w_up", tn=1408)
        fw8 = jnp.pad(wl["ffn_dw_w"], ((0, 8 - FFN_K), (0, 0)))
        f, ft = _ffn_mid(up, fw8, row(ffn_dw_b[l]), B, C, S)
        fo = _matmul(f, wl["w_down"], "nn", name="w_down", tm=512)
        if l + 1 < DEPTH:
            y2, h_next, ht_next = _resid_ln(y1, fo, modt[l], 5, row(ln2_g[l]), row(ln2_b[l]), modt[l + 1], 0)
        else:
            y2, h_next, ht_next = _resid_ln(y1, fo, modt[l], 5, row(ln2_g[l]), row(ln2_b[l])), None, None
        saved.append(dict(xin=xin, h1t=h1t, z_qkv=z_qkv, z_conv=z_conv, z_gate=z_gate, qkv=qkv, attn=attn, hc=hc, swt=swt,
                          conv_o=conv_o, pooled=pooled, pool_o=pool_o, mt=mt, mo=mo, y1=y1, h2t=h2t, up=up, ft=ft, fo=fo,
                          dw32=dw32, fw8=fw8))
        xin, h1, h1t = y2, h_next, ht_next

    dy, loss_part = _loss_grad(xin, loss_target.reshape(B * S, D), tpe, ncq)

    small = {n: [None] * DEPTH for n, _ in REPLICATED}
    d_c_ctx = jnp.zeros((D,), F32)
    dmods_t = [[None] * N_MOD for _ in range(DEPTH)]
    layer_grads = [None] * DEPTH
    dh1 = None
    for l in reversed(range(DEPTH)):
        wl, sv = W[l], saved[l]
        dmod = dmods_t[l]
        if dh1 is None:
            dy1p, dfo, dgate2, dg, db, _ = _ln_bwd(dy, sv["y1"], sv["fo"], modt[l], 5, row(ln2_g[l]))
        else:
            dy1p, dfo, dgate2, dg, db, _, dsh, dsc = _ln_bwd(dy, sv["y1"], sv["fo"], modt[l], 5, row(ln2_g[l]), dh=dh1,
                                                             y=saved[l + 1]["xin"], mod_next=modt[l + 1], k_shift_next=0)
            dmods_t[l + 1][0], dmods_t[l + 1][1] = dsh, dsc
        small["ln2_g"][l], small["ln2_b"][l] = dg[0], db[0]
        dmod[5] = dgate2
        df = _matmul(dfo, wl["w_down"], "nt", name="d_f", tn=1408)
        g_w_down = _matmul(sv["ft"], dfo, "nn", name="g_w_down", tm=1408, tk=1152)
        da2, du2, g_fdw, g_fdb = _ffn_mid_bwd(df, sv["up"], sv["fw8"], row(ffn_dw_b[l]), B, C, S)
        small["ffn_dw_b"][l] = g_fdb[0]
        dh2 = _matmul(da2, wl["w_up"][:, :D_FF], "nt", name="d_h2a", tm=512)
        dh2 = _matmul(du2, wl["w_up"][:, D_FF:], "nt", name="d_h2u", tm=512, acc_in=dh2)
        g_w_up = _matmul(sv["h2t"], da2, "nn", name="g_w_up_a", tn=1408, tk=1152, into=(2 * D_FF, 0, None))
        g_w_up = _matmul(sv["h2t"], du2, "nn", name="g_w_up_u", tn=1408, tk=1152, into=(2 * D_FF, D_FF, g_w_up))
        dxp, dmo, dgate1, dg, db, dbo, dsh, dsc = _ln_bwd(dy1p, sv["xin"], sv["mo"], modt[l], 2, row(ln1_g[l]), dh=dh2,
                                                          y=sv["y1"], mod_next=modt[l], k_shift_next=3)
        small["ln1_g"][l], small["ln1_b"][l], small["b_out"][l] = dg[0], db[0], dbo[0]
        dmod[2], dmod[3], dmod[4] = dgate1, dsh, dsc
        dm = _matmul(dmo, wl["w_out"], "nt", name="d_m")
        g_w_out = _matmul(sv["mt"], dmo, "nn", name="g_w_out", tk=2304)
        dattn, dconv_o, dpool_o, dzg, g_pwb, gsum_gate = _merge_bwd(dm, sv["attn"], sv["conv_o"], sv["pool_o"], sv["z_gate"])
        small["conv_pw_b"][l] = g_pwb[0]
        du, g_pool_w, g_pool_sc, gsum_pool = _pool_bwd(dpool_o, sv["pooled"], wl["pool_w"], row(pool_scale[l]), B, C, S)
        small["pool_scale"][l] = g_pool_sc[0]
        dsw = _matmul(dconv_o, wl["conv_pw_w"], "nt", name="d_sw")
        g_pw = _matmul(sv["swt"], dconv_o, "nn", name="g_conv_pw", tk=2304)
        dhc, g_cg, g_cb, g_cdb = _ln_silu_bwd(dsw, sv["hc"], row(conv_ln_g[l]), row(conv_ln_b[l]))
        small["conv_ln_g"][l], small["conv_ln_b"][l], small["conv_dw_b"][l] = g_cg[0], g_cb[0], g_cdb[0]
        da, dgt, g_cdw, gsum_a, gsum_gt = _conv_bwd(dhc, sv["z_conv"], sv["dw32"], B, C, S)
        dq, dk, dv = _attn_bwd(sv["qkv"], sv["attn"], dattn, B, C, R)
        dz_qkv, g_qg, g_kg, gsum_qkv = _qk_bwd(dq, dk, dv, sv["z_qkv"], cos_t, sin_t, row(q_gain[l]), row(k_gain[l]), tpe)
        small["q_gain"][l], small["k_gain"][l] = g_qg[0], g_kg[0]
        small["b_in"][l] = jnp.concatenate([gsum_qkv[0], gsum_a[0], gsum_gt[0], gsum_pool[0], gsum_gate[0]])
        w_inl = wl["w_in"]
        g0 = QKV_W + 2 * CONV_CH + POOL_CH
        pieces = ((dzg, g0, N_GATE), (da, QKV_W, CONV_CH), (dgt, QKV_W + CONV_CH, CONV_CH), (du, QKV_W + 2 * CONV_CH, POOL_CH),
                  (dz_qkv, 0, QKV_W))
        dh1 = g_w_in = None
        at = 0
        for k, (dz, c0, wd) in enumerate(pieces):
            dh1 = _matmul(dz, w_inl[:, c0:c0 + wd], "nt", name=f"d_h1_{k}", tm=512, acc_in=dh1)
            g_w_in = _matmul(sv["h1t"], dz, "nn", name=f"g_w_in_{k}", tk=2304, into=(D_IN, at, g_w_in))
            at += wd
        g_w_in = jnp.concatenate([g_w_in[:, N_GATE + 2 * CONV_CH + POOL_CH:], g_w_in[:, N_GATE:N_GATE + 2 * CONV_CH + POOL_CH],
                                  g_w_in[:, :N_GATE]], axis=1)

        layer_grads[l] = {"w_in": g_w_in, "conv_pw_w": g_pw, "pool_w": g_pool_w, "w_out": g_w_out, "w_up": g_w_up,
                          "w_down": g_w_down, "conv_dw_w": g_cdw[:CONV_K], "ffn_dw_w": g_fdw[:FFN_K]}
        dy = dxp
    gx_u, dmods_t[0][0], dmods_t[0][1] = _mod_bwd(dy, dh1, saved[0]["xin"], modt[0], 0)
    grad_x = gx_u.reshape(B, R, D)[:, C:]

    kinds = ("grad_", "delta_", "new_m_", "new_v_")
    stacks = {n: [given[pre + n].reshape((DEPTH,) + _as_rows(s)) for pre in ("", "m_", "v_")] for n, s, _ in SHARDED}
    results = {n: None for n, _, _ in SHARDED}
    for l in range(DEPTH):
        dmods = jnp.concatenate([_segment_sums(p, B, tpe, ncq) for p in dmods_t[l]], axis=1)
        small["b_ada"][l] = jnp.sum(dmods, axis=0)
        dm16 = jnp.concatenate([dmods, jnp.zeros((16 - B - 1, N_MOD * D), F32)], axis=0).astype(MXU)
        layer_grads[l]["w_ada"] = _matmul(s_cond, dm16, "tn", name="g_w_ada", tm=1024, tn=1024)
        dcond = _matmul(dm16, W[l]["w_ada"], "nt", name="d_cond", tm=16, tn=1024, tk=2048)
        d_c_ctx = d_c_ctx + dcond[B] * ds_cond[B]
        gs = [_blocks_from_full(layer_grads[l][n], a).reshape((N_DEV,) + _as_rows(s)) for n, s, a in SHARDED]
        r1 = _sibling_exchange(gs)
        r2 = _chip_exchange([_pair_sum(g, r) for g, r in zip(gs, r1)])
        for (n, _, _), parts in zip(SHARDED, r2):
            results[n] = _adamw(parts, *stacks[n], l, results[n], name="adamw_sharded")
    outs = {}
    for n, s, _ in SHARDED:
        for kind, buf in zip(kinds, results[n]):
            outs[kind + n] = buf.reshape((DEPTH,) + tuple(s))

    def small_pack(pieces):
        flat = jnp.concatenate([p.reshape(-1) for p in pieces])
        return jnp.pad(flat, (0, SMALL_ROWS * LANES - flat.shape[0])).reshape(SMALL_ROWS, LANES)

    zero1 = jnp.zeros((1,), F32)
    g_pack = small_pack([small[n][l] for n, _ in REPLICATED for l in range(DEPTH)] + [d_c_ctx, loss_part[0, :1]])
    g_small, = _all_gather([g_pack], name="gather_small")
    wmv = [small_pack([given[pre + n] for n, _ in REPLICATED] + [given[pre + "c_ctx"], zero1])[None] for pre in ("", "m_", "v_")]
    res = _adamw(g_small, *wmv, 0, None, name="adamw_small")
    for kind, buf in zip(kinds, res):
        flat = buf.reshape(-1)
        off = 0
        for n, sz in REPLICATED:
            outs[kind + n] = flat[off:off + DEPTH * sz].reshape(DEPTH, sz)
            off += DEPTH * sz
        outs[kind + "c_ctx"] = flat[off:off + D]
        if kind == "grad_":
            loss = flat[off + D]

    names = ["c_ctx", "w_ada", "b_ada", "w_in", "b_in", "q_gain", "k_gain", "conv_dw_w", "conv_dw_b", "conv_ln_g", "conv_ln_b",
             "conv_pw_w", "conv_pw_b", "pool_w", "pool_scale", "w_out", "b_out", "ln1_g", "ln1_b", "ln2_g", "ln2_b", "w_up",
             "ffn_dw_w", "ffn_dw_b", "w_down"]
    return (loss, grad_x, *[outs[k + n] for k in ("grad_", "delta_", "new_m_", "new_v_") for n in names])
```

```python
import functools

import jax
import jax.numpy as jnp
import numpy as np
from jax import lax
from jax.experimental import pallas as pl
from jax.experimental.pallas import tpu as pltpu

F32 = jnp.float32
MXU = jnp.bfloat16
WIRE = jnp.bfloat16

D = 1024
HD = 128
NH = 8
NKV = 2
QG = NH // NKV
KV_W = NKV * HD
QKV_W = NH * HD + 2 * KV_W
CONV_CH = D
POOL_CH = D
POOL_WINDOWS = (2, 4, 8, 16)
POOL_GCH = POOL_CH // len(POOL_WINDOWS)
N_GATE = 3 * D
D_IN = QKV_W + 2 * CONV_CH + POOL_CH + N_GATE
D_FF = 2816
N_MOD = 6
DEPTH = 4
CONV_K = 31
FFN_K = 3
GRID_W = 64
ROPE_THETA = 10000.0
ROPE_PAIRS = HD // 4
ALPHA = (2 * DEPTH) ** 0.25
LN_EPS = 1e-5
RMS_EPS = 1e-6
ATTN_SCALE = HD ** -0.5
LOG2_E = 1.4426950408889634
ADAM_LR, ADAM_B1, ADAM_B2, ADAM_EPS, ADAM_WD, ADAM_STEP = 0.001, 0.9, 0.999, 1e-08, 0.01, 10

N_DEV = 8
TM = 256
GAP = 16
LANES = 1024
VMEM_MB = 48

NN = (((1,), (0,)), ((), ()))
NT = (((1,), (1,)), ((), ()))
TN = (((0,), (0,)), ((), ()))

_pcall = pl.pallas_call


def _call(body, *, name, grid, in_specs, out_specs, out_shape, scratch=(), aliases=None, vmem=VMEM_MB):
    return _pcall(
        body, name=name, grid=grid, in_specs=in_specs, out_specs=out_specs, out_shape=out_shape,
        scratch_shapes=list(scratch), input_output_aliases=aliases or {},
        compiler_params=pltpu.CompilerParams(dimension_semantics=("arbitrary",) * len(grid), vmem_limit_bytes=vmem * 2 ** 20),
    )


def _sds(shape, dtype):
    return jax.ShapeDtypeStruct(tuple(shape), dtype)


def _pick(n, cap, mult):
    best = None
    for t in range(mult, min(n, cap) + 1, mult):
        if n % t == 0:
            best = t
    return best if best is not None else n


def _dot(a, b, dims):
    return lax.dot_general(a, b, dims, preferred_element_type=F32)


def _sigmoid(x):
    return 1.0 / (1.0 + jnp.exp(-x))


def _matmul(a, b, mode, *, name, bias=None, acc_in=None, into=None, out_dtype=F32, tm=1024, tn=1024, tk=None):
    if mode == "nn":
        (M, K), (K2, N) = a.shape, b.shape
    elif mode == "nt":
        (M, K), (N, K2) = a.shape, b.shape
    else:
        (K, M), (K2, N) = a.shape, b.shape
    assert K == K2, (a.shape, b.shape, mode)
    tm = _pick(M, tm, 16)
    tn = _pick(N, tn, 128)
    tk = K if tk is None else _pick(K, tk, 128 if mode != "tn" else 16)
    gk = K // tk
    dims = {"nn": NN, "nt": NT, "tn": TN}[mode]
    a_spec = pl.BlockSpec((tk, tm), lambda j, i, k: (k, i)) if mode == "tn" else pl.BlockSpec((tm, tk), lambda j, i, k: (i, k))
    b_spec = pl.BlockSpec((tn, tk), lambda j, i, k: (j, k)) if mode == "nt" else pl.BlockSpec((tk, tn), lambda j, i, k: (k, j))
    in_specs, args = [a_spec, b_spec], [a, b]
    if bias is not None:
        in_specs.append(pl.BlockSpec((1, tn), lambda j, i, k: (0, j)))
        args.append(bias)
    aliases = {}
    if acc_in is not None:
        aliases = {len(args): 0}
        in_specs.append(pl.BlockSpec((tm, tn), lambda j, i, k: (i, j)))
        args.append(acc_in)
    n_total, col0, prev = (N, 0, None) if into is None else into
    assert col0 % tn == 0
    jb = col0 // tn
    if prev is not None:
        aliases = {len(args): 0}
        in_specs.append(pl.BlockSpec(memory_space=pl.ANY))
        args.append(prev)
    n_in = len(args)

    def body(*refs):
        a_ref, b_ref = refs[0], refs[1]
        pos = 2
        bias_ref = acc_in_ref = None
        if bias is not None:
            bias_ref = refs[pos]
            pos += 1
        if acc_in is not None:
            acc_in_ref = refs[pos]
        pos = n_in
        o_ref = refs[pos]
        part = _dot(a_ref[...].astype(MXU), b_ref[...].astype(MXU), dims)

        def finish(acc):
            if bias_ref is not None:
                acc = acc + bias_ref[...]
            if acc_in_ref is not None:
                acc = acc + acc_in_ref[...]
            o_ref[...] = acc.astype(out_dtype)

        if gk == 1:
            finish(part)
        else:
            acc_ref = refs[pos + 1]
            k = pl.program_id(2)

            @pl.when(k == 0)
            def _():
                acc_ref[...] = part

            @pl.when(k > 0)
            def _():
                acc_ref[...] += part

            @pl.when(k == gk - 1)
            def _():
                finish(acc_ref[...])

    return _call(
        body, name=name, grid=(N // tn, M // tm, gk), in_specs=in_specs,
        out_specs=pl.BlockSpec((tm, tn), lambda j, i, k: (i, j + jb)), out_shape=_sds((M, n_total), out_dtype),
        scratch=[pltpu.VMEM((tm, tn), F32)] if gk > 1 else [], aliases=aliases,
    )(*args)


def _rt(w, cb=0):
    return pl.BlockSpec((TM, w), lambda i: (i, cb))


def _ct(w):
    return pl.BlockSpec((w, TM), lambda i: (0, i))


def _vec(w):
    return pl.BlockSpec((1, w), lambda i: (0, 0))


def _part(w):
    return pl.BlockSpec((1, 1, w), lambda i: (i, 0, 0))


def _mod(ref, k):
    return ref[0, :, k * D:(k + 1) * D]


def _colsum(x):
    return jnp.sum(x, axis=0, keepdims=True)


def _ln_stats(s):
    mu = jnp.mean(s, axis=1, keepdims=True)
    cen = s - mu
    var = jnp.mean(cen * cen, axis=1, keepdims=True)
    rstd = lax.rsqrt(var + LN_EPS)
    return cen * rstd, rstd


def _modulate_cast(x, modt, k_shift):
    T = x.shape[0]
    nt = T // TM

    def body(x_ref, mod_ref, h_ref, ht_ref):
        h = (x_ref[...] * (1.0 + _mod(mod_ref, k_shift + 1)) + _mod(mod_ref, k_shift)).astype(MXU)
        h_ref[...] = h
        ht_ref[...] = h.T

    return _call(body, name="modulate", grid=(nt,), in_specs=[_rt(D), _part(N_MOD * D)], out_specs=[_rt(D), _ct(D)],
                 out_shape=[_sds((T, D), MXU), _sds((D, T), MXU)])(x, modt)


def _resid_ln(x, br, modt, k_gate, g, b, mod_next=None, k_shift_next=0):
    T = x.shape[0]
    nt = T // TM
    with_h = mod_next is not None

    def body(*refs):
        x_ref, br_ref, mod_ref, g_ref, b_ref = refs[:5]
        s = ALPHA * x_ref[...] + _mod(mod_ref, k_gate) * br_ref[...]
        xhat, _ = _ln_stats(s)
        y = xhat * g_ref[...] + b_ref[...]
        if with_h:
            modn_ref, y_ref, h_ref, ht_ref = refs[5:]
            y_ref[...] = y
            h = (y * (1.0 + _mod(modn_ref, k_shift_next + 1)) + _mod(modn_ref, k_shift_next)).astype(MXU)
            h_ref[...] = h
            ht_ref[...] = h.T
        else:
            refs[5][...] = y

    in_specs = [_rt(D), _rt(D), _part(N_MOD * D), _vec(D), _vec(D)]
    args = [x, br, modt, g, b]
    if with_h:
        in_specs.append(_part(N_MOD * D))
        args.append(mod_next)
        return _call(body, name="resid_ln_mod", grid=(nt,), in_specs=in_specs, out_specs=[_rt(D), _rt(D), _ct(D)],
                     out_shape=[_sds((T, D), F32), _sds((T, D), MXU), _sds((D, T), MXU)])(*args)
    return _call(body, name="resid_ln", grid=(nt,), in_specs=in_specs, out_specs=_rt(D), out_shape=_sds((T, D), F32))(*args)


def _ln_bwd(dy_part, x, br, modt, k_gate, g, dh=None, y=None, mod_next=None, k_shift_next=0):
    T = x.shape[0]
    nt = T // TM
    with_h = dh is not None

    def body(*refs):
        if with_h:
            dyp_ref, x_ref, br_ref, mod_ref, g_ref, dh_ref, y_ref, modn_ref = refs[:8]
            outs = refs[8:]
        else:
            dyp_ref, x_ref, br_ref, mod_ref, g_ref = refs[:5]
            outs = refs[5:]
        dx_ref, dbr_ref, dgate_ref, dlg_ref, dlb_ref, dbsum_ref = outs[:6]
        i = pl.program_id(0)

        @pl.when(i == 0)
        def _():
            dlg_ref[...] = jnp.zeros_like(dlg_ref)
            dlb_ref[...] = jnp.zeros_like(dlb_ref)
            dbsum_ref[...] = jnp.zeros_like(dbsum_ref)

        dy = dyp_ref[...]
        if with_h:
            dshift_ref, dscale_ref = outs[6:]
            dhv = dh_ref[...]
            dy = dy + dhv * (1.0 + _mod(modn_ref, k_shift_next + 1))
            dshift_ref[0] = _colsum(dhv)
            dscale_ref[0] = _colsum(dhv * y_ref[...])
        gate = _mod(mod_ref, k_gate)
        brv = br_ref[...]
        s = ALPHA * x_ref[...] + gate * brv
        xhat, rstd = _ln_stats(s)
        dlg_ref[...] += _colsum(dy * xhat)
        dlb_ref[...] += _colsum(dy)
        dyg = dy * g_ref[...]
        m1 = jnp.mean(dyg, axis=1, keepdims=True)
        m2 = jnp.mean(dyg * xhat, axis=1, keepdims=True)
        ds = rstd * (dyg - m1 - xhat * m2)
        dx_ref[...] = ALPHA * ds
        dbr = gate * ds
        dbr_ref[...] = dbr.astype(MXU)
        dbsum_ref[...] += _colsum(dbr)
        dgate_ref[0] = _colsum(ds * brv)

    in_specs = [_rt(D), _rt(D), _rt(D), _part(N_MOD * D), _vec(D)]
    args = [dy_part, x, br, modt, g]
    out_specs = [_rt(D), _rt(D), _part(D), _vec(D), _vec(D), _vec(D)]
    out_shape = [_sds((T, D), F32), _sds((T, D), MXU), _sds((nt, 1, D), F32), _sds((1, D), F32), _sds((1, D), F32), _sds((1, D), F32)]
    if with_h:
        in_specs += [_rt(D), _rt(D), _part(N_MOD * D)]
        args += [dh, y, mod_next]
        out_specs += [_part(D), _part(D)]
        out_shape += [_sds((nt, 1, D), F32), _sds((nt, 1, D), F32)]
    return _call(body, name="ln_bwd_mod" if with_h else "ln_bwd", grid=(nt,), in_specs=in_specs, out_specs=out_specs,
                 out_shape=out_shape)(*args)


def _mod_bwd(dx_part, dh, x, modt, k_shift):
    T = x.shape[0]
    nt = T // TM

    def body(dxp_ref, dh_ref, x_ref, mod_ref, dx_ref, dshift_ref, dscale_ref):
        dhv = dh_ref[...]
        dx_ref[...] = dxp_ref[...] + dhv * (1.0 + _mod(mod_ref, k_shift + 1))
        dshift_ref[0] = _colsum(dhv)
        dscale_ref[0] = _colsum(dhv * x_ref[...])

    return _call(body, name="mod_bwd", grid=(nt,), in_specs=[_rt(D), _rt(D), _rt(D), _part(N_MOD * D)],
                 out_specs=[_rt(D), _part(D), _part(D)],
                 out_shape=[_sds((T, D), F32), _sds((nt, 1, D), F32), _sds((nt, 1, D), F32)])(dx_part, dh, x, modt)


def _loss_grad(y, target, tpe, ncq):
    T = y.shape[0]
    nt = T // TM
    nl = tpe - ncq

    def body(y_ref, t_ref, dy_ref, loss_ref):
        i = pl.program_id(0)

        @pl.when(i == 0)
        def _():
            loss_ref[...] = jnp.zeros_like(loss_ref)

        @pl.when(i % tpe < ncq)
        def _():
            dy_ref[...] = jnp.zeros_like(dy_ref)

        @pl.when(i % tpe >= ncq)
        def _():
            err = y_ref[...] - t_ref[...]
            dy_ref[...] = err * (1.0 / D)
            loss_ref[...] += (0.5 / D) * jnp.sum(_colsum(err * err), axis=1, keepdims=True)

    tgt_spec = pl.BlockSpec((TM, D), lambda i: ((i // tpe) * nl + jnp.maximum(i % tpe - ncq, 0), 0))
    return _call(body, name="loss_grad", grid=(nt,), in_specs=[_rt(D), tgt_spec], out_specs=[_rt(D), _vec(128)],
                 out_shape=[_sds((T, D), F32), _sds((1, 128), F32)])(y, target)


def _rope_partner(x):
    lane = lax.broadcasted_iota(jnp.int32, x.shape, 1)
    first = (lane % (2 * ROPE_PAIRS)) < ROPE_PAIRS
    return jnp.where(first, pltpu.roll(x, HD - ROPE_PAIRS, 1), pltpu.roll(x, ROPE_PAIRS, 1))


def _qk_prep(z_qkv, cos_t, sin_t, q_gain, k_gain, tpe):
    T = z_qkv.shape[0]
    nt = T // TM

    def body(z_ref, cos_ref, sin_ref, qg_ref, kg_ref, o_ref):
        cos, sin = cos_ref[...], sin_ref[...]
        for h in range(NH + NKV):
            sl = slice(h * HD, (h + 1) * HD)
            t = z_ref[:, sl]
            gain = qg_ref[...] if h < NH else kg_ref[...]
            n = t * lax.rsqrt(jnp.mean(t * t, axis=1, keepdims=True) + RMS_EPS) * gain
            o_ref[:, sl] = (n * cos + _rope_partner(n) * sin).astype(MXU)
        o_ref[:, (NH + NKV) * HD:] = z_ref[:, (NH + NKV) * HD:].astype(MXU)

    tab = pl.BlockSpec((TM, HD), lambda i: (i % tpe, 0))
    return _call(body, name="qk_prep", grid=(nt,), in_specs=[_rt(QKV_W), tab, tab, _vec(HD), _vec(HD)], out_specs=_rt(QKV_W),
                 out_shape=_sds((T, QKV_W), MXU))(z_qkv, cos_t, sin_t, q_gain, k_gain)


def _qk_bwd(dq, dk, dv, z_qkv, cos_t, sin_t, q_gain, k_gain, tpe):
    T = z_qkv.shape[0]
    nt = T // TM

    def body(dq_ref, dk_ref, dv_ref, z_ref, cos_ref, sin_ref, qg_ref, kg_ref, dz_ref, dqg_ref, dkg_ref, bsum_ref):
        i = pl.program_id(0)

        @pl.when(i == 0)
        def _():
            dqg_ref[...] = jnp.zeros_like(dqg_ref)
            dkg_ref[...] = jnp.zeros_like(dkg_ref)
            bsum_ref[...] = jnp.zeros_like(bsum_ref)

        cos, sin = cos_ref[...], sin_ref[...]
        for h in range(NH + NKV):
            sl = slice(h * HD, (h + 1) * HD)
            dr = dq_ref[:, sl] if h < NH else dk_ref[:, (h - NH) * HD:(h - NH + 1) * HD]
            gain = qg_ref[...] if h < NH else kg_ref[...]
            dn = dr * cos + _rope_partner(dr * sin)
            t = z_ref[:, sl]
            rstd = lax.rsqrt(jnp.mean(t * t, axis=1, keepdims=True) + RMS_EPS)
            that = t * rstd
            dgain = _colsum(dn * that)
            if h < NH:
                dqg_ref[...] += dgain
            else:
                dkg_ref[...] += dgain
            dthat = dn * gain
            dt = rstd * (dthat - that * jnp.mean(dthat * that, axis=1, keepdims=True))
            dz_ref[:, sl] = dt.astype(MXU)
            bsum_ref[:, sl] += _colsum(dt)
        dvv = dv_ref[...]
        dz_ref[:, (NH + NKV) * HD:] = dvv.astype(MXU)
        bsum_ref[:, (NH + NKV) * HD:] += _colsum(dvv)

    tab = pl.BlockSpec((TM, HD), lambda i: (i % tpe, 0))
    return _call(body, name="qk_bwd", grid=(nt,),
                 in_specs=[_rt(NH * HD), _rt(KV_W), _rt(KV_W), _rt(QKV_W), tab, tab, _vec(HD), _vec(HD)],
                 out_specs=[_rt(QKV_W), _vec(HD), _vec(HD), _vec(QKV_W)],
                 out_shape=[_sds((T, QKV_W), MXU), _sds((1, HD), F32), _sds((1, HD), F32), _sds((1, QKV_W), F32)],
                 )(dq, dk, dv, z_qkv, cos_t, sin_t, q_gain, k_gain)


def _ln_silu(hc, g, b):
    T = hc.shape[0]

    def body(h_ref, g_ref, b_ref, o_ref, ot_ref):
        xhat, _ = _ln_stats(h_ref[...])
        n = xhat * g_ref[...] + b_ref[...]
        sw = (n * _sigmoid(n)).astype(MXU)
        o_ref[...] = sw
        ot_ref[...] = sw.T

    return _call(body, name="ln_silu", grid=(T // TM,), in_specs=[_rt(D), _vec(D), _vec(D)], out_specs=[_rt(D), _ct(D)],
                 out_shape=[_sds((T, D), MXU), _sds((D, T), MXU)])(hc, g, b)


def _ln_silu_bwd(dsw, hc, g, b):
    T = hc.shape[0]

    def body(d_ref, h_ref, g_ref, b_ref, dh_ref, dg_ref, db_ref, dcb_ref):
        i = pl.program_id(0)

        @pl.when(i == 0)
        def _():
            dg_ref[...] = jnp.zeros_like(dg_ref)
            db_ref[...] = jnp.zeros_like(db_ref)
            dcb_ref[...] = jnp.zeros_like(dcb_ref)

        xhat, rstd = _ln_stats(h_ref[...])
        n = xhat * g_ref[...] + b_ref[...]
        sg = _sigmoid(n)
        dn = d_ref[...] * (sg * (1.0 + n * (1.0 - sg)))
        dg_ref[...] += _colsum(dn * xhat)
        db_ref[...] += _colsum(dn)
        dng = dn * g_ref[...]
        m1 = jnp.mean(dng, axis=1, keepdims=True)
        m2 = jnp.mean(dng * xhat, axis=1, keepdims=True)
        dh = rstd * (dng - m1 - xhat * m2)
        dh_ref[...] = dh
        dcb_ref[...] += _colsum(dh)

    return _call(body, name="ln_silu_bwd", grid=(T // TM,), in_specs=[_rt(D), _rt(D), _vec(D), _vec(D)],
                 out_specs=[_rt(D), _vec(D), _vec(D), _vec(D)],
                 out_shape=[_sds((T, D), F32)] + [_sds((1, D), F32)] * 3)(dsw, hc, g, b)


def _merge(attn, conv_o, pool_o, z_gate):
    T = attn.shape[0]

    def body(a_ref, c_ref, p_ref, zg_ref, m_ref, mt_ref):
        m = (_sigmoid(zg_ref[:, 0:D]) * a_ref[...] + _sigmoid(zg_ref[:, D:2 * D]) * c_ref[...]
             + _sigmoid(zg_ref[:, 2 * D:3 * D]) * p_ref[...]).astype(MXU)
        m_ref[...] = m
        mt_ref[...] = m.T

    return _call(body, name="merge", grid=(T // TM,), in_specs=[_rt(D), _rt(D), _rt(D), _rt(N_GATE)], out_specs=[_rt(D), _ct(D)],
                 out_shape=[_sds((T, D), MXU), _sds((D, T), MXU)])(attn, conv_o, pool_o, z_gate)


def _merge_bwd(dm, attn, conv_o, pool_o, z_gate):
    T = attn.shape[0]

    def body(dm_ref, a_ref, c_ref, p_ref, zg_ref, da_ref, dc_ref, dp_ref, dzg_ref, dcsum_ref, gsum_ref):
        i = pl.program_id(0)

        @pl.when(i == 0)
        def _():
            dcsum_ref[...] = jnp.zeros_like(dcsum_ref)
            gsum_ref[...] = jnp.zeros_like(gsum_ref)

        dmv = dm_ref[...]
        for k, (br_ref, out_ref) in enumerate(((a_ref, da_ref), (c_ref, dc_ref), (p_ref, dp_ref))):
            gk = _sigmoid(zg_ref[:, k * D:(k + 1) * D])
            dbr = dmv * gk
            out_ref[...] = dbr.astype(out_ref.dtype)
            if k == 1:
                dcsum_ref[...] += _colsum(dbr)
            dzg = dmv * br_ref[...] * gk * (1.0 - gk)
            dzg_ref[:, k * D:(k + 1) * D] = dzg.astype(MXU)
            gsum_ref[:, k * D:(k + 1) * D] += _colsum(dzg)

    return _call(body, name="merge_bwd", grid=(T // TM,), in_specs=[_rt(D), _rt(D), _rt(D), _rt(D), _rt(N_GATE)],
                 out_specs=[_rt(D), _rt(D), _rt(D), _rt(N_GATE), _vec(D), _vec(N_GATE)],
                 out_shape=[_sds((T, D), F32), _sds((T, D), MXU), _sds((T, D), F32), _sds((T, N_GATE), MXU),
                            _sds((1, D), F32), _sds((1, N_GATE), F32)])(dm, attn, conv_o, pool_o, z_gate)


def _softmax_parts(q, k):
    s = _dot(q, k, NT)
    p = jnp.exp2((s - jnp.max(s, axis=1, keepdims=True)) * (ATTN_SCALE * LOG2_E))
    return p, 1.0 / jnp.sum(p, axis=1, keepdims=True)


def _attn_specs(nq):
    q_spec = pl.BlockSpec((TM, QG * HD), lambda b, h, q: (b * nq + q, h))
    k_spec = pl.BlockSpec((nq * TM, HD), lambda b, h, q: (b, NH + h))
    v_spec = pl.BlockSpec((nq * TM, HD), lambda b, h, q: (b, NH + NKV + h))
    return q_spec, k_spec, v_spec


class _Carried:
    def __init__(self, ex, n_in, n_out, n_scratch):
        self.ex, self.n_in, self.n_out, self.n_scratch = ex, n_in, n_out, n_scratch
        self.ci = len(ex.inputs) if ex else 0
        self.co = len(ex.out_shape) if ex else 0

    def in_specs(self):
        return [ANY] * self.ci

    def out_specs(self):
        return [ANY] * self.co

    def split(self, refs):
        a = self.n_in
        b = a + self.ci
        c = b + self.n_out
        d = c + self.co
        e = d + self.n_scratch
        return (refs[:a], refs[b:c], refs[d:e]), (refs[a:b], refs[c:d], refs[e:])

    def before(self, step, parts):
        if self.ex:
            pl.when(step == 0)(lambda: self.ex.start(*parts))

    def after(self, step, n_steps, parts):
        if self.ex:
            pl.when(step == (3 * n_steps) // 4)(lambda: self.ex.mid(*parts))
            pl.when(step == n_steps - 1)(lambda: self.ex.finish(*parts))


def _attn_fwd(qkv, B, C, R, carry=None):
    nq, ncq = R // TM, C // TM
    car = _Carried(carry, 3, 1, 0)

    def body(*refs):
        (q_ref, k_ref, v_ref), (o_ref,), _ = car.split(refs)[0]
        parts = car.split(refs)[1]
        qi = pl.program_id(2)
        step = (pl.program_id(0) * NKV + pl.program_id(1)) * nq + qi
        car.before(step, parts)

        def attend(L):
            k, v = k_ref[0:L, :], v_ref[0:L, :]
            for i in range(QG):
                sl = slice(i * HD, (i + 1) * HD)
                p, inv_l = _softmax_parts(q_ref[:, sl], k)
                o_ref[:, sl] = _dot(p.astype(MXU), v, NN) * inv_l

        pl.when(qi < ncq)(functools.partial(attend, C))
        pl.when(qi >= ncq)(functools.partial(attend, R))
        car.after(step, B * NKV * nq, parts)

    q_spec, k_spec, v_spec = _attn_specs(nq)
    res = _call(body, name="attn_fwd", grid=(B, NKV, nq), in_specs=[q_spec, k_spec, v_spec] + car.in_specs(),
                out_specs=[q_spec] + car.out_specs(), out_shape=[_sds((B * R, NH * HD), F32)] + list(carry.out_shape if carry else []),
                scratch=list(carry.scratch) if carry else [])(qkv, qkv, qkv, *(carry.inputs if carry else []))
    return res[0], res[1:]


def _attn_bwd(qkv, o, do, B, C, R, carry=None):
    nq, ncq = R // TM, C // TM
    car = _Carried(carry, 5, 3, 2)

    def body(*refs):
        (q_ref, k_ref, v_ref, o_ref, do_ref), (dq_ref, dk_ref, dv_ref), (dkt, dvt) = car.split(refs)[0]
        parts = car.split(refs)[1]
        qi = pl.program_id(2)
        step = (pl.program_id(0) * NKV + pl.program_id(1)) * nq + qi
        car.before(step, parts)

        @pl.when(qi == 0)
        def _():
            dkt[...] = jnp.zeros_like(dkt)
            dvt[...] = jnp.zeros_like(dvt)

        def bwd(L):
            k, v = k_ref[0:L, :], v_ref[0:L, :]
            for i in range(QG):
                sl = slice(i * HD, (i + 1) * HD)
                q = q_ref[:, sl]
                p, inv_l = _softmax_parts(q, k)
                dov = do_ref[:, sl]
                dp = _dot(dov.astype(MXU), v, NT)
                dl = jnp.sum(dov * o_ref[:, sl], axis=1, keepdims=True)
                ds = (p * ((dp - dl) * (inv_l * ATTN_SCALE))).astype(MXU)
                dq_ref[:, sl] = _dot(ds, k, NN)
                dkt[:, 0:L] += _dot(q, ds, TN)
                dvt[:, 0:L] += _dot((dov * inv_l).astype(MXU), p.astype(MXU), TN)

        pl.when(qi < ncq)(functools.partial(bwd, C))
        pl.when(qi >= ncq)(functools.partial(bwd, R))

        @pl.when(qi == nq - 1)
        def _():
            dk_ref[...] = dkt[...].T
            dv_ref[...] = dvt[...].T

        car.after(step, B * NKV * nq, parts)

    q_spec, k_spec, v_spec = _attn_specs(nq)
    kv_out = pl.BlockSpec((R, HD), lambda b, h, q: (b, h))
    res = _call(body, name="attn_bwd", grid=(B, NKV, nq), in_specs=[q_spec, k_spec, v_spec, q_spec, q_spec] + car.in_specs(),
                out_specs=[q_spec, kv_out, kv_out] + car.out_specs(),
                out_shape=[_sds((B * R, NH * HD), F32), _sds((B * R, KV_W), F32), _sds((B * R, KV_W), F32)]
                + list(carry.out_shape if carry else []),
                scratch=[pltpu.VMEM((HD, R), F32), pltpu.VMEM((HD, R), F32)] + list(carry.scratch if carry else []),
                )(qkv, qkv, qkv, o, do, *(carry.inputs if carry else []))
    return res[0], res[1], res[2], res[3:]


def _segments(C, S):
    return ((0, GAP, C), (C, 2 * GAP + C, S))


def _padded_rows(C, S):
    return 3 * GAP + C + S


def _zero_gaps(pad_ref, C, S):
    for off in (0, GAP + C, 2 * GAP + C + S):
        pad_ref[off:off + GAP, :] = jnp.zeros((GAP, pad_ref.shape[1]), pad_ref.dtype)


def _chunks(n, ch, fn):
    def step(i, carry):
        fn(pl.multiple_of(i * ch, ch))
        return carry

    lax.fori_loop(0, n // ch, step, 0)


class _Window:
    def __init__(self, pad_ref, row, ch, moved_ref=None):
        self.pad_ref, self.row, self.ch, self.moved_ref = pad_ref, row, ch, moved_ref
        self.win = pad_ref[pl.ds(row - GAP, ch + 2 * GAP), :]
        self.done = set()

    def at(self, off):
        if self.moved_ref is None:
            return self.win[GAP + off:GAP + off + self.ch, :]
        s, q = off % 8, off // 8
        if s == 0:
            return self.pad_ref[pl.ds(pl.multiple_of(self.row + off, 8), self.ch), :]
        if s not in self.done:
            self.moved_ref[s] = self.win[s:s + self.ch + 2 * GAP - 8, :]
            self.done.add(s)
        return self.moved_ref[s, GAP + 8 * q:GAP + 8 * q + self.ch, :]


def _taps(pad_ref, w, row, ch, n_taps, flip=False, moved_ref=None):
    half = (n_taps - 1) // 2
    win = _Window(pad_ref, row, ch, moved_ref)
    acc = None
    for k in range(n_taps):
        term = w[k:k + 1, :] * win.at((half - k) if flip else (k - half))
        acc = term if acc is None else acc + term
    return acc


def _tap_grads(dw_ref, d, pad_ref, row, ch, n_taps, moved_ref=None):
    half = (n_taps - 1) // 2
    win = _Window(pad_ref, row, ch, moved_ref)
    for k in range(n_taps):
        prod = d * win.at(k - half)
        dw_ref[k] += jnp.sum(prod.reshape(ch // 8, 8, prod.shape[1]), axis=0)


def _moved_scratch(ch, width):
    return pltpu.VMEM((8, ch + 2 * GAP - 8, width), F32)


def _conv_fwd(z_conv, w, bias, B, C, S, cw=128, ch=128):
    R = C + S
    nj = CONV_CH // cw
    segs = _segments(C, S)

    def body(a_ref, g_ref, w_ref, b_ref, o_ref, pad, moved):
        _zero_gaps(pad, C, S)
        wv = w_ref[...]
        for so, po, n in segs:
            def fill(r, so=so, po=po):
                pad[pl.ds(po + r, ch), :] = a_ref[pl.ds(so + r, ch), :] * _sigmoid(g_ref[pl.ds(so + r, ch), :])

            _chunks(n, ch, fill)
        for so, po, n in segs:
            def conv(r, so=so, po=po):
                o_ref[pl.ds(so + r, ch), :] = _taps(pad, wv, po + r, ch, CONV_K, moved_ref=moved) + b_ref[...]

            _chunks(n, ch, conv)

    return _call(
        body, name="conv_fwd", grid=(nj, B),
        in_specs=[pl.BlockSpec((R, cw), lambda j, b: (b, j)), pl.BlockSpec((R, cw), lambda j, b: (b, nj + j)),
                  pl.BlockSpec((32, cw), lambda j, b: (0, j)), pl.BlockSpec((1, cw), lambda j, b: (0, j))],
        out_specs=pl.BlockSpec((R, cw), lambda j, b: (b, j)), out_shape=_sds((B * R, CONV_CH), F32),
        scratch=[pltpu.VMEM((_padded_rows(C, S), cw), F32), _moved_scratch(ch, cw)])(z_conv, z_conv, w, bias)


def _conv_bwd(dhc, z_conv, w, B, C, S, cw=128, ch=128):
    R = C + S
    nj = CONV_CH // cw
    segs = _segments(C, S)

    def body(d_ref, a_ref, g_ref, w_ref, da_ref, dg_ref, dw_ref, sa_ref, sg_ref, gpad, dpad, dwacc, moved):
        b = pl.program_id(1)

        @pl.when(b == 0)
        def _():
            dw_ref[...] = jnp.zeros_like(dw_ref)
            sa_ref[...] = jnp.zeros_like(sa_ref)
            sg_ref[...] = jnp.zeros_like(sg_ref)

        _zero_gaps(gpad, C, S)
        _zero_gaps(dpad, C, S)
        dwacc[...] = jnp.zeros_like(dwacc)
        wv = w_ref[...]
        for so, po, n in segs:
            def fill(r, so=so, po=po):
                gpad[pl.ds(po + r, ch), :] = a_ref[pl.ds(so + r, ch), :] * _sigmoid(g_ref[pl.ds(so + r, ch), :])
                dpad[pl.ds(po + r, ch), :] = d_ref[pl.ds(so + r, ch), :]

            _chunks(n, ch, fill)
        for so, po, n in segs:
            def step(r, so=so, po=po):
                _tap_grads(dwacc, dpad[pl.ds(po + r, ch), :], gpad, po + r, ch, CONV_K, moved_ref=moved)
                dglu = _taps(dpad, wv, po + r, ch, CONV_K, flip=True, moved_ref=moved)
                av = a_ref[pl.ds(so + r, ch), :]
                sg = _sigmoid(g_ref[pl.ds(so + r, ch), :])
                da = dglu * sg
                dg = dglu * av * sg * (1.0 - sg)
                da_ref[pl.ds(so + r, ch), :] = da.astype(MXU)
                dg_ref[pl.ds(so + r, ch), :] = dg.astype(MXU)
                sa_ref[...] += _colsum(da)
                sg_ref[...] += _colsum(dg)

            _chunks(n, ch, step)
        for k in range(CONV_K):
            dw_ref[k:k + 1, :] += _colsum(dwacc[k])

    blk = pl.BlockSpec((R, cw), lambda j, b: (b, j))
    acc1 = pl.BlockSpec((1, cw), lambda j, b: (0, j))
    return _call(
        body, name="conv_bwd", grid=(nj, B),
        in_specs=[blk, blk, pl.BlockSpec((R, cw), lambda j, b: (b, nj + j)), pl.BlockSpec((32, cw), lambda j, b: (0, j))],
        out_specs=[blk, blk, pl.BlockSpec((32, cw), lambda j, b: (0, j)), acc1, acc1],
        out_shape=[_sds((B * R, CONV_CH), MXU), _sds((B * R, CONV_CH), MXU), _sds((32, CONV_CH), F32),
                   _sds((1, CONV_CH), F32), _sds((1, CONV_CH), F32)],
        scratch=[pltpu.VMEM((_padded_rows(C, S), cw), F32), pltpu.VMEM((_padded_rows(C, S), cw), F32),
                 pltpu.VMEM((32, 8, cw), F32), _moved_scratch(ch, cw)])(dhc, z_conv, z_conv, w)


def _ffn_mid(up, w, bias, B, C, S, cw=256, ch=64):
    R = C + S
    nj = D_FF // cw
    segs = _segments(C, S)

    def body(a_ref, u_ref, w_ref, b_ref, f_ref, ft_ref, pad):
        _zero_gaps(pad, C, S)
        wv = w_ref[...]
        for so, po, n in segs:
            def fill(r, so=so, po=po):
                pad[pl.ds(po + r, ch), :] = a_ref[pl.ds(so + r, ch), :]

            _chunks(n, ch, fill)
        for so, po, n in segs:
            def conv(r, so=so, po=po):
                ac = _taps(pad, wv, po + r, ch, FFN_K) + b_ref[...]
                f_ref[pl.ds(so + r, ch), :] = (ac * _sigmoid(ac) * u_ref[pl.ds(so + r, ch), :]).astype(MXU)

            _chunks(n, ch, conv)
        ft_ref[...] = f_ref[...].T

    return _call(
        body, name="ffn_mid", grid=(nj, B),
        in_specs=[pl.BlockSpec((R, cw), lambda j, b: (b, j)), pl.BlockSpec((R, cw), lambda j, b: (b, nj + j)),
                  pl.BlockSpec((8, cw), lambda j, b: (0, j)), pl.BlockSpec((1, cw), lambda j, b: (0, j))],
        out_specs=[pl.BlockSpec((R, cw), lambda j, b: (b, j)), pl.BlockSpec((cw, R), lambda j, b: (j, b))],
        out_shape=[_sds((B * R, D_FF), MXU), _sds((D_FF, B * R), MXU)],
        scratch=[pltpu.VMEM((_padded_rows(C, S), cw), F32)])(up, up, w, bias)


def _ffn_mid_bwd(df, up, w, bias, B, C, S, cw=128, ch=128):
    R = C + S
    nj = D_FF // cw
    segs = _segments(C, S)

    def body(d_ref, a_ref, u_ref, w_ref, b_ref, da_ref, du_ref, dw_ref, db_ref, apad, dpad, dwacc):
        b = pl.program_id(1)

        @pl.when(b == 0)
        def _():
            dw_ref[...] = jnp.zeros_like(dw_ref)
            db_ref[...] = jnp.zeros_like(db_ref)

        _zero_gaps(apad, C, S)
        _zero_gaps(dpad, C, S)
        dwacc[...] = jnp.zeros_like(dwacc)
        wv = w_ref[...]
        for so, po, n in segs:
            def fill(r, so=so, po=po):
                apad[pl.ds(po + r, ch), :] = a_ref[pl.ds(so + r, ch), :]

            _chunks(n, ch, fill)
        for so, po, n in segs:
            def first(r, so=so, po=po):
                ac = _taps(apad, wv, po + r, ch, FFN_K) + b_ref[...]
                sg = _sigmoid(ac)
                dfv = d_ref[pl.ds(so + r, ch), :]
                du_ref[pl.ds(so + r, ch), :] = (dfv * ac * sg).astype(MXU)
                dac = dfv * u_ref[pl.ds(so + r, ch), :] * (sg * (1.0 + ac * (1.0 - sg)))
                dpad[pl.ds(po + r, ch), :] = dac
                db_ref[...] += _colsum(dac)

            _chunks(n, ch, first)
        for so, po, n in segs:
            def second(r, so=so, po=po):
                _tap_grads(dwacc, dpad[pl.ds(po + r, ch), :], apad, po + r, ch, FFN_K)
                da_ref[pl.ds(so + r, ch), :] = _taps(dpad, wv, po + r, ch, FFN_K, flip=True).astype(MXU)

            _chunks(n, ch, second)
        for k in range(FFN_K):
            dw_ref[k:k + 1, :] += _colsum(dwacc[k])

    blk = pl.BlockSpec((R, cw), lambda j, b: (b, j))
    return _call(
        body, name="ffn_mid_bwd", grid=(nj, B),
        in_specs=[blk, blk, pl.BlockSpec((R, cw), lambda j, b: (b, nj + j)), pl.BlockSpec((8, cw), lambda j, b: (0, j)),
                  pl.BlockSpec((1, cw), lambda j, b: (0, j))],
        out_specs=[blk, blk, pl.BlockSpec((8, cw), lambda j, b: (0, j)), pl.BlockSpec((1, cw), lambda j, b: (0, j))],
        out_shape=[_sds((B * R, D_FF), MXU), _sds((B * R, D_FF), MXU), _sds((8, D_FF), F32), _sds((1, D_FF), F32)],
        scratch=[pltpu.VMEM((_padded_rows(C, S), cw), F32), pltpu.VMEM((_padded_rows(C, S), cw), F32),
                 pltpu.VMEM((8, 8, cw), F32)])(df, up, up, w, bias)


def _window_count(r, ch, n, w):
    t = r + lax.broadcasted_iota(jnp.int32, (ch, 1), 0)
    return (jnp.minimum(t + w // 2, n) - jnp.maximum(t - w // 2, 0)).astype(F32)


def _pool_fwd(z_pool, pool_w, pool_scale, B, C, S, ch=128):
    R = C + S
    gch = POOL_GCH
    segs = _segments(C, S)

    def body(u_ref, pw_ref, sc_ref, pooled_ref, po_ref, pad):
        g = pl.program_id(1)
        _zero_gaps(pad, C, S)
        for so, po, n in segs:
            def fill(r, so=so, po=po):
                pad[pl.ds(po + r, ch), :] = u_ref[pl.ds(so + r, ch), :]

            _chunks(n, ch, fill)
        for gi, w in enumerate(POOL_WINDOWS):
            @pl.when(g == gi)
            def _(w=w):
                for so, po, n in segs:
                    def step(r, so=so, po=po, n=n):
                        win = _Window(pad, po + r, ch)
                        acc = win.at(-(w // 2))
                        for o in range(1 - w // 2, w // 2):
                            acc = acc + win.at(o)
                        pooled = (acc / _window_count(r, ch, n, w) - win.at(0)).astype(MXU)
                        pooled_ref[pl.ds(so + r, ch), :] = pooled
                        po_ref[pl.ds(so + r, ch), :] = _dot(pooled, pw_ref[0], NN) * sc_ref[...]

                    _chunks(n, ch, step)

    blk = pl.BlockSpec((R, gch), lambda b, g: (b, g))
    return _call(
        body, name="pool_fwd", grid=(B, len(POOL_WINDOWS)),
        in_specs=[blk, pl.BlockSpec((1, gch, gch), lambda b, g: (g, 0, 0)), pl.BlockSpec((1, gch), lambda b, g: (0, g))],
        out_specs=[blk, blk], out_shape=[_sds((B * R, POOL_CH), MXU), _sds((B * R, POOL_CH), F32)],
        scratch=[pltpu.VMEM((_padded_rows(C, S), gch), F32)])(z_pool, pool_w, pool_scale)


def _pool_bwd(dpo, pooled, pool_w, pool_scale, B, C, S, ch=128):
    R = C + S
    gch = POOL_GCH
    segs = _segments(C, S)

    def body(d_ref, p_ref, pw_ref, sc_ref, du_ref, dpw_ref, dsc_ref, su_ref, qpad, dpl):
        g, b = pl.program_id(0), pl.program_id(1)

        @pl.when(b == 0)
        def _():
            dpw_ref[...] = jnp.zeros_like(dpw_ref)
            dsc_ref[...] = jnp.zeros_like(dsc_ref)
            su_ref[...] = jnp.zeros_like(su_ref)

        _zero_gaps(qpad, C, S)
        pw = pw_ref[0]
        for gi, w in enumerate(POOL_WINDOWS):
            @pl.when(g == gi)
            def _(w=w):
                for so, po, n in segs:
                    def first(r, so=so, po=po, n=n):
                        pv = p_ref[pl.ds(so + r, ch), :]
                        dv = d_ref[pl.ds(so + r, ch), :]
                        dsc_ref[...] += _colsum(dv * _dot(pv, pw, NN))
                        dmx = (dv * sc_ref[...]).astype(MXU)
                        dpw_ref[0] += _dot(pv, dmx, TN)
                        dp = _dot(dmx, pw, NT)
                        dpl[pl.ds(so + r, ch), :] = dp
                        qpad[pl.ds(po + r, ch), :] = dp / _window_count(r, ch, n, w)

                    _chunks(n, ch, first)
                for so, po, n in segs:
                    def second(r, so=so, po=po):
                        win = _Window(qpad, po + r, ch)
                        acc = win.at(1 - w // 2)
                        for o in range(2 - w // 2, w // 2 + 1):
                            acc = acc + win.at(o)
                        du = acc - dpl[pl.ds(so + r, ch), :]
                        du_ref[pl.ds(so + r, ch), :] = du.astype(MXU)
                        su_ref[...] += _colsum(du)

                    _chunks(n, ch, second)

    blk = pl.BlockSpec((R, gch), lambda g, b: (b, g))
    vec = pl.BlockSpec((1, gch), lambda g, b: (0, g))
    wblk = pl.BlockSpec((1, gch, gch), lambda g, b: (g, 0, 0))
    return _call(
        body, name="pool_bwd", grid=(len(POOL_WINDOWS), B), in_specs=[blk, blk, wblk, vec], out_specs=[blk, wblk, vec, vec],
        out_shape=[_sds((B * R, POOL_CH), MXU), _sds((len(POOL_WINDOWS), gch, gch), F32), _sds((1, POOL_CH), F32),
                   _sds((1, POOL_CH), F32)],
        scratch=[pltpu.VMEM((_padded_rows(C, S), gch), F32), pltpu.VMEM((R, gch), F32)])(dpo, pooled, pool_w, pool_scale)


def _silu_rows(cond):
    def body(c_ref, s_ref, d_ref):
        c = c_ref[...]
        sg = _sigmoid(c)
        s_ref[...] = (c * sg).astype(MXU)
        d_ref[...] = sg * (1.0 + c * (1.0 - sg))

    full = pl.BlockSpec(cond.shape, lambda i: (0, 0))
    return _call(body, name="silu_rows", grid=(1,), in_specs=[full], out_specs=[full, full],
                 out_shape=[_sds(cond.shape, MXU), _sds(cond.shape, F32)])(cond)


def _row_tile(rows, cols, n_bufs):
    cap = max(16, (16 * 2 ** 20) // (4 * n_bufs * max(cols, 128)))
    return rows if rows <= cap else _pick(rows, cap, 16)


def _adamw(parts, w, m, v, layer, prev, *, name):
    n_parts, rows, cols = parts.shape
    layers = w.shape[0]
    c1 = 1.0 - ADAM_B1 ** ADAM_STEP
    c2 = 1.0 - ADAM_B2 ** ADAM_STEP
    tr = _row_tile(rows, cols, n_parts + 7)

    def body(p_ref, w_ref, m_ref, v_ref, *rest):
        g_ref, d_ref, nm_ref, nv_ref = rest[-4:]
        g = p_ref[0].astype(F32)
        for k in range(1, n_parts):
            g = g + p_ref[k].astype(F32)
        nm = ADAM_B1 * m_ref[...] + (1.0 - ADAM_B1) * g
        nv = ADAM_B2 * v_ref[...] + (1.0 - ADAM_B2) * (g * g)
        g_ref[...] = g
        nm_ref[...] = nm
        nv_ref[...] = nv
        d_ref[...] = -ADAM_LR * ((nm / c1) / (jnp.sqrt(nv / c2) + ADAM_EPS) + ADAM_WD * w_ref[...])

    blk = pl.BlockSpec((None, tr, cols), lambda i: (layer, i, 0))
    in_specs = [pl.BlockSpec((n_parts, tr, cols), lambda i: (0, i, 0)), blk, blk, blk]
    args = [parts, w, m, v]
    aliases = {}
    if prev is not None:
        in_specs += [ANY] * 4
        aliases = {4 + k: k for k in range(4)}
        args += list(prev)
    return _call(body, name=name, grid=(rows // tr,), in_specs=in_specs, out_specs=[blk] * 4,
                 out_shape=[_sds((layers, rows, cols), F32)] * 4, aliases=aliases)(*args)


def _pair_sum(g, r1):
    _, rows, cols = g.shape
    c = lax.axis_index("c")
    g4 = g.reshape(4, 2, rows, cols)
    tr = _row_tile(rows, cols, 10)

    def body(c_ref, g_ref, r_ref, o_ref):
        o_ref[...] = (g_ref[...] + r_ref[...]).astype(WIRE)

    return _pcall(
        body, name="pair_sum", out_shape=_sds((4, rows, cols), WIRE),
        grid_spec=pltpu.PrefetchScalarGridSpec(
            num_scalar_prefetch=1, grid=(rows // tr,),
            in_specs=[pl.BlockSpec((4, None, tr, cols), lambda i, c_ref: (0, c_ref[0], i, 0)),
                      pl.BlockSpec((4, tr, cols), lambda i, c_ref: (0, i, 0))],
            out_specs=pl.BlockSpec((4, tr, cols), lambda i, c_ref: (0, i, 0))),
        compiler_params=pltpu.CompilerParams(dimension_semantics=("arbitrary",), vmem_limit_bytes=VMEM_MB * 2 ** 20),
    )(jnp.reshape(c, (1,)).astype(jnp.int32), g4, r1)


MESH = pl.DeviceIdType.MESH
ANY = pl.BlockSpec(memory_space=pl.ANY)


class _Exchange:
    inputs, out_shape, scratch = (), (), ()

    def start(self, ins, outs, sems):
        raise NotImplementedError

    def mid(self, ins, outs, sems):
        pass

    def finish(self, ins, outs, sems):
        raise NotImplementedError


def _run_exchange(ex, *, name):
    n_in, n_out = len(ex.inputs), len(ex.out_shape)

    def body(*refs):
        parts = refs[:n_in], refs[n_in:n_in + n_out], refs[n_in + n_out:]
        ex.start(*parts)
        ex.mid(*parts)
        ex.finish(*parts)

    return _pcall(body, name=name, out_shape=list(ex.out_shape), in_specs=[ANY] * n_in, out_specs=[ANY] * n_out,
                  scratch_shapes=list(ex.scratch))(*ex.inputs)


class _Gather(_Exchange):
    def __init__(self, shards, layer=None):
        self.inputs, self.layer, self.n_t = list(shards), layer, len(shards)
        self.out_shape = [_sds((N_DEV,) + tuple(s.shape if layer is None else s.shape[1:]), s.dtype) for s in shards]
        self.scratch = [pltpu.SemaphoreType.DMA((7 * self.n_t,)), pltpu.SemaphoreType.DMA((7 * self.n_t,)),
                        pltpu.SemaphoreType.DMA((self.n_t,))]

    def _place(self):
        x, y, c = lax.axis_index("x"), lax.axis_index("y"), lax.axis_index("c")
        return (x, y, c), (x, y, 1 - c), [(1 - x, y), (x, 1 - y), (1 - x, 1 - y)]

    def _own(self, ins, n):
        return ins[n] if self.layer is None else ins[n].at[self.layer]

    def _copy(self, ins, outs, sems, n, k, blk, to, own=False):
        dst = outs[n].at[4 * blk[0] + 2 * blk[1] + blk[2]]
        return pltpu.make_async_remote_copy(src_ref=self._own(ins, n) if own else dst, dst_ref=dst, send_sem=sems[0].at[n * 7 + k],
                                            recv_sem=sems[1].at[n * 7 + k], device_id=to, device_id_type=MESH)

    def _mine(self, ins, outs, sems, n):
        (x, y, c), _, _ = self._place()
        return pltpu.make_async_copy(self._own(ins, n), outs[n].at[4 * x + 2 * y + c], sems[2].at[n])

    def start(self, ins, outs, sems):
        me, sibling, chips = self._place()
        for n in range(self.n_t):
            self._mine(ins, outs, sems, n).start()
        for j, chip in enumerate(chips):
            for n in range(self.n_t):
                self._copy(ins, outs, sems, n, 1 + j, me, (*chip, me[2]), own=True).start()
        for n in range(self.n_t):
            self._copy(ins, outs, sems, n, 0, me, sibling, own=True).start()

    def mid(self, ins, outs, sems):
        me, sibling, chips = self._place()
        for j, chip in enumerate(chips):
            for n in range(self.n_t):
                self._copy(ins, outs, sems, n, 1 + j, (*chip, me[2]), me).wait_recv()
                self._copy(ins, outs, sems, n, 4 + j, (*chip, me[2]), sibling).start()

    def finish(self, ins, outs, sems):
        me, sibling, chips = self._place()
        for n in range(self.n_t):
            self._copy(ins, outs, sems, n, 0, sibling, me).wait_recv()
        for j, chip in enumerate(chips):
            for n in range(self.n_t):
                self._copy(ins, outs, sems, n, 4 + j, (*chip, 1 - me[2]), me).wait_recv()
        for j, chip in enumerate(chips):
            for n in range(self.n_t):
                self._copy(ins, outs, sems, n, 1 + j, me, (*chip, me[2]), own=True).wait_send()
                self._copy(ins, outs, sems, n, 4 + j, (*chip, me[2]), sibling).wait_send()
        for n in range(self.n_t):
            self._copy(ins, outs, sems, n, 0, me, sibling, own=True).wait_send()
            self._mine(ins, outs, sems, n).wait()


def _all_gather(shards, *, name, layer=None):
    return _run_exchange(_Gather(shards, layer), name=name)


class _SiblingExchange(_Exchange):
    def __init__(self, gs):
        self.n_t = len(gs)
        self.inputs = [g.reshape((4, 2) + g.shape[1:]) for g in gs]
        self.out_shape = [_sds((4,) + g.shape[1:], g.dtype) for g in gs]
        self.scratch = [pltpu.SemaphoreType.DMA((self.n_t,)), pltpu.SemaphoreType.DMA((self.n_t,))]

    def _copy(self, ins, outs, sems, n):
        x, y, c = lax.axis_index("x"), lax.axis_index("y"), lax.axis_index("c")
        return pltpu.make_async_remote_copy(src_ref=ins[n].at[:, 1 - c], dst_ref=outs[n], send_sem=sems[0].at[n],
                                            recv_sem=sems[1].at[n], device_id=(x, y, 1 - c), device_id_type=MESH)

    def start(self, ins, outs, sems):
        for n in range(self.n_t):
            self._copy(ins, outs, sems, n).start()

    def finish(self, ins, outs, sems):
        for n in range(self.n_t):
            self._copy(ins, outs, sems, n).wait_recv()
        for n in range(self.n_t):
            self._copy(ins, outs, sems, n).wait_send()


class _ChipExchange(_Exchange):
    def __init__(self, ps):
        self.n_t, self.inputs = len(ps), list(ps)
        self.out_shape = [_sds(p.shape, p.dtype) for p in ps]
        self.scratch = [pltpu.SemaphoreType.DMA((3 * self.n_t,)), pltpu.SemaphoreType.DMA((3 * self.n_t,)),
                        pltpu.SemaphoreType.DMA((self.n_t,))]

    def _place(self):
        x, y, c = lax.axis_index("x"), lax.axis_index("y"), lax.axis_index("c")
        return 2 * x + y, c, [(1 - x, y), (x, 1 - y), (1 - x, 1 - y)]

    def _copy(self, ins, outs, sems, n, j, src_chip, dst_slot):
        _, c, chips = self._place()
        return pltpu.make_async_remote_copy(
            src_ref=ins[n].at[src_chip], dst_ref=outs[n].at[dst_slot], send_sem=sems[0].at[n * 3 + j],
            recv_sem=sems[1].at[n * 3 + j], device_id=(*chips[j], c), device_id_type=MESH)

    def _own(self, ins, outs, sems, n):
        mine, _, _ = self._place()
        return pltpu.make_async_copy(ins[n].at[mine], outs[n].at[mine], sems[2].at[n])

    def start(self, ins, outs, sems):
        mine, _, chips = self._place()
        for n in range(self.n_t):
            self._own(ins, outs, sems, n).start()
        for j, (px, py) in enumerate(chips):
            for n in range(self.n_t):
                self._copy(ins, outs, sems, n, j, 2 * px + py, mine).start()

    def finish(self, ins, outs, sems):
        mine, _, chips = self._place()
        for j, (px, py) in enumerate(chips):
            for n in range(self.n_t):
                self._copy(ins, outs, sems, n, j, mine, 2 * px + py).wait_recv()
        for j, (px, py) in enumerate(chips):
            for n in range(self.n_t):
                self._copy(ins, outs, sems, n, j, 2 * px + py, mine).wait_send()
        for n in range(self.n_t):
            self._own(ins, outs, sems, n).wait()


BIG = (("w_ada", (D, N_MOD * D // N_DEV), 1), ("w_in", (D, D_IN // N_DEV), 1), ("conv_pw_w", (CONV_CH // N_DEV, D), 0),
       ("pool_w", (len(POOL_WINDOWS), POOL_GCH // N_DEV, POOL_GCH), 1), ("w_out", (D // N_DEV, D), 0),
       ("w_up", (D, 2 * D_FF // N_DEV), 1), ("w_down", (D_FF // N_DEV, D), 0))
TAPS = (("conv_dw_w", (CONV_K, CONV_CH // N_DEV), 1), ("ffn_dw_w", (FFN_K, D_FF // N_DEV), 1))
SHARDED = BIG + TAPS
REPLICATED = (("b_ada", N_MOD * D), ("b_in", D_IN), ("q_gain", HD), ("k_gain", HD), ("conv_dw_b", CONV_CH), ("conv_ln_g", CONV_CH),
              ("conv_ln_b", CONV_CH), ("conv_pw_b", D), ("pool_scale", POOL_CH), ("b_out", D), ("ln1_g", D), ("ln1_b", D),
              ("ln2_g", D), ("ln2_b", D), ("ffn_dw_b", D_FF))


def _as_rows(shape):
    return (int(np.prod(shape[:-1])), shape[-1])


def _full_from_blocks(blocks, axis):
    moved = jnp.moveaxis(blocks, 0, axis)
    shape = list(moved.shape)
    shape[axis:axis + 2] = [shape[axis] * shape[axis + 1]]
    return moved.reshape(shape)


def _blocks_from_full(full, axis):
    shape = list(full.shape)
    shape[axis:axis + 1] = [N_DEV, shape[axis] // N_DEV]
    return jnp.moveaxis(full.reshape(shape), axis, 0)


SMALL_N = DEPTH * sum(n for _, n in REPLICATED) + D
SMALL_ROWS = -(-(SMALL_N + 1) // (8 * LANES)) * 8


def _rope_tables(C, S):
    t = np.arange(S)
    inv_freq = ROPE_THETA ** (-np.arange(ROPE_PAIRS, dtype=np.float32) / ROPE_PAIRS)
    row = jnp.asarray((t // GRID_W).astype(np.float32))[:, None] * jnp.asarray(inv_freq, F32)
    col = jnp.asarray((t % GRID_W).astype(np.float32))[:, None] * jnp.asarray(inv_freq, F32)
    cos = jnp.concatenate([jnp.cos(row), jnp.cos(row), jnp.cos(col), jnp.cos(col)], axis=1)
    sin = jnp.concatenate([-jnp.sin(row), jnp.sin(row), -jnp.sin(col), jnp.sin(col)], axis=1)
    cos = jnp.concatenate([jnp.ones((C, HD), F32), cos], axis=0)
    sin = jnp.concatenate([jnp.zeros((C, HD), F32), sin], axis=0)
    return cos, sin


def _segment_sums(parts, B, tpe, ncq):
    p = parts.reshape(B, tpe, D)
    return jnp.concatenate([jnp.sum(p[:, ncq:], axis=1), jnp.sum(p[:, :ncq], axis=(0, 1))[None]], axis=0)


def kernel(x, c, ctx, c_ctx, w_ada, b_ada, w_in, b_in, q_gain, k_gain, conv_dw_w, conv_dw_b, conv_ln_g, conv_ln_b, conv_pw_w, conv_pw_b, pool_w, pool_scale, w_out, b_out, ln1_g, ln1_b, ln2_g, ln2_b, w_up, ffn_dw_w, ffn_dw_b, w_down, loss_target, m_c_ctx, m_w_ada, m_b_ada, m_w_in, m_b_in, m_q_gain, m_k_gain, m_conv_dw_w, m_conv_dw_b, m_conv_ln_g, m_conv_ln_b, m_conv_pw_w, m_conv_pw_b, m_pool_w, m_pool_scale, m_w_out, m_b_out, m_ln1_g, m_ln1_b, m_ln2_g, m_ln2_b, m_w_up, m_ffn_dw_w, m_ffn_dw_b, m_w_down, v_c_ctx, v_w_ada, v_b_ada, v_w_in, v_b_in, v_q_gain, v_k_gain, v_conv_dw_w, v_conv_dw_b, v_conv_ln_g, v_conv_ln_b, v_conv_pw_w, v_conv_pw_b, v_pool_w, v_pool_scale, v_w_out, v_b_out, v_ln1_g, v_ln1_b, v_ln2_g, v_ln2_b, v_w_up, v_ffn_dw_w, v_ffn_dw_b, v_w_down):
    given = dict(locals())
    B, S, _ = x.shape
    C = ctx.shape[1]
    R = C + S
    T = B * R
    tpe, ncq = R // TM, C // TM
    nt = T // TM
    assert S % TM == 0 and C % TM == 0 and B + 1 <= 16

    operands = [given[n].astype(MXU) for n, _, _ in BIG]
    taps = _all_gather([given[n] for n, _, _ in TAPS], name="gather_taps")
    W, modt = [], []

    xu = jnp.concatenate([ctx, x], axis=1).reshape(T, D)
    cond = jnp.concatenate([c, c_ctx[None], jnp.zeros((16 - B - 1, D), F32)], axis=0)
    s_cond, ds_cond = _silu_rows(cond)
    ctx_tile = jnp.asarray((np.arange(tpe) < ncq)[None, :, None])
    cos_t, sin_t = _rope_tables(C, S)
    row = lambda v: v.reshape(1, -1)

    def add_layer(blocks):
        l = len(W)
        wl = {n: _full_from_blocks(blk, a) for (n, _, a), blk in zip(BIG, blocks)}
        for (n, _, a), blk in zip(TAPS, taps):
            wl[n] = _full_from_blocks(blk[:, l], a)
        W.append(wl)
        m = _matmul(s_cond, wl["w_ada"], "nn", name="ada", bias=row(b_ada[l]), tm=16, tn=1024)
        modt.append(jnp.where(ctx_tile, m[B][None, None, :], m[:B][:, None, :]).reshape(nt, 1, N_MOD * D))

    add_layer(_all_gather(operands, name="gather_weights", layer=0))

    saved = []
    h1, h1t = _modulate_cast(xu, modt[0], 0)
    xin = xu
    for l in range(DEPTH):
        wl = W[l]
        w_inl, b_inl = wl["w_in"], b_in[l]
        z_qkv = _matmul(h1, w_inl[:, :QKV_W], "nn", name="z_qkv", bias=row(b_inl[:QKV_W]), tn=768)
        c0, p0, g0 = QKV_W, QKV_W + 2 * CONV_CH, QKV_W + 2 * CONV_CH + POOL_CH
        z_conv = _matmul(h1, w_inl[:, c0:p0], "nn", name="z_conv", bias=row(b_inl[c0:p0]))
        z_pool = _matmul(h1, w_inl[:, p0:g0], "nn", name="z_pool", bias=row(b_inl[p0:g0]))
        z_gate = _matmul(h1, w_inl[:, g0:], "nn", name="z_gate", bias=row(b_inl[g0:]))
        qkv = _qk_prep(z_qkv, cos_t, sin_t, row(q_gain[l]), row(k_gain[l]), tpe)
        if l + 1 < DEPTH:
            attn, blocks = _attn_fwd(qkv, B, C, R, carry=_Gather(operands, l + 1))
            add_layer(blocks)
        else:
            attn, _ = _attn_fwd(qkv, B, C, R)
        dw32 =jnp.pad(wl["conv_dw_w"], ((0, 32 - CONV_K), (0, 0)))
        hc = _conv_fwd(z_conv, dw32, row(conv_dw_b[l]), B, C, S)
        sw, swt = _ln_silu(hc, row(conv_ln_g[l]), row(conv_ln_b[l]))
        conv_o = _matmul(sw, wl["conv_pw_w"], "nn", name="conv_pw", bias=row(conv_pw_b[l]))
        pooled, pool_o = _pool_fwd(z_pool, wl["pool_w"], row(pool_scale[l]), B, C, S)
        m, mt = _merge(attn, conv_o, pool_o, z_gate)
        mo = _matmul(m, wl["w_out"], "nn", name="w_out", bias=row(b_out[l]))
        y1, h2, h2t = _resid_ln(xin, mo, modt[l], 2, row(ln1_g[l]), row(ln1_b[l]), modt[l], 3)
        up = _matmul(h2, wl["w_up"], "nn", name="w_up", tn=1408)
        fw8 = jnp.pad(wl["ffn_dw_w"], ((0, 8 - FFN_K), (0, 0)))
        f, ft = _ffn_mid(up, fw8, row(ffn_dw_b[l]), B, C, S)
        fo = _matmul(f, wl["w_down"], "nn", name="w_down", tm=512)
        if l + 1 < DEPTH:
            y2, h_next, ht_next = _resid_ln(y1, fo, modt[l], 5, row(ln2_g[l]), row(ln2_b[l]), modt[l + 1], 0)
        else:
            y2, h_next, ht_next = _resid_ln(y1, fo, modt[l], 5, row(ln2_g[l]), row(ln2_b[l])), None, None
        saved.append(dict(xin=xin, h1t=h1t, z_qkv=z_qkv, z_conv=z_conv, z_gate=z_gate, qkv=qkv, attn=attn, hc=hc, swt=swt,
                          conv_o=conv_o, pooled=pooled, pool_o=pool_o, mt=mt, mo=mo, y1=y1, h2t=h2t, up=up, ft=ft, fo=fo,
                          dw32=dw32, fw8=fw8))
        xin, h1, h1t = y2, h_next, ht_next

    dy, loss_part = _loss_grad(xin, loss_target.reshape(B * S, D), tpe, ncq)

    small = {n: [None] * DEPTH for n, _ in REPLICATED}
    d_c_ctx = jnp.zeros((D,), F32)
    dmods_t = [[None] * N_MOD for _ in range(DEPTH)]
    layer_grads = [None] * DEPTH
    kinds = ("grad_", "delta_", "new_m_", "new_v_")
    stacks = {n: [given[pre + n].reshape((DEPTH,) + _as_rows(s)) for pre in ("", "m_", "v_")] for n, s, _ in SHARDED}
    results = {n: None for n, _, _ in SHARDED}

    def reduce_begin(l):
        nonlocal d_c_ctx
        dmods = jnp.concatenate([_segment_sums(p, B, tpe, ncq) for p in dmods_t[l]], axis=1)
        small["b_ada"][l] = jnp.sum(dmods, axis=0)
        dm16 = jnp.concatenate([dmods, jnp.zeros((16 - B - 1, N_MOD * D), F32)], axis=0).astype(MXU)
        layer_grads[l]["w_ada"] = _matmul(s_cond, dm16, "tn", name="g_w_ada", tm=1024, tn=1024)
        dcond = _matmul(dm16, W[l]["w_ada"], "nt", name="d_cond", tm=16, tn=1024, tk=2048)
        d_c_ctx = d_c_ctx + dcond[B] * ds_cond[B]
        gs = [_blocks_from_full(layer_grads[l][n], a).reshape((N_DEV,) + _as_rows(s)) for n, s, a in SHARDED]
        r1 = _run_exchange(_SiblingExchange(gs), name="sibling_exchange")
        return [_pair_sum(g, r) for g, r in zip(gs, r1)]

    def reduce_end(l, r2):
        for (n, _, _), parts in zip(SHARDED, r2):
            results[n] = _adamw(parts, *stacks[n], l, results[n], name="adamw_sharded")

    dh1 = None
    pending = None
    for l in reversed(range(DEPTH)):
        wl, sv = W[l], saved[l]
        dmod = dmods_t[l]
        if dh1 is None:
            dy1p, dfo, dgate2, dg, db, _ = _ln_bwd(dy, sv["y1"], sv["fo"], modt[l], 5, row(ln2_g[l]))
        else:
            dy1p, dfo, dgate2, dg, db, _, dsh, dsc = _ln_bwd(dy, sv["y1"], sv["fo"], modt[l], 5, row(ln2_g[l]), dh=dh1,
                                                             y=saved[l + 1]["xin"], mod_next=modt[l + 1], k_shift_next=0)
            dmods_t[l + 1][0], dmods_t[l + 1][1] = dsh, dsc
            pending = (l + 1, reduce_begin(l + 1))
        small["ln2_g"][l], small["ln2_b"][l] = dg[0], db[0]
        dmod[5] = dgate2
        df = _matmul(dfo, wl["w_down"], "nt", name="d_f", tn=1408)
        g_w_down = _matmul(sv["ft"], dfo, "nn", name="g_w_down", tm=1408, tk=1152)
        da2, du2, g_fdw, g_fdb = _ffn_mid_bwd(df, sv["up"], sv["fw8"], row(ffn_dw_b[l]), B, C, S)
        small["ffn_dw_b"][l] = g_fdb[0]
        dh2 = _matmul(da2, wl["w_up"][:, :D_FF], "nt", name="d_h2a", tm=512)
        dh2 = _matmul(du2, wl["w_up"][:, D_FF:], "nt", name="d_h2u", tm=512, acc_in=dh2)
        g_w_up = _matmul(sv["h2t"], da2, "nn", name="g_w_up_a", tn=1408, tk=1152, into=(2 * D_FF, 0, None))
        g_w_up = _matmul(sv["h2t"], du2, "nn", name="g_w_up_u", tn=1408, tk=1152, into=(2 * D_FF, D_FF, g_w_up))
        dxp, dmo, dgate1, dg, db, dbo, dsh, dsc = _ln_bwd(dy1p, sv["xin"], sv["mo"], modt[l], 2, row(ln1_g[l]), dh=dh2,
                                                          y=sv["y1"], mod_next=modt[l], k_shift_next=3)
        small["ln1_g"][l], small["ln1_b"][l], small["b_out"][l] = dg[0], db[0], dbo[0]
        dmod[2], dmod[3], dmod[4] = dgate1, dsh, dsc
        dm = _matmul(dmo, wl["w_out"], "nt", name="d_m")
        g_w_out = _matmul(sv["mt"], dmo, "nn", name="g_w_out", tk=2304)
        dattn, dconv_o, dpool_o, dzg, g_pwb, gsum_gate = _merge_bwd(dm, sv["attn"], sv["conv_o"], sv["pool_o"], sv["z_gate"])
        small["conv_pw_b"][l] = g_pwb[0]
        du, g_pool_w, g_pool_sc, gsum_pool = _pool_bwd(dpool_o, sv["pooled"], wl["pool_w"], row(pool_scale[l]), B, C, S)
        small["pool_scale"][l] = g_pool_sc[0]
        dsw = _matmul(dconv_o, wl["conv_pw_w"], "nt", name="d_sw")
        g_pw = _matmul(sv["swt"], dconv_o, "nn", name="g_conv_pw", tk=2304)
        dhc, g_cg, g_cb, g_cdb = _ln_silu_bwd(dsw, sv["hc"], row(conv_ln_g[l]), row(conv_ln_b[l]))
        small["conv_ln_g"][l], small["conv_ln_b"][l], small["conv_dw_b"][l] = g_cg[0], g_cb[0], g_cdb[0]
        da, dgt, g_cdw, gsum_a, gsum_gt = _conv_bwd(dhc, sv["z_conv"], sv["dw32"], B, C, S)
        if pending is None:
            dq, dk, dv, _ = _attn_bwd(sv["qkv"], sv["attn"], dattn, B, C, R)
        else:
            dq, dk, dv, r2 = _attn_bwd(sv["qkv"], sv["attn"], dattn, B, C, R, carry=_ChipExchange(pending[1]))
            reduce_end(pending[0], r2)
            pending = None
        dz_qkv, g_qg, g_kg, gsum_qkv = _qk_bwd(dq, dk, dv, sv["z_qkv"], cos_t, sin_t, row(q_gain[l]), row(k_gain[l]), tpe)
        small["q_gain"][l], small["k_gain"][l] = g_qg[0], g_kg[0]
        small["b_in"][l] = jnp.concatenate([gsum_qkv[0], gsum_a[0], gsum_gt[0], gsum_pool[0], gsum_gate[0]])
        w_inl = wl["w_in"]
        g0 = QKV_W + 2 * CONV_CH + POOL_CH
        pieces = ((dzg, g0, N_GATE), (da, QKV_W, CONV_CH), (dgt, QKV_W + CONV_CH, CONV_CH), (du, QKV_W + 2 * CONV_CH, POOL_CH),
                  (dz_qkv, 0, QKV_W))
        dh1 = g_w_in = None
        at = 0
        for k, (dz, c0, wd) in enumerate(pieces):
            dh1 = _matmul(dz, w_inl[:, c0:c0 + wd], "nt", name=f"d_h1_{k}", tm=512, acc_in=dh1)
            g_w_in = _matmul(sv["h1t"], dz, "nn", name=f"g_w_in_{k}", tk=2304, into=(D_IN, at, g_w_in))
            at += wd
        g_w_in = jnp.concatenate([g_w_in[:, N_GATE + 2 * CONV_CH + POOL_CH:], g_w_in[:, N_GATE:N_GATE + 2 * CONV_CH + POOL_CH],
                                  g_w_in[:, :N_GATE]], axis=1)

        layer_grads[l] = {"w_in": g_w_in, "conv_pw_w": g_pw, "pool_w": g_pool_w, "w_out": g_w_out, "w_up": g_w_up,
                          "w_down": g_w_down, "conv_dw_w": g_cdw[:CONV_K], "ffn_dw_w": g_fdw[:FFN_K]}
        dy = dxp
    gx_u, dmods_t[0][0], dmods_t[0][1] = _mod_bwd(dy, dh1, saved[0]["xin"], modt[0], 0)
    grad_x = gx_u.reshape(B, R, D)[:, C:]

    reduce_end(0, _run_exchange(_ChipExchange(reduce_begin(0)), name="chip_exchange"))
    outs = {}
    for n, s, _ in SHARDED:
        for kind, buf in zip(kinds, results[n]):
            outs[kind + n] = buf.reshape((DEPTH,) + tuple(s))

    def small_pack(pieces):
        flat = jnp.concatenate([p.reshape(-1) for p in pieces])
        return jnp.pad(flat, (0, SMALL_ROWS * LANES - flat.shape[0])).reshape(SMALL_ROWS, LANES)

    zero1 = jnp.zeros((1,), F32)
    g_pack = small_pack([small[n][l] for n, _ in REPLICATED for l in range(DEPTH)] + [d_c_ctx, loss_part[0, :1]])
    g_small, = _all_gather([g_pack], name="gather_small")
    wmv = [small_pack([given[pre + n] for n, _ in REPLICATED] + [given[pre + "c_ctx"], zero1])[None] for pre in ("", "m_", "v_")]
    res = _adamw(g_small, *wmv, 0, None, name="adamw_small")
    for kind, buf in zip(kinds, res):
        flat = buf.reshape(-1)
        off = 0
        for n, sz in REPLICATED:
            outs[kind + n] = flat[off:off + DEPTH * sz].reshape(DEPTH, sz)
            off += DEPTH * sz
        outs[kind + "c_ctx"] = flat[off:off + D]
        if kind == "grad_":
            loss = flat[off + D]

    names = ["c_ctx", "w_ada", "b_ada", "w_in", "b_in", "q_gain", "k_gain", "conv_dw_w", "conv_dw_b", "conv_ln_g", "conv_ln_b",
             "conv_pw_w", "conv_pw_b", "pool_w", "pool_scale", "w_out", "b_out", "ln1_g", "ln1_b", "ln2_g", "ln2_b", "w_up",
             "ffn_dw_w", "ffn_dw_b", "w_down"]
    return (loss, grad_x, *[outs[k + n] for k in ("grad_", "delta_", "new_m_", "new_v_") for n in names])
```

```python
import functools

import jax
import jax.numpy as jnp
import numpy as np
from jax import lax
from jax.experimental import pallas as pl
from jax.experimental.pallas import tpu as pltpu

F32 = jnp.float32
MXU = jnp.bfloat16
WIRE = jnp.bfloat16

D = 1024
HD = 128
NH = 8
NKV = 2
QG = NH // NKV
KV_W = NKV * HD
QKV_W = NH * HD + 2 * KV_W
CONV_CH = D
POOL_CH = D
POOL_WINDOWS = (2, 4, 8, 16)
POOL_GCH = POOL_CH // len(POOL_WINDOWS)
N_GATE = 3 * D
D_IN = QKV_W + 2 * CONV_CH + POOL_CH + N_GATE
D_FF = 2816
N_MOD = 6
DEPTH = 4
CONV_K = 31
FFN_K = 3
GRID_W = 64
ROPE_THETA = 10000.0
ROPE_PAIRS = HD // 4
ALPHA = (2 * DEPTH) ** 0.25
LN_EPS = 1e-5
RMS_EPS = 1e-6
ATTN_SCALE = HD ** -0.5
LOG2_E = 1.4426950408889634
ADAM_LR, ADAM_B1, ADAM_B2, ADAM_EPS, ADAM_WD, ADAM_STEP = 0.001, 0.9, 0.999, 1e-08, 0.01, 10

N_DEV = 8
TM = 256
GAP = 16
LANES = 1024
VMEM_MB = 48

NN = (((1,), (0,)), ((), ()))
NT = (((1,), (1,)), ((), ()))
TN = (((0,), (0,)), ((), ()))

_pcall = pl.pallas_call


def _call(body, *, name, grid, in_specs, out_specs, out_shape, scratch=(), aliases=None, vmem=VMEM_MB):
    return _pcall(
        body, name=name, grid=grid, in_specs=in_specs, out_specs=out_specs, out_shape=out_shape,
        scratch_shapes=list(scratch), input_output_aliases=aliases or {},
        compiler_params=pltpu.CompilerParams(dimension_semantics=("arbitrary",) * len(grid), vmem_limit_bytes=vmem * 2 ** 20),
    )


def _sds(shape, dtype):
    return jax.ShapeDtypeStruct(tuple(shape), dtype)


def _pick(n, cap, mult):
    best = None
    for t in range(mult, min(n, cap) + 1, mult):
        if n % t == 0:
            best = t
    return best if best is not None else n


def _dot(a, b, dims):
    return lax.dot_general(a, b, dims, preferred_element_type=F32)


def _sigmoid(x):
    return 1.0 / (1.0 + jnp.exp(-x))


def _matmul(a, b, mode, *, name, bias=None, acc_in=None, into=None, out_dtype=F32, tm=1024, tn=1024, tk=None):
    if mode == "nn":
        (M, K), (K2, N) = a.shape, b.shape
    elif mode == "nt":
        (M, K), (N, K2) = a.shape, b.shape
    else:
        (K, M), (K2, N) = a.shape, b.shape
    assert K == K2, (a.shape, b.shape, mode)
    tm = _pick(M, tm, 16)
    tn = _pick(N, tn, 128)
    tk = K if tk is None else _pick(K, tk, 128 if mode != "tn" else 16)
    gk = K // tk
    dims = {"nn": NN, "nt": NT, "tn": TN}[mode]
    a_spec = pl.BlockSpec((tk, tm), lambda j, i, k: (k, i)) if mode == "tn" else pl.BlockSpec((tm, tk), lambda j, i, k: (i, k))
    b_spec = pl.BlockSpec((tn, tk), lambda j, i, k: (j, k)) if mode == "nt" else pl.BlockSpec((tk, tn), lambda j, i, k: (k, j))
    in_specs, args = [a_spec, b_spec], [a, b]
    if bias is not None:
        in_specs.append(pl.BlockSpec((1, tn), lambda j, i, k: (0, j)))
        args.append(bias)
    aliases = {}
    if acc_in is not None:
        aliases = {len(args): 0}
        in_specs.append(pl.BlockSpec((tm, tn), lambda j, i, k: (i, j)))
        args.append(acc_in)
    n_total, col0, prev = (N, 0, None) if into is None else into
    assert col0 % tn == 0
    jb = col0 // tn
    if prev is not None:
        aliases = {len(args): 0}
        in_specs.append(pl.BlockSpec(memory_space=pl.ANY))
        args.append(prev)
    n_in = len(args)

    def body(*refs):
        a_ref, b_ref = refs[0], refs[1]
        pos = 2
        bias_ref = acc_in_ref = None
        if bias is not None:
            bias_ref = refs[pos]
            pos += 1
        if acc_in is not None:
            acc_in_ref = refs[pos]
        pos = n_in
        o_ref = refs[pos]
        part = _dot(a_ref[...].astype(MXU), b_ref[...].astype(MXU), dims)

        def finish(acc):
            if bias_ref is not None:
                acc = acc + bias_ref[...]
            if acc_in_ref is not None:
                acc = acc + acc_in_ref[...]
            o_ref[...] = acc.astype(out_dtype)

        if gk == 1:
            finish(part)
        else:
            acc_ref = refs[pos + 1]
            k = pl.program_id(2)

            @pl.when(k == 0)
            def _():
                acc_ref[...] = part

            @pl.when(k > 0)
            def _():
                acc_ref[...] += part

            @pl.when(k == gk - 1)
            def _():
                finish(acc_ref[...])

    return _call(
        body, name=name, grid=(N // tn, M // tm, gk), in_specs=in_specs,
        out_specs=pl.BlockSpec((tm, tn), lambda j, i, k: (i, j + jb)), out_shape=_sds((M, n_total), out_dtype),
        scratch=[pltpu.VMEM((tm, tn), F32)] if gk > 1 else [], aliases=aliases,
    )(*args)


def _rt(w, cb=0):
    return pl.BlockSpec((TM, w), lambda i: (i, cb))


def _ct(w):
    return pl.BlockSpec((w, TM), lambda i: (0, i))


def _vec(w):
    return pl.BlockSpec((1, w), lambda i: (0, 0))


def _part(w):
    return pl.BlockSpec((1, 1, w), lambda i: (i, 0, 0))


def _mod(ref, k):
    return ref[0, :, k * D:(k + 1) * D]


def _colsum(x):
    return jnp.sum(x, axis=0, keepdims=True)


def _ln_stats(s):
    mu = jnp.mean(s, axis=1, keepdims=True)
    cen = s - mu
    var = jnp.mean(cen * cen, axis=1, keepdims=True)
    rstd = lax.rsqrt(var + LN_EPS)
    return cen * rstd, rstd


def _modulate_cast(x, modt, k_shift):
    T = x.shape[0]
    nt = T // TM

    def body(x_ref, mod_ref, h_ref, ht_ref):
        h = (x_ref[...] * (1.0 + _mod(mod_ref, k_shift + 1)) + _mod(mod_ref, k_shift)).astype(MXU)
        h_ref[...] = h
        ht_ref[...] = h.T

    return _call(body, name="modulate", grid=(nt,), in_specs=[_rt(D), _part(N_MOD * D)], out_specs=[_rt(D), _ct(D)],
                 out_shape=[_sds((T, D), MXU), _sds((D, T), MXU)])(x, modt)


def _resid_ln(x, br, modt, k_gate, g, b, mod_next=None, k_shift_next=0):
    T = x.shape[0]
    nt = T // TM
    with_h = mod_next is not None

    def body(*refs):
        x_ref, br_ref, mod_ref, g_ref, b_ref = refs[:5]
        s = ALPHA * x_ref[...] + _mod(mod_ref, k_gate) * br_ref[...]
        xhat, _ = _ln_stats(s)
        y = xhat * g_ref[...] + b_ref[...]
        if with_h:
            modn_ref, y_ref, h_ref, ht_ref = refs[5:]
            y_ref[...] = y
            h = (y * (1.0 + _mod(modn_ref, k_shift_next + 1)) + _mod(modn_ref, k_shift_next)).astype(MXU)
            h_ref[...] = h
            ht_ref[...] = h.T
        else:
            refs[5][...] = y

    in_specs = [_rt(D), _rt(D), _part(N_MOD * D), _vec(D), _vec(D)]
    args = [x, br, modt, g, b]
    if with_h:
        in_specs.append(_part(N_MOD * D))
        args.append(mod_next)
        return _call(body, name="resid_ln_mod", grid=(nt,), in_specs=in_specs, out_specs=[_rt(D), _rt(D), _ct(D)],
                     out_shape=[_sds((T, D), F32), _sds((T, D), MXU), _sds((D, T), MXU)])(*args)
    return _call(body, name="resid_ln", grid=(nt,), in_specs=in_specs, out_specs=_rt(D), out_shape=_sds((T, D), F32))(*args)


def _ln_bwd(dy_part, x, br, modt, k_gate, g, dh=None, y=None, mod_next=None, k_shift_next=0):
    T = x.shape[0]
    nt = T // TM
    with_h = dh is not None

    def body(*refs):
        if with_h:
            dyp_ref, x_ref, br_ref, mod_ref, g_ref, dh_ref, y_ref, modn_ref = refs[:8]
            outs = refs[8:]
        else:
            dyp_ref, x_ref, br_ref, mod_ref, g_ref = refs[:5]
            outs = refs[5:]
        dx_ref, dbr_ref, dgate_ref, dlg_ref, dlb_ref, dbsum_ref = outs[:6]
        i = pl.program_id(0)

        @pl.when(i == 0)
        def _():
            dlg_ref[...] = jnp.zeros_like(dlg_ref)
            dlb_ref[...] = jnp.zeros_like(dlb_ref)
            dbsum_ref[...] = jnp.zeros_like(dbsum_ref)

        dy = dyp_ref[...]
        if with_h:
            dshift_ref, dscale_ref = outs[6:]
            dhv = dh_ref[...]
            dy = dy + dhv * (1.0 + _mod(modn_ref, k_shift_next + 1))
            dshift_ref[0] = _colsum(dhv)
            dscale_ref[0] = _colsum(dhv * y_ref[...])
        gate = _mod(mod_ref, k_gate)
        brv = br_ref[...]
        s = ALPHA * x_ref[...] + gate * brv
        xhat, rstd = _ln_stats(s)
        dlg_ref[...] += _colsum(dy * xhat)
        dlb_ref[...] += _colsum(dy)
        dyg = dy * g_ref[...]
        m1 = jnp.mean(dyg, axis=1, keepdims=True)
        m2 = jnp.mean(dyg * xhat, axis=1, keepdims=True)
        ds = rstd * (dyg - m1 - xhat * m2)
        dx_ref[...] = ALPHA * ds
        dbr = gate * ds
        dbr_ref[...] = dbr.astype(MXU)
        dbsum_ref[...] += _colsum(dbr)
        dgate_ref[0] = _colsum(ds * brv)

    in_specs = [_rt(D), _rt(D), _rt(D), _part(N_MOD * D), _vec(D)]
    args = [dy_part, x, br, modt, g]
    out_specs = [_rt(D), _rt(D), _part(D), _vec(D), _vec(D), _vec(D)]
    out_shape = [_sds((T, D), F32), _sds((T, D), MXU), _sds((nt, 1, D), F32), _sds((1, D), F32), _sds((1, D), F32), _sds((1, D), F32)]
    if with_h:
        in_specs += [_rt(D), _rt(D), _part(N_MOD * D)]
        args += [dh, y, mod_next]
        out_specs += [_part(D), _part(D)]
        out_shape += [_sds((nt, 1, D), F32), _sds((nt, 1, D), F32)]
    return _call(body, name="ln_bwd_mod" if with_h else "ln_bwd", grid=(nt,), in_specs=in_specs, out_specs=out_specs,
                 out_shape=out_shape)(*args)


def _mod_bwd(dx_part, dh, x, modt, k_shift):
    T = x.shape[0]
    nt = T // TM

    def body(dxp_ref, dh_ref, x_ref, mod_ref, dx_ref, dshift_ref, dscale_ref):
        dhv = dh_ref[...]
        dx_ref[...] = dxp_ref[...] + dhv * (1.0 + _mod(mod_ref, k_shift + 1))
        dshift_ref[0] = _colsum(dhv)
        dscale_ref[0] = _colsum(dhv * x_ref[...])

    return _call(body, name="mod_bwd", grid=(nt,), in_specs=[_rt(D), _rt(D), _rt(D), _part(N_MOD * D)],
                 out_specs=[_rt(D), _part(D), _part(D)],
                 out_shape=[_sds((T, D), F32), _sds((nt, 1, D), F32), _sds((nt, 1, D), F32)])(dx_part, dh, x, modt)


def _loss_grad(y, target, tpe, ncq):
    T = y.shape[0]
    nt = T // TM
    nl = tpe - ncq

    def body(y_ref, t_ref, dy_ref, loss_ref):
        i = pl.program_id(0)

        @pl.when(i == 0)
        def _():
            loss_ref[...] = jnp.zeros_like(loss_ref)

        @pl.when(i % tpe < ncq)
        def _():
            dy_ref[...] = jnp.zeros_like(dy_ref)

        @pl.when(i % tpe >= ncq)
        def _():
            err = y_ref[...] - t_ref[...]
            dy_ref[...] = err * (1.0 / D)
            loss_ref[...] += (0.5 / D) * jnp.sum(_colsum(err * err), axis=1, keepdims=True)

    tgt_spec = pl.BlockSpec((TM, D), lambda i: ((i // tpe) * nl + jnp.maximum(i % tpe - ncq, 0), 0))
    return _call(body, name="loss_grad", grid=(nt,), in_specs=[_rt(D), tgt_spec], out_specs=[_rt(D), _vec(128)],
                 out_shape=[_sds((T, D), F32), _sds((1, 128), F32)])(y, target)


def _rope_partner(x):
    lane = lax.broadcasted_iota(jnp.int32, x.shape, 1)
    first = (lane % (2 * ROPE_PAIRS)) < ROPE_PAIRS
    return jnp.where(first, pltpu.roll(x, HD - ROPE_PAIRS, 1), pltpu.roll(x, ROPE_PAIRS, 1))


def _qk_prep(z_qkv, cos_t, sin_t, q_gain, k_gain, tpe):
    T = z_qkv.shape[0]
    nt = T // TM

    def body(z_ref, cos_ref, sin_ref, qg_ref, kg_ref, o_ref):
        cos, sin = cos_ref[...], sin_ref[...]
        for h in range(NH + NKV):
            sl = slice(h * HD, (h + 1) * HD)
            t = z_ref[:, sl]
            gain = qg_ref[...] if h < NH else kg_ref[...]
            n = t * lax.rsqrt(jnp.mean(t * t, axis=1, keepdims=True) + RMS_EPS) * gain
            o_ref[:, sl] = (n * cos + _rope_partner(n) * sin).astype(MXU)
        o_ref[:, (NH + NKV) * HD:] = z_ref[:, (NH + NKV) * HD:].astype(MXU)

    tab = pl.BlockSpec((TM, HD), lambda i: (i % tpe, 0))
    return _call(body, name="qk_prep", grid=(nt,), in_specs=[_rt(QKV_W), tab, tab, _vec(HD), _vec(HD)], out_specs=_rt(QKV_W),
                 out_shape=_sds((T, QKV_W), MXU))(z_qkv, cos_t, sin_t, q_gain, k_gain)


def _qk_bwd(dq, dk, dv, z_qkv, cos_t, sin_t, q_gain, k_gain, tpe):
    T = z_qkv.shape[0]
    nt = T // TM

    def body(dq_ref, dk_ref, dv_ref, z_ref, cos_ref, sin_ref, qg_ref, kg_ref, dz_ref, dqg_ref, dkg_ref, bsum_ref):
        i = pl.program_id(0)

        @pl.when(i == 0)
        def _():
            dqg_ref[...] = jnp.zeros_like(dqg_ref)
            dkg_ref[...] = jnp.zeros_like(dkg_ref)
            bsum_ref[...] = jnp.zeros_like(bsum_ref)

        cos, sin = cos_ref[...], sin_ref[...]
        for h in range(NH + NKV):
            sl = slice(h * HD, (h + 1) * HD)
            dr = dq_ref[:, sl] if h < NH else dk_ref[:, (h - NH) * HD:(h - NH + 1) * HD]
            gain = qg_ref[...] if h < NH else kg_ref[...]
            dn = dr * cos + _rope_partner(dr * sin)
            t = z_ref[:, sl]
            rstd = lax.rsqrt(jnp.mean(t * t, axis=1, keepdims=True) + RMS_EPS)
            that = t * rstd
            dgain = _colsum(dn * that)
            if h < NH:
                dqg_ref[...] += dgain
            else:
                dkg_ref[...] += dgain
            dthat = dn * gain
            dt = rstd * (dthat - that * jnp.mean(dthat * that, axis=1, keepdims=True))
            dz_ref[:, sl] = dt.astype(MXU)
            bsum_ref[:, sl] += _colsum(dt)
        dvv = dv_ref[...]
        dz_ref[:, (NH + NKV) * HD:] = dvv.astype(MXU)
        bsum_ref[:, (NH + NKV) * HD:] += _colsum(dvv)

    tab = pl.BlockSpec((TM, HD), lambda i: (i % tpe, 0))
    return _call(body, name="qk_bwd", grid=(nt,),
                 in_specs=[_rt(NH * HD), _rt(KV_W), _rt(KV_W), _rt(QKV_W), tab, tab, _vec(HD), _vec(HD)],
                 out_specs=[_rt(QKV_W), _vec(HD), _vec(HD), _vec(QKV_W)],
                 out_shape=[_sds((T, QKV_W), MXU), _sds((1, HD), F32), _sds((1, HD), F32), _sds((1, QKV_W), F32)],
                 )(dq, dk, dv, z_qkv, cos_t, sin_t, q_gain, k_gain)


def _ln_silu(hc, g, b):
    T = hc.shape[0]

    def body(h_ref, g_ref, b_ref, o_ref, ot_ref):
        xhat, _ = _ln_stats(h_ref[...])
        n = xhat * g_ref[...] + b_ref[...]
        sw = (n * _sigmoid(n)).astype(MXU)
        o_ref[...] = sw
        ot_ref[...] = sw.T

    return _call(body, name="ln_silu", grid=(T // TM,), in_specs=[_rt(D), _vec(D), _vec(D)], out_specs=[_rt(D), _ct(D)],
                 out_shape=[_sds((T, D), MXU), _sds((D, T), MXU)])(hc, g, b)


def _ln_silu_bwd(dsw, hc, g, b):
    T = hc.shape[0]

    def body(d_ref, h_ref, g_ref, b_ref, dh_ref, dg_ref, db_ref, dcb_ref):
        i = pl.program_id(0)

        @pl.when(i == 0)
        def _():
            dg_ref[...] = jnp.zeros_like(dg_ref)
            db_ref[...] = jnp.zeros_like(db_ref)
            dcb_ref[...] = jnp.zeros_like(dcb_ref)

        xhat, rstd = _ln_stats(h_ref[...])
        n = xhat * g_ref[...] + b_ref[...]
        sg = _sigmoid(n)
        dn = d_ref[...] * (sg * (1.0 + n * (1.0 - sg)))
        dg_ref[...] += _colsum(dn * xhat)
        db_ref[...] += _colsum(dn)
        dng = dn * g_ref[...]
        m1 = jnp.mean(dng, axis=1, keepdims=True)
        m2 = jnp.mean(dng * xhat, axis=1, keepdims=True)
        dh = rstd * (dng - m1 - xhat * m2)
        dh_ref[...] = dh
        dcb_ref[...] += _colsum(dh)

    return _call(body, name="ln_silu_bwd", grid=(T // TM,), in_specs=[_rt(D), _rt(D), _vec(D), _vec(D)],
                 out_specs=[_rt(D), _vec(D), _vec(D), _vec(D)],
                 out_shape=[_sds((T, D), F32)] + [_sds((1, D), F32)] * 3)(dsw, hc, g, b)


def _merge(attn, conv_o, pool_o, z_gate):
    T = attn.shape[0]

    def body(a_ref, c_ref, p_ref, zg_ref, m_ref, mt_ref):
        m = (_sigmoid(zg_ref[:, 0:D]) * a_ref[...] + _sigmoid(zg_ref[:, D:2 * D]) * c_ref[...]
             + _sigmoid(zg_ref[:, 2 * D:3 * D]) * p_ref[...]).astype(MXU)
        m_ref[...] = m
        mt_ref[...] = m.T

    return _call(body, name="merge", grid=(T // TM,), in_specs=[_rt(D), _rt(D), _rt(D), _rt(N_GATE)], out_specs=[_rt(D), _ct(D)],
                 out_shape=[_sds((T, D), MXU), _sds((D, T), MXU)])(attn, conv_o, pool_o, z_gate)


def _merge_bwd(dm, attn, conv_o, pool_o, z_gate):
    T = attn.shape[0]

    def body(dm_ref, a_ref, c_ref, p_ref, zg_ref, da_ref, dc_ref, dp_ref, dzg_ref, dcsum_ref, gsum_ref):
        i = pl.program_id(0)

        @pl.when(i == 0)
        def _():
            dcsum_ref[...] = jnp.zeros_like(dcsum_ref)
            gsum_ref[...] = jnp.zeros_like(gsum_ref)

        dmv = dm_ref[...]
        for k, (br_ref, out_ref) in enumerate(((a_ref, da_ref), (c_ref, dc_ref), (p_ref, dp_ref))):
            gk = _sigmoid(zg_ref[:, k * D:(k + 1) * D])
            dbr = dmv * gk
            out_ref[...] = dbr.astype(out_ref.dtype)
            if k == 1:
                dcsum_ref[...] += _colsum(dbr)
            dzg = dmv * br_ref[...] * gk * (1.0 - gk)
            dzg_ref[:, k * D:(k + 1) * D] = dzg.astype(MXU)
            gsum_ref[:, k * D:(k + 1) * D] += _colsum(dzg)

    return _call(body, name="merge_bwd", grid=(T // TM,), in_specs=[_rt(D), _rt(D), _rt(D), _rt(D), _rt(N_GATE)],
                 out_specs=[_rt(D), _rt(D), _rt(D), _rt(N_GATE), _vec(D), _vec(N_GATE)],
                 out_shape=[_sds((T, D), F32), _sds((T, D), MXU), _sds((T, D), F32), _sds((T, N_GATE), MXU),
                            _sds((1, D), F32), _sds((1, N_GATE), F32)])(dm, attn, conv_o, pool_o, z_gate)


def _softmax_parts(q, k):
    s = _dot(q, k, NT)
    p = jnp.exp2((s - jnp.max(s, axis=1, keepdims=True)) * (ATTN_SCALE * LOG2_E))
    return p, 1.0 / jnp.sum(p, axis=1, keepdims=True)


def _attn_specs(nq):
    q_spec = pl.BlockSpec((TM, QG * HD), lambda b, h, q: (b * nq + q, h))
    k_spec = pl.BlockSpec((nq * TM, HD), lambda b, h, q: (b, NH + h))
    v_spec = pl.BlockSpec((nq * TM, HD), lambda b, h, q: (b, NH + NKV + h))
    return q_spec, k_spec, v_spec


class _Carried:
    def __init__(self, ex, n_in, n_out, n_scratch):
        self.ex, self.n_in, self.n_out, self.n_scratch = ex, n_in, n_out, n_scratch
        self.ci = len(ex.inputs) if ex else 0
        self.co = len(ex.out_shape) if ex else 0

    def in_specs(self):
        return [ANY] * self.ci

    def out_specs(self):
        return [ANY] * self.co

    def split(self, refs):
        a = self.n_in
        b = a + self.ci
        c = b + self.n_out
        d = c + self.co
        e = d + self.n_scratch
        return (refs[:a], refs[b:c], refs[d:e]), (refs[a:b], refs[c:d], refs[e:])

    def before(self, step, parts):
        if self.ex:
            pl.when(step == 0)(lambda: self.ex.start(*parts))

    def after(self, step, n_steps, parts):
        if self.ex:
            pl.when(step == (3 * n_steps) // 4)(lambda: self.ex.mid(*parts))
            pl.when(step == n_steps - 1)(lambda: self.ex.finish(*parts))


def _attn_fwd(qkv, B, C, R, carry=None):
    nq, ncq = R // TM, C // TM
    car = _Carried(carry, 3, 1, 0)

    def body(*refs):
        (q_ref, k_ref, v_ref), (o_ref,), _ = car.split(refs)[0]
        parts = car.split(refs)[1]
        qi = pl.program_id(2)
        step = (pl.program_id(0) * NKV + pl.program_id(1)) * nq + qi
        car.before(step, parts)

        def attend(L):
            k, v = k_ref[0:L, :], v_ref[0:L, :]
            for i in range(QG):
                sl = slice(i * HD, (i + 1) * HD)
                p, inv_l = _softmax_parts(q_ref[:, sl], k)
                o_ref[:, sl] = _dot(p.astype(MXU), v, NN) * inv_l

        pl.when(qi < ncq)(functools.partial(attend, C))
        pl.when(qi >= ncq)(functools.partial(attend, R))
        car.after(step, B * NKV * nq, parts)

    q_spec, k_spec, v_spec = _attn_specs(nq)
    res = _call(body, name="attn_fwd", grid=(B, NKV, nq), in_specs=[q_spec, k_spec, v_spec] + car.in_specs(),
                out_specs=[q_spec] + car.out_specs(), out_shape=[_sds((B * R, NH * HD), F32)] + list(carry.out_shape if carry else []),
                scratch=list(carry.scratch) if carry else [])(qkv, qkv, qkv, *(carry.inputs if carry else []))
    return res[0], res[1:]


def _attn_bwd(qkv, o, do, B, C, R, carry=None):
    nq, ncq = R // TM, C // TM
    car = _Carried(carry, 5, 3, 2)

    def body(*refs):
        (q_ref, k_ref, v_ref, o_ref, do_ref), (dq_ref, dk_ref, dv_ref), (dkt, dvt) = car.split(refs)[0]
        parts = car.split(refs)[1]
        qi = pl.program_id(2)
        step = (pl.program_id(0) * NKV + pl.program_id(1)) * nq + qi
        car.before(step, parts)

        @pl.when(qi == 0)
        def _():
            dkt[...] = jnp.zeros_like(dkt)
            dvt[...] = jnp.zeros_like(dvt)

        def bwd(L):
            k, v = k_ref[0:L, :], v_ref[0:L, :]
            for i in range(QG):
                sl = slice(i * HD, (i + 1) * HD)
                q = q_ref[:, sl]
                p, inv_l = _softmax_parts(q, k)
                dov = do_ref[:, sl]
                dp = _dot(dov.astype(MXU), v, NT)
                dl = jnp.sum(dov * o_ref[:, sl], axis=1, keepdims=True)
                ds = (p * ((dp - dl) * (inv_l * ATTN_SCALE))).astype(MXU)
                dq_ref[:, sl] = _dot(ds, k, NN)
                dkt[:, 0:L] += _dot(q, ds, TN)
                dvt[:, 0:L] += _dot((dov * inv_l).astype(MXU), p.astype(MXU), TN)

        pl.when(qi < ncq)(functools.partial(bwd, C))
        pl.when(qi >= ncq)(functools.partial(bwd, R))

        @pl.when(qi == nq - 1)
        def _():
            dk_ref[...] = dkt[...].T
            dv_ref[...] = dvt[...].T

        car.after(step, B * NKV * nq, parts)

    q_spec, k_spec, v_spec = _attn_specs(nq)
    kv_out = pl.BlockSpec((R, HD), lambda b, h, q: (b, h))
    res = _call(body, name="attn_bwd", grid=(B, NKV, nq), in_specs=[q_spec, k_spec, v_spec, q_spec, q_spec] + car.in_specs(),
                out_specs=[q_spec, kv_out, kv_out] + car.out_specs(),
                out_shape=[_sds((B * R, NH * HD), F32), _sds((B * R, KV_W), F32), _sds((B * R, KV_W), F32)]
                + list(carry.out_shape if carry else []),
                scratch=[pltpu.VMEM((HD, R), F32), pltpu.VMEM((HD, R), F32)] + list(carry.scratch if carry else []),
                )(qkv, qkv, qkv, o, do, *(carry.inputs if carry else []))
    return res[0], res[1], res[2], res[3:]


def _segments(C, S):
    return ((0, GAP, C), (C, 2 * GAP + C, S))


def _padded_rows(C, S):
    return 3 * GAP + C + S


def _zero_gaps(pad_ref, C, S):
    for off in (0, GAP + C, 2 * GAP + C + S):
        pad_ref[off:off + GAP, :] = jnp.zeros((GAP, pad_ref.shape[1]), pad_ref.dtype)


def _chunks(n, ch, fn):
    def step(i, carry):
        fn(pl.multiple_of(i * ch, ch))
        return carry

    lax.fori_loop(0, n // ch, step, 0)


class _Window:
    def __init__(self, pad_ref, row, ch, moved_ref=None):
        self.pad_ref, self.row, self.ch, self.moved_ref = pad_ref, row, ch, moved_ref
        self.win = pad_ref[pl.ds(row - GAP, ch + 2 * GAP), :]
        self.done = set()

    def at(self, off):
        if self.moved_ref is None:
            return self.win[GAP + off:GAP + off + self.ch, :]
        s, q = off % 8, off // 8
        if s == 0:
            return self.pad_ref[pl.ds(pl.multiple_of(self.row + off, 8), self.ch), :]
        if s not in self.done:
            self.moved_ref[s] = self.win[s:s + self.ch + 2 * GAP - 8, :]
            self.done.add(s)
        return self.moved_ref[s, GAP + 8 * q:GAP + 8 * q + self.ch, :]


def _taps(pad_ref, w, row, ch, n_taps, flip=False, moved_ref=None):
    half = (n_taps - 1) // 2
    win = _Window(pad_ref, row, ch, moved_ref)
    acc = None
    for k in range(n_taps):
        term = w[k:k + 1, :] * win.at((half - k) if flip else (k - half))
        acc = term if acc is None else acc + term
    return acc


def _tap_grads(dw_ref, d, pad_ref, row, ch, n_taps, moved_ref=None):
    half = (n_taps - 1) // 2
    win = _Window(pad_ref, row, ch, moved_ref)
    for k in range(n_taps):
        prod = d * win.at(k - half)
        dw_ref[k] += jnp.sum(prod.reshape(ch // 8, 8, prod.shape[1]), axis=0)


def _moved_scratch(ch, width):
    return pltpu.VMEM((8, ch + 2 * GAP - 8, width), F32)


def _conv_fwd(z_conv, w, bias, B, C, S, cw=128, ch=128):
    R = C + S
    nj = CONV_CH // cw
    segs = _segments(C, S)

    def body(a_ref, g_ref, w_ref, b_ref, o_ref, pad, moved):
        _zero_gaps(pad, C, S)
        wv = w_ref[...]
        for so, po, n in segs:
            def fill(r, so=so, po=po):
                pad[pl.ds(po + r, ch), :] = a_ref[pl.ds(so + r, ch), :] * _sigmoid(g_ref[pl.ds(so + r, ch), :])

            _chunks(n, ch, fill)
        for so, po, n in segs:
            def conv(r, so=so, po=po):
                o_ref[pl.ds(so + r, ch), :] = _taps(pad, wv, po + r, ch, CONV_K, moved_ref=moved) + b_ref[...]

            _chunks(n, ch, conv)

    return _call(
        body, name="conv_fwd", grid=(nj, B),
        in_specs=[pl.BlockSpec((R, cw), lambda j, b: (b, j)), pl.BlockSpec((R, cw), lambda j, b: (b, nj + j)),
                  pl.BlockSpec((32, cw), lambda j, b: (0, j)), pl.BlockSpec((1, cw), lambda j, b: (0, j))],
        out_specs=pl.BlockSpec((R, cw), lambda j, b: (b, j)), out_shape=_sds((B * R, CONV_CH), F32),
        scratch=[pltpu.VMEM((_padded_rows(C, S), cw), F32), _moved_scratch(ch, cw)])(z_conv, z_conv, w, bias)


def _conv_bwd(dhc, z_conv, w, B, C, S, cw=128, ch=128):
    R = C + S
    nj = CONV_CH // cw
    segs = _segments(C, S)

    def body(d_ref, a_ref, g_ref, w_ref, da_ref, dg_ref, dw_ref, sa_ref, sg_ref, gpad, dpad, dwacc, moved):
        b = pl.program_id(1)

        @pl.when(b == 0)
        def _():
            dw_ref[...] = jnp.zeros_like(dw_ref)
            sa_ref[...] = jnp.zeros_like(sa_ref)
            sg_ref[...] = jnp.zeros_like(sg_ref)

        _zero_gaps(gpad, C, S)
        _zero_gaps(dpad, C, S)
        dwacc[...] = jnp.zeros_like(dwacc)
        wv = w_ref[...]
        for so, po, n in segs:
            def fill(r, so=so, po=po):
                gpad[pl.ds(po + r, ch), :] = a_ref[pl.ds(so + r, ch), :] * _sigmoid(g_ref[pl.ds(so + r, ch), :])
                dpad[pl.ds(po + r, ch), :] = d_ref[pl.ds(so + r, ch), :]

            _chunks(n, ch, fill)
        for so, po, n in segs:
            def step(r, so=so, po=po):
                _tap_grads(dwacc, dpad[pl.ds(po + r, ch), :], gpad, po + r, ch, CONV_K, moved_ref=moved)
                dglu = _taps(dpad, wv, po + r, ch, CONV_K, flip=True, moved_ref=moved)
                av = a_ref[pl.ds(so + r, ch), :]
                sg = _sigmoid(g_ref[pl.ds(so + r, ch), :])
                da = dglu * sg
                dg = dglu * av * sg * (1.0 - sg)
                da_ref[pl.ds(so + r, ch), :] = da.astype(MXU)
                dg_ref[pl.ds(so + r, ch), :] = dg.astype(MXU)
                sa_ref[...] += _colsum(da)
                sg_ref[...] += _colsum(dg)

            _chunks(n, ch, step)
        for k in range(CONV_K):
            dw_ref[k:k + 1, :] += _colsum(dwacc[k])

    blk = pl.BlockSpec((R, cw), lambda j, b: (b, j))
    acc1 = pl.BlockSpec((1, cw), lambda j, b: (0, j))
    return _call(
        body, name="conv_bwd", grid=(nj, B),
        in_specs=[blk, blk, pl.BlockSpec((R, cw), lambda j, b: (b, nj + j)), pl.BlockSpec((32, cw), lambda j, b: (0, j))],
        out_specs=[blk, blk, pl.BlockSpec((32, cw), lambda j, b: (0, j)), acc1, acc1],
        out_shape=[_sds((B * R, CONV_CH), MXU), _sds((B * R, CONV_CH), MXU), _sds((32, CONV_CH), F32),
                   _sds((1, CONV_CH), F32), _sds((1, CONV_CH), F32)],
        scratch=[pltpu.VMEM((_padded_rows(C, S), cw), F32), pltpu.VMEM((_padded_rows(C, S), cw), F32),
                 pltpu.VMEM((32, 8, cw), F32), _moved_scratch(ch, cw)])(dhc, z_conv, z_conv, w)


def _ffn_mid(up, w, bias, B, C, S, cw=256, ch=64, carry=None):
    R = C + S
    nj = D_FF // cw
    segs = _segments(C, S)
    car = _Carried(carry, 4, 2, 1)

    def body(*refs):
        (a_ref, u_ref, w_ref, b_ref), (f_ref, ft_ref), (pad,) = car.split(refs)[0]
        parts = car.split(refs)[1]
        step = pl.program_id(0) * B + pl.program_id(1)
        car.before(step, parts)
        _zero_gaps(pad, C, S)
        wv = w_ref[...]
        for so, po, n in segs:
            def fill(r, so=so, po=po):
                pad[pl.ds(po + r, ch), :] = a_ref[pl.ds(so + r, ch), :]

            _chunks(n, ch, fill)
        for so, po, n in segs:
            def conv(r, so=so, po=po):
                ac = _taps(pad, wv, po + r, ch, FFN_K) + b_ref[...]
                f_ref[pl.ds(so + r, ch), :] = (ac * _sigmoid(ac) * u_ref[pl.ds(so + r, ch), :]).astype(MXU)

            _chunks(n, ch, conv)
        ft_ref[...] = f_ref[...].T
        car.after(step, nj * B, parts)

    res = _call(
        body, name="ffn_mid", grid=(nj, B),
        in_specs=[pl.BlockSpec((R, cw), lambda j, b: (b, j)), pl.BlockSpec((R, cw), lambda j, b: (b, nj + j)),
                  pl.BlockSpec((8, cw), lambda j, b: (0, j)), pl.BlockSpec((1, cw), lambda j, b: (0, j))] + car.in_specs(),
        out_specs=[pl.BlockSpec((R, cw), lambda j, b: (b, j)), pl.BlockSpec((cw, R), lambda j, b: (j, b))] + car.out_specs(),
        out_shape=[_sds((B * R, D_FF), MXU), _sds((D_FF, B * R), MXU)] + list(carry.out_shape if carry else []),
        scratch=[pltpu.VMEM((_padded_rows(C, S), cw), F32)] + list(carry.scratch if carry else []),
    )(up, up, w, bias, *(carry.inputs if carry else []))
    return res[0], res[1], res[2:]


def _ffn_mid_bwd(df, up, w, bias, B, C, S, cw=128, ch=128, carry=None):
    R = C + S
    nj = D_FF // cw
    segs = _segments(C, S)
    car = _Carried(carry, 5, 4, 3)

    def body(*refs):
        (d_ref, a_ref, u_ref, w_ref, b_ref), (da_ref, du_ref, dw_ref, db_ref), (apad, dpad, dwacc) = car.split(refs)[0]
        parts = car.split(refs)[1]
        b = pl.program_id(1)
        step = pl.program_id(0) * B + b
        car.before(step, parts)

        @pl.when(b == 0)
        def _():
            dw_ref[...] = jnp.zeros_like(dw_ref)
            db_ref[...] = jnp.zeros_like(db_ref)

        _zero_gaps(apad, C, S)
        _zero_gaps(dpad, C, S)
        dwacc[...] = jnp.zeros_like(dwacc)
        wv = w_ref[...]
        for so, po, n in segs:
            def fill(r, so=so, po=po):
                apad[pl.ds(po + r, ch), :] = a_ref[pl.ds(so + r, ch), :]

            _chunks(n, ch, fill)
        for so, po, n in segs:
            def first(r, so=so, po=po):
                ac = _taps(apad, wv, po + r, ch, FFN_K) + b_ref[...]
                sg = _sigmoid(ac)
                dfv = d_ref[pl.ds(so + r, ch), :]
                du_ref[pl.ds(so + r, ch), :] = (dfv * ac * sg).astype(MXU)
                dac = dfv * u_ref[pl.ds(so + r, ch), :] * (sg * (1.0 + ac * (1.0 - sg)))
                dpad[pl.ds(po + r, ch), :] = dac
                db_ref[...] += _colsum(dac)

            _chunks(n, ch, first)
        for so, po, n in segs:
            def second(r, so=so, po=po):
                _tap_grads(dwacc, dpad[pl.ds(po + r, ch), :], apad, po + r, ch, FFN_K)
                da_ref[pl.ds(so + r, ch), :] = _taps(dpad, wv, po + r, ch, FFN_K, flip=True).astype(MXU)

            _chunks(n, ch, second)
        for k in range(FFN_K):
            dw_ref[k:k + 1, :] += _colsum(dwacc[k])
        car.after(step, nj * B, parts)

    blk = pl.BlockSpec((R, cw), lambda j, b: (b, j))
    res = _call(
        body, name="ffn_mid_bwd", grid=(nj, B),
        in_specs=[blk, blk, pl.BlockSpec((R, cw), lambda j, b: (b, nj + j)), pl.BlockSpec((8, cw), lambda j, b: (0, j)),
                  pl.BlockSpec((1, cw), lambda j, b: (0, j))] + car.in_specs(),
        out_specs=[blk, blk, pl.BlockSpec((8, cw), lambda j, b: (0, j)), pl.BlockSpec((1, cw), lambda j, b: (0, j))]
        + car.out_specs(),
        out_shape=[_sds((B * R, D_FF), MXU), _sds((B * R, D_FF), MXU), _sds((8, D_FF), F32), _sds((1, D_FF), F32)]
        + list(carry.out_shape if carry else []),
        scratch=[pltpu.VMEM((_padded_rows(C, S), cw), F32), pltpu.VMEM((_padded_rows(C, S), cw), F32),
                 pltpu.VMEM((8, 8, cw), F32)] + list(carry.scratch if carry else []),
    )(df, up, up, w, bias, *(carry.inputs if carry else []))
    return res[0], res[1], res[2], res[3], res[4:]


def _trailing_sums(x, w):
    s, k = x, 1
    while k < w:
        s = s + pltpu.roll(s, k, 0)
        k *= 2
    return s


def _window_count(r, ch, n, w):
    t = r + lax.broadcasted_iota(jnp.int32, (ch, 1), 0)
    return (jnp.minimum(t + w // 2, n) - jnp.maximum(t - w // 2, 0)).astype(F32)


def _pool_fwd(z_pool, pool_w, pool_scale, B, C, S, ch=128):
    R = C + S
    gch = POOL_GCH
    segs = _segments(C, S)

    def body(u_ref, pw_ref, sc_ref, pooled_ref, po_ref, pad):
        g = pl.program_id(1)
        _zero_gaps(pad, C, S)
        for so, po, n in segs:
            def fill(r, so=so, po=po):
                pad[pl.ds(po + r, ch), :] = u_ref[pl.ds(so + r, ch), :]

            _chunks(n, ch, fill)
        for gi, w in enumerate(POOL_WINDOWS):
            @pl.when(g == gi)
            def _(w=w):
                for so, po, n in segs:
                    def step(r, so=so, po=po, n=n):
                        win = _Window(pad, po + r, ch)
                        acc = _trailing_sums(win.win, w)[GAP + w // 2 - 1:GAP + w // 2 - 1 + ch, :]
                        pooled = (acc / _window_count(r, ch, n, w) - win.at(0)).astype(MXU)
                        pooled_ref[pl.ds(so + r, ch), :] = pooled
                        po_ref[pl.ds(so + r, ch), :] = _dot(pooled, pw_ref[0], NN) * sc_ref[...]

                    _chunks(n, ch, step)

    blk = pl.BlockSpec((R, gch), lambda b, g: (b, g))
    return _call(
        body, name="pool_fwd", grid=(B, len(POOL_WINDOWS)),
        in_specs=[blk, pl.BlockSpec((1, gch, gch), lambda b, g: (g, 0, 0)), pl.BlockSpec((1, gch), lambda b, g: (0, g))],
        out_specs=[blk, blk], out_shape=[_sds((B * R, POOL_CH), MXU), _sds((B * R, POOL_CH), F32)],
        scratch=[pltpu.VMEM((_padded_rows(C, S), gch), F32)])(z_pool, pool_w, pool_scale)


def _pool_bwd(dpo, pooled, pool_w, pool_scale, B, C, S, ch=128):
    R = C + S
    gch = POOL_GCH
    segs = _segments(C, S)

    def body(d_ref, p_ref, pw_ref, sc_ref, du_ref, dpw_ref, dsc_ref, su_ref, qpad, dpl):
        g, b = pl.program_id(0), pl.program_id(1)

        @pl.when(b == 0)
        def _():
            dpw_ref[...] = jnp.zeros_like(dpw_ref)
            dsc_ref[...] = jnp.zeros_like(dsc_ref)
            su_ref[...] = jnp.zeros_like(su_ref)

        _zero_gaps(qpad, C, S)
        pw = pw_ref[0]
        for gi, w in enumerate(POOL_WINDOWS):
            @pl.when(g == gi)
            def _(w=w):
                for so, po, n in segs:
                    def first(r, so=so, po=po, n=n):
                        pv = p_ref[pl.ds(so + r, ch), :]
                        dv = d_ref[pl.ds(so + r, ch), :]
                        dsc_ref[...] += _colsum(dv * _dot(pv, pw, NN))
                        dmx = (dv * sc_ref[...]).astype(MXU)
                        dpw_ref[0] += _dot(pv, dmx, TN)
                        dp = _dot(dmx, pw, NT)
                        dpl[pl.ds(so + r, ch), :] = dp
                        qpad[pl.ds(po + r, ch), :] = dp / _window_count(r, ch, n, w)

                    _chunks(n, ch, first)
                for so, po, n in segs:
                    def second(r, so=so, po=po):
                        win = _Window(qpad, po + r, ch)
                        acc = _trailing_sums(win.win, w)[GAP + w // 2:GAP + w // 2 + ch, :]
                        du = acc - dpl[pl.ds(so + r, ch), :]
                        du_ref[pl.ds(so + r, ch), :] = du.astype(MXU)
                        su_ref[...] += _colsum(du)

                    _chunks(n, ch, second)

    blk = pl.BlockSpec((R, gch), lambda g, b: (b, g))
    vec = pl.BlockSpec((1, gch), lambda g, b: (0, g))
    wblk = pl.BlockSpec((1, gch, gch), lambda g, b: (g, 0, 0))
    return _call(
        body, name="pool_bwd", grid=(len(POOL_WINDOWS), B), in_specs=[blk, blk, wblk, vec], out_specs=[blk, wblk, vec, vec],
        out_shape=[_sds((B * R, POOL_CH), MXU), _sds((len(POOL_WINDOWS), gch, gch), F32), _sds((1, POOL_CH), F32),
                   _sds((1, POOL_CH), F32)],
        scratch=[pltpu.VMEM((_padded_rows(C, S), gch), F32), pltpu.VMEM((R, gch), F32)])(dpo, pooled, pool_w, pool_scale)


def _silu_rows(cond):
    def body(c_ref, s_ref, d_ref):
        c = c_ref[...]
        sg = _sigmoid(c)
        s_ref[...] = (c * sg).astype(MXU)
        d_ref[...] = sg * (1.0 + c * (1.0 - sg))

    full = pl.BlockSpec(cond.shape, lambda i: (0, 0))
    return _call(body, name="silu_rows", grid=(1,), in_specs=[full], out_specs=[full, full],
                 out_shape=[_sds(cond.shape, MXU), _sds(cond.shape, F32)])(cond)


def _row_tile(rows, cols, n_bufs):
    cap = max(16, (16 * 2 ** 20) // (4 * n_bufs * max(cols, 128)))
    return rows if rows <= cap else _pick(rows, cap, 16)


def _adamw(parts, w, m, v, layer, prev, *, name):
    n_parts, rows, cols = parts.shape
    layers = w.shape[0]
    c1 = 1.0 - ADAM_B1 ** ADAM_STEP
    c2 = 1.0 - ADAM_B2 ** ADAM_STEP
    tr = _row_tile(rows, cols, n_parts + 7)

    def body(p_ref, w_ref, m_ref, v_ref, *rest):
        g_ref, d_ref, nm_ref, nv_ref = rest[-4:]
        g = p_ref[0].astype(F32)
        for k in range(1, n_parts):
            g = g + p_ref[k].astype(F32)
        nm = ADAM_B1 * m_ref[...] + (1.0 - ADAM_B1) * g
        nv = ADAM_B2 * v_ref[...] + (1.0 - ADAM_B2) * (g * g)
        g_ref[...] = g
        nm_ref[...] = nm
        nv_ref[...] = nv
        d_ref[...] = -ADAM_LR * ((nm / c1) / (jnp.sqrt(nv / c2) + ADAM_EPS) + ADAM_WD * w_ref[...])

    blk = pl.BlockSpec((None, tr, cols), lambda i: (layer, i, 0))
    in_specs = [pl.BlockSpec((n_parts, tr, cols), lambda i: (0, i, 0)), blk, blk, blk]
    args = [parts, w, m, v]
    aliases = {}
    if prev is not None:
        in_specs += [ANY] * 4
        aliases = {4 + k: k for k in range(4)}
        args += list(prev)
    return _call(body, name=name, grid=(rows // tr,), in_specs=in_specs, out_specs=[blk] * 4,
                 out_shape=[_sds((layers, rows, cols), F32)] * 4, aliases=aliases)(*args)


def _pair_sum(g, r1):
    _, rows, cols = g.shape
    c = lax.axis_index("c")
    g4 = g.reshape(4, 2, rows, cols)
    tr = _row_tile(rows, cols, 10)

    def body(c_ref, g_ref, r_ref, o_ref):
        o_ref[...] = (g_ref[...] + r_ref[...]).astype(WIRE)

    return _pcall(
        body, name="pair_sum", out_shape=_sds((4, rows, cols), WIRE),
        grid_spec=pltpu.PrefetchScalarGridSpec(
            num_scalar_prefetch=1, grid=(rows // tr,),
            in_specs=[pl.BlockSpec((4, None, tr, cols), lambda i, c_ref: (0, c_ref[0], i, 0)),
                      pl.BlockSpec((4, tr, cols), lambda i, c_ref: (0, i, 0))],
            out_specs=pl.BlockSpec((4, tr, cols), lambda i, c_ref: (0, i, 0))),
        compiler_params=pltpu.CompilerParams(dimension_semantics=("arbitrary",), vmem_limit_bytes=VMEM_MB * 2 ** 20),
    )(jnp.reshape(c, (1,)).astype(jnp.int32), g4, r1)


MESH = pl.DeviceIdType.MESH
ANY = pl.BlockSpec(memory_space=pl.ANY)


class _Exchange:
    inputs, out_shape, scratch = (), (), ()

    def start(self, ins, outs, sems):
        raise NotImplementedError

    def mid(self, ins, outs, sems):
        pass

    def finish(self, ins, outs, sems):
        raise NotImplementedError


def _run_exchange(ex, *, name):
    n_in, n_out = len(ex.inputs), len(ex.out_shape)

    def body(*refs):
        parts = refs[:n_in], refs[n_in:n_in + n_out], refs[n_in + n_out:]
        ex.start(*parts)
        ex.mid(*parts)
        ex.finish(*parts)

    return _pcall(body, name=name, out_shape=list(ex.out_shape), in_specs=[ANY] * n_in, out_specs=[ANY] * n_out,
                  scratch_shapes=list(ex.scratch))(*ex.inputs)


class _Gather(_Exchange):
    def __init__(self, shards, layer=None):
        self.inputs, self.layer, self.n_t = list(shards), layer, len(shards)
        self.out_shape = [_sds((N_DEV,) + tuple(s.shape if layer is None else s.shape[1:]), s.dtype) for s in shards]
        self.scratch = [pltpu.SemaphoreType.DMA((7 * self.n_t,)), pltpu.SemaphoreType.DMA((7 * self.n_t,)),
                        pltpu.SemaphoreType.DMA((self.n_t,))]

    def _place(self):
        x, y, c = lax.axis_index("x"), lax.axis_index("y"), lax.axis_index("c")
        return (x, y, c), (x, y, 1 - c), [(1 - x, y), (x, 1 - y), (1 - x, 1 - y)]

    def _own(self, ins, n):
        return ins[n] if self.layer is None else ins[n].at[self.layer]

    def _copy(self, ins, outs, sems, n, k, blk, to, own=False):
        dst = outs[n].at[4 * blk[0] + 2 * blk[1] + blk[2]]
        return pltpu.make_async_remote_copy(src_ref=self._own(ins, n) if own else dst, dst_ref=dst, send_sem=sems[0].at[n * 7 + k],
                                            recv_sem=sems[1].at[n * 7 + k], device_id=to, device_id_type=MESH)

    def _mine(self, ins, outs, sems, n):
        (x, y, c), _, _ = self._place()
        return pltpu.make_async_copy(self._own(ins, n), outs[n].at[4 * x + 2 * y + c], sems[2].at[n])

    def start(self, ins, outs, sems):
        me, sibling, chips = self._place()
        for n in range(self.n_t):
            self._mine(ins, outs, sems, n).start()
        for j, chip in enumerate(chips):
            for n in range(self.n_t):
                self._copy(ins, outs, sems, n, 1 + j, me, (*chip, me[2]), own=True).start()
        for n in range(self.n_t):
            self._copy(ins, outs, sems, n, 0, me, sibling, own=True).start()

    def mid(self, ins, outs, sems):
        me, sibling, chips = self._place()
        for j, chip in enumerate(chips):
            for n in range(self.n_t):
                self._copy(ins, outs, sems, n, 1 + j, (*chip, me[2]), me).wait_recv()
                self._copy(ins, outs, sems, n, 4 + j, (*chip, me[2]), sibling).start()

    def finish(self, ins, outs, sems):
        me, sibling, chips = self._place()
        for n in range(self.n_t):
            self._copy(ins, outs, sems, n, 0, sibling, me).wait_recv()
        for j, chip in enumerate(chips):
            for n in range(self.n_t):
                self._copy(ins, outs, sems, n, 4 + j, (*chip, 1 - me[2]), me).wait_recv()
        for j, chip in enumerate(chips):
            for n in range(self.n_t):
                self._copy(ins, outs, sems, n, 1 + j, me, (*chip, me[2]), own=True).wait_send()
                self._copy(ins, outs, sems, n, 4 + j, (*chip, me[2]), sibling).wait_send()
        for n in range(self.n_t):
            self._copy(ins, outs, sems, n, 0, me, sibling, own=True).wait_send()
            self._mine(ins, outs, sems, n).wait()


def _all_gather(shards, *, name, layer=None):
    return _run_exchange(_Gather(shards, layer), name=name)


class _SiblingExchange(_Exchange):
    def __init__(self, gs):
        self.n_t = len(gs)
        self.inputs = [g.reshape((4, 2) + g.shape[1:]) for g in gs]
        self.out_shape = [_sds((4,) + g.shape[1:], g.dtype) for g in gs]
        self.scratch = [pltpu.SemaphoreType.DMA((self.n_t,)), pltpu.SemaphoreType.DMA((self.n_t,))]

    def _copy(self, ins, outs, sems, n):
        x, y, c = lax.axis_index("x"), lax.axis_index("y"), lax.axis_index("c")
        return pltpu.make_async_remote_copy(src_ref=ins[n].at[:, 1 - c], dst_ref=outs[n], send_sem=sems[0].at[n],
                                            recv_sem=sems[1].at[n], device_id=(x, y, 1 - c), device_id_type=MESH)

    def start(self, ins, outs, sems):
        for n in range(self.n_t):
            self._copy(ins, outs, sems, n).start()

    def finish(self, ins, outs, sems):
        for n in range(self.n_t):
            self._copy(ins, outs, sems, n).wait_recv()
        for n in range(self.n_t):
            self._copy(ins, outs, sems, n).wait_send()


class _ChipExchange(_Exchange):
    def __init__(self, ps):
        self.n_t, self.inputs = len(ps), list(ps)
        self.out_shape = [_sds(p.shape, p.dtype) for p in ps]
        self.scratch = [pltpu.SemaphoreType.DMA((3 * self.n_t,)), pltpu.SemaphoreType.DMA((3 * self.n_t,)),
                        pltpu.SemaphoreType.DMA((self.n_t,))]

    def _place(self):
        x, y, c = lax.axis_index("x"), lax.axis_index("y"), lax.axis_index("c")
        return 2 * x + y, c, [(1 - x, y), (x, 1 - y), (1 - x, 1 - y)]

    def _copy(self, ins, outs, sems, n, j, src_chip, dst_slot):
        _, c, chips = self._place()
        return pltpu.make_async_remote_copy(
            src_ref=ins[n].at[src_chip], dst_ref=outs[n].at[dst_slot], send_sem=sems[0].at[n * 3 + j],
            recv_sem=sems[1].at[n * 3 + j], device_id=(*chips[j], c), device_id_type=MESH)

    def _own(self, ins, outs, sems, n):
        mine, _, _ = self._place()
        return pltpu.make_async_copy(ins[n].at[mine], outs[n].at[mine], sems[2].at[n])

    def start(self, ins, outs, sems):
        mine, _, chips = self._place()
        for n in range(self.n_t):
            self._own(ins, outs, sems, n).start()
        for j, (px, py) in enumerate(chips):
            for n in range(self.n_t):
                self._copy(ins, outs, sems, n, j, 2 * px + py, mine).start()

    def finish(self, ins, outs, sems):
        mine, _, chips = self._place()
        for j, (px, py) in enumerate(chips):
            for n in range(self.n_t):
                self._copy(ins, outs, sems, n, j, mine, 2 * px + py).wait_recv()
        for j, (px, py) in enumerate(chips):
            for n in range(self.n_t):
                self._copy(ins, outs, sems, n, j, 2 * px + py, mine).wait_send()
        for n in range(self.n_t):
            self._own(ins, outs, sems, n).wait()


BIG = (("w_ada", (D, N_MOD * D // N_DEV), 1), ("w_in", (D, D_IN // N_DEV), 1), ("conv_pw_w", (CONV_CH // N_DEV, D), 0),
       ("pool_w", (len(POOL_WINDOWS), POOL_GCH // N_DEV, POOL_GCH), 1), ("w_out", (D // N_DEV, D), 0),
       ("w_up", (D, 2 * D_FF // N_DEV), 1), ("w_down", (D_FF // N_DEV, D), 0))
TAPS = (("conv_dw_w", (CONV_K, CONV_CH // N_DEV), 1), ("ffn_dw_w", (FFN_K, D_FF // N_DEV), 1))
SHARDED = BIG + TAPS
EARLY = (0, 1, 5)
LATE = (2, 3, 4, 6)
REPLICATED =(("b_ada", N_MOD * D), ("b_in", D_IN), ("q_gain", HD), ("k_gain", HD), ("conv_dw_b", CONV_CH), ("conv_ln_g", CONV_CH),
              ("conv_ln_b", CONV_CH), ("conv_pw_b", D), ("pool_scale", POOL_CH), ("b_out", D), ("ln1_g", D), ("ln1_b", D),
              ("ln2_g", D), ("ln2_b", D), ("ffn_dw_b", D_FF))


def _as_rows(shape):
    return (int(np.prod(shape[:-1])), shape[-1])


def _full_from_blocks(blocks, axis):
    moved = jnp.moveaxis(blocks, 0, axis)
    shape = list(moved.shape)
    shape[axis:axis + 2] = [shape[axis] * shape[axis + 1]]
    return moved.reshape(shape)


def _blocks_from_full(full, axis):
    shape = list(full.shape)
    shape[axis:axis + 1] = [N_DEV, shape[axis] // N_DEV]
    return jnp.moveaxis(full.reshape(shape), axis, 0)


SMALL_N = DEPTH * sum(n for _, n in REPLICATED) + D
SMALL_ROWS = -(-(SMALL_N + 1) // (8 * LANES)) * 8


def _rope_tables(C, S):
    t = np.arange(S)
    inv_freq = ROPE_THETA ** (-np.arange(ROPE_PAIRS, dtype=np.float32) / ROPE_PAIRS)
    row = jnp.asarray((t // GRID_W).astype(np.float32))[:, None] * jnp.asarray(inv_freq, F32)
    col = jnp.asarray((t % GRID_W).astype(np.float32))[:, None] * jnp.asarray(inv_freq, F32)
    cos = jnp.concatenate([jnp.cos(row), jnp.cos(row), jnp.cos(col), jnp.cos(col)], axis=1)
    sin = jnp.concatenate([-jnp.sin(row), jnp.sin(row), -jnp.sin(col), jnp.sin(col)], axis=1)
    cos = jnp.concatenate([jnp.ones((C, HD), F32), cos], axis=0)
    sin = jnp.concatenate([jnp.zeros((C, HD), F32), sin], axis=0)
    return cos, sin


def _segment_sums(parts, B, tpe, ncq):
    p = parts.reshape(B, tpe, D)
    return jnp.concatenate([jnp.sum(p[:, ncq:], axis=1), jnp.sum(p[:, :ncq], axis=(0, 1))[None]], axis=0)


def kernel(x, c, ctx, c_ctx, w_ada, b_ada, w_in, b_in, q_gain, k_gain, conv_dw_w, conv_dw_b, conv_ln_g, conv_ln_b, conv_pw_w, conv_pw_b, pool_w, pool_scale, w_out, b_out, ln1_g, ln1_b, ln2_g, ln2_b, w_up, ffn_dw_w, ffn_dw_b, w_down, loss_target, m_c_ctx, m_w_ada, m_b_ada, m_w_in, m_b_in, m_q_gain, m_k_gain, m_conv_dw_w, m_conv_dw_b, m_conv_ln_g, m_conv_ln_b, m_conv_pw_w, m_conv_pw_b, m_pool_w, m_pool_scale, m_w_out, m_b_out, m_ln1_g, m_ln1_b, m_ln2_g, m_ln2_b, m_w_up, m_ffn_dw_w, m_ffn_dw_b, m_w_down, v_c_ctx, v_w_ada, v_b_ada, v_w_in, v_b_in, v_q_gain, v_k_gain, v_conv_dw_w, v_conv_dw_b, v_conv_ln_g, v_conv_ln_b, v_conv_pw_w, v_conv_pw_b, v_pool_w, v_pool_scale, v_w_out, v_b_out, v_ln1_g, v_ln1_b, v_ln2_g, v_ln2_b, v_w_up, v_ffn_dw_w, v_ffn_dw_b, v_w_down):
    given = dict(locals())
    B, S, _ = x.shape
    C = ctx.shape[1]
    R = C + S
    T = B * R
    tpe, ncq = R // TM, C // TM
    nt = T // TM
    assert S % TM == 0 and C % TM == 0 and B + 1 <= 16

    operands = [given[n].astype(MXU) for n, _, _ in BIG]
    taps = _all_gather([given[n] for n, _, _ in TAPS], name="gather_taps")
    W, modt = [], []

    xu = jnp.concatenate([ctx, x], axis=1).reshape(T, D)
    cond = jnp.concatenate([c, c_ctx[None], jnp.zeros((16 - B - 1, D), F32)], axis=0)
    s_cond, ds_cond = _silu_rows(cond)
    ctx_tile = jnp.asarray((np.arange(tpe) < ncq)[None, :, None])
    cos_t, sin_t = _rope_tables(C, S)
    row = lambda v: v.reshape(1, -1)

    def add_layer(blocks):
        l = len(W)
        wl = {n: _full_from_blocks(blk, a) for (n, _, a), blk in zip(BIG, blocks)}
        for (n, _, a), blk in zip(TAPS, taps):
            wl[n] = _full_from_blocks(blk[:, l], a)
        W.append(wl)
        m = _matmul(s_cond, wl["w_ada"], "nn", name="ada", bias=row(b_ada[l]), tm=16, tn=1024)
        modt.append(jnp.where(ctx_tile, m[B][None, None, :], m[:B][:, None, :]).reshape(nt, 1, N_MOD * D))

    add_layer(_all_gather(operands, name="gather_weights", layer=0))

    saved = []
    h1, h1t = _modulate_cast(xu, modt[0], 0)
    xin = xu
    for l in range(DEPTH):
        wl = W[l]
        w_inl, b_inl = wl["w_in"], b_in[l]
        z_qkv = _matmul(h1, w_inl[:, :QKV_W], "nn", name="z_qkv", bias=row(b_inl[:QKV_W]), tn=768)
        c0, p0, g0 = QKV_W, QKV_W + 2 * CONV_CH, QKV_W + 2 * CONV_CH + POOL_CH
        z_conv = _matmul(h1, w_inl[:, c0:p0], "nn", name="z_conv", bias=row(b_inl[c0:p0]))
        z_pool = _matmul(h1, w_inl[:, p0:g0], "nn", name="z_pool", bias=row(b_inl[p0:g0]))
        z_gate = _matmul(h1, w_inl[:, g0:], "nn", name="z_gate", bias=row(b_inl[g0:]))
        qkv = _qk_prep(z_qkv, cos_t, sin_t, row(q_gain[l]), row(k_gain[l]), tpe)
        if l + 1 < DEPTH:
            attn, early_blocks = _attn_fwd(qkv, B, C, R, carry=_Gather([operands[k] for k in EARLY], l + 1))
        else:
            attn, _ = _attn_fwd(qkv, B, C, R)
        dw32 =jnp.pad(wl["conv_dw_w"], ((0, 32 - CONV_K), (0, 0)))
        hc = _conv_fwd(z_conv, dw32, row(conv_dw_b[l]), B, C, S)
        sw, swt = _ln_silu(hc, row(conv_ln_g[l]), row(conv_ln_b[l]))
        conv_o = _matmul(sw, wl["conv_pw_w"], "nn", name="conv_pw", bias=row(conv_pw_b[l]))
        pooled, pool_o = _pool_fwd(z_pool, wl["pool_w"], row(pool_scale[l]), B, C, S)
        m, mt = _merge(attn, conv_o, pool_o, z_gate)
        mo = _matmul(m, wl["w_out"], "nn", name="w_out", bias=row(b_out[l]))
        y1, h2, h2t = _resid_ln(xin, mo, modt[l], 2, row(ln1_g[l]), row(ln1_b[l]), modt[l], 3)
        up = _matmul(h2, wl["w_up"], "nn", name="w_up", tn=1408)
        fw8 = jnp.pad(wl["ffn_dw_w"], ((0, 8 - FFN_K), (0, 0)))
        if l + 1 < DEPTH:
            f, ft, late_blocks = _ffn_mid(up, fw8, row(ffn_dw_b[l]), B, C, S, carry=_Gather([operands[k] for k in LATE], l + 1))
            blocks = dict(zip(EARLY + LATE, list(early_blocks) + list(late_blocks)))
            add_layer([blocks[k] for k in range(len(BIG))])
        else:
            f, ft, _ = _ffn_mid(up, fw8, row(ffn_dw_b[l]), B, C, S)
        fo = _matmul(f, wl["w_down"], "nn", name="w_down", tm=512)
        if l + 1 < DEPTH:
            y2, h_next, ht_next = _resid_ln(y1, fo, modt[l], 5, row(ln2_g[l]), row(ln2_b[l]), modt[l + 1], 0)
        else:
            y2, h_next, ht_next = _resid_ln(y1, fo, modt[l], 5, row(ln2_g[l]), row(ln2_b[l])), None, None
        saved.append(dict(xin=xin, h1t=h1t, z_qkv=z_qkv, z_conv=z_conv, z_gate=z_gate, qkv=qkv, attn=attn, hc=hc, swt=swt,
                          conv_o=conv_o, pooled=pooled, pool_o=pool_o, mt=mt, mo=mo, y1=y1, h2t=h2t, up=up, ft=ft, fo=fo,
                          dw32=dw32, fw8=fw8))
        xin, h1, h1t = y2, h_next, ht_next

    dy, loss_part = _loss_grad(xin, loss_target.reshape(B * S, D), tpe, ncq)

    small = {n: [None] * DEPTH for n, _ in REPLICATED}
    d_c_ctx = jnp.zeros((D,), F32)
    dmods_t = [[None] * N_MOD for _ in range(DEPTH)]
    layer_grads = [None] * DEPTH
    kinds = ("grad_", "delta_", "new_m_", "new_v_")
    stacks = {n: [given[pre + n].reshape((DEPTH,) + _as_rows(s)) for pre in ("", "m_", "v_")] for n, s, _ in SHARDED}
    results = {n: None for n, _, _ in SHARDED}

    def reduce_prepare(l):
        nonlocal d_c_ctx
        dmods = jnp.concatenate([_segment_sums(p, B, tpe, ncq) for p in dmods_t[l]], axis=1)
        small["b_ada"][l] = jnp.sum(dmods, axis=0)
        dm16 = jnp.concatenate([dmods, jnp.zeros((16 - B - 1, N_MOD * D), F32)], axis=0).astype(MXU)
        layer_grads[l]["w_ada"] = _matmul(s_cond, dm16, "tn", name="g_w_ada", tm=1024, tn=1024)
        dcond = _matmul(dm16, W[l]["w_ada"], "nt", name="d_cond", tm=16, tn=1024, tk=2048)
        d_c_ctx = d_c_ctx + dcond[B] * ds_cond[B]
        return [_blocks_from_full(layer_grads[l][n], a).reshape((N_DEV,) + _as_rows(s)) for n, s, a in SHARDED]

    def reduce_end(l, r2):
        for (n, _, _), parts in zip(SHARDED, r2):
            results[n] = _adamw(parts, *stacks[n], l, results[n], name="adamw_sharded")

    dh1 = None
    for l in reversed(range(DEPTH)):
        gs_above = None
        wl, sv = W[l], saved[l]
        dmod = dmods_t[l]
        if dh1 is None:
            dy1p, dfo, dgate2, dg, db, _ = _ln_bwd(dy, sv["y1"], sv["fo"], modt[l], 5, row(ln2_g[l]))
        else:
            dy1p, dfo, dgate2, dg, db, _, dsh, dsc = _ln_bwd(dy, sv["y1"], sv["fo"], modt[l], 5, row(ln2_g[l]), dh=dh1,
                                                             y=saved[l + 1]["xin"], mod_next=modt[l + 1], k_shift_next=0)
            dmods_t[l + 1][0], dmods_t[l + 1][1] = dsh, dsc
            gs_above = reduce_prepare(l + 1)
        small["ln2_g"][l], small["ln2_b"][l] = dg[0], db[0]
        dmod[5] = dgate2
        df = _matmul(dfo, wl["w_down"], "nt", name="d_f", tn=1408)
        g_w_down = _matmul(sv["ft"], dfo, "nn", name="g_w_down", tm=1408, tk=2304)
        if gs_above is None:
            da2, du2, g_fdw, g_fdb, _ = _ffn_mid_bwd(df, sv["up"], sv["fw8"], row(ffn_dw_b[l]), B, C, S)
            ps_above = None
        else:
            da2, du2, g_fdw, g_fdb, r1 = _ffn_mid_bwd(df, sv["up"], sv["fw8"], row(ffn_dw_b[l]), B, C, S,
                                                      carry=_SiblingExchange(gs_above))
            ps_above = [_pair_sum(g, r) for g, r in zip(gs_above, r1)]
        small["ffn_dw_b"][l] = g_fdb[0]
        dh2 = _matmul(da2, wl["w_up"][:, :D_FF], "nt", name="d_h2a", tm=512)
        dh2 = _matmul(du2, wl["w_up"][:, D_FF:], "nt", name="d_h2u", tm=512, acc_in=dh2)
        g_w_up = _matmul(sv["h2t"], da2, "nn", name="g_w_up_a", tn=1408, tk=2304, into=(2 * D_FF, 0, None))
        g_w_up = _matmul(sv["h2t"], du2, "nn", name="g_w_up_u", tn=1408, tk=2304, into=(2 * D_FF, D_FF, g_w_up))
        dxp, dmo, dgate1, dg, db, dbo, dsh, dsc = _ln_bwd(dy1p, sv["xin"], sv["mo"], modt[l], 2, row(ln1_g[l]), dh=dh2,
                                                          y=sv["y1"], mod_next=modt[l], k_shift_next=3)
        small["ln1_g"][l], small["ln1_b"][l], small["b_out"][l] = dg[0], db[0], dbo[0]
        dmod[2], dmod[3], dmod[4] = dgate1, dsh, dsc
        dm = _matmul(dmo, wl["w_out"], "nt", name="d_m")
        g_w_out = _matmul(sv["mt"], dmo, "nn", name="g_w_out", tk=2304)
        dattn, dconv_o, dpool_o, dzg, g_pwb, gsum_gate = _merge_bwd(dm, sv["attn"], sv["conv_o"], sv["pool_o"], sv["z_gate"])
        small["conv_pw_b"][l] = g_pwb[0]
        du, g_pool_w, g_pool_sc, gsum_pool = _pool_bwd(dpool_o, sv["pooled"], wl["pool_w"], row(pool_scale[l]), B, C, S)
        small["pool_scale"][l] = g_pool_sc[0]
        dsw = _matmul(dconv_o, wl["conv_pw_w"], "nt", name="d_sw")
        g_pw = _matmul(sv["swt"], dconv_o, "nn", name="g_conv_pw", tk=2304)
        dhc, g_cg, g_cb, g_cdb = _ln_silu_bwd(dsw, sv["hc"], row(conv_ln_g[l]), row(conv_ln_b[l]))
        small["conv_ln_g"][l], small["conv_ln_b"][l], small["conv_dw_b"][l] = g_cg[0], g_cb[0], g_cdb[0]
        da, dgt, g_cdw, gsum_a, gsum_gt = _conv_bwd(dhc, sv["z_conv"], sv["dw32"], B, C, S)
        if ps_above is None:
            dq, dk, dv, _ = _attn_bwd(sv["qkv"], sv["attn"], dattn, B, C, R)
        else:
            dq, dk, dv, r2 = _attn_bwd(sv["qkv"], sv["attn"], dattn, B, C, R, carry=_ChipExchange(ps_above))
            reduce_end(l + 1, r2)
        dz_qkv, g_qg, g_kg, gsum_qkv = _qk_bwd(dq, dk, dv, sv["z_qkv"], cos_t, sin_t, row(q_gain[l]), row(k_gain[l]), tpe)
        small["q_gain"][l], small["k_gain"][l] = g_qg[0], g_kg[0]
        small["b_in"][l] = jnp.concatenate([gsum_qkv[0], gsum_a[0], gsum_gt[0], gsum_pool[0], gsum_gate[0]])
        w_inl = wl["w_in"]
        g0 = QKV_W + 2 * CONV_CH + POOL_CH
        pieces = ((dzg, g0, N_GATE), (da, QKV_W, CONV_CH), (dgt, QKV_W + CONV_CH, CONV_CH), (du, QKV_W + 2 * CONV_CH, POOL_CH),
                  (dz_qkv, 0, QKV_W))
        dh1 = g_w_in = None
        at = 0
        for k, (dz, c0, wd) in enumerate(pieces):
            dh1 = _matmul(dz, w_inl[:, c0:c0 + wd], "nt", name=f"d_h1_{k}", tm=512, acc_in=dh1)
            g_w_in = _matmul(sv["h1t"], dz, "nn", name=f"g_w_in_{k}", tk=2304, into=(D_IN, at, g_w_in))
            at += wd
        g_w_in = jnp.concatenate([g_w_in[:, N_GATE + 2 * CONV_CH + POOL_CH:], g_w_in[:, N_GATE:N_GATE + 2 * CONV_CH + POOL_CH],
                                  g_w_in[:, :N_GATE]], axis=1)

        layer_grads[l] = {"w_in": g_w_in, "conv_pw_w": g_pw, "pool_w": g_pool_w, "w_out": g_w_out, "w_up": g_w_up,
                          "w_down": g_w_down, "conv_dw_w": g_cdw[:CONV_K], "ffn_dw_w": g_fdw[:FFN_K]}
        dy = dxp
    gx_u, dmods_t[0][0], dmods_t[0][1] = _mod_bwd(dy, dh1, saved[0]["xin"], modt[0], 0)
    grad_x = gx_u.reshape(B, R, D)[:, C:]

    gs = reduce_prepare(0)
    r1 = _run_exchange(_SiblingExchange(gs), name="sibling_exchange")
    reduce_end(0, _run_exchange(_ChipExchange([_pair_sum(g, r) for g, r in zip(gs, r1)]), name="chip_exchange"))
    outs = {}
    for n, s, _ in SHARDED:
        for kind, buf in zip(kinds, results[n]):
            outs[kind + n] = buf.reshape((DEPTH,) + tuple(s))

    def small_pack(pieces):
        flat = jnp.concatenate([p.reshape(-1) for p in pieces])
        return jnp.pad(flat, (0, SMALL_ROWS * LANES - flat.shape[0])).reshape(SMALL_ROWS, LANES)

    zero1 = jnp.zeros((1,), F32)
    g_pack = small_pack([small[n][l] for n, _ in REPLICATED for l in range(DEPTH)] + [d_c_ctx, loss_part[0, :1]])
    g_small, = _all_gather([g_pack], name="gather_small")
    wmv = [small_pack([given[pre + n] for n, _ in REPLICATED] + [given[pre + "c_ctx"], zero1])[None] for pre in ("", "m_", "v_")]
    res = _adamw(g_small, *wmv, 0, None, name="adamw_small")
    for kind, buf in zip(kinds, res):
        flat = buf.reshape(-1)
        off = 0
        for n, sz in REPLICATED:
            outs[kind + n] = flat[off:off + DEPTH * sz].reshape(DEPTH, sz)
            off += DEPTH * sz
        outs[kind + "c_ctx"] = flat[off:off + D]
        if kind == "grad_":
            loss = flat[off + D]

    names = ["c_ctx", "w_ada", "b_ada", "w_in", "b_in", "q_gain", "k_gain", "conv_dw_w", "conv_dw_b", "conv_ln_g", "conv_ln_b",
             "conv_pw_w", "conv_pw_b", "pool_w", "pool_scale", "w_out", "b_out", "ln1_g", "ln1_b", "ln2_g", "ln2_b", "w_up",
             "ffn_dw_w", "ffn_dw_b", "w_down"]
    return (loss, grad_x, *[outs[k + n] for k in ("grad_", "delta_", "new_m_", "new_v_") for n in names])
```

```python
import functools

import jax
import jax.numpy as jnp
import numpy as np
from jax import lax
from jax.experimental import pallas as pl
from jax.experimental.pallas import tpu as pltpu

F32 = jnp.float32
MXU = jnp.bfloat16
WIRE = jnp.bfloat16

D = 1024
HD = 128
NH = 8
NKV = 2
QG = NH // NKV
KV_W = NKV * HD
QKV_W = NH * HD + 2 * KV_W
CONV_CH = D
POOL_CH = D
POOL_WINDOWS = (2, 4, 8, 16)
POOL_GCH = POOL_CH // len(POOL_WINDOWS)
N_GATE = 3 * D
D_IN = QKV_W + 2 * CONV_CH + POOL_CH + N_GATE
D_FF = 2816
N_MOD = 6
DEPTH = 4
CONV_K = 31
FFN_K = 3
GRID_W = 64
ROPE_THETA = 10000.0
ROPE_PAIRS = HD // 4
ALPHA = (2 * DEPTH) ** 0.25
LN_EPS = 1e-5
RMS_EPS = 1e-6
ATTN_SCALE = HD ** -0.5
LOG2_E = 1.4426950408889634
ADAM_LR, ADAM_B1, ADAM_B2, ADAM_EPS, ADAM_WD, ADAM_STEP = 0.001, 0.9, 0.999, 1e-08, 0.01, 10

N_DEV = 8
TM = 256
GAP = 16
LANES = 1024
VMEM_MB = 48

NN = (((1,), (0,)), ((), ()))
NT = (((1,), (1,)), ((), ()))
TN = (((0,), (0,)), ((), ()))

_pcall = pl.pallas_call


def _call(body, *, name, grid, in_specs, out_specs, out_shape, scratch=(), aliases=None, vmem=VMEM_MB):
    return _pcall(
        body, name=name, grid=grid, in_specs=in_specs, out_specs=out_specs, out_shape=out_shape,
        scratch_shapes=list(scratch), input_output_aliases=aliases or {},
        compiler_params=pltpu.CompilerParams(dimension_semantics=("arbitrary",) * len(grid), vmem_limit_bytes=vmem * 2 ** 20),
    )


def _sds(shape, dtype):
    return jax.ShapeDtypeStruct(tuple(shape), dtype)


def _pick(n, cap, mult):
    best = None
    for t in range(mult, min(n, cap) + 1, mult):
        if n % t == 0:
            best = t
    return best if best is not None else n


def _dot(a, b, dims):
    return lax.dot_general(a, b, dims, preferred_element_type=F32)


def _sigmoid(x):
    return 1.0 / (1.0 + jnp.exp(-x))


def _matmul(a, b, mode, *, name, bias=None, acc_in=None, into=None, out_dtype=F32, tm=1024, tn=1024, tk=None):
    if mode == "nn":
        (M, K), (K2, N) = a.shape, b.shape
    elif mode == "nt":
        (M, K), (N, K2) = a.shape, b.shape
    else:
        (K, M), (K2, N) = a.shape, b.shape
    assert K == K2, (a.shape, b.shape, mode)
    tm = _pick(M, tm, 16)
    tn = _pick(N, tn, 128)
    tk = K if tk is None else _pick(K, tk, 128 if mode != "tn" else 16)
    gk = K // tk
    dims = {"nn": NN, "nt": NT, "tn": TN}[mode]
    a_spec = pl.BlockSpec((tk, tm), lambda j, i, k: (k, i)) if mode == "tn" else pl.BlockSpec((tm, tk), lambda j, i, k: (i, k))
    b_spec = pl.BlockSpec((tn, tk), lambda j, i, k: (j, k)) if mode == "nt" else pl.BlockSpec((tk, tn), lambda j, i, k: (k, j))
    in_specs, args = [a_spec, b_spec], [a, b]
    if bias is not None:
        in_specs.append(pl.BlockSpec((1, tn), lambda j, i, k: (0, j)))
        args.append(bias)
    aliases = {}
    if acc_in is not None:
        aliases = {len(args): 0}
        in_specs.append(pl.BlockSpec((tm, tn), lambda j, i, k: (i, j)))
        args.append(acc_in)
    n_total, col0, prev = (N, 0, None) if into is None else into
    assert col0 % tn == 0
    jb = col0 // tn
    if prev is not None:
        aliases = {len(args): 0}
        in_specs.append(pl.BlockSpec(memory_space=pl.ANY))
        args.append(prev)
    n_in = len(args)

    def body(*refs):
        a_ref, b_ref = refs[0], refs[1]
        pos = 2
        bias_ref = acc_in_ref = None
        if bias is not None:
            bias_ref = refs[pos]
            pos += 1
        if acc_in is not None:
            acc_in_ref = refs[pos]
        pos = n_in
        o_ref = refs[pos]
        part = _dot(a_ref[...].astype(MXU), b_ref[...].astype(MXU), dims)

        def finish(acc):
            if bias_ref is not None:
                acc = acc + bias_ref[...]
            if acc_in_ref is not None:
                acc = acc + acc_in_ref[...]
            o_ref[...] = acc.astype(out_dtype)

        if gk == 1:
            finish(part)
        else:
            acc_ref = refs[pos + 1]
            k = pl.program_id(2)

            @pl.when(k == 0)
            def _():
                acc_ref[...] = part

            @pl.when(k > 0)
            def _():
                acc_ref[...] += part

            @pl.when(k == gk - 1)
            def _():
                finish(acc_ref[...])

    return _call(
        body, name=name, grid=(N // tn, M // tm, gk), in_specs=in_specs,
        out_specs=pl.BlockSpec((tm, tn), lambda j, i, k: (i, j + jb)), out_shape=_sds((M, n_total), out_dtype),
        scratch=[pltpu.VMEM((tm, tn), F32)] if gk > 1 else [], aliases=aliases,
    )(*args)


def _rt(w, cb=0):
    return pl.BlockSpec((TM, w), lambda i: (i, cb))


def _ct(w):
    return pl.BlockSpec((w, TM), lambda i: (0, i))


def _vec(w):
    return pl.BlockSpec((1, w), lambda i: (0, 0))


def _part(w):
    return pl.BlockSpec((1, 1, w), lambda i: (i, 0, 0))


def _mod(ref, k):
    return ref[0, :, k * D:(k + 1) * D]


def _colsum(x):
    return jnp.sum(x, axis=0, keepdims=True)


def _ln_stats(s):
    mu = jnp.mean(s, axis=1, keepdims=True)
    cen = s - mu
    var = jnp.mean(cen * cen, axis=1, keepdims=True)
    rstd = lax.rsqrt(var + LN_EPS)
    return cen * rstd, rstd


def _modulate_cast(x, modt, k_shift):
    T = x.shape[0]
    nt = T // TM

    def body(x_ref, mod_ref, h_ref, ht_ref):
        h = (x_ref[...] * (1.0 + _mod(mod_ref, k_shift + 1)) + _mod(mod_ref, k_shift)).astype(MXU)
        h_ref[...] = h
        ht_ref[...] = h.T

    return _call(body, name="modulate", grid=(nt,), in_specs=[_rt(D), _part(N_MOD * D)], out_specs=[_rt(D), _ct(D)],
                 out_shape=[_sds((T, D), MXU), _sds((D, T), MXU)])(x, modt)


def _resid_ln(x, br, modt, k_gate, g, b, mod_next=None, k_shift_next=0):
    T = x.shape[0]
    nt = T // TM
    with_h = mod_next is not None

    def body(*refs):
        x_ref, br_ref, mod_ref, g_ref, b_ref = refs[:5]
        s = ALPHA * x_ref[...] + _mod(mod_ref, k_gate) * br_ref[...]
        xhat, _ = _ln_stats(s)
        y = xhat * g_ref[...] + b_ref[...]
        if with_h:
            modn_ref, y_ref, h_ref, ht_ref = refs[5:]
            y_ref[...] = y
            h = (y * (1.0 + _mod(modn_ref, k_shift_next + 1)) + _mod(modn_ref, k_shift_next)).astype(MXU)
            h_ref[...] = h
            ht_ref[...] = h.T
        else:
            refs[5][...] = y

    in_specs = [_rt(D), _rt(D), _part(N_MOD * D), _vec(D), _vec(D)]
    args = [x, br, modt, g, b]
    if with_h:
        in_specs.append(_part(N_MOD * D))
        args.append(mod_next)
        return _call(body, name="resid_ln_mod", grid=(nt,), in_specs=in_specs, out_specs=[_rt(D), _rt(D), _ct(D)],
                     out_shape=[_sds((T, D), F32), _sds((T, D), MXU), _sds((D, T), MXU)])(*args)
    return _call(body, name="resid_ln", grid=(nt,), in_specs=in_specs, out_specs=_rt(D), out_shape=_sds((T, D), F32))(*args)


def _ln_bwd(dy_part, x, br, modt, k_gate, g, dh=None, y=None, mod_next=None, k_shift_next=0):
    T = x.shape[0]
    nt = T // TM
    with_h = dh is not None

    def body(*refs):
        if with_h:
            dyp_ref, x_ref, br_ref, mod_ref, g_ref, dh_ref, y_ref, modn_ref = refs[:8]
            outs = refs[8:]
        else:
            dyp_ref, x_ref, br_ref, mod_ref, g_ref = refs[:5]
            outs = refs[5:]
        dx_ref, dbr_ref, dgate_ref, dlg_ref, dlb_ref, dbsum_ref = outs[:6]
        i = pl.program_id(0)

        @pl.when(i == 0)
        def _():
            dlg_ref[...] = jnp.zeros_like(dlg_ref)
            dlb_ref[...] = jnp.zeros_like(dlb_ref)
            dbsum_ref[...] = jnp.zeros_like(dbsum_ref)

        dy = dyp_ref[...]
        if with_h:
            dshift_ref, dscale_ref = outs[6:]
            dhv = dh_ref[...]
            dy = dy + dhv * (1.0 + _mod(modn_ref, k_shift_next + 1))
            dshift_ref[0] = _colsum(dhv)
            dscale_ref[0] = _colsum(dhv * y_ref[...])
        gate = _mod(mod_ref, k_gate)
        brv = br_ref[...]
        s = ALPHA * x_ref[...] + gate * brv
        xhat, rstd = _ln_stats(s)
        dlg_ref[...] += _colsum(dy * xhat)
        dlb_ref[...] += _colsum(dy)
        dyg = dy * g_ref[...]
        m1 = jnp.mean(dyg, axis=1, keepdims=True)
        m2 = jnp.mean(dyg * xhat, axis=1, keepdims=True)
        ds = rstd * (dyg - m1 - xhat * m2)
        dx_ref[...] = ALPHA * ds
        dbr = gate * ds
        dbr_ref[...] = dbr.astype(MXU)
        dbsum_ref[...] += _colsum(dbr)
        dgate_ref[0] = _colsum(ds * brv)

    in_specs = [_rt(D), _rt(D), _rt(D), _part(N_MOD * D), _vec(D)]
    args = [dy_part, x, br, modt, g]
    out_specs = [_rt(D), _rt(D), _part(D), _vec(D), _vec(D), _vec(D)]
    out_shape = [_sds((T, D), F32), _sds((T, D), MXU), _sds((nt, 1, D), F32), _sds((1, D), F32), _sds((1, D), F32), _sds((1, D), F32)]
    if with_h:
        in_specs += [_rt(D), _rt(D), _part(N_MOD * D)]
        args += [dh, y, mod_next]
        out_specs += [_part(D), _part(D)]
        out_shape += [_sds((nt, 1, D), F32), _sds((nt, 1, D), F32)]
    return _call(body, name="ln_bwd_mod" if with_h else "ln_bwd", grid=(nt,), in_specs=in_specs, out_specs=out_specs,
                 out_shape=out_shape)(*args)


def _mod_bwd(dx_part, dh, x, modt, k_shift):
    T = x.shape[0]
    nt = T // TM

    def body(dxp_ref, dh_ref, x_ref, mod_ref, dx_ref, dshift_ref, dscale_ref):
        dhv = dh_ref[...]
        dx_ref[...] = dxp_ref[...] + dhv * (1.0 + _mod(mod_ref, k_shift + 1))
        dshift_ref[0] = _colsum(dhv)
        dscale_ref[0] = _colsum(dhv * x_ref[...])

    return _call(body, name="mod_bwd", grid=(nt,), in_specs=[_rt(D), _rt(D), _rt(D), _part(N_MOD * D)],
                 out_specs=[_rt(D), _part(D), _part(D)],
                 out_shape=[_sds((T, D), F32), _sds((nt, 1, D), F32), _sds((nt, 1, D), F32)])(dx_part, dh, x, modt)


def _loss_grad(y, target, tpe, ncq):
    T = y.shape[0]
    nt = T // TM
    nl = tpe - ncq

    def body(y_ref, t_ref, dy_ref, loss_ref):
        i = pl.program_id(0)

        @pl.when(i == 0)
        def _():
            loss_ref[...] = jnp.zeros_like(loss_ref)

        @pl.when(i % tpe < ncq)
        def _():
            dy_ref[...] = jnp.zeros_like(dy_ref)

        @pl.when(i % tpe >= ncq)
        def _():
            err = y_ref[...] - t_ref[...]
            dy_ref[...] = err * (1.0 / D)
            loss_ref[...] += (0.5 / D) * jnp.sum(_colsum(err * err), axis=1, keepdims=True)

    tgt_spec = pl.BlockSpec((TM, D), lambda i: ((i // tpe) * nl + jnp.maximum(i % tpe - ncq, 0), 0))
    return _call(body, name="loss_grad", grid=(nt,), in_specs=[_rt(D), tgt_spec], out_specs=[_rt(D), _vec(128)],
                 out_shape=[_sds((T, D), F32), _sds((1, 128), F32)])(y, target)


def _rope_partner(x):
    lane = lax.broadcasted_iota(jnp.int32, x.shape, 1)
    first = (lane % (2 * ROPE_PAIRS)) < ROPE_PAIRS
    return jnp.where(first, pltpu.roll(x, HD - ROPE_PAIRS, 1), pltpu.roll(x, ROPE_PAIRS, 1))


def _qk_prep(z_qkv, cos_t, sin_t, q_gain, k_gain, tpe):
    T = z_qkv.shape[0]
    nt = T // TM

    def body(z_ref, cos_ref, sin_ref, qg_ref, kg_ref, o_ref):
        cos, sin = cos_ref[...], sin_ref[...]
        for h in range(NH + NKV):
            sl = slice(h * HD, (h + 1) * HD)
            t = z_ref[:, sl]
            gain = qg_ref[...] if h < NH else kg_ref[...]
            n = t * lax.rsqrt(jnp.mean(t * t, axis=1, keepdims=True) + RMS_EPS) * gain
            o_ref[:, sl] = (n * cos + _rope_partner(n) * sin).astype(MXU)
        o_ref[:, (NH + NKV) * HD:] = z_ref[:, (NH + NKV) * HD:].astype(MXU)

    tab = pl.BlockSpec((TM, HD), lambda i: (i % tpe, 0))
    return _call(body, name="qk_prep", grid=(nt,), in_specs=[_rt(QKV_W), tab, tab, _vec(HD), _vec(HD)], out_specs=_rt(QKV_W),
                 out_shape=_sds((T, QKV_W), MXU))(z_qkv, cos_t, sin_t, q_gain, k_gain)


def _qk_bwd(dq, dk, dv, z_qkv, cos_t, sin_t, q_gain, k_gain, tpe):
    T = z_qkv.shape[0]
    nt = T // TM

    def body(dq_ref, dk_ref, dv_ref, z_ref, cos_ref, sin_ref, qg_ref, kg_ref, dz_ref, dqg_ref, dkg_ref, bsum_ref):
        i = pl.program_id(0)

        @pl.when(i == 0)
        def _():
            dqg_ref[...] = jnp.zeros_like(dqg_ref)
            dkg_ref[...] = jnp.zeros_like(dkg_ref)
            bsum_ref[...] = jnp.zeros_like(bsum_ref)

        cos, sin = cos_ref[...], sin_ref[...]
        for h in range(NH + NKV):
            sl = slice(h * HD, (h + 1) * HD)
            dr = dq_ref[:, sl] if h < NH else dk_ref[:, (h - NH) * HD:(h - NH + 1) * HD]
            gain = qg_ref[...] if h < NH else kg_ref[...]
            dn = dr * cos + _rope_partner(dr * sin)
            t = z_ref[:, sl]
            rstd = lax.rsqrt(jnp.mean(t * t, axis=1, keepdims=True) + RMS_EPS)
            that = t * rstd
            dgain = _colsum(dn * that)
            if h < NH:
                dqg_ref[...] += dgain
            else:
                dkg_ref[...] += dgain
            dthat = dn * gain
            dt = rstd * (dthat - that * jnp.mean(dthat * that, axis=1, keepdims=True))
            dz_ref[:, sl] = dt.astype(MXU)
            bsum_ref[:, sl] += _colsum(dt)
        dvv = dv_ref[...]
        dz_ref[:, (NH + NKV) * HD:] = dvv.astype(MXU)
        bsum_ref[:, (NH + NKV) * HD:] += _colsum(dvv)

    tab = pl.BlockSpec((TM, HD), lambda i: (i % tpe, 0))
    return _call(body, name="qk_bwd", grid=(nt,),
                 in_specs=[_rt(NH * HD), _rt(KV_W), _rt(KV_W), _rt(QKV_W), tab, tab, _vec(HD), _vec(HD)],
                 out_specs=[_rt(QKV_W), _vec(HD), _vec(HD), _vec(QKV_W)],
                 out_shape=[_sds((T, QKV_W), MXU), _sds((1, HD), F32), _sds((1, HD), F32), _sds((1, QKV_W), F32)],
                 )(dq, dk, dv, z_qkv, cos_t, sin_t, q_gain, k_gain)


def _ln_silu(hc, g, b):
    T = hc.shape[0]

    def body(h_ref, g_ref, b_ref, o_ref, ot_ref):
        xhat, _ = _ln_stats(h_ref[...])
        n = xhat * g_ref[...] + b_ref[...]
        sw = (n * _sigmoid(n)).astype(MXU)
        o_ref[...] = sw
        ot_ref[...] = sw.T

    return _call(body, name="ln_silu", grid=(T // TM,), in_specs=[_rt(D), _vec(D), _vec(D)], out_specs=[_rt(D), _ct(D)],
                 out_shape=[_sds((T, D), MXU), _sds((D, T), MXU)])(hc, g, b)


def _ln_silu_bwd(dsw, hc, g, b):
    T = hc.shape[0]

    def body(d_ref, h_ref, g_ref, b_ref, dh_ref, dg_ref, db_ref, dcb_ref):
        i = pl.program_id(0)

        @pl.when(i == 0)
        def _():
            dg_ref[...] = jnp.zeros_like(dg_ref)
            db_ref[...] = jnp.zeros_like(db_ref)
            dcb_ref[...] = jnp.zeros_like(dcb_ref)

        xhat, rstd = _ln_stats(h_ref[...])
        n = xhat * g_ref[...] + b_ref[...]
        sg = _sigmoid(n)
        dn = d_ref[...] * (sg * (1.0 + n * (1.0 - sg)))
        dg_ref[...] += _colsum(dn * xhat)
        db_ref[...] += _colsum(dn)
        dng = dn * g_ref[...]
        m1 = jnp.mean(dng, axis=1, keepdims=True)
        m2 = jnp.mean(dng * xhat, axis=1, keepdims=True)
        dh = rstd * (dng - m1 - xhat * m2)
        dh_ref[...] = dh
        dcb_ref[...] += _colsum(dh)

    return _call(body, name="ln_silu_bwd", grid=(T // TM,), in_specs=[_rt(D), _rt(D), _vec(D), _vec(D)],
                 out_specs=[_rt(D), _vec(D), _vec(D), _vec(D)],
                 out_shape=[_sds((T, D), F32)] + [_sds((1, D), F32)] * 3)(dsw, hc, g, b)


def _merge(attn, conv_o, pool_o, z_gate):
    T = attn.shape[0]

    def body(a_ref, c_ref, p_ref, zg_ref, m_ref, mt_ref):
        m = (_sigmoid(zg_ref[:, 0:D]) * a_ref[...] + _sigmoid(zg_ref[:, D:2 * D]) * c_ref[...]
             + _sigmoid(zg_ref[:, 2 * D:3 * D]) * p_ref[...]).astype(MXU)
        m_ref[...] = m
        mt_ref[...] = m.T

    return _call(body, name="merge", grid=(T // TM,), in_specs=[_rt(D), _rt(D), _rt(D), _rt(N_GATE)], out_specs=[_rt(D), _ct(D)],
                 out_shape=[_sds((T, D), MXU), _sds((D, T), MXU)])(attn, conv_o, pool_o, z_gate)


def _merge_bwd(dm, attn, conv_o, pool_o, z_gate):
    T = attn.shape[0]

    def body(dm_ref, a_ref, c_ref, p_ref, zg_ref, da_ref, dc_ref, dp_ref, dzg_ref, dcsum_ref, gsum_ref):
        i = pl.program_id(0)

        @pl.when(i == 0)
        def _():
            dcsum_ref[...] = jnp.zeros_like(dcsum_ref)
            gsum_ref[...] = jnp.zeros_like(gsum_ref)

        dmv = dm_ref[...]
        for k, (br_ref, out_ref) in enumerate(((a_ref, da_ref), (c_ref, dc_ref), (p_ref, dp_ref))):
            gk = _sigmoid(zg_ref[:, k * D:(k + 1) * D])
            dbr = dmv * gk
            out_ref[...] = dbr.astype(out_ref.dtype)
            if k == 1:
                dcsum_ref[...] += _colsum(dbr)
            dzg = dmv * br_ref[...] * gk * (1.0 - gk)
            dzg_ref[:, k * D:(k + 1) * D] = dzg.astype(MXU)
            gsum_ref[:, k * D:(k + 1) * D] += _colsum(dzg)

    return _call(body, name="merge_bwd", grid=(T // TM,), in_specs=[_rt(D), _rt(D), _rt(D), _rt(D), _rt(N_GATE)],
                 out_specs=[_rt(D), _rt(D), _rt(D), _rt(N_GATE), _vec(D), _vec(N_GATE)],
                 out_shape=[_sds((T, D), F32), _sds((T, D), MXU), _sds((T, D), F32), _sds((T, N_GATE), MXU),
                            _sds((1, D), F32), _sds((1, N_GATE), F32)])(dm, attn, conv_o, pool_o, z_gate)


def _softmax_parts(q, k):
    s = _dot(q, k, NT)
    p = jnp.exp2((s - jnp.max(s, axis=1, keepdims=True)) * (ATTN_SCALE * LOG2_E))
    return p, 1.0 / jnp.sum(p, axis=1, keepdims=True)


def _attn_specs(nq):
    q_spec = pl.BlockSpec((TM, QG * HD), lambda b, h, q: (b * nq + q, h))
    k_spec = pl.BlockSpec((nq * TM, HD), lambda b, h, q: (b, NH + h))
    v_spec = pl.BlockSpec((nq * TM, HD), lambda b, h, q: (b, NH + NKV + h))
    return q_spec, k_spec, v_spec


class _Carried:
    def __init__(self, ex, n_in, n_out, n_scratch):
        self.ex, self.n_in, self.n_out, self.n_scratch = ex, n_in, n_out, n_scratch
        self.ci = len(ex.inputs) if ex else 0
        self.co = len(ex.out_shape) if ex else 0

    def in_specs(self):
        return [ANY] * self.ci

    def out_specs(self):
        return [ANY] * self.co

    def split(self, refs):
        a = self.n_in
        b = a + self.ci
        c = b + self.n_out
        d = c + self.co
        e = d + self.n_scratch
        return (refs[:a], refs[b:c], refs[d:e]), (refs[a:b], refs[c:d], refs[e:])

    def before(self, step, parts):
        if self.ex:
            pl.when(step == 0)(lambda: self.ex.start(*parts))

    def after(self, step, n_steps, parts):
        if self.ex:
            pl.when(step == (3 * n_steps) // 4)(lambda: self.ex.mid(*parts))
            pl.when(step == n_steps - 1)(lambda: self.ex.finish(*parts))


def _attn_fwd(qkv, B, C, R, carry=None):
    nq, ncq = R // TM, C // TM
    car = _Carried(carry, 3, 1, 0)

    def body(*refs):
        (q_ref, k_ref, v_ref), (o_ref,), _ = car.split(refs)[0]
        parts = car.split(refs)[1]
        qi = pl.program_id(2)
        step = (pl.program_id(0) * NKV + pl.program_id(1)) * nq + qi
        car.before(step, parts)

        def attend(L):
            k, v = k_ref[0:L, :], v_ref[0:L, :]
            for i in range(QG):
                sl = slice(i * HD, (i + 1) * HD)
                p, inv_l = _softmax_parts(q_ref[:, sl], k)
                o_ref[:, sl] = _dot(p.astype(MXU), v, NN) * inv_l

        pl.when(qi < ncq)(functools.partial(attend, C))
        pl.when(qi >= ncq)(functools.partial(attend, R))
        car.after(step, B * NKV * nq, parts)

    q_spec, k_spec, v_spec = _attn_specs(nq)
    res = _call(body, name="attn_fwd", grid=(B, NKV, nq), in_specs=[q_spec, k_spec, v_spec] + car.in_specs(),
                out_specs=[q_spec] + car.out_specs(), out_shape=[_sds((B * R, NH * HD), F32)] + list(carry.out_shape if carry else []),
                scratch=list(carry.scratch) if carry else [])(qkv, qkv, qkv, *(carry.inputs if carry else []))
    return res[0], res[1:]


def _attn_bwd(qkv, o, do, B, C, R, carry=None):
    nq, ncq = R // TM, C // TM
    car = _Carried(carry, 5, 3, 2)

    def body(*refs):
        (q_ref, k_ref, v_ref, o_ref, do_ref), (dq_ref, dk_ref, dv_ref), (dkt, dvt) = car.split(refs)[0]
        parts = car.split(refs)[1]
        qi = pl.program_id(2)
        step = (pl.program_id(0) * NKV + pl.program_id(1)) * nq + qi
        car.before(step, parts)

        @pl.when(qi == 0)
        def _():
            dkt[...] = jnp.zeros_like(dkt)
            dvt[...] = jnp.zeros_like(dvt)

        def bwd(L):
            k, v = k_ref[0:L, :], v_ref[0:L, :]
            for i in range(QG):
                sl = slice(i * HD, (i + 1) * HD)
                q = q_ref[:, sl]
                p, inv_l = _softmax_parts(q, k)
                dov = do_ref[:, sl]
                dp = _dot(dov.astype(MXU), v, NT)
                dl = jnp.sum(dov * o_ref[:, sl], axis=1, keepdims=True)
                ds = (p * ((dp - dl) * (inv_l * ATTN_SCALE))).astype(MXU)
                dq_ref[:, sl] = _dot(ds, k, NN)
                dkt[:, 0:L] += _dot(q, ds, TN)
                dvt[:, 0:L] += _dot((dov * inv_l).astype(MXU), p.astype(MXU), TN)

        pl.when(qi < ncq)(functools.partial(bwd, C))
        pl.when(qi >= ncq)(functools.partial(bwd, R))

        @pl.when(qi == nq - 1)
        def _():
            dk_ref[...] = dkt[...].T
            dv_ref[...] = dvt[...].T

        car.after(step, B * NKV * nq, parts)

    q_spec, k_spec, v_spec = _attn_specs(nq)
    kv_out = pl.BlockSpec((R, HD), lambda b, h, q: (b, h))
    res = _call(body, name="attn_bwd", grid=(B, NKV, nq), in_specs=[q_spec, k_spec, v_spec, q_spec, q_spec] + car.in_specs(),
                out_specs=[q_spec, kv_out, kv_out] + car.out_specs(),
                out_shape=[_sds((B * R, NH * HD), F32), _sds((B * R, KV_W), F32), _sds((B * R, KV_W), F32)]
                + list(carry.out_shape if carry else []),
                scratch=[pltpu.VMEM((HD, R), F32), pltpu.VMEM((HD, R), F32)] + list(carry.scratch if carry else []),
                )(qkv, qkv, qkv, o, do, *(carry.inputs if carry else []))
    return res[0], res[1], res[2], res[3:]


def _segments(C, S):
    return ((0, GAP, C), (C, 2 * GAP + C, S))


def _padded_rows(C, S):
    return 3 * GAP + C + S


def _zero_gaps(pad_ref, C, S):
    for off in (0, GAP + C, 2 * GAP + C + S):
        pad_ref[off:off + GAP, :] = jnp.zeros((GAP, pad_ref.shape[1]), pad_ref.dtype)


def _chunks(n, ch, fn):
    def step(i, carry):
        fn(pl.multiple_of(i * ch, ch))
        return carry

    lax.fori_loop(0, n // ch, step, 0)


class _Window:
    def __init__(self, pad_ref, row, ch):
        self.pad_ref, (self.r, self.po), self.ch = pad_ref, row, ch

    @property
    def win(self):
        return self.pad_ref[pl.ds(self.r + (self.po - GAP), self.ch + 2 * GAP), :]

    def at(self, off):
        return self.pad_ref[pl.ds(self.r + (self.po + off), self.ch), :]


def _taps(pad_ref, w, row, ch, n_taps, flip=False, sliced=False):
    half = (n_taps - 1) // 2
    win = _Window(pad_ref, row, ch)
    whole = win.win if sliced else None
    acc = None
    for k in range(n_taps):
        off = (half - k) if flip else (k - half)
        view = whole[GAP + off:GAP + off + ch, :] if sliced else win.at(off)
        term = w[k:k + 1, :] * view
        acc = term if acc is None else acc + term
    return acc


def _tap_grads(dw_ref, d, pad_ref, row, ch, n_taps):
    half = (n_taps - 1) // 2
    win = _Window(pad_ref, row, ch)
    for k in range(n_taps):
        prod = d * win.at(k - half)
        dw_ref[k] += jnp.sum(prod.reshape(ch // 8, 8, prod.shape[1]), axis=0)


def _conv_fwd(z_conv, w, bias, B, C, S, cw=128, ch=128):
    R = C + S
    nj = CONV_CH // cw
    segs = _segments(C, S)

    def body(a_ref, g_ref, w_ref, b_ref, o_ref, pad):
        _zero_gaps(pad, C, S)
        wv = w_ref[...]
        for so, po, n in segs:
            def fill(r, so=so, po=po):
                pad[pl.ds(po + r, ch), :] = a_ref[pl.ds(so + r, ch), :] * _sigmoid(g_ref[pl.ds(so + r, ch), :])

            _chunks(n, ch, fill)
        for so, po, n in segs:
            def conv(r, so=so, po=po):
                o_ref[pl.ds(so + r, ch), :] = _taps(pad, wv, (r, po), ch, CONV_K) + b_ref[...]

            _chunks(n, ch, conv)

    return _call(
        body, name="conv_fwd", grid=(nj, B),
        in_specs=[pl.BlockSpec((R, cw), lambda j, b: (b, j)), pl.BlockSpec((R, cw), lambda j, b: (b, nj + j)),
                  pl.BlockSpec((32, cw), lambda j, b: (0, j)), pl.BlockSpec((1, cw), lambda j, b: (0, j))],
        out_specs=pl.BlockSpec((R, cw), lambda j, b: (b, j)), out_shape=_sds((B * R, CONV_CH), F32),
        scratch=[pltpu.VMEM((_padded_rows(C, S), cw), F32)])(z_conv, z_conv, w, bias)


def _conv_bwd(dhc, z_conv, w, B, C, S, cw=128, ch=128):
    R = C + S
    nj = CONV_CH // cw
    segs = _segments(C, S)

    def body(d_ref, a_ref, g_ref, w_ref, da_ref, dg_ref, dw_ref, sa_ref, sg_ref, gpad, dpad, dwacc):
        b = pl.program_id(1)

        @pl.when(b == 0)
        def _():
            dw_ref[...] = jnp.zeros_like(dw_ref)
            sa_ref[...] = jnp.zeros_like(sa_ref)
            sg_ref[...] = jnp.zeros_like(sg_ref)

        _zero_gaps(gpad, C, S)
        _zero_gaps(dpad, C, S)
        dwacc[...] = jnp.zeros_like(dwacc)
        wv = w_ref[...]
        for so, po, n in segs:
            def fill(r, so=so, po=po):
                gpad[pl.ds(po + r, ch), :] = a_ref[pl.ds(so + r, ch), :] * _sigmoid(g_ref[pl.ds(so + r, ch), :])
                dpad[pl.ds(po + r, ch), :] = d_ref[pl.ds(so + r, ch), :]

            _chunks(n, ch, fill)
        for so, po, n in segs:
            def step(r, so=so, po=po):
                _tap_grads(dwacc, dpad[pl.ds(po + r, ch), :], gpad, (r, po), ch, CONV_K)
                dglu = _taps(dpad, wv, (r, po), ch, CONV_K, flip=True)
                av = a_ref[pl.ds(so + r, ch), :]
                sg = _sigmoid(g_ref[pl.ds(so + r, ch), :])
                da = dglu * sg
                dg = dglu * av * sg * (1.0 - sg)
                da_ref[pl.ds(so + r, ch), :] = da.astype(MXU)
                dg_ref[pl.ds(so + r, ch), :] = dg.astype(MXU)
                sa_ref[...] += _colsum(da)
                sg_ref[...] += _colsum(dg)

            _chunks(n, ch, step)
        for k in range(CONV_K):
            dw_ref[k:k + 1, :] += _colsum(dwacc[k])

    blk = pl.BlockSpec((R, cw), lambda j, b: (b, j))
    acc1 = pl.BlockSpec((1, cw), lambda j, b: (0, j))
    return _call(
        body, name="conv_bwd", grid=(nj, B),
        in_specs=[blk, blk, pl.BlockSpec((R, cw), lambda j, b: (b, nj + j)), pl.BlockSpec((32, cw), lambda j, b: (0, j))],
        out_specs=[blk, blk, pl.BlockSpec((32, cw), lambda j, b: (0, j)), acc1, acc1],
        out_shape=[_sds((B * R, CONV_CH), MXU), _sds((B * R, CONV_CH), MXU), _sds((32, CONV_CH), F32),
                   _sds((1, CONV_CH), F32), _sds((1, CONV_CH), F32)],
        scratch=[pltpu.VMEM((_padded_rows(C, S), cw), F32), pltpu.VMEM((_padded_rows(C, S), cw), F32),
                 pltpu.VMEM((32, 8, cw), F32)])(dhc, z_conv, z_conv, w)


def _ffn_mid(up, w, bias, B, C, S, cw=256, ch=64, carry=None):
    R = C + S
    nj = D_FF // cw
    segs = _segments(C, S)
    car = _Carried(carry, 4, 2, 1)

    def body(*refs):
        (a_ref, u_ref, w_ref, b_ref), (f_ref, ft_ref), (pad,) = car.split(refs)[0]
        parts = car.split(refs)[1]
        step = pl.program_id(0) * B + pl.program_id(1)
        car.before(step, parts)
        _zero_gaps(pad, C, S)
        wv = w_ref[...]
        for so, po, n in segs:
            def fill(r, so=so, po=po):
                pad[pl.ds(po + r, ch), :] = a_ref[pl.ds(so + r, ch), :]

            _chunks(n, ch, fill)
        for so, po, n in segs:
            def conv(r, so=so, po=po):
                ac = _taps(pad, wv, (r, po), ch, FFN_K, sliced=True) + b_ref[...]
                f_ref[pl.ds(so + r, ch), :] = (ac * _sigmoid(ac) * u_ref[pl.ds(so + r, ch), :]).astype(MXU)

            _chunks(n, ch, conv)
        ft_ref[...] = f_ref[...].T
        car.after(step, nj * B, parts)

    res = _call(
        body, name="ffn_mid", grid=(nj, B),
        in_specs=[pl.BlockSpec((R, cw), lambda j, b: (b, j)), pl.BlockSpec((R, cw), lambda j, b: (b, nj + j)),
                  pl.BlockSpec((8, cw), lambda j, b: (0, j)), pl.BlockSpec((1, cw), lambda j, b: (0, j))] + car.in_specs(),
        out_specs=[pl.BlockSpec((R, cw), lambda j, b: (b, j)), pl.BlockSpec((cw, R), lambda j, b: (j, b))] + car.out_specs(),
        out_shape=[_sds((B * R, D_FF), MXU), _sds((D_FF, B * R), MXU)] + list(carry.out_shape if carry else []),
        scratch=[pltpu.VMEM((_padded_rows(C, S), cw), F32)] + list(carry.scratch if carry else []),
    )(up, up, w, bias, *(carry.inputs if carry else []))
    return res[0], res[1], res[2:]


def _ffn_mid_bwd(df, up, w, bias, B, C, S, cw=128, ch=128, carry=None):
    R = C + S
    nj = D_FF // cw
    segs = _segments(C, S)
    car = _Carried(carry, 5, 4, 3)

    def body(*refs):
        (d_ref, a_ref, u_ref, w_ref, b_ref), (da_ref, du_ref, dw_ref, db_ref), (apad, dpad, dwacc) = car.split(refs)[0]
        parts = car.split(refs)[1]
        b = pl.program_id(1)
        step = pl.program_id(0) * B + b
        car.before(step, parts)

        @pl.when(b == 0)
        def _():
            dw_ref[...] = jnp.zeros_like(dw_ref)
            db_ref[...] = jnp.zeros_like(db_ref)

        _zero_gaps(apad, C, S)
        _zero_gaps(dpad, C, S)
        dwacc[...] = jnp.zeros_like(dwacc)
        wv = w_ref[...]
        for so, po, n in segs:
            def fill(r, so=so, po=po):
                apad[pl.ds(po + r, ch), :] = a_ref[pl.ds(so + r, ch), :]

            _chunks(n, ch, fill)
        for so, po, n in segs:
            def first(r, so=so, po=po):
                ac = _taps(apad, wv, (r, po), ch, FFN_K) + b_ref[...]
                sg = _sigmoid(ac)
                dfv = d_ref[pl.ds(so + r, ch), :]
                du_ref[pl.ds(so + r, ch), :] = (dfv * ac * sg).astype(MXU)
                dac = dfv * u_ref[pl.ds(so + r, ch), :] * (sg * (1.0 + ac * (1.0 - sg)))
                dpad[pl.ds(po + r, ch), :] = dac
                db_ref[...] += _colsum(dac)

            _chunks(n, ch, first)
        for so, po, n in segs:
            def second(r, so=so, po=po):
                _tap_grads(dwacc, dpad[pl.ds(po + r, ch), :], apad, (r, po), ch, FFN_K)
                da_ref[pl.ds(so + r, ch), :] = _taps(dpad, wv, (r, po), ch, FFN_K, flip=True).astype(MXU)

            _chunks(n, ch, second)
        for k in range(FFN_K):
            dw_ref[k:k + 1, :] += _colsum(dwacc[k])
        car.after(step, nj * B, parts)

    blk = pl.BlockSpec((R, cw), lambda j, b: (b, j))
    res = _call(
        body, name="ffn_mid_bwd", grid=(nj, B),
        in_specs=[blk, blk, pl.BlockSpec((R, cw), lambda j, b: (b, nj + j)), pl.BlockSpec((8, cw), lambda j, b: (0, j)),
                  pl.BlockSpec((1, cw), lambda j, b: (0, j))] + car.in_specs(),
        out_specs=[blk, blk, pl.BlockSpec((8, cw), lambda j, b: (0, j)), pl.BlockSpec((1, cw), lambda j, b: (0, j))]
        + car.out_specs(),
        out_shape=[_sds((B * R, D_FF), MXU), _sds((B * R, D_FF), MXU), _sds((8, D_FF), F32), _sds((1, D_FF), F32)]
        + list(carry.out_shape if carry else []),
        scratch=[pltpu.VMEM((_padded_rows(C, S), cw), F32), pltpu.VMEM((_padded_rows(C, S), cw), F32),
                 pltpu.VMEM((8, 8, cw), F32)] + list(carry.scratch if carry else []),
    )(df, up, up, w, bias, *(carry.inputs if carry else []))
    return res[0], res[1], res[2], res[3], res[4:]


def _trailing_sums(x, w):
    s, k = x, 1
    while k < w:
        s = s + pltpu.roll(s, k, 0)
        k *= 2
    return s


def _window_count(r, ch, n, w):
    t = r + lax.broadcasted_iota(jnp.int32, (ch, 1), 0)
    return (jnp.minimum(t + w // 2, n) - jnp.maximum(t - w // 2, 0)).astype(F32)


def _pool_fwd(z_pool, pool_w, pool_scale, B, C, S, ch=128):
    R = C + S
    gch = POOL_GCH
    segs = _segments(C, S)

    def body(u_ref, pw_ref, sc_ref, pooled_ref, po_ref, pad):
        g = pl.program_id(1)
        _zero_gaps(pad, C, S)
        for so, po, n in segs:
            def fill(r, so=so, po=po):
                pad[pl.ds(po + r, ch), :] = u_ref[pl.ds(so + r, ch), :]

            _chunks(n, ch, fill)
        for gi, w in enumerate(POOL_WINDOWS):
            @pl.when(g == gi)
            def _(w=w):
                for so, po, n in segs:
                    def step(r, so=so, po=po, n=n):
                        win = _Window(pad, (r, po), ch)
                        acc = _trailing_sums(win.win, w)[GAP + w // 2 - 1:GAP + w // 2 - 1 + ch, :]
                        pooled = (acc / _window_count(r, ch, n, w) - win.at(0)).astype(MXU)
                        pooled_ref[pl.ds(so + r, ch), :] = pooled
                        po_ref[pl.ds(so + r, ch), :] = _dot(pooled, pw_ref[0], NN) * sc_ref[...]

                    _chunks(n, ch, step)

    blk = pl.BlockSpec((R, gch), lambda b, g: (b, g))
    return _call(
        body, name="pool_fwd", grid=(B, len(POOL_WINDOWS)),
        in_specs=[blk, pl.BlockSpec((1, gch, gch), lambda b, g: (g, 0, 0)), pl.BlockSpec((1, gch), lambda b, g: (0, g))],
        out_specs=[blk, blk], out_shape=[_sds((B * R, POOL_CH), MXU), _sds((B * R, POOL_CH), F32)],
        scratch=[pltpu.VMEM((_padded_rows(C, S), gch), F32)])(z_pool, pool_w, pool_scale)


def _pool_bwd(dpo, pooled, pool_w, pool_scale, B, C, S, ch=128):
    R = C + S
    gch = POOL_GCH
    segs = _segments(C, S)

    def body(d_ref, p_ref, pw_ref, sc_ref, du_ref, dpw_ref, dsc_ref, su_ref, qpad, dpl):
        g, b = pl.program_id(0), pl.program_id(1)

        @pl.when(b == 0)
        def _():
            dpw_ref[...] = jnp.zeros_like(dpw_ref)
            dsc_ref[...] = jnp.zeros_like(dsc_ref)
            su_ref[...] = jnp.zeros_like(su_ref)

        _zero_gaps(qpad, C, S)
        pw = pw_ref[0]
        for gi, w in enumerate(POOL_WINDOWS):
            @pl.when(g == gi)
            def _(w=w):
                for so, po, n in segs:
                    def first(r, so=so, po=po, n=n):
                        pv = p_ref[pl.ds(so + r, ch), :]
                        dv = d_ref[pl.ds(so + r, ch), :]
                        dsc_ref[...] += _colsum(dv * _dot(pv, pw, NN))
                        dmx = (dv * sc_ref[...]).astype(MXU)
                        dpw_ref[0] += _dot(pv, dmx, TN)
                        dp = _dot(dmx, pw, NT)
                        dpl[pl.ds(so + r, ch), :] = dp
                        qpad[pl.ds(po + r, ch), :] = dp / _window_count(r, ch, n, w)

                    _chunks(n, ch, first)
                for so, po, n in segs:
                    def second(r, so=so, po=po):
                        win = _Window(qpad, (r, po), ch)
                        acc = _trailing_sums(win.win, w)[GAP + w // 2:GAP + w // 2 + ch, :]
                        du = acc - dpl[pl.ds(so + r, ch), :]
                        du_ref[pl.ds(so + r, ch), :] = du.astype(MXU)
                        su_ref[...] += _colsum(du)

                    _chunks(n, ch, second)

    blk = pl.BlockSpec((R, gch), lambda g, b: (b, g))
    vec = pl.BlockSpec((1, gch), lambda g, b: (0, g))
    wblk = pl.BlockSpec((1, gch, gch), lambda g, b: (g, 0, 0))
    return _call(
        body, name="pool_bwd", grid=(len(POOL_WINDOWS), B), in_specs=[blk, blk, wblk, vec], out_specs=[blk, wblk, vec, vec],
        out_shape=[_sds((B * R, POOL_CH), MXU), _sds((len(POOL_WINDOWS), gch, gch), F32), _sds((1, POOL_CH), F32),
                   _sds((1, POOL_CH), F32)],
        scratch=[pltpu.VMEM((_padded_rows(C, S), gch), F32), pltpu.VMEM((R, gch), F32)])(dpo, pooled, pool_w, pool_scale)


def _silu_rows(cond):
    def body(c_ref, s_ref, d_ref):
        c = c_ref[...]
        sg = _sigmoid(c)
        s_ref[...] = (c * sg).astype(MXU)
        d_ref[...] = sg * (1.0 + c * (1.0 - sg))

    full = pl.BlockSpec(cond.shape, lambda i: (0, 0))
    return _call(body, name="silu_rows", grid=(1,), in_specs=[full], out_specs=[full, full],
                 out_shape=[_sds(cond.shape, MXU), _sds(cond.shape, F32)])(cond)


def _row_tile(rows, cols, n_bufs):
    cap = max(16, (16 * 2 ** 20) // (4 * n_bufs * max(cols, 128)))
    return rows if rows <= cap else _pick(rows, cap, 16)


def _adamw(parts, w, m, v, layer, prev, *, name):
    n_parts, rows, cols = parts.shape
    layers = w.shape[0]
    c1 = 1.0 - ADAM_B1 ** ADAM_STEP
    c2 = 1.0 - ADAM_B2 ** ADAM_STEP
    tr = _row_tile(rows, cols, n_parts + 7)

    def body(p_ref, w_ref, m_ref, v_ref, *rest):
        g_ref, d_ref, nm_ref, nv_ref = rest[-4:]
        g = p_ref[0].astype(F32)
        for k in range(1, n_parts):
            g = g + p_ref[k].astype(F32)
        nm = ADAM_B1 * m_ref[...] + (1.0 - ADAM_B1) * g
        nv = ADAM_B2 * v_ref[...] + (1.0 - ADAM_B2) * (g * g)
        g_ref[...] = g
        nm_ref[...] = nm
        nv_ref[...] = nv
        d_ref[...] = -ADAM_LR * ((nm / c1) / (jnp.sqrt(nv / c2) + ADAM_EPS) + ADAM_WD * w_ref[...])

    blk = pl.BlockSpec((None, tr, cols), lambda i: (layer, i, 0))
    in_specs = [pl.BlockSpec((n_parts, tr, cols), lambda i: (0, i, 0)), blk, blk, blk]
    args = [parts, w, m, v]
    aliases = {}
    if prev is not None:
        in_specs += [ANY] * 4
        aliases = {4 + k: k for k in range(4)}
        args += list(prev)
    return _call(body, name=name, grid=(rows // tr,), in_specs=in_specs, out_specs=[blk] * 4,
                 out_shape=[_sds((layers, rows, cols), F32)] * 4, aliases=aliases)(*args)


def _pair_sum(g, r1):
    _, rows, cols = g.shape
    c = lax.axis_index("c")
    g4 = g.reshape(4, 2, rows, cols)
    tr = _row_tile(rows, cols, 10)

    def body(c_ref, g_ref, r_ref, o_ref):
        o_ref[...] = (g_ref[...] + r_ref[...]).astype(WIRE)

    return _pcall(
        body, name="pair_sum", out_shape=_sds((4, rows, cols), WIRE),
        grid_spec=pltpu.PrefetchScalarGridSpec(
            num_scalar_prefetch=1, grid=(rows // tr,),
            in_specs=[pl.BlockSpec((4, None, tr, cols), lambda i, c_ref: (0, c_ref[0], i, 0)),
                      pl.BlockSpec((4, tr, cols), lambda i, c_ref: (0, i, 0))],
            out_specs=pl.BlockSpec((4, tr, cols), lambda i, c_ref: (0, i, 0))),
        compiler_params=pltpu.CompilerParams(dimension_semantics=("arbitrary",), vmem_limit_bytes=VMEM_MB * 2 ** 20),
    )(jnp.reshape(c, (1,)).astype(jnp.int32), g4, r1)


MESH = pl.DeviceIdType.MESH
ANY = pl.BlockSpec(memory_space=pl.ANY)


class _Exchange:
    inputs, out_shape, scratch = (), (), ()

    def start(self, ins, outs, sems):
        raise NotImplementedError

    def mid(self, ins, outs, sems):
        pass

    def finish(self, ins, outs, sems):
        raise NotImplementedError


def _run_exchange(ex, *, name):
    n_in, n_out = len(ex.inputs), len(ex.out_shape)

    def body(*refs):
        parts = refs[:n_in], refs[n_in:n_in + n_out], refs[n_in + n_out:]
        ex.start(*parts)
        ex.mid(*parts)
        ex.finish(*parts)

    return _pcall(body, name=name, out_shape=list(ex.out_shape), in_specs=[ANY] * n_in, out_specs=[ANY] * n_out,
                  scratch_shapes=list(ex.scratch))(*ex.inputs)


class _Gather(_Exchange):
    def __init__(self, shards, layer=None):
        self.inputs, self.layer, self.n_t = list(shards), layer, len(shards)
        self.out_shape = [_sds((N_DEV,) + tuple(s.shape if layer is None else s.shape[1:]), s.dtype) for s in shards]
        self.scratch = [pltpu.SemaphoreType.DMA((7 * self.n_t,)), pltpu.SemaphoreType.DMA((7 * self.n_t,)),
                        pltpu.SemaphoreType.DMA((self.n_t,))]

    def _place(self):
        x, y, c = lax.axis_index("x"), lax.axis_index("y"), lax.axis_index("c")
        return (x, y, c), (x, y, 1 - c), [(1 - x, y), (x, 1 - y), (1 - x, 1 - y)]

    def _own(self, ins, n):
        return ins[n] if self.layer is None else ins[n].at[self.layer]

    def _copy(self, ins, outs, sems, n, k, blk, to, own=False):
        dst = outs[n].at[4 * blk[0] + 2 * blk[1] + blk[2]]
        return pltpu.make_async_remote_copy(src_ref=self._own(ins, n) if own else dst, dst_ref=dst, send_sem=sems[0].at[n * 7 + k],
                                            recv_sem=sems[1].at[n * 7 + k], device_id=to, device_id_type=MESH)

    def _mine(self, ins, outs, sems, n):
        (x, y, c), _, _ = self._place()
        return pltpu.make_async_copy(self._own(ins, n), outs[n].at[4 * x + 2 * y + c], sems[2].at[n])

    def start(self, ins, outs, sems):
        me, sibling, chips = self._place()
        for n in range(self.n_t):
            self._mine(ins, outs, sems, n).start()
        for j, chip in enumerate(chips):
            for n in range(self.n_t):
                self._copy(ins, outs, sems, n, 1 + j, me, (*chip, me[2]), own=True).start()
        for n in range(self.n_t):
            self._copy(ins, outs, sems, n, 0, me, sibling, own=True).start()

    def mid(self, ins, outs, sems):
        me, sibling, chips = self._place()
        for j, chip in enumerate(chips):
            for n in range(self.n_t):
                self._copy(ins, outs, sems, n, 1 + j, (*chip, me[2]), me).wait_recv()
                self._copy(ins, outs, sems, n, 4 + j, (*chip, me[2]), sibling).start()

    def finish(self, ins, outs, sems):
        me, sibling, chips = self._place()
        for n in range(self.n_t):
            self._copy(ins, outs, sems, n, 0, sibling, me).wait_recv()
        for j, chip in enumerate(chips):
            for n in range(self.n_t):
                self._copy(ins, outs, sems, n, 4 + j, (*chip, 1 - me[2]), me).wait_recv()
        for j, chip in enumerate(chips):
            for n in range(self.n_t):
                self._copy(ins, outs, sems, n, 1 + j, me, (*chip, me[2]), own=True).wait_send()
                self._copy(ins, outs, sems, n, 4 + j, (*chip, me[2]), sibling).wait_send()
        for n in range(self.n_t):
            self._copy(ins, outs, sems, n, 0, me, sibling, own=True).wait_send()
            self._mine(ins, outs, sems, n).wait()


def _all_gather(shards, *, name, layer=None):
    return _run_exchange(_Gather(shards, layer), name=name)


class _SiblingExchange(_Exchange):
    def __init__(self, gs):
        self.n_t = len(gs)
        self.inputs = [g.reshape((4, 2) + g.shape[1:]) for g in gs]
        self.out_shape = [_sds((4,) + g.shape[1:], g.dtype) for g in gs]
        self.scratch = [pltpu.SemaphoreType.DMA((self.n_t,)), pltpu.SemaphoreType.DMA((self.n_t,))]

    def _copy(self, ins, outs, sems, n):
        x, y, c = lax.axis_index("x"), lax.axis_index("y"), lax.axis_index("c")
        return pltpu.make_async_remote_copy(src_ref=ins[n].at[:, 1 - c], dst_ref=outs[n], send_sem=sems[0].at[n],
                                            recv_sem=sems[1].at[n], device_id=(x, y, 1 - c), device_id_type=MESH)

    def start(self, ins, outs, sems):
        for n in range(self.n_t):
            self._copy(ins, outs, sems, n).start()

    def finish(self, ins, outs, sems):
        for n in range(self.n_t):
            self._copy(ins, outs, sems, n).wait_recv()
        for n in range(self.n_t):
            self._copy(ins, outs, sems, n).wait_send()


class _ChipExchange(_Exchange):
    def __init__(self, ps):
        self.n_t, self.inputs = len(ps), list(ps)
        self.out_shape = [_sds(p.shape, p.dtype) for p in ps]
        self.scratch = [pltpu.SemaphoreType.DMA((3 * self.n_t,)), pltpu.SemaphoreType.DMA((3 * self.n_t,)),
                        pltpu.SemaphoreType.DMA((self.n_t,))]

    def _place(self):
        x, y, c = lax.axis_index("x"), lax.axis_index("y"), lax.axis_index("c")
        return 2 * x + y, c, [(1 - x, y), (x, 1 - y), (1 - x, 1 - y)]

    def _copy(self, ins, outs, sems, n, j, src_chip, dst_slot):
        _, c, chips = self._place()
        return pltpu.make_async_remote_copy(
            src_ref=ins[n].at[src_chip], dst_ref=outs[n].at[dst_slot], send_sem=sems[0].at[n * 3 + j],
            recv_sem=sems[1].at[n * 3 + j], device_id=(*chips[j], c), device_id_type=MESH)

    def _own(self, ins, outs, sems, n):
        mine, _, _ = self._place()
        return pltpu.make_async_copy(ins[n].at[mine], outs[n].at[mine], sems[2].at[n])

    def start(self, ins, outs, sems):
        mine, _, chips = self._place()
        for n in range(self.n_t):
            self._own(ins, outs, sems, n).start()
        for j, (px, py) in enumerate(chips):
            for n in range(self.n_t):
                self._copy(ins, outs, sems, n, j, 2 * px + py, mine).start()

    def finish(self, ins, outs, sems):
        mine, _, chips = self._place()
        for j, (px, py) in enumerate(chips):
            for n in range(self.n_t):
                self._copy(ins, outs, sems, n, j, mine, 2 * px + py).wait_recv()
        for j, (px, py) in enumerate(chips):
            for n in range(self.n_t):
                self._copy(ins, outs, sems, n, j, 2 * px + py, mine).wait_send()
        for n in range(self.n_t):
            self._own(ins, outs, sems, n).wait()


BIG = (("w_ada", (D, N_MOD * D // N_DEV), 1), ("w_in", (D, D_IN // N_DEV), 1), ("conv_pw_w", (CONV_CH // N_DEV, D), 0),
       ("pool_w", (len(POOL_WINDOWS), POOL_GCH // N_DEV, POOL_GCH), 1), ("w_out", (D // N_DEV, D), 0),
       ("w_up", (D, 2 * D_FF // N_DEV), 1), ("w_down", (D_FF // N_DEV, D), 0))
TAPS = (("conv_dw_w", (CONV_K, CONV_CH // N_DEV), 1), ("ffn_dw_w", (FFN_K, D_FF // N_DEV), 1))
SHARDED = BIG + TAPS
EARLY = (0, 1, 5)
LATE = (2, 3, 4, 6)
REPLICATED =(("b_ada", N_MOD * D), ("b_in", D_IN), ("q_gain", HD), ("k_gain", HD), ("conv_dw_b", CONV_CH), ("conv_ln_g", CONV_CH),
              ("conv_ln_b", CONV_CH), ("conv_pw_b", D), ("pool_scale", POOL_CH), ("b_out", D), ("ln1_g", D), ("ln1_b", D),
              ("ln2_g", D), ("ln2_b", D), ("ffn_dw_b", D_FF))


def _as_rows(shape):
    return (int(np.prod(shape[:-1])), shape[-1])


def _full_from_blocks(blocks, axis):
    moved = jnp.moveaxis(blocks, 0, axis)
    shape = list(moved.shape)
    shape[axis:axis + 2] = [shape[axis] * shape[axis + 1]]
    return moved.reshape(shape)


def _blocks_from_full(full, axis):
    shape = list(full.shape)
    shape[axis:axis + 1] = [N_DEV, shape[axis] // N_DEV]
    return jnp.moveaxis(full.reshape(shape), axis, 0)


SMALL_N = DEPTH * sum(n for _, n in REPLICATED) + D
SMALL_ROWS = -(-(SMALL_N + 1) // (8 * LANES)) * 8


def _rope_tables(C, S):
    t = np.arange(S)
    inv_freq = ROPE_THETA ** (-np.arange(ROPE_PAIRS, dtype=np.float32) / ROPE_PAIRS)
    row = jnp.asarray((t // GRID_W).astype(np.float32))[:, None] * jnp.asarray(inv_freq, F32)
    col = jnp.asarray((t % GRID_W).astype(np.float32))[:, None] * jnp.asarray(inv_freq, F32)
    cos = jnp.concatenate([jnp.cos(row), jnp.cos(row), jnp.cos(col), jnp.cos(col)], axis=1)
    sin = jnp.concatenate([-jnp.sin(row), jnp.sin(row), -jnp.sin(col), jnp.sin(col)], axis=1)
    cos = jnp.concatenate([jnp.ones((C, HD), F32), cos], axis=0)
    sin = jnp.concatenate([jnp.zeros((C, HD), F32), sin], axis=0)
    return cos, sin


def _segment_sums(parts, B, tpe, ncq):
    p = parts.reshape(B, tpe, D)
    return jnp.concatenate([jnp.sum(p[:, ncq:], axis=1), jnp.sum(p[:, :ncq], axis=(0, 1))[None]], axis=0)


def kernel(x, c, ctx, c_ctx, w_ada, b_ada, w_in, b_in, q_gain, k_gain, conv_dw_w, conv_dw_b, conv_ln_g, conv_ln_b, conv_pw_w, conv_pw_b, pool_w, pool_scale, w_out, b_out, ln1_g, ln1_b, ln2_g, ln2_b, w_up, ffn_dw_w, ffn_dw_b, w_down, loss_target, m_c_ctx, m_w_ada, m_b_ada, m_w_in, m_b_in, m_q_gain, m_k_gain, m_conv_dw_w, m_conv_dw_b, m_conv_ln_g, m_conv_ln_b, m_conv_pw_w, m_conv_pw_b, m_pool_w, m_pool_scale, m_w_out, m_b_out, m_ln1_g, m_ln1_b, m_ln2_g, m_ln2_b, m_w_up, m_ffn_dw_w, m_ffn_dw_b, m_w_down, v_c_ctx, v_w_ada, v_b_ada, v_w_in, v_b_in, v_q_gain, v_k_gain, v_conv_dw_w, v_conv_dw_b, v_conv_ln_g, v_conv_ln_b, v_conv_pw_w, v_conv_pw_b, v_pool_w, v_pool_scale, v_w_out, v_b_out, v_ln1_g, v_ln1_b, v_ln2_g, v_ln2_b, v_w_up, v_ffn_dw_w, v_ffn_dw_b, v_w_down):
    given = dict(locals())
    B, S, _ = x.shape
    C = ctx.shape[1]
    R = C + S
    T = B * R
    tpe, ncq = R // TM, C // TM
    nt = T // TM
    assert S % TM == 0 and C % TM == 0 and B + 1 <= 16

    operands = [given[n].astype(MXU) for n, _, _ in BIG]
    taps = _all_gather([given[n] for n, _, _ in TAPS], name="gather_taps")
    W, modt = [], []

    xu = jnp.concatenate([ctx, x], axis=1).reshape(T, D)
    cond = jnp.concatenate([c, c_ctx[None], jnp.zeros((16 - B - 1, D), F32)], axis=0)
    s_cond, ds_cond = _silu_rows(cond)
    ctx_tile = jnp.asarray((np.arange(tpe) < ncq)[None, :, None])
    cos_t, sin_t = _rope_tables(C, S)
    row = lambda v: v.reshape(1, -1)

    def add_layer(blocks):
        l = len(W)
        wl = {n: _full_from_blocks(blk, a) for (n, _, a), blk in zip(BIG, blocks)}
        for (n, _, a), blk in zip(TAPS, taps):
            wl[n] = _full_from_blocks(blk[:, l], a)
        W.append(wl)
        m = _matmul(s_cond, wl["w_ada"], "nn", name="ada", bias=row(b_ada[l]), tm=16, tn=1024)
        modt.append(jnp.where(ctx_tile, m[B][None, None, :], m[:B][:, None, :]).reshape(nt, 1, N_MOD * D))

    add_layer(_all_gather(operands, name="gather_weights", layer=0))

    saved = []
    h1, h1t = _modulate_cast(xu, modt[0], 0)
    xin = xu
    for l in range(DEPTH):
        wl = W[l]
        w_inl, b_inl = wl["w_in"], b_in[l]
        z_qkv = _matmul(h1, w_inl[:, :QKV_W], "nn", name="z_qkv", bias=row(b_inl[:QKV_W]), tn=768)
        c0, p0, g0 = QKV_W, QKV_W + 2 * CONV_CH, QKV_W + 2 * CONV_CH + POOL_CH
        z_conv = _matmul(h1, w_inl[:, c0:p0], "nn", name="z_conv", bias=row(b_inl[c0:p0]))
        z_pool = _matmul(h1, w_inl[:, p0:g0], "nn", name="z_pool", bias=row(b_inl[p0:g0]))
        z_gate = _matmul(h1, w_inl[:, g0:], "nn", name="z_gate", bias=row(b_inl[g0:]))
        qkv = _qk_prep(z_qkv, cos_t, sin_t, row(q_gain[l]), row(k_gain[l]), tpe)
        if l + 1 < DEPTH:
            attn, early_blocks = _attn_fwd(qkv, B, C, R, carry=_Gather([operands[k] for k in EARLY], l + 1))
        else:
            attn, _ = _attn_fwd(qkv, B, C, R)
        dw32 =jnp.pad(wl["conv_dw_w"], ((0, 32 - CONV_K), (0, 0)))
        hc = _conv_fwd(z_conv, dw32, row(conv_dw_b[l]), B, C, S)
        sw, swt = _ln_silu(hc, row(conv_ln_g[l]), row(conv_ln_b[l]))
        conv_o = _matmul(sw, wl["conv_pw_w"], "nn", name="conv_pw", bias=row(conv_pw_b[l]))
        pooled, pool_o = _pool_fwd(z_pool, wl["pool_w"], row(pool_scale[l]), B, C, S)
        m, mt = _merge(attn, conv_o, pool_o, z_gate)
        mo = _matmul(m, wl["w_out"], "nn", name="w_out", bias=row(b_out[l]))
        y1, h2, h2t = _resid_ln(xin, mo, modt[l], 2, row(ln1_g[l]), row(ln1_b[l]), modt[l], 3)
        up = _matmul(h2, wl["w_up"], "nn", name="w_up", tn=1408)
        fw8 = jnp.pad(wl["ffn_dw_w"], ((0, 8 - FFN_K), (0, 0)))
        if l + 1 < DEPTH:
            f, ft, late_blocks = _ffn_mid(up, fw8, row(ffn_dw_b[l]), B, C, S, carry=_Gather([operands[k] for k in LATE], l + 1))
            blocks = dict(zip(EARLY + LATE, list(early_blocks) + list(late_blocks)))
            add_layer([blocks[k] for k in range(len(BIG))])
        else:
            f, ft, _ = _ffn_mid(up, fw8, row(ffn_dw_b[l]), B, C, S)
        fo = _matmul(f, wl["w_down"], "nn", name="w_down", tm=512)
        if l + 1 < DEPTH:
            y2, h_next, ht_next = _resid_ln(y1, fo, modt[l], 5, row(ln2_g[l]), row(ln2_b[l]), modt[l + 1], 0)
        else:
            y2, h_next, ht_next = _resid_ln(y1, fo, modt[l], 5, row(ln2_g[l]), row(ln2_b[l])), None, None
        saved.append(dict(xin=xin, h1t=h1t, z_qkv=z_qkv, z_conv=z_conv, z_gate=z_gate, qkv=qkv, attn=attn, hc=hc, swt=swt,
                          conv_o=conv_o, pooled=pooled, pool_o=pool_o, mt=mt, mo=mo, y1=y1, h2t=h2t, up=up, ft=ft, fo=fo,
                          dw32=dw32, fw8=fw8))
        xin, h1, h1t = y2, h_next, ht_next

    dy, loss_part = _loss_grad(xin, loss_target.reshape(B * S, D), tpe, ncq)

    small = {n: [None] * DEPTH for n, _ in REPLICATED}
    d_c_ctx = jnp.zeros((D,), F32)
    dmods_t = [[None] * N_MOD for _ in range(DEPTH)]
    layer_grads = [None] * DEPTH
    kinds = ("grad_", "delta_", "new_m_", "new_v_")
    stacks = {n: [given[pre + n].reshape((DEPTH,) + _as_rows(s)) for pre in ("", "m_", "v_")] for n, s, _ in SHARDED}
    results = {n: None for n, _, _ in SHARDED}

    def reduce_prepare(l):
        nonlocal d_c_ctx
        dmods = jnp.concatenate([_segment_sums(p, B, tpe, ncq) for p in dmods_t[l]], axis=1)
        small["b_ada"][l] = jnp.sum(dmods, axis=0)
        dm16 = jnp.concatenate([dmods, jnp.zeros((16 - B - 1, N_MOD * D), F32)], axis=0).astype(MXU)
        layer_grads[l]["w_ada"] = _matmul(s_cond, dm16, "tn", name="g_w_ada", tm=1024, tn=1024)
        dcond = _matmul(dm16, W[l]["w_ada"], "nt", name="d_cond", tm=16, tn=1024, tk=2048)
        d_c_ctx = d_c_ctx + dcond[B] * ds_cond[B]
        return [_blocks_from_full(layer_grads[l][n], a).reshape((N_DEV,) + _as_rows(s)) for n, s, a in SHARDED]

    def reduce_end(l, r2):
        for (n, _, _), parts in zip(SHARDED, r2):
            results[n] = _adamw(parts, *stacks[n], l, results[n], name="adamw_sharded")

    dh1 = None
    for l in reversed(range(DEPTH)):
        gs_above = None
        wl, sv = W[l], saved[l]
        dmod = dmods_t[l]
        if dh1 is None:
            dy1p, dfo, dgate2, dg, db, _ = _ln_bwd(dy, sv["y1"], sv["fo"], modt[l], 5, row(ln2_g[l]))
        else:
            dy1p, dfo, dgate2, dg, db, _, dsh, dsc = _ln_bwd(dy, sv["y1"], sv["fo"], modt[l], 5, row(ln2_g[l]), dh=dh1,
                                                             y=saved[l + 1]["xin"], mod_next=modt[l + 1], k_shift_next=0)
            dmods_t[l + 1][0], dmods_t[l + 1][1] = dsh, dsc
            gs_above = reduce_prepare(l + 1)
        small["ln2_g"][l], small["ln2_b"][l] = dg[0], db[0]
        dmod[5] = dgate2
        df = _matmul(dfo, wl["w_down"], "nt", name="d_f", tn=1408)
        g_w_down = _matmul(sv["ft"], dfo, "nn", name="g_w_down", tm=1408, tk=2304)
        if gs_above is None:
            da2, du2, g_fdw, g_fdb, _ = _ffn_mid_bwd(df, sv["up"], sv["fw8"], row(ffn_dw_b[l]), B, C, S)
            ps_above = None
        else:
            da2, du2, g_fdw, g_fdb, r1 = _ffn_mid_bwd(df, sv["up"], sv["fw8"], row(ffn_dw_b[l]), B, C, S,
                                                      carry=_SiblingExchange(gs_above))
            ps_above = [_pair_sum(g, r) for g, r in zip(gs_above, r1)]
        small["ffn_dw_b"][l] = g_fdb[0]
        dh2 = _matmul(da2, wl["w_up"][:, :D_FF], "nt", name="d_h2a", tm=512)
        dh2 = _matmul(du2, wl["w_up"][:, D_FF:], "nt", name="d_h2u", tm=512, acc_in=dh2)
        g_w_up = _matmul(sv["h2t"], da2, "nn", name="g_w_up_a", tn=1408, tk=2304, into=(2 * D_FF, 0, None))
        g_w_up = _matmul(sv["h2t"], du2, "nn", name="g_w_up_u", tn=1408, tk=2304, into=(2 * D_FF, D_FF, g_w_up))
        dxp, dmo, dgate1, dg, db, dbo, dsh, dsc = _ln_bwd(dy1p, sv["xin"], sv["mo"], modt[l], 2, row(ln1_g[l]), dh=dh2,
                                                          y=sv["y1"], mod_next=modt[l], k_shift_next=3)
        small["ln1_g"][l], small["ln1_b"][l], small["b_out"][l] = dg[0], db[0], dbo[0]
        dmod[2], dmod[3], dmod[4] = dgate1, dsh, dsc
        dm = _matmul(dmo, wl["w_out"], "nt", name="d_m")
        g_w_out = _matmul(sv["mt"], dmo, "nn", name="g_w_out", tk=2304)
        dattn, dconv_o, dpool_o, dzg, g_pwb, gsum_gate = _merge_bwd(dm, sv["attn"], sv["conv_o"], sv["pool_o"], sv["z_gate"])
        small["conv_pw_b"][l] = g_pwb[0]
        du, g_pool_w, g_pool_sc, gsum_pool = _pool_bwd(dpool_o, sv["pooled"], wl["pool_w"], row(pool_scale[l]), B, C, S)
        small["pool_scale"][l] = g_pool_sc[0]
        dsw = _matmul(dconv_o, wl["conv_pw_w"], "nt", name="d_sw")
        g_pw = _matmul(sv["swt"], dconv_o, "nn", name="g_conv_pw", tk=2304)
        dhc, g_cg, g_cb, g_cdb = _ln_silu_bwd(dsw, sv["hc"], row(conv_ln_g[l]), row(conv_ln_b[l]))
        small["conv_ln_g"][l], small["conv_ln_b"][l], small["conv_dw_b"][l] = g_cg[0], g_cb[0], g_cdb[0]
        da, dgt, g_cdw, gsum_a, gsum_gt = _conv_bwd(dhc, sv["z_conv"], sv["dw32"], B, C, S)
        if ps_above is None:
            dq, dk, dv, _ = _attn_bwd(sv["qkv"], sv["attn"], dattn, B, C, R)
        else:
            dq, dk, dv, r2 = _attn_bwd(sv["qkv"], sv["attn"], dattn, B, C, R, carry=_ChipExchange(ps_above))
            reduce_end(l + 1, r2)
        dz_qkv, g_qg, g_kg, gsum_qkv = _qk_bwd(dq, dk, dv, sv["z_qkv"], cos_t, sin_t, row(q_gain[l]), row(k_gain[l]), tpe)
        small["q_gain"][l], small["k_gain"][l] = g_qg[0], g_kg[0]
        small["b_in"][l] = jnp.concatenate([gsum_qkv[0], gsum_a[0], gsum_gt[0], gsum_pool[0], gsum_gate[0]])
        w_inl = wl["w_in"]
        g0 = QKV_W + 2 * CONV_CH + POOL_CH
        pieces = ((dzg, g0, N_GATE), (da, QKV_W, CONV_CH), (dgt, QKV_W + CONV_CH, CONV_CH), (du, QKV_W + 2 * CONV_CH, POOL_CH),
                  (dz_qkv, 0, QKV_W))
        dh1 = g_w_in = None
        at = 0
        for k, (dz, c0, wd) in enumerate(pieces):
            dh1 = _matmul(dz, w_inl[:, c0:c0 + wd], "nt", name=f"d_h1_{k}", tm=512, acc_in=dh1)
            g_w_in = _matmul(sv["h1t"], dz, "nn", name=f"g_w_in_{k}", tk=2304, into=(D_IN, at, g_w_in))
            at += wd
        g_w_in = jnp.concatenate([g_w_in[:, N_GATE + 2 * CONV_CH + POOL_CH:], g_w_in[:, N_GATE:N_GATE + 2 * CONV_CH + POOL_CH],
                                  g_w_in[:, :N_GATE]], axis=1)

        layer_grads[l] = {"w_in": g_w_in, "conv_pw_w": g_pw, "pool_w": g_pool_w, "w_out": g_w_out, "w_up": g_w_up,
                          "w_down": g_w_down, "conv_dw_w": g_cdw[:CONV_K], "ffn_dw_w": g_fdw[:FFN_K]}
        dy = dxp
    gx_u, dmods_t[0][0], dmods_t[0][1] = _mod_bwd(dy, dh1, saved[0]["xin"], modt[0], 0)
    grad_x = gx_u.reshape(B, R, D)[:, C:]

    gs = reduce_prepare(0)
    r1 = _run_exchange(_SiblingExchange(gs), name="sibling_exchange")
    reduce_end(0, _run_exchange(_ChipExchange([_pair_sum(g, r) for g, r in zip(gs, r1)]), name="chip_exchange"))
    outs = {}
    for n, s, _ in SHARDED:
        for kind, buf in zip(kinds, results[n]):
            outs[kind + n] = buf.reshape((DEPTH,) + tuple(s))

    def small_pack(pieces):
        flat = jnp.concatenate([p.reshape(-1) for p in pieces])
        return jnp.pad(flat, (0, SMALL_ROWS * LANES - flat.shape[0])).reshape(SMALL_ROWS, LANES)

    zero1 = jnp.zeros((1,), F32)
    g_pack = small_pack([small[n][l] for n, _ in REPLICATED for l in range(DEPTH)] + [d_c_ctx, loss_part[0, :1]])
    g_small, = _all_gather([g_pack], name="gather_small")
    wmv = [small_pack([given[pre + n] for n, _ in REPLICATED] + [given[pre + "c_ctx"], zero1])[None] for pre in ("", "m_", "v_")]
    res = _adamw(g_small, *wmv, 0, None, name="adamw_small")
    for kind, buf in zip(kinds, res):
        flat = buf.reshape(-1)
        off = 0
        for n, sz in REPLICATED:
            outs[kind + n] = flat[off:off + DEPTH * sz].reshape(DEPTH, sz)
            off += DEPTH * sz
        outs[kind + "c_ctx"] = flat[off:off + D]
        if kind == "grad_":
            loss = flat[off + D]

    names = ["c_ctx", "w_ada", "b_ada", "w_in", "b_in", "q_gain", "k_gain", "conv_dw_w", "conv_dw_b", "conv_ln_g", "conv_ln_b",
             "conv_pw_w", "conv_pw_b", "pool_w", "pool_scale", "w_out", "b_out", "ln1_g", "ln1_b", "ln2_g", "ln2_b", "w_up",
             "ffn_dw_w", "ffn_dw_b", "w_down"]
    return (loss, grad_x, *[outs[k + n] for k in ("grad_", "delta_", "new_m_", "new_v_") for n in names])
```

```python
import functools

import jax
import jax.numpy as jnp
import numpy as np
from jax import lax
from jax.experimental import pallas as pl
from jax.experimental.pallas import tpu as pltpu

F32 = jnp.float32
MXU = jnp.bfloat16
WIRE = jnp.bfloat16

D = 1024
HD = 128
NH = 8
NKV = 2
QG = NH // NKV
KV_W = NKV * HD
QKV_W = NH * HD + 2 * KV_W
CONV_CH = D
POOL_CH = D
POOL_WINDOWS = (2, 4, 8, 16)
POOL_GCH = POOL_CH // len(POOL_WINDOWS)
N_GATE = 3 * D
D_IN = QKV_W + 2 * CONV_CH + POOL_CH + N_GATE
D_FF = 2816
N_MOD = 6
DEPTH = 4
CONV_K = 31
FFN_K = 3
GRID_W = 64
ROPE_THETA = 10000.0
ROPE_PAIRS = HD // 4
ALPHA = (2 * DEPTH) ** 0.25
LN_EPS = 1e-5
RMS_EPS = 1e-6
ATTN_SCALE = HD ** -0.5
LOG2_E = 1.4426950408889634
ADAM_LR, ADAM_B1, ADAM_B2, ADAM_EPS, ADAM_WD, ADAM_STEP = 0.001, 0.9, 0.999, 1e-08, 0.01, 10

N_DEV = 8
TM = 256
GAP = 16
LANES = 1024
LANE = 128
VMEM_MB = 48

NN = (((1,), (0,)), ((), ()))
NT = (((1,), (1,)), ((), ()))
TN = (((0,), (0,)), ((), ()))

_pcall = pl.pallas_call


def _call(body, *, name, grid, in_specs, out_specs, out_shape, scratch=(), aliases=None, vmem=VMEM_MB):
    return _pcall(
        body, name=name, grid=grid, in_specs=in_specs, out_specs=out_specs, out_shape=out_shape,
        scratch_shapes=list(scratch), input_output_aliases=aliases or {},
        compiler_params=pltpu.CompilerParams(dimension_semantics=("arbitrary",) * len(grid), vmem_limit_bytes=vmem * 2 ** 20),
    )


def _sds(shape, dtype):
    return jax.ShapeDtypeStruct(tuple(shape), dtype)


def _pick(n, cap, mult):
    best = None
    for t in range(mult, min(n, cap) + 1, mult):
        if n % t == 0:
            best = t
    return best if best is not None else n


def _dot(a, b, dims):
    return lax.dot_general(a, b, dims, preferred_element_type=F32)


def _sigmoid(x):
    return 1.0 / (1.0 + jnp.exp(-x))


def _matmul(a, b, mode, *, name, bias=None, acc_in=None, into=None, out_dtype=F32, tm=1024, tn=1024, tk=None):
    if mode == "nn":
        (M, K), (K2, N) = a.shape, b.shape
    elif mode == "nt":
        (M, K), (N, K2) = a.shape, b.shape
    else:
        (K, M), (K2, N) = a.shape, b.shape
    assert K == K2, (a.shape, b.shape, mode)
    tm = _pick(M, tm, 16)
    tn = _pick(N, tn, 128)
    tk = K if tk is None else _pick(K, tk, 128 if mode != "tn" else 16)
    gk = K // tk
    dims = {"nn": NN, "nt": NT, "tn": TN}[mode]
    a_spec = pl.BlockSpec((tk, tm), lambda j, i, k: (k, i)) if mode == "tn" else pl.BlockSpec((tm, tk), lambda j, i, k: (i, k))
    b_spec = pl.BlockSpec((tn, tk), lambda j, i, k: (j, k)) if mode == "nt" else pl.BlockSpec((tk, tn), lambda j, i, k: (k, j))
    in_specs, args = [a_spec, b_spec], [a, b]
    if bias is not None:
        in_specs.append(pl.BlockSpec((1, tn), lambda j, i, k: (0, j)))
        args.append(bias)
    aliases = {}
    if acc_in is not None:
        aliases = {len(args): 0}
        in_specs.append(pl.BlockSpec((tm, tn), lambda j, i, k: (i, j)))
        args.append(acc_in)
    n_total, col0, prev = (N, 0, None) if into is None else into
    assert col0 % tn == 0
    jb = col0 // tn
    if prev is not None:
        aliases = {len(args): 0}
        in_specs.append(pl.BlockSpec(memory_space=pl.ANY))
        args.append(prev)
    n_in = len(args)

    def body(*refs):
        a_ref, b_ref = refs[0], refs[1]
        pos = 2
        bias_ref = acc_in_ref = None
        if bias is not None:
            bias_ref = refs[pos]
            pos += 1
        if acc_in is not None:
            acc_in_ref = refs[pos]
        pos = n_in
        o_ref = refs[pos]
        part = _dot(a_ref[...].astype(MXU), b_ref[...].astype(MXU), dims)

        def finish(acc):
            if bias_ref is not None:
                acc = acc + bias_ref[...]
            if acc_in_ref is not None:
                acc = acc + acc_in_ref[...]
            o_ref[...] = acc.astype(out_dtype)

        if gk == 1:
            finish(part)
        else:
            acc_ref = refs[pos + 1]
            k = pl.program_id(2)

            @pl.when(k == 0)
            def _():
                acc_ref[...] = part

            @pl.when(k > 0)
            def _():
                acc_ref[...] += part

            @pl.when(k == gk - 1)
            def _():
                finish(acc_ref[...])

    return _call(
        body, name=name, grid=(N // tn, M // tm, gk), in_specs=in_specs,
        out_specs=pl.BlockSpec((tm, tn), lambda j, i, k: (i, j + jb)), out_shape=_sds((M, n_total), out_dtype),
        scratch=[pltpu.VMEM((tm, tn), F32)] if gk > 1 else [], aliases=aliases,
    )(*args)


def _rt(w, cb=0):
    return pl.BlockSpec((TM, w), lambda i: (i, cb))


def _ct(w):
    return pl.BlockSpec((w, TM), lambda i: (0, i))


def _vec(w):
    return pl.BlockSpec((1, w), lambda i: (0, 0))


def _part(w):
    return pl.BlockSpec((1, 1, w), lambda i: (i, 0, 0))


def _mod(ref, k):
    return ref[0, :, k * D:(k + 1) * D]


def _colsum(x):
    return jnp.sum(x, axis=0, keepdims=True)


def _ln_stats(s):
    mu = jnp.mean(s, axis=1, keepdims=True)
    cen = s - mu
    var = jnp.mean(cen * cen, axis=1, keepdims=True)
    rstd = lax.rsqrt(var + LN_EPS)
    return cen * rstd, rstd


def _modulate_cast(x, modt, k_shift):
    T = x.shape[0]
    nt = T // TM

    def body(x_ref, mod_ref, h_ref, ht_ref):
        h = (x_ref[...] * (1.0 + _mod(mod_ref, k_shift + 1)) + _mod(mod_ref, k_shift)).astype(MXU)
        h_ref[...] = h
        ht_ref[...] = h.T

    return _call(body, name="modulate", grid=(nt,), in_specs=[_rt(D), _part(N_MOD * D)], out_specs=[_rt(D), _ct(D)],
                 out_shape=[_sds((T, D), MXU), _sds((D, T), MXU)])(x, modt)


def _resid_ln(x, br, modt, k_gate, g, b, mod_next=None, k_shift_next=0):
    T = x.shape[0]
    nt = T // TM
    with_h = mod_next is not None

    def body(*refs):
        x_ref, br_ref, mod_ref, g_ref, b_ref = refs[:5]
        s = ALPHA * x_ref[...] + _mod(mod_ref, k_gate) * br_ref[...]
        xhat, _ = _ln_stats(s)
        y = xhat * g_ref[...] + b_ref[...]
        if with_h:
            modn_ref, y_ref, h_ref, ht_ref = refs[5:]
            y_ref[...] = y
            h = (y * (1.0 + _mod(modn_ref, k_shift_next + 1)) + _mod(modn_ref, k_shift_next)).astype(MXU)
            h_ref[...] = h
            ht_ref[...] = h.T
        else:
            refs[5][...] = y

    in_specs = [_rt(D), _rt(D), _part(N_MOD * D), _vec(D), _vec(D)]
    args = [x, br, modt, g, b]
    if with_h:
        in_specs.append(_part(N_MOD * D))
        args.append(mod_next)
        return _call(body, name="resid_ln_mod", grid=(nt,), in_specs=in_specs, out_specs=[_rt(D), _rt(D), _ct(D)],
                     out_shape=[_sds((T, D), F32), _sds((T, D), MXU), _sds((D, T), MXU)])(*args)
    return _call(body, name="resid_ln", grid=(nt,), in_specs=in_specs, out_specs=_rt(D), out_shape=_sds((T, D), F32))(*args)


def _ln_bwd(dy_part, x, br, modt, k_gate, g, dh=None, y=None, mod_next=None, k_shift_next=0):
    T = x.shape[0]
    nt = T // TM
    with_h = dh is not None

    def body(*refs):
        if with_h:
            dyp_ref, x_ref, br_ref, mod_ref, g_ref, dh_ref, y_ref, modn_ref = refs[:8]
            outs = refs[8:]
        else:
            dyp_ref, x_ref, br_ref, mod_ref, g_ref = refs[:5]
            outs = refs[5:]
        dx_ref, dbr_ref, dgate_ref, dlg_ref, dlb_ref, dbsum_ref = outs[:6]
        i = pl.program_id(0)

        @pl.when(i == 0)
        def _():
            dlg_ref[...] = jnp.zeros_like(dlg_ref)
            dlb_ref[...] = jnp.zeros_like(dlb_ref)
            dbsum_ref[...] = jnp.zeros_like(dbsum_ref)

        dy = dyp_ref[...]
        if with_h:
            dshift_ref, dscale_ref = outs[6:]
            dhv = dh_ref[...]
            dy = dy + dhv * (1.0 + _mod(modn_ref, k_shift_next + 1))
            dshift_ref[0] = _colsum(dhv)
            dscale_ref[0] = _colsum(dhv * y_ref[...])
        gate = _mod(mod_ref, k_gate)
        brv = br_ref[...]
        s = ALPHA * x_ref[...] + gate * brv
        xhat, rstd = _ln_stats(s)
        dlg_ref[...] += _colsum(dy * xhat)
        dlb_ref[...] += _colsum(dy)
        dyg = dy * g_ref[...]
        m1 = jnp.mean(dyg, axis=1, keepdims=True)
        m2 = jnp.mean(dyg * xhat, axis=1, keepdims=True)
        ds = rstd * (dyg - m1 - xhat * m2)
        dx_ref[...] = ALPHA * ds
        dbr = gate * ds
        dbr_ref[...] = dbr.astype(MXU)
        dbsum_ref[...] += _colsum(dbr)
        dgate_ref[0] = _colsum(ds * brv)

    in_specs = [_rt(D), _rt(D), _rt(D), _part(N_MOD * D), _vec(D)]
    args = [dy_part, x, br, modt, g]
    out_specs = [_rt(D), _rt(D), _part(D), _vec(D), _vec(D), _vec(D)]
    out_shape = [_sds((T, D), F32), _sds((T, D), MXU), _sds((nt, 1, D), F32), _sds((1, D), F32), _sds((1, D), F32), _sds((1, D), F32)]
    if with_h:
        in_specs += [_rt(D), _rt(D), _part(N_MOD * D)]
        args += [dh, y, mod_next]
        out_specs += [_part(D), _part(D)]
        out_shape += [_sds((nt, 1, D), F32), _sds((nt, 1, D), F32)]
    return _call(body, name="ln_bwd_mod" if with_h else "ln_bwd", grid=(nt,), in_specs=in_specs, out_specs=out_specs,
                 out_shape=out_shape)(*args)


def _mod_bwd(dx_part, dh, x, modt, k_shift):
    T = x.shape[0]
    nt = T // TM

    def body(dxp_ref, dh_ref, x_ref, mod_ref, dx_ref, dshift_ref, dscale_ref):
        dhv = dh_ref[...]
        dx_ref[...] = dxp_ref[...] + dhv * (1.0 + _mod(mod_ref, k_shift + 1))
        dshift_ref[0] = _colsum(dhv)
        dscale_ref[0] = _colsum(dhv * x_ref[...])

    return _call(body, name="mod_bwd", grid=(nt,), in_specs=[_rt(D), _rt(D), _rt(D), _part(N_MOD * D)],
                 out_specs=[_rt(D), _part(D), _part(D)],
                 out_shape=[_sds((T, D), F32), _sds((nt, 1, D), F32), _sds((nt, 1, D), F32)])(dx_part, dh, x, modt)


def _loss_grad(y, target, tpe, ncq):
    T = y.shape[0]
    nt = T // TM
    nl = tpe - ncq

    def body(y_ref, t_ref, dy_ref, loss_ref):
        i = pl.program_id(0)

        @pl.when(i == 0)
        def _():
            loss_ref[...] = jnp.zeros_like(loss_ref)

        @pl.when(i % tpe < ncq)
        def _():
            dy_ref[...] = jnp.zeros_like(dy_ref)

        @pl.when(i % tpe >= ncq)
        def _():
            err = y_ref[...] - t_ref[...]
            dy_ref[...] = err * (1.0 / D)
            loss_ref[...] += (0.5 / D) * jnp.sum(_colsum(err * err), axis=1, keepdims=True)

    tgt_spec = pl.BlockSpec((TM, D), lambda i: ((i // tpe) * nl + jnp.maximum(i % tpe - ncq, 0), 0))
    return _call(body, name="loss_grad", grid=(nt,), in_specs=[_rt(D), tgt_spec], out_specs=[_rt(D), _vec(128)],
                 out_shape=[_sds((T, D), F32), _sds((1, 128), F32)])(y, target)


def _rope_partner(x):
    lane = lax.broadcasted_iota(jnp.int32, x.shape, 1)
    first = (lane % (2 * ROPE_PAIRS)) < ROPE_PAIRS
    return jnp.where(first, pltpu.roll(x, HD - ROPE_PAIRS, 1), pltpu.roll(x, ROPE_PAIRS, 1))


def _qk_prep(z_qkv, cos_t, sin_t, q_gain, k_gain, tpe):
    T = z_qkv.shape[0]
    nt = T // TM

    def body(z_ref, cos_ref, sin_ref, qg_ref, kg_ref, o_ref):
        cos, sin = cos_ref[...], sin_ref[...]
        for h in range(NH + NKV):
            sl = slice(h * HD, (h + 1) * HD)
            t = z_ref[:, sl]
            gain = qg_ref[...] if h < NH else kg_ref[...]
            n = t * lax.rsqrt(jnp.mean(t * t, axis=1, keepdims=True) + RMS_EPS) * gain
            o_ref[:, sl] = (n * cos + _rope_partner(n) * sin).astype(MXU)
        o_ref[:, (NH + NKV) * HD:] = z_ref[:, (NH + NKV) * HD:].astype(MXU)

    tab = pl.BlockSpec((TM, HD), lambda i: (i % tpe, 0))
    return _call(body, name="qk_prep", grid=(nt,), in_specs=[_rt(QKV_W), tab, tab, _vec(HD), _vec(HD)], out_specs=_rt(QKV_W),
                 out_shape=_sds((T, QKV_W), MXU))(z_qkv, cos_t, sin_t, q_gain, k_gain)


def _qk_bwd(dq, dk, dv, z_qkv, cos_t, sin_t, q_gain, k_gain, tpe):
    T = z_qkv.shape[0]
    nt = T // TM

    def body(dq_ref, dk_ref, dv_ref, z_ref, cos_ref, sin_ref, qg_ref, kg_ref, dz_ref, dqg_ref, dkg_ref, bsum_ref):
        i = pl.program_id(0)

        @pl.when(i == 0)
        def _():
            dqg_ref[...] = jnp.zeros_like(dqg_ref)
            dkg_ref[...] = jnp.zeros_like(dkg_ref)
            bsum_ref[...] = jnp.zeros_like(bsum_ref)

        cos, sin = cos_ref[...], sin_ref[...]
        for h in range(NH + NKV):
            sl = slice(h * HD, (h + 1) * HD)
            dr = dq_ref[:, sl] if h < NH else dk_ref[:, (h - NH) * HD:(h - NH + 1) * HD]
            gain = qg_ref[...] if h < NH else kg_ref[...]
            dn = dr * cos + _rope_partner(dr * sin)
            t = z_ref[:, sl]
            rstd = lax.rsqrt(jnp.mean(t * t, axis=1, keepdims=True) + RMS_EPS)
            that = t * rstd
            dgain = _colsum(dn * that)
            if h < NH:
                dqg_ref[...] += dgain
            else:
                dkg_ref[...] += dgain
            dthat = dn * gain
            dt = rstd * (dthat - that * jnp.mean(dthat * that, axis=1, keepdims=True))
            dz_ref[:, sl] = dt.astype(MXU)
            bsum_ref[:, sl] += _colsum(dt)
        dvv = dv_ref[...]
        dz_ref[:, (NH + NKV) * HD:] = dvv.astype(MXU)
        bsum_ref[:, (NH + NKV) * HD:] += _colsum(dvv)

    tab = pl.BlockSpec((TM, HD), lambda i: (i % tpe, 0))
    return _call(body, name="qk_bwd", grid=(nt,),
                 in_specs=[_rt(NH * HD), _rt(KV_W), _rt(KV_W), _rt(QKV_W), tab, tab, _vec(HD), _vec(HD)],
                 out_specs=[_rt(QKV_W), _vec(HD), _vec(HD), _vec(QKV_W)],
                 out_shape=[_sds((T, QKV_W), MXU), _sds((1, HD), F32), _sds((1, HD), F32), _sds((1, QKV_W), F32)],
                 )(dq, dk, dv, z_qkv, cos_t, sin_t, q_gain, k_gain)


def _ln_silu(hc, g, b):
    T = hc.shape[0]

    def body(h_ref, g_ref, b_ref, o_ref, ot_ref):
        xhat, _ = _ln_stats(h_ref[...])
        n = xhat * g_ref[...] + b_ref[...]
        sw = (n * _sigmoid(n)).astype(MXU)
        o_ref[...] = sw
        ot_ref[...] = sw.T

    return _call(body, name="ln_silu", grid=(T // TM,), in_specs=[_rt(D), _vec(D), _vec(D)], out_specs=[_rt(D), _ct(D)],
                 out_shape=[_sds((T, D), MXU), _sds((D, T), MXU)])(hc, g, b)


def _ln_silu_bwd(dsw, hc, g, b):
    T = hc.shape[0]

    def body(d_ref, h_ref, g_ref, b_ref, dh_ref, dg_ref, db_ref, dcb_ref):
        i = pl.program_id(0)

        @pl.when(i == 0)
        def _():
            dg_ref[...] = jnp.zeros_like(dg_ref)
            db_ref[...] = jnp.zeros_like(db_ref)
            dcb_ref[...] = jnp.zeros_like(dcb_ref)

        xhat, rstd = _ln_stats(h_ref[...])
        n = xhat * g_ref[...] + b_ref[...]
        sg = _sigmoid(n)
        dn = d_ref[...] * (sg * (1.0 + n * (1.0 - sg)))
        dg_ref[...] += _colsum(dn * xhat)
        db_ref[...] += _colsum(dn)
        dng = dn * g_ref[...]
        m1 = jnp.mean(dng, axis=1, keepdims=True)
        m2 = jnp.mean(dng * xhat, axis=1, keepdims=True)
        dh = rstd * (dng - m1 - xhat * m2)
        dh_ref[...] = dh
        dcb_ref[...] += _colsum(dh)

    return _call(body, name="ln_silu_bwd", grid=(T // TM,), in_specs=[_rt(D), _rt(D), _vec(D), _vec(D)],
                 out_specs=[_rt(D), _vec(D), _vec(D), _vec(D)],
                 out_shape=[_sds((T, D), F32)] + [_sds((1, D), F32)] * 3)(dsw, hc, g, b)


def _merge(attn, conv_o, pool_o, z_gate):
    T = attn.shape[0]

    def body(a_ref, c_ref, p_ref, zg_ref, m_ref, mt_ref):
        m = (_sigmoid(zg_ref[:, 0:D]) * a_ref[...] + _sigmoid(zg_ref[:, D:2 * D]) * c_ref[...]
             + _sigmoid(zg_ref[:, 2 * D:3 * D]) * p_ref[...]).astype(MXU)
        m_ref[...] = m
        mt_ref[...] = m.T

    return _call(body, name="merge", grid=(T // TM,), in_specs=[_rt(D), _rt(D), _rt(D), _rt(N_GATE)], out_specs=[_rt(D), _ct(D)],
                 out_shape=[_sds((T, D), MXU), _sds((D, T), MXU)])(attn, conv_o, pool_o, z_gate)


def _merge_bwd(dm, attn, conv_o, pool_o, z_gate):
    T = attn.shape[0]

    def body(dm_ref, a_ref, c_ref, p_ref, zg_ref, da_ref, dc_ref, dp_ref, dzg_ref, dcsum_ref, gsum_ref):
        i = pl.program_id(0)

        @pl.when(i == 0)
        def _():
            dcsum_ref[...] = jnp.zeros_like(dcsum_ref)
            gsum_ref[...] = jnp.zeros_like(gsum_ref)

        dmv = dm_ref[...]
        for k, (br_ref, out_ref) in enumerate(((a_ref, da_ref), (c_ref, dc_ref), (p_ref, dp_ref))):
            gk = _sigmoid(zg_ref[:, k * D:(k + 1) * D])
            dbr = dmv * gk
            out_ref[...] = dbr.astype(out_ref.dtype)
            if k == 1:
                dcsum_ref[...] += _colsum(dbr)
            dzg = dmv * br_ref[...] * gk * (1.0 - gk)
            dzg_ref[:, k * D:(k + 1) * D] = dzg.astype(MXU)
            gsum_ref[:, k * D:(k + 1) * D] += _colsum(dzg)

    return _call(body, name="merge_bwd", grid=(T // TM,), in_specs=[_rt(D), _rt(D), _rt(D), _rt(D), _rt(N_GATE)],
                 out_specs=[_rt(D), _rt(D), _rt(D), _rt(N_GATE), _vec(D), _vec(N_GATE)],
                 out_shape=[_sds((T, D), F32), _sds((T, D), MXU), _sds((T, D), F32), _sds((T, N_GATE), MXU),
                            _sds((1, D), F32), _sds((1, N_GATE), F32)])(dm, attn, conv_o, pool_o, z_gate)


def _softmax_parts(q, k):
    s = _dot(q, k, NT)
    p = jnp.exp2((s - jnp.max(s, axis=1, keepdims=True)) * (ATTN_SCALE * LOG2_E))
    return p, 1.0 / jnp.sum(p, axis=1, keepdims=True)


def _attn_specs(nq):
    q_spec = pl.BlockSpec((TM, QG * HD), lambda b, h, q: (b * nq + q, h))
    k_spec = pl.BlockSpec((nq * TM, HD), lambda b, h, q: (b, NH + h))
    v_spec = pl.BlockSpec((nq * TM, HD), lambda b, h, q: (b, NH + NKV + h))
    return q_spec, k_spec, v_spec


class _Carried:
    def __init__(self, ex, n_in, n_out, n_scratch):
        self.ex, self.n_in, self.n_out, self.n_scratch = ex, n_in, n_out, n_scratch
        self.ci = len(ex.inputs) if ex else 0
        self.co = len(ex.out_shape) if ex else 0

    def in_specs(self):
        return [ANY] * self.ci

    def out_specs(self):
        return [ANY] * self.co

    def split(self, refs):
        a = self.n_in
        b = a + self.ci
        c = b + self.n_out
        d = c + self.co
        e = d + self.n_scratch
        return (refs[:a], refs[b:c], refs[d:e]), (refs[a:b], refs[c:d], refs[e:])

    def before(self, step, parts):
        if self.ex:
            pl.when(step == 0)(lambda: self.ex.start(*parts))

    def after(self, step, n_steps, parts):
        if self.ex:
            pl.when(step == (3 * n_steps) // 4)(lambda: self.ex.mid(*parts))
            pl.when(step == n_steps - 1)(lambda: self.ex.finish(*parts))


def _attn_fwd(qkv, B, C, R, carry=None):
    nq, ncq = R // TM, C // TM
    car = _Carried(carry, 3, 1, 0)

    def body(*refs):
        (q_ref, k_ref, v_ref), (o_ref,), _ = car.split(refs)[0]
        parts = car.split(refs)[1]
        qi = pl.program_id(2)
        step = (pl.program_id(0) * NKV + pl.program_id(1)) * nq + qi
        car.before(step, parts)

        def attend(L):
            k, v = k_ref[0:L, :], v_ref[0:L, :]
            for i in range(QG):
                sl = slice(i * HD, (i + 1) * HD)
                p, inv_l = _softmax_parts(q_ref[:, sl], k)
                o_ref[:, sl] = _dot(p.astype(MXU), v, NN) * inv_l

        pl.when(qi < ncq)(functools.partial(attend, C))
        pl.when(qi >= ncq)(functools.partial(attend, R))
        car.after(step, B * NKV * nq, parts)

    q_spec, k_spec, v_spec = _attn_specs(nq)
    res = _call(body, name="attn_fwd", grid=(B, NKV, nq), in_specs=[q_spec, k_spec, v_spec] + car.in_specs(),
                out_specs=[q_spec] + car.out_specs(), out_shape=[_sds((B * R, NH * HD), F32)] + list(carry.out_shape if carry else []),
                scratch=list(carry.scratch) if carry else [])(qkv, qkv, qkv, *(carry.inputs if carry else []))
    return res[0], res[1:]


def _attn_bwd(qkv, o, do, B, C, R, carry=None):
    nq, ncq = R // TM, C // TM
    car = _Carried(carry, 5, 3, 2)

    def body(*refs):
        (q_ref, k_ref, v_ref, o_ref, do_ref), (dq_ref, dk_ref, dv_ref), (dkt, dvt) = car.split(refs)[0]
        parts = car.split(refs)[1]
        qi = pl.program_id(2)
        step = (pl.program_id(0) * NKV + pl.program_id(1)) * nq + qi
        car.before(step, parts)

        @pl.when(qi == 0)
        def _():
            dkt[...] = jnp.zeros_like(dkt)
            dvt[...] = jnp.zeros_like(dvt)

        def bwd(L):
            k, v = k_ref[0:L, :], v_ref[0:L, :]
            for i in range(QG):
                sl = slice(i * HD, (i + 1) * HD)
                q = q_ref[:, sl]
                p, inv_l = _softmax_parts(q, k)
                dov = do_ref[:, sl]
                dp = _dot(dov.astype(MXU), v, NT)
                dl = jnp.sum(dov * o_ref[:, sl], axis=1, keepdims=True)
                ds = (p * ((dp - dl) * (inv_l * ATTN_SCALE))).astype(MXU)
                dq_ref[:, sl] = _dot(ds, k, NN)
                dkt[:, 0:L] += _dot(q, ds, TN)
                dvt[:, 0:L] += _dot((dov * inv_l).astype(MXU), p.astype(MXU), TN)

        pl.when(qi < ncq)(functools.partial(bwd, C))
        pl.when(qi >= ncq)(functools.partial(bwd, R))

        @pl.when(qi == nq - 1)
        def _():
            dk_ref[...] = dkt[...].T
            dv_ref[...] = dvt[...].T

        car.after(step, B * NKV * nq, parts)

    q_spec, k_spec, v_spec = _attn_specs(nq)
    kv_out = pl.BlockSpec((R, HD), lambda b, h, q: (b, h))
    res = _call(body, name="attn_bwd", grid=(B, NKV, nq), in_specs=[q_spec, k_spec, v_spec, q_spec, q_spec] + car.in_specs(),
                out_specs=[q_spec, kv_out, kv_out] + car.out_specs(),
                out_shape=[_sds((B * R, NH * HD), F32), _sds((B * R, KV_W), F32), _sds((B * R, KV_W), F32)]
                + list(carry.out_shape if carry else []),
                scratch=[pltpu.VMEM((HD, R), F32), pltpu.VMEM((HD, R), F32)] + list(carry.scratch if carry else []),
                )(qkv, qkv, qkv, o, do, *(carry.inputs if carry else []))
    return res[0], res[1], res[2], res[3:]


def _segments(C, S):
    return ((0, GAP, C), (C, 2 * GAP + C, S))


def _padded_rows(C, S):
    return 3 * GAP + C + S


def _zero_gaps(pad_ref, C, S):
    for off in (0, GAP + C, 2 * GAP + C + S):
        pad_ref[off:off + GAP, :] = jnp.zeros((GAP, pad_ref.shape[1]), pad_ref.dtype)


def _chunks(n, ch, fn):
    def step(i, carry):
        fn(pl.multiple_of(i * ch, ch))
        return carry

    lax.fori_loop(0, n // ch, step, 0)


class _Window:
    def __init__(self, pad_ref, row, ch):
        self.pad_ref, (self.r, self.po), self.ch = pad_ref, row, ch

    @property
    def win(self):
        return self.pad_ref[pl.ds(self.r + (self.po - GAP), self.ch + 2 * GAP), :]

    def at(self, off):
        return self.pad_ref[pl.ds(self.r + (self.po + off), self.ch), :]


def _taps(pad_ref, w, row, ch, n_taps, flip=False, sliced=False):
    half = (n_taps - 1) // 2
    win = _Window(pad_ref, row, ch)
    whole = win.win if sliced else None
    acc = None
    for k in range(n_taps):
        off = (half - k) if flip else (k - half)
        view = whole[GAP + off:GAP + off + ch, :] if sliced else win.at(off)
        term = w[k:k + 1, :] * view
        acc = term if acc is None else acc + term
    return acc


def _tap_grads(dw_ref, d, pad_ref, row, ch, n_taps):
    half = (n_taps - 1) // 2
    win = _Window(pad_ref, row, ch)
    for k in range(n_taps):
        prod = d * win.at(k - half)
        dw_ref[k] += jnp.sum(prod.reshape(ch // 8, 8, prod.shape[1]), axis=0)


def _conv_fwd(z_conv, w, bias, B, C, S, cw=128, ch=128):
    R = C + S
    nj = CONV_CH // cw
    segs = _segments(C, S)

    def body(a_ref, g_ref, w_ref, b_ref, o_ref, pad):
        _zero_gaps(pad, C, S)
        wv = w_ref[...]
        for so, po, n in segs:
            def fill(r, so=so, po=po):
                pad[pl.ds(po + r, ch), :] = a_ref[pl.ds(so + r, ch), :] * _sigmoid(g_ref[pl.ds(so + r, ch), :])

            _chunks(n, ch, fill)
        for so, po, n in segs:
            def conv(r, so=so, po=po):
                o_ref[pl.ds(so + r, ch), :] = _taps(pad, wv, (r, po), ch, CONV_K) + b_ref[...]

            _chunks(n, ch, conv)

    return _call(
        body, name="conv_fwd", grid=(nj, B),
        in_specs=[pl.BlockSpec((R, cw), lambda j, b: (b, j)), pl.BlockSpec((R, cw), lambda j, b: (b, nj + j)),
                  pl.BlockSpec((32, cw), lambda j, b: (0, j)), pl.BlockSpec((1, cw), lambda j, b: (0, j))],
        out_specs=pl.BlockSpec((R, cw), lambda j, b: (b, j)), out_shape=_sds((B * R, CONV_CH), F32),
        scratch=[pltpu.VMEM((_padded_rows(C, S), cw), F32)])(z_conv, z_conv, w, bias)


def _conv_bwd(dhc, z_conv, w, B, C, S, cw=128, ch=128):
    R = C + S
    nj = CONV_CH // cw
    segs = _segments(C, S)

    def body(d_ref, a_ref, g_ref, w_ref, da_ref, dg_ref, dw_ref, sa_ref, sg_ref, gpad, dpad, dwacc):
        b = pl.program_id(1)

        @pl.when(b == 0)
        def _():
            dw_ref[...] = jnp.zeros_like(dw_ref)
            sa_ref[...] = jnp.zeros_like(sa_ref)
            sg_ref[...] = jnp.zeros_like(sg_ref)

        _zero_gaps(gpad, C, S)
        _zero_gaps(dpad, C, S)
        dwacc[...] = jnp.zeros_like(dwacc)
        wv = w_ref[...]
        for so, po, n in segs:
            def fill(r, so=so, po=po):
                gpad[pl.ds(po + r, ch), :] = a_ref[pl.ds(so + r, ch), :] * _sigmoid(g_ref[pl.ds(so + r, ch), :])
                dpad[pl.ds(po + r, ch), :] = d_ref[pl.ds(so + r, ch), :]

            _chunks(n, ch, fill)
        for so, po, n in segs:
            def step(r, so=so, po=po):
                _tap_grads(dwacc, dpad[pl.ds(po + r, ch), :], gpad, (r, po), ch, CONV_K)
                dglu = _taps(dpad, wv, (r, po), ch, CONV_K, flip=True)
                av = a_ref[pl.ds(so + r, ch), :]
                sg = _sigmoid(g_ref[pl.ds(so + r, ch), :])
                da = dglu * sg
                dg = dglu * av * sg * (1.0 - sg)
                da_ref[pl.ds(so + r, ch), :] = da.astype(MXU)
                dg_ref[pl.ds(so + r, ch), :] = dg.astype(MXU)
                sa_ref[...] += _colsum(da)
                sg_ref[...] += _colsum(dg)

            _chunks(n, ch, step)
        for k in range(CONV_K):
            dw_ref[k:k + 1, :] += _colsum(dwacc[k])

    blk = pl.BlockSpec((R, cw), lambda j, b: (b, j))
    acc1 = pl.BlockSpec((1, cw), lambda j, b: (0, j))
    return _call(
        body, name="conv_bwd", grid=(nj, B),
        in_specs=[blk, blk, pl.BlockSpec((R, cw), lambda j, b: (b, nj + j)), pl.BlockSpec((32, cw), lambda j, b: (0, j))],
        out_specs=[blk, blk, pl.BlockSpec((32, cw), lambda j, b: (0, j)), acc1, acc1],
        out_shape=[_sds((B * R, CONV_CH), MXU), _sds((B * R, CONV_CH), MXU), _sds((32, CONV_CH), F32),
                   _sds((1, CONV_CH), F32), _sds((1, CONV_CH), F32)],
        scratch=[pltpu.VMEM((_padded_rows(C, S), cw), F32), pltpu.VMEM((_padded_rows(C, S), cw), F32),
                 pltpu.VMEM((32, 8, cw), F32)])(dhc, z_conv, z_conv, w)


def _ffn_mid(up, w, bias, B, C, S, cw=256, ch=64, carry=None):
    R = C + S
    nj = D_FF // cw
    segs = _segments(C, S)
    car = _Carried(carry, 4, 2, 1)

    def body(*refs):
        (a_ref, u_ref, w_ref, b_ref), (f_ref, ft_ref), (pad,) = car.split(refs)[0]
        parts = car.split(refs)[1]
        step = pl.program_id(0) * B + pl.program_id(1)
        car.before(step, parts)
        for h in range(cw // LANE):
            _zero_gaps(pad.at[h], C, S)
        wv = w_ref[...]
        for so, po, n in segs:
            def fill(r, so=so, po=po):
                for h in range(cw // LANE):
                    pad[h, pl.ds(po + r, ch), :] = a_ref[pl.ds(so + r, ch), h * LANE:(h + 1) * LANE]

            _chunks(n, ch, fill)
        for so, po, n in segs:
            def conv(r, so=so, po=po):
                ac = jnp.concatenate([_taps(pad.at[h], wv[:, h * LANE:(h + 1) * LANE], (r, po), ch, FFN_K)
                                      for h in range(cw // LANE)], axis=1) + b_ref[...]
                f_ref[pl.ds(so + r, ch), :] = (ac * _sigmoid(ac) * u_ref[pl.ds(so + r, ch), :]).astype(MXU)

            _chunks(n, ch, conv)
        ft_ref[...] = f_ref[...].T
        car.after(step, nj * B, parts)

    res = _call(
        body, name="ffn_mid", grid=(nj, B),
        in_specs=[pl.BlockSpec((R, cw), lambda j, b: (b, j)), pl.BlockSpec((R, cw), lambda j, b: (b, nj + j)),
                  pl.BlockSpec((8, cw), lambda j, b: (0, j)), pl.BlockSpec((1, cw), lambda j, b: (0, j))] + car.in_specs(),
        out_specs=[pl.BlockSpec((R, cw), lambda j, b: (b, j)), pl.BlockSpec((cw, R), lambda j, b: (j, b))] + car.out_specs(),
        out_shape=[_sds((B * R, D_FF), MXU), _sds((D_FF, B * R), MXU)] + list(carry.out_shape if carry else []),
        scratch=[pltpu.VMEM((cw // LANE, _padded_rows(C, S), LANE), F32)] + list(carry.scratch if carry else []),
    )(up, up, w, bias, *(carry.inputs if carry else []))
    return res[0], res[1], res[2:]


def _ffn_mid_bwd(df, up, w, bias, B, C, S, cw=128, ch=128, carry=None):
    R = C + S
    nj = D_FF // cw
    segs = _segments(C, S)
    car = _Carried(carry, 5, 4, 3)

    def body(*refs):
        (d_ref, a_ref, u_ref, w_ref, b_ref), (da_ref, du_ref, dw_ref, db_ref), (apad, dpad, dwacc) = car.split(refs)[0]
        parts = car.split(refs)[1]
        b = pl.program_id(1)
        step = pl.program_id(0) * B + b
        car.before(step, parts)

        @pl.when(b == 0)
        def _():
            dw_ref[...] = jnp.zeros_like(dw_ref)
            db_ref[...] = jnp.zeros_like(db_ref)

        _zero_gaps(apad, C, S)
        _zero_gaps(dpad, C, S)
        dwacc[...] = jnp.zeros_like(dwacc)
        wv = w_ref[...]
        for so, po, n in segs:
            def fill(r, so=so, po=po):
                apad[pl.ds(po + r, ch), :] = a_ref[pl.ds(so + r, ch), :]

            _chunks(n, ch, fill)
        for so, po, n in segs:
            def first(r, so=so, po=po):
                ac = _taps(apad, wv, (r, po), ch, FFN_K) + b_ref[...]
                sg = _sigmoid(ac)
                dfv = d_ref[pl.ds(so + r, ch), :]
                du_ref[pl.ds(so + r, ch), :] = (dfv * ac * sg).astype(MXU)
                dac = dfv * u_ref[pl.ds(so + r, ch), :] * (sg * (1.0 + ac * (1.0 - sg)))
                dpad[pl.ds(po + r, ch), :] = dac
                db_ref[...] += _colsum(dac)

            _chunks(n, ch, first)
        for so, po, n in segs:
            def second(r, so=so, po=po):
                _tap_grads(dwacc, dpad[pl.ds(po + r, ch), :], apad, (r, po), ch, FFN_K)
                da_ref[pl.ds(so + r, ch), :] = _taps(dpad, wv, (r, po), ch, FFN_K, flip=True).astype(MXU)

            _chunks(n, ch, second)
        for k in range(FFN_K):
            dw_ref[k:k + 1, :] += _colsum(dwacc[k])
        car.after(step, nj * B, parts)

    blk = pl.BlockSpec((R, cw), lambda j, b: (b, j))
    res = _call(
        body, name="ffn_mid_bwd", grid=(nj, B),
        in_specs=[blk, blk, pl.BlockSpec((R, cw), lambda j, b: (b, nj + j)), pl.BlockSpec((8, cw), lambda j, b: (0, j)),
                  pl.BlockSpec((1, cw), lambda j, b: (0, j))] + car.in_specs(),
        out_specs=[blk, blk, pl.BlockSpec((8, cw), lambda j, b: (0, j)), pl.BlockSpec((1, cw), lambda j, b: (0, j))]
        + car.out_specs(),
        out_shape=[_sds((B * R, D_FF), MXU), _sds((B * R, D_FF), MXU), _sds((8, D_FF), F32), _sds((1, D_FF), F32)]
        + list(carry.out_shape if carry else []),
        scratch=[pltpu.VMEM((_padded_rows(C, S), cw), F32), pltpu.VMEM((_padded_rows(C, S), cw), F32),
                 pltpu.VMEM((8, 8, cw), F32)] + list(carry.scratch if carry else []),
    )(df, up, up, w, bias, *(carry.inputs if carry else []))
    return res[0], res[1], res[2], res[3], res[4:]


def _view_sum(win, lo, hi):
    acc = win.at(lo)
    for off in range(lo + 1, hi + 1):
        acc = acc + win.at(off)
    return acc


def _window_count(r, ch, n, w):
    t = r + lax.broadcasted_iota(jnp.int32, (ch, 1), 0)
    return (jnp.minimum(t + w // 2, n) - jnp.maximum(t - w // 2, 0)).astype(F32)


def _pool_fwd(z_pool, pool_w, pool_scale, B, C, S, ch=128):
    R = C + S
    gch = POOL_GCH
    segs = _segments(C, S)

    def body(u_ref, pw_ref, sc_ref, pooled_ref, po_ref, pad):
        g = pl.program_id(1)
        for h in range(gch // LANE):
            _zero_gaps(pad.at[h], C, S)
        for so, po, n in segs:
            def fill(r, so=so, po=po):
                for h in range(gch // LANE):
                    pad[h, pl.ds(po + r, ch), :] = u_ref[pl.ds(so + r, ch), h * LANE:(h + 1) * LANE]

            _chunks(n, ch, fill)
        for gi, w in enumerate(POOL_WINDOWS):
            @pl.when(g == gi)
            def _(w=w):
                for so, po, n in segs:
                    def step(r, so=so, po=po, n=n):
                        cnt = _window_count(r, ch, n, w)
                        halves = []
                        for h in range(gch // LANE):
                            win = _Window(pad.at[h], (r, po), ch)
                            halves.append(_view_sum(win, -(w // 2), w // 2 - 1) / cnt - win.at(0))
                        pooled = jnp.concatenate(halves, axis=1).astype(MXU)
                        pooled_ref[pl.ds(so + r, ch), :] = pooled
                        po_ref[pl.ds(so + r, ch), :] = _dot(pooled, pw_ref[0], NN) * sc_ref[...]

                    _chunks(n, ch, step)

    blk = pl.BlockSpec((R, gch), lambda b, g: (b, g))
    return _call(
        body, name="pool_fwd", grid=(B, len(POOL_WINDOWS)),
        in_specs=[blk, pl.BlockSpec((1, gch, gch), lambda b, g: (g, 0, 0)), pl.BlockSpec((1, gch), lambda b, g: (0, g))],
        out_specs=[blk, blk], out_shape=[_sds((B * R, POOL_CH), MXU), _sds((B * R, POOL_CH), F32)],
        scratch=[pltpu.VMEM((gch // LANE, _padded_rows(C, S), LANE), F32)])(z_pool, pool_w, pool_scale)


def _pool_bwd(dpo, pooled, pool_w, pool_scale, B, C, S, ch=128):
    R = C + S
    gch = POOL_GCH
    segs = _segments(C, S)

    def body(d_ref, p_ref, pw_ref, sc_ref, du_ref, dpw_ref, dsc_ref, su_ref, qpad, dpl):
        g, b = pl.program_id(0), pl.program_id(1)

        @pl.when(b == 0)
        def _():
            dpw_ref[...] = jnp.zeros_like(dpw_ref)
            dsc_ref[...] = jnp.zeros_like(dsc_ref)
            su_ref[...] = jnp.zeros_like(su_ref)

        for h in range(gch // LANE):
            _zero_gaps(qpad.at[h], C, S)
        pw = pw_ref[0]
        for gi, w in enumerate(POOL_WINDOWS):
            @pl.when(g == gi)
            def _(w=w):
                for so, po, n in segs:
                    def first(r, so=so, po=po, n=n):
                        pv = p_ref[pl.ds(so + r, ch), :]
                        dv = d_ref[pl.ds(so + r, ch), :]
                        dsc_ref[...] += _colsum(dv * _dot(pv, pw, NN))
                        dmx = (dv * sc_ref[...]).astype(MXU)
                        dpw_ref[0] += _dot(pv, dmx, TN)
                        dp = _dot(dmx, pw, NT)
                        dpl[pl.ds(so + r, ch), :] = dp
                        q = dp / _window_count(r, ch, n, w)
                        for h in range(gch // LANE):
                            qpad[h, pl.ds(po + r, ch), :] = q[:, h * LANE:(h + 1) * LANE]

                    _chunks(n, ch, first)
                for so, po, n in segs:
                    def second(r, so=so, po=po):
                        acc = jnp.concatenate([_view_sum(_Window(qpad.at[h], (r, po), ch), 1 - w // 2, w // 2)
                                               for h in range(gch // LANE)], axis=1)
                        du = acc - dpl[pl.ds(so + r, ch), :]
                        du_ref[pl.ds(so + r, ch), :] = du.astype(MXU)
                        su_ref[...] += _colsum(du)

                    _chunks(n, ch, second)

    blk = pl.BlockSpec((R, gch), lambda g, b: (b, g))
    vec = pl.BlockSpec((1, gch), lambda g, b: (0, g))
    wblk = pl.BlockSpec((1, gch, gch), lambda g, b: (g, 0, 0))
    return _call(
        body, name="pool_bwd", grid=(len(POOL_WINDOWS), B), in_specs=[blk, blk, wblk, vec], out_specs=[blk, wblk, vec, vec],
        out_shape=[_sds((B * R, POOL_CH), MXU), _sds((len(POOL_WINDOWS), gch, gch), F32), _sds((1, POOL_CH), F32),
                   _sds((1, POOL_CH), F32)],
        scratch=[pltpu.VMEM((gch // LANE, _padded_rows(C, S), LANE), F32), pltpu.VMEM((R, gch), F32)],
    )(dpo, pooled, pool_w, pool_scale)


def _silu_rows(cond):
    def body(c_ref, s_ref, d_ref):
        c = c_ref[...]
        sg = _sigmoid(c)
        s_ref[...] = (c * sg).astype(MXU)
        d_ref[...] = sg * (1.0 + c * (1.0 - sg))

    full = pl.BlockSpec(cond.shape, lambda i: (0, 0))
    return _call(body, name="silu_rows", grid=(1,), in_specs=[full], out_specs=[full, full],
                 out_shape=[_sds(cond.shape, MXU), _sds(cond.shape, F32)])(cond)


def _row_tile(rows, cols, n_bufs):
    cap = max(16, (16 * 2 ** 20) // (4 * n_bufs * max(cols, 128)))
    return rows if rows <= cap else _pick(rows, cap, 16)


def _adamw(parts, w, m, v, layer, prev, *, name):
    n_parts, rows, cols = parts.shape
    layers = w.shape[0]
    c1 = 1.0 - ADAM_B1 ** ADAM_STEP
    c2 = 1.0 - ADAM_B2 ** ADAM_STEP
    tr = _row_tile(rows, cols, n_parts + 7)

    def body(p_ref, w_ref, m_ref, v_ref, *rest):
        g_ref, d_ref, nm_ref, nv_ref = rest[-4:]
        g = p_ref[0].astype(F32)
        for k in range(1, n_parts):
            g = g + p_ref[k].astype(F32)
        nm = ADAM_B1 * m_ref[...] + (1.0 - ADAM_B1) * g
        nv = ADAM_B2 * v_ref[...] + (1.0 - ADAM_B2) * (g * g)
        g_ref[...] = g
        nm_ref[...] = nm
        nv_ref[...] = nv
        d_ref[...] = -ADAM_LR * ((nm / c1) / (jnp.sqrt(nv / c2) + ADAM_EPS) + ADAM_WD * w_ref[...])

    blk = pl.BlockSpec((None, tr, cols), lambda i: (layer, i, 0))
    in_specs = [pl.BlockSpec((n_parts, tr, cols), lambda i: (0, i, 0)), blk, blk, blk]
    args = [parts, w, m, v]
    aliases = {}
    if prev is not None:
        in_specs += [ANY] * 4
        aliases = {4 + k: k for k in range(4)}
        args += list(prev)
    return _call(body, name=name, grid=(rows // tr,), in_specs=in_specs, out_specs=[blk] * 4,
                 out_shape=[_sds((layers, rows, cols), F32)] * 4, aliases=aliases)(*args)


def _pair_sum(g, r1):
    _, rows, cols = g.shape
    c = lax.axis_index("c")
    g4 = g.reshape(4, 2, rows, cols)
    tr = _row_tile(rows, cols, 10)

    def body(c_ref, g_ref, r_ref, o_ref):
        o_ref[...] = (g_ref[...] + r_ref[...]).astype(WIRE)

    return _pcall(
        body, name="pair_sum", out_shape=_sds((4, rows, cols), WIRE),
        grid_spec=pltpu.PrefetchScalarGridSpec(
            num_scalar_prefetch=1, grid=(rows // tr,),
            in_specs=[pl.BlockSpec((4, None, tr, cols), lambda i, c_ref: (0, c_ref[0], i, 0)),
                      pl.BlockSpec((4, tr, cols), lambda i, c_ref: (0, i, 0))],
            out_specs=pl.BlockSpec((4, tr, cols), lambda i, c_ref: (0, i, 0))),
        compiler_params=pltpu.CompilerParams(dimension_semantics=("arbitrary",), vmem_limit_bytes=VMEM_MB * 2 ** 20),
    )(jnp.reshape(c, (1,)).astype(jnp.int32), g4, r1)


MESH = pl.DeviceIdType.MESH
ANY = pl.BlockSpec(memory_space=pl.ANY)


class _Exchange:
    inputs, out_shape, scratch = (), (), ()

    def start(self, ins, outs, sems):
        raise NotImplementedError

    def mid(self, ins, outs, sems):
        pass

    def finish(self, ins, outs, sems):
        raise NotImplementedError


def _run_exchange(ex, *, name):
    n_in, n_out = len(ex.inputs), len(ex.out_shape)

    def body(*refs):
        parts = refs[:n_in], refs[n_in:n_in + n_out], refs[n_in + n_out:]
        ex.start(*parts)
        ex.mid(*parts)
        ex.finish(*parts)

    return _pcall(body, name=name, out_shape=list(ex.out_shape), in_specs=[ANY] * n_in, out_specs=[ANY] * n_out,
                  scratch_shapes=list(ex.scratch))(*ex.inputs)


class _Gather(_Exchange):
    def __init__(self, shards, layer=None):
        self.inputs, self.layer, self.n_t = list(shards), layer, len(shards)
        self.out_shape = [_sds((N_DEV,) + tuple(s.shape if layer is None else s.shape[1:]), s.dtype) for s in shards]
        self.scratch = [pltpu.SemaphoreType.DMA((7 * self.n_t,)), pltpu.SemaphoreType.DMA((7 * self.n_t,)),
                        pltpu.SemaphoreType.DMA((self.n_t,))]

    def _place(self):
        x, y, c = lax.axis_index("x"), lax.axis_index("y"), lax.axis_index("c")
        return (x, y, c), (x, y, 1 - c), [(1 - x, y), (x, 1 - y), (1 - x, 1 - y)]

    def _own(self, ins, n):
        return ins[n] if self.layer is None else ins[n].at[self.layer]

    def _copy(self, ins, outs, sems, n, k, blk, to, own=False):
        dst = outs[n].at[4 * blk[0] + 2 * blk[1] + blk[2]]
        return pltpu.make_async_remote_copy(src_ref=self._own(ins, n) if own else dst, dst_ref=dst, send_sem=sems[0].at[n * 7 + k],
                                            recv_sem=sems[1].at[n * 7 + k], device_id=to, device_id_type=MESH)

    def _mine(self, ins, outs, sems, n):
        (x, y, c), _, _ = self._place()
        return pltpu.make_async_copy(self._own(ins, n), outs[n].at[4 * x + 2 * y + c], sems[2].at[n])

    def start(self, ins, outs, sems):
        me, sibling, chips = self._place()
        for n in range(self.n_t):
            self._mine(ins, outs, sems, n).start()
        for j, chip in enumerate(chips):
            for n in range(self.n_t):
                self._copy(ins, outs, sems, n, 1 + j, me, (*chip, me[2]), own=True).start()
        for n in range(self.n_t):
            self._copy(ins, outs, sems, n, 0, me, sibling, own=True).start()

    def mid(self, ins, outs, sems):
        me, sibling, chips = self._place()
        for j, chip in enumerate(chips):
            for n in range(self.n_t):
                self._copy(ins, outs, sems, n, 1 + j, (*chip, me[2]), me).wait_recv()
                self._copy(ins, outs, sems, n, 4 + j, (*chip, me[2]), sibling).start()

    def finish(self, ins, outs, sems):
        me, sibling, chips = self._place()
        for n in range(self.n_t):
            self._copy(ins, outs, sems, n, 0, sibling, me).wait_recv()
        for j, chip in enumerate(chips):
            for n in range(self.n_t):
                self._copy(ins, outs, sems, n, 4 + j, (*chip, 1 - me[2]), me).wait_recv()
        for j, chip in enumerate(chips):
            for n in range(self.n_t):
                self._copy(ins, outs, sems, n, 1 + j, me, (*chip, me[2]), own=True).wait_send()
                self._copy(ins, outs, sems, n, 4 + j, (*chip, me[2]), sibling).wait_send()
        for n in range(self.n_t):
            self._copy(ins, outs, sems, n, 0, me, sibling, own=True).wait_send()
            self._mine(ins, outs, sems, n).wait()


def _all_gather(shards, *, name, layer=None):
    return _run_exchange(_Gather(shards, layer), name=name)


class _Both(_Exchange):
    def __init__(self, a, b):
        self.a, self.b = a, b
        self.inputs = list(a.inputs) + list(b.inputs)
        self.out_shape = list(a.out_shape) + list(b.out_shape)
        self.scratch = list(a.scratch) + list(b.scratch)

    def _parts(self, ins, outs, sems):
        i, o, s = len(self.a.inputs), len(self.a.out_shape), len(self.a.scratch)
        return (ins[:i], outs[:o], sems[:s]), (ins[i:], outs[o:], sems[s:])

    def start(self, ins, outs, sems):
        pa, pb = self._parts(ins, outs, sems)
        self.a.start(*pa)
        self.b.start(*pb)

    def mid(self, ins, outs, sems):
        pa, pb = self._parts(ins, outs, sems)
        self.a.mid(*pa)
        self.b.mid(*pb)

    def finish(self, ins, outs, sems):
        pa, pb = self._parts(ins, outs, sems)
        self.a.finish(*pa)
        self.b.finish(*pb)


class _SiblingExchange(_Exchange):
    def __init__(self, gs):
        self.n_t = len(gs)
        self.inputs = [g.reshape((4, 2) + g.shape[1:]) for g in gs]
        self.out_shape = [_sds((4,) + g.shape[1:], g.dtype) for g in gs]
        self.scratch = [pltpu.SemaphoreType.DMA((self.n_t,)), pltpu.SemaphoreType.DMA((self.n_t,))]

    def _copy(self, ins, outs, sems, n):
        x, y, c = lax.axis_index("x"), lax.axis_index("y"), lax.axis_index("c")
        return pltpu.make_async_remote_copy(src_ref=ins[n].at[:, 1 - c], dst_ref=outs[n], send_sem=sems[0].at[n],
                                            recv_sem=sems[1].at[n], device_id=(x, y, 1 - c), device_id_type=MESH)

    def start(self, ins, outs, sems):
        for n in range(self.n_t):
            self._copy(ins, outs, sems, n).start()

    def finish(self, ins, outs, sems):
        for n in range(self.n_t):
            self._copy(ins, outs, sems, n).wait_recv()
        for n in range(self.n_t):
            self._copy(ins, outs, sems, n).wait_send()


class _ChipExchange(_Exchange):
    def __init__(self, ps):
        self.n_t, self.inputs = len(ps), list(ps)
        self.out_shape = [_sds(p.shape, p.dtype) for p in ps]
        self.scratch = [pltpu.SemaphoreType.DMA((3 * self.n_t,)), pltpu.SemaphoreType.DMA((3 * self.n_t,)),
                        pltpu.SemaphoreType.DMA((self.n_t,))]

    def _place(self):
        x, y, c = lax.axis_index("x"), lax.axis_index("y"), lax.axis_index("c")
        return 2 * x + y, c, [(1 - x, y), (x, 1 - y), (1 - x, 1 - y)]

    def _copy(self, ins, outs, sems, n, j, src_chip, dst_slot):
        _, c, chips = self._place()
        return pltpu.make_async_remote_copy(
            src_ref=ins[n].at[src_chip], dst_ref=outs[n].at[dst_slot], send_sem=sems[0].at[n * 3 + j],
            recv_sem=sems[1].at[n * 3 + j], device_id=(*chips[j], c), device_id_type=MESH)

    def _own(self, ins, outs, sems, n):
        mine, _, _ = self._place()
        return pltpu.make_async_copy(ins[n].at[mine], outs[n].at[mine], sems[2].at[n])

    def start(self, ins, outs, sems):
        mine, _, chips = self._place()
        for n in range(self.n_t):
            self._own(ins, outs, sems, n).start()
        for j, (px, py) in enumerate(chips):
            for n in range(self.n_t):
                self._copy(ins, outs, sems, n, j, 2 * px + py, mine).start()

    def finish(self, ins, outs, sems):
        mine, _, chips = self._place()
        for j, (px, py) in enumerate(chips):
            for n in range(self.n_t):
                self._copy(ins, outs, sems, n, j, mine, 2 * px + py).wait_recv()
        for j, (px, py) in enumerate(chips):
            for n in range(self.n_t):
                self._copy(ins, outs, sems, n, j, 2 * px + py, mine).wait_send()
        for n in range(self.n_t):
            self._own(ins, outs, sems, n).wait()


BIG = (("w_ada", (D, N_MOD * D // N_DEV), 1), ("w_in", (D, D_IN // N_DEV), 1), ("conv_pw_w", (CONV_CH // N_DEV, D), 0),
       ("pool_w", (len(POOL_WINDOWS), POOL_GCH // N_DEV, POOL_GCH), 1), ("w_out", (D // N_DEV, D), 0),
       ("w_up", (D, 2 * D_FF // N_DEV), 1), ("w_down", (D_FF // N_DEV, D), 0))
TAPS = (("conv_dw_w", (CONV_K, CONV_CH // N_DEV), 1), ("ffn_dw_w", (FFN_K, D_FF // N_DEV), 1))
SHARDED = BIG + TAPS
EARLY = (0, 1)
LATE = (2, 3, 4, 5, 6)
REPLICATED =(("b_ada", N_MOD * D), ("b_in", D_IN), ("q_gain", HD), ("k_gain", HD), ("conv_dw_b", CONV_CH), ("conv_ln_g", CONV_CH),
              ("conv_ln_b", CONV_CH), ("conv_pw_b", D), ("pool_scale", POOL_CH), ("b_out", D), ("ln1_g", D), ("ln1_b", D),
              ("ln2_g", D), ("ln2_b", D), ("ffn_dw_b", D_FF))


def _as_rows(shape):
    return (int(np.prod(shape[:-1])), shape[-1])


def _full_from_blocks(blocks, axis):
    moved = jnp.moveaxis(blocks, 0, axis)
    shape = list(moved.shape)
    shape[axis:axis + 2] = [shape[axis] * shape[axis + 1]]
    return moved.reshape(shape)


def _blocks_from_full(full, axis):
    shape = list(full.shape)
    shape[axis:axis + 1] = [N_DEV, shape[axis] // N_DEV]
    return jnp.moveaxis(full.reshape(shape), axis, 0)


SMALL_N = DEPTH * sum(n for _, n in REPLICATED) + D
SMALL_ROWS = -(-(SMALL_N + 1) // (8 * LANES)) * 8


def _rope_tables(C, S):
    t = np.arange(S)
    inv_freq = ROPE_THETA ** (-np.arange(ROPE_PAIRS, dtype=np.float32) / ROPE_PAIRS)
    row = jnp.asarray((t // GRID_W).astype(np.float32))[:, None] * jnp.asarray(inv_freq, F32)
    col = jnp.asarray((t % GRID_W).astype(np.float32))[:, None] * jnp.asarray(inv_freq, F32)
    cos = jnp.concatenate([jnp.cos(row), jnp.cos(row), jnp.cos(col), jnp.cos(col)], axis=1)
    sin = jnp.concatenate([-jnp.sin(row), jnp.sin(row), -jnp.sin(col), jnp.sin(col)], axis=1)
    cos = jnp.concatenate([jnp.ones((C, HD), F32), cos], axis=0)
    sin = jnp.concatenate([jnp.zeros((C, HD), F32), sin], axis=0)
    return cos, sin


def _segment_sums(parts, B, tpe, ncq):
    p = parts.reshape(B, tpe, D)
    return jnp.concatenate([jnp.sum(p[:, ncq:], axis=1), jnp.sum(p[:, :ncq], axis=(0, 1))[None]], axis=0)


def kernel(x, c, ctx, c_ctx, w_ada, b_ada, w_in, b_in, q_gain, k_gain, conv_dw_w, conv_dw_b, conv_ln_g, conv_ln_b, conv_pw_w, conv_pw_b, pool_w, pool_scale, w_out, b_out, ln1_g, ln1_b, ln2_g, ln2_b, w_up, ffn_dw_w, ffn_dw_b, w_down, loss_target, m_c_ctx, m_w_ada, m_b_ada, m_w_in, m_b_in, m_q_gain, m_k_gain, m_conv_dw_w, m_conv_dw_b, m_conv_ln_g, m_conv_ln_b, m_conv_pw_w, m_conv_pw_b, m_pool_w, m_pool_scale, m_w_out, m_b_out, m_ln1_g, m_ln1_b, m_ln2_g, m_ln2_b, m_w_up, m_ffn_dw_w, m_ffn_dw_b, m_w_down, v_c_ctx, v_w_ada, v_b_ada, v_w_in, v_b_in, v_q_gain, v_k_gain, v_conv_dw_w, v_conv_dw_b, v_conv_ln_g, v_conv_ln_b, v_conv_pw_w, v_conv_pw_b, v_pool_w, v_pool_scale, v_w_out, v_b_out, v_ln1_g, v_ln1_b, v_ln2_g, v_ln2_b, v_w_up, v_ffn_dw_w, v_ffn_dw_b, v_w_down):
    given = dict(locals())
    B, S, _ = x.shape
    C = ctx.shape[1]
    R = C + S
    T = B * R
    tpe, ncq = R // TM, C // TM
    nt = T // TM
    assert S % TM == 0 and C % TM == 0 and B + 1 <= 16

    operands = [given[n].astype(MXU) for n, _, _ in BIG]
    taps = _all_gather([given[n] for n, _, _ in TAPS], name="gather_taps")
    W, modt = [], []

    xu = jnp.concatenate([ctx, x], axis=1).reshape(T, D)
    cond = jnp.concatenate([c, c_ctx[None], jnp.zeros((16 - B - 1, D), F32)], axis=0)
    s_cond, ds_cond = _silu_rows(cond)
    ctx_tile = jnp.asarray((np.arange(tpe) < ncq)[None, :, None])
    cos_t, sin_t = _rope_tables(C, S)
    row = lambda v: v.reshape(1, -1)

    early_ops, late_ops = [operands[k] for k in EARLY], [operands[k] for k in LATE]

    def add_early(blocks):
        l = len(W)
        wl = {BIG[k][0]: _full_from_blocks(blk, BIG[k][2]) for k, blk in zip(EARLY, blocks)}
        for (n, _, a), blk in zip(TAPS, taps):
            wl[n] = _full_from_blocks(blk[:, l], a)
        W.append(wl)
        m = _matmul(s_cond, wl["w_ada"], "nn", name="ada", bias=row(b_ada[l]), tm=16, tn=1024)
        modt.append(jnp.where(ctx_tile, m[B][None, None, :], m[:B][:, None, :]).reshape(nt, 1, N_MOD * D))

    def add_late(l, blocks):
        for k, blk in zip(LATE, blocks):
            W[l][BIG[k][0]] = _full_from_blocks(blk, BIG[k][2])

    add_early(_all_gather(early_ops, name="gather_weights", layer=0))

    saved = []
    h1, h1t = _modulate_cast(xu, modt[0], 0)
    xin = xu
    for l in range(DEPTH):
        wl = W[l]
        w_inl, b_inl = wl["w_in"], b_in[l]
        z_qkv = _matmul(h1, w_inl[:, :QKV_W], "nn", name="z_qkv", bias=row(b_inl[:QKV_W]), tn=768)
        c0, p0, g0 = QKV_W, QKV_W + 2 * CONV_CH, QKV_W + 2 * CONV_CH + POOL_CH
        z_conv = _matmul(h1, w_inl[:, c0:p0], "nn", name="z_conv", bias=row(b_inl[c0:p0]))
        z_pool = _matmul(h1, w_inl[:, p0:g0], "nn", name="z_pool", bias=row(b_inl[p0:g0]))
        z_gate = _matmul(h1, w_inl[:, g0:], "nn", name="z_gate", bias=row(b_inl[g0:]))
        qkv = _qk_prep(z_qkv, cos_t, sin_t, row(q_gain[l]), row(k_gain[l]), tpe)
        if l + 1 < DEPTH:
            if l == 0:
                attn, res = _attn_fwd(qkv, B, C, R, carry=_Both(_Gather(early_ops, 1), _Gather(late_ops, 0)))
                add_late(0, res[len(EARLY):])
            else:
                attn, res = _attn_fwd(qkv, B, C, R, carry=_Gather(early_ops, l + 1))
            add_early(res[:len(EARLY)])
        else:
            attn, _ = _attn_fwd(qkv, B, C, R)
        dw32 =jnp.pad(wl["conv_dw_w"], ((0, 32 - CONV_K), (0, 0)))
        hc = _conv_fwd(z_conv, dw32, row(conv_dw_b[l]), B, C, S)
        sw, swt = _ln_silu(hc, row(conv_ln_g[l]), row(conv_ln_b[l]))
        conv_o = _matmul(sw, wl["conv_pw_w"], "nn", name="conv_pw", bias=row(conv_pw_b[l]))
        pooled, pool_o = _pool_fwd(z_pool, wl["pool_w"], row(pool_scale[l]), B, C, S)
        m, mt = _merge(attn, conv_o, pool_o, z_gate)
        mo = _matmul(m, wl["w_out"], "nn", name="w_out", bias=row(b_out[l]))
        y1, h2, h2t = _resid_ln(xin, mo, modt[l], 2, row(ln1_g[l]), row(ln1_b[l]), modt[l], 3)
        up = _matmul(h2, wl["w_up"], "nn", name="w_up", tn=1408)
        fw8 = jnp.pad(wl["ffn_dw_w"], ((0, 8 - FFN_K), (0, 0)))
        if l + 1 < DEPTH:
            f, ft, late_blocks = _ffn_mid(up, fw8, row(ffn_dw_b[l]), B, C, S, carry=_Gather(late_ops, l + 1))
            add_late(l + 1, late_blocks)
        else:
            f, ft, _ = _ffn_mid(up, fw8, row(ffn_dw_b[l]), B, C, S)
        fo = _matmul(f, wl["w_down"], "nn", name="w_down", tm=512)
        if l + 1 < DEPTH:
            y2, h_next, ht_next = _resid_ln(y1, fo, modt[l], 5, row(ln2_g[l]), row(ln2_b[l]), modt[l + 1], 0)
        else:
            y2, h_next, ht_next = _resid_ln(y1, fo, modt[l], 5, row(ln2_g[l]), row(ln2_b[l])), None, None
        saved.append(dict(xin=xin, h1t=h1t, z_qkv=z_qkv, z_conv=z_conv, z_gate=z_gate, qkv=qkv, attn=attn, hc=hc, swt=swt,
                          conv_o=conv_o, pooled=pooled, pool_o=pool_o, mt=mt, mo=mo, y1=y1, h2t=h2t, up=up, ft=ft, fo=fo,
                          dw32=dw32, fw8=fw8))
        xin, h1, h1t = y2, h_next, ht_next

    dy, loss_part = _loss_grad(xin, loss_target.reshape(B * S, D), tpe, ncq)

    small = {n: [None] * DEPTH for n, _ in REPLICATED}
    d_c_ctx = jnp.zeros((D,), F32)
    dmods_t = [[None] * N_MOD for _ in range(DEPTH)]
    layer_grads = [None] * DEPTH
    kinds = ("grad_", "delta_", "new_m_", "new_v_")
    stacks = {n: [given[pre + n].reshape((DEPTH,) + _as_rows(s)) for pre in ("", "m_", "v_")] for n, s, _ in SHARDED}
    results = {n: None for n, _, _ in SHARDED}

    def reduce_prepare(l):
        nonlocal d_c_ctx
        dmods = jnp.concatenate([_segment_sums(p, B, tpe, ncq) for p in dmods_t[l]], axis=1)
        small["b_ada"][l] = jnp.sum(dmods, axis=0)
        dm16 = jnp.concatenate([dmods, jnp.zeros((16 - B - 1, N_MOD * D), F32)], axis=0).astype(MXU)
        layer_grads[l]["w_ada"] = _matmul(s_cond, dm16, "tn", name="g_w_ada", tm=1024, tn=1024)
        dcond = _matmul(dm16, W[l]["w_ada"], "nt", name="d_cond", tm=16, tn=1024, tk=2048)
        d_c_ctx = d_c_ctx + dcond[B] * ds_cond[B]
        return [_blocks_from_full(layer_grads[l][n], a).reshape((N_DEV,) + _as_rows(s)) for n, s, a in SHARDED]

    def reduce_end(l, r2):
        for (n, _, _), parts in zip(SHARDED, r2):
            results[n] = _adamw(parts, *stacks[n], l, results[n], name="adamw_sharded")

    dh1 = None
    for l in reversed(range(DEPTH)):
        gs_above = None
        wl, sv = W[l], saved[l]
        dmod = dmods_t[l]
        if dh1 is None:
            dy1p, dfo, dgate2, dg, db, _ = _ln_bwd(dy, sv["y1"], sv["fo"], modt[l], 5, row(ln2_g[l]))
        else:
            dy1p, dfo, dgate2, dg, db, _, dsh, dsc = _ln_bwd(dy, sv["y1"], sv["fo"], modt[l], 5, row(ln2_g[l]), dh=dh1,
                                                             y=saved[l + 1]["xin"], mod_next=modt[l + 1], k_shift_next=0)
            dmods_t[l + 1][0], dmods_t[l + 1][1] = dsh, dsc
            gs_above = reduce_prepare(l + 1)
        small["ln2_g"][l], small["ln2_b"][l] = dg[0], db[0]
        dmod[5] = dgate2
        df = _matmul(dfo, wl["w_down"], "nt", name="d_f", tn=1408)
        g_w_down = _matmul(sv["ft"], dfo, "nn", name="g_w_down", tm=1408, tk=2304)
        if gs_above is None:
            da2, du2, g_fdw, g_fdb, _ = _ffn_mid_bwd(df, sv["up"], sv["fw8"], row(ffn_dw_b[l]), B, C, S)
            ps_above = None
        else:
            da2, du2, g_fdw, g_fdb, r1 = _ffn_mid_bwd(df, sv["up"], sv["fw8"], row(ffn_dw_b[l]), B, C, S,
                                                      carry=_SiblingExchange(gs_above))
            ps_above = [_pair_sum(g, r) for g, r in zip(gs_above, r1)]
        small["ffn_dw_b"][l] = g_fdb[0]
        dh2 = _matmul(da2, wl["w_up"][:, :D_FF], "nt", name="d_h2a", tm=512)
        dh2 = _matmul(du2, wl["w_up"][:, D_FF:], "nt", name="d_h2u", tm=512, acc_in=dh2)
        g_w_up = _matmul(sv["h2t"], da2, "nn", name="g_w_up_a", tn=1408, tk=2304, into=(2 * D_FF, 0, None))
        g_w_up = _matmul(sv["h2t"], du2, "nn", name="g_w_up_u", tn=1408, tk=2304, into=(2 * D_FF, D_FF, g_w_up))
        dxp, dmo, dgate1, dg, db, dbo, dsh, dsc = _ln_bwd(dy1p, sv["xin"], sv["mo"], modt[l], 2, row(ln1_g[l]), dh=dh2,
                                                          y=sv["y1"], mod_next=modt[l], k_shift_next=3)
        small["ln1_g"][l], small["ln1_b"][l], small["b_out"][l] = dg[0], db[0], dbo[0]
        dmod[2], dmod[3], dmod[4] = dgate1, dsh, dsc
        dm = _matmul(dmo, wl["w_out"], "nt", name="d_m")
        g_w_out = _matmul(sv["mt"], dmo, "nn", name="g_w_out", tk=2304)
        dattn, dconv_o, dpool_o, dzg, g_pwb, gsum_gate = _merge_bwd(dm, sv["attn"], sv["conv_o"], sv["pool_o"], sv["z_gate"])
        small["conv_pw_b"][l] = g_pwb[0]
        du, g_pool_w, g_pool_sc, gsum_pool = _pool_bwd(dpool_o, sv["pooled"], wl["pool_w"], row(pool_scale[l]), B, C, S)
        small["pool_scale"][l] = g_pool_sc[0]
        dsw = _matmul(dconv_o, wl["conv_pw_w"], "nt", name="d_sw")
        g_pw = _matmul(sv["swt"], dconv_o, "nn", name="g_conv_pw", tk=2304)
        dhc, g_cg, g_cb, g_cdb = _ln_silu_bwd(dsw, sv["hc"], row(conv_ln_g[l]), row(conv_ln_b[l]))
        small["conv_ln_g"][l], small["conv_ln_b"][l], small["conv_dw_b"][l] = g_cg[0], g_cb[0], g_cdb[0]
        da, dgt, g_cdw, gsum_a, gsum_gt = _conv_bwd(dhc, sv["z_conv"], sv["dw32"], B, C, S)
        if ps_above is None:
            dq, dk, dv, _ = _attn_bwd(sv["qkv"], sv["attn"], dattn, B, C, R)
        else:
            dq, dk, dv, r2 = _attn_bwd(sv["qkv"], sv["attn"], dattn, B, C, R, carry=_ChipExchange(ps_above))
            reduce_end(l + 1, r2)
        dz_qkv, g_qg, g_kg, gsum_qkv = _qk_bwd(dq, dk, dv, sv["z_qkv"], cos_t, sin_t, row(q_gain[l]), row(k_gain[l]), tpe)
        small["q_gain"][l], small["k_gain"][l] = g_qg[0], g_kg[0]
        small["b_in"][l] = jnp.concatenate([gsum_qkv[0], gsum_a[0], gsum_gt[0], gsum_pool[0], gsum_gate[0]])
        w_inl = wl["w_in"]
        g0 = QKV_W + 2 * CONV_CH + POOL_CH
        pieces = ((dzg, g0, N_GATE), (da, QKV_W, CONV_CH), (dgt, QKV_W + CONV_CH, CONV_CH), (du, QKV_W + 2 * CONV_CH, POOL_CH),
                  (dz_qkv, 0, QKV_W))
        dh1 = g_w_in = None
        at = 0
        for k, (dz, c0, wd) in enumerate(pieces):
            dh1 = _matmul(dz, w_inl[:, c0:c0 + wd], "nt", name=f"d_h1_{k}", tm=512, acc_in=dh1)
            g_w_in = _matmul(sv["h1t"], dz, "nn", name=f"g_w_in_{k}", tk=2304, into=(D_IN, at, g_w_in))
            at += wd
        g_w_in = jnp.concatenate([g_w_in[:, N_GATE + 2 * CONV_CH + POOL_CH:], g_w_in[:, N_GATE:N_GATE + 2 * CONV_CH + POOL_CH],
                                  g_w_in[:, :N_GATE]], axis=1)

        layer_grads[l] = {"w_in": g_w_in, "conv_pw_w": g_pw, "pool_w": g_pool_w, "w_out": g_w_out, "w_up": g_w_up,
                          "w_down": g_w_down, "conv_dw_w": g_cdw[:CONV_K], "ffn_dw_w": g_fdw[:FFN_K]}
        dy = dxp
    gx_u, dmods_t[0][0], dmods_t[0][1] = _mod_bwd(dy, dh1, saved[0]["xin"], modt[0], 0)
    grad_x = gx_u.reshape(B, R, D)[:, C:]

    gs = reduce_prepare(0)
    r1 = _run_exchange(_SiblingExchange(gs), name="sibling_exchange")
    reduce_end(0, _run_exchange(_ChipExchange([_pair_sum(g, r) for g, r in zip(gs, r1)]), name="chip_exchange"))
    outs = {}
    for n, s, _ in SHARDED:
        for kind, buf in zip(kinds, results[n]):
            outs[kind + n] = buf.reshape((DEPTH,) + tuple(s))

    def small_pack(pieces):
        flat = jnp.concatenate([p.reshape(-1) for p in pieces])
        return jnp.pad(flat, (0, SMALL_ROWS * LANES - flat.shape[0])).reshape(SMALL_ROWS, LANES)

    zero1 = jnp.zeros((1,), F32)
    g_pack = small_pack([small[n][l] for n, _ in REPLICATED for l in range(DEPTH)] + [d_c_ctx, loss_part[0, :1]])
    g_small, = _all_gather([g_pack], name="gather_small")
    wmv = [small_pack([given[pre + n] for n, _ in REPLICATED] + [given[pre + "c_ctx"], zero1])[None] for pre in ("", "m_", "v_")]
    res = _adamw(g_small, *wmv, 0, None, name="adamw_small")
    for kind, buf in zip(kinds, res):
        flat = buf.reshape(-1)
        off = 0
        for n, sz in REPLICATED:
            outs[kind + n] = flat[off:off + DEPTH * sz].reshape(DEPTH, sz)
            off += DEPTH * sz
        outs[kind + "c_ctx"] = flat[off:off + D]
        if kind == "grad_":
            loss = flat[off + D]

    names = ["c_ctx", "w_ada", "b_ada", "w_in", "b_in", "q_gain", "k_gain", "conv_dw_w", "conv_dw_b", "conv_ln_g", "conv_ln_b",
             "conv_pw_w", "conv_pw_b", "pool_w", "pool_scale", "w_out", "b_out", "ln1_g", "ln1_b", "ln2_g", "ln2_b", "w_up",
             "ffn_dw_w", "ffn_dw_b", "w_down"]
    return (loss, grad_x, *[outs[k + n] for k in ("grad_", "delta_", "new_m_", "new_v_") for n in names])
```

```python
import functools

import jax
import jax.numpy as jnp
import numpy as np
from jax import lax
from jax.experimental import pallas as pl
from jax.experimental.pallas import tpu as pltpu

F32 = jnp.float32
MXU = jnp.bfloat16
WIRE = jnp.bfloat16

D = 1024
HD = 128
NH = 8
NKV = 2
QG = NH // NKV
KV_W = NKV * HD
QKV_W = NH * HD + 2 * KV_W
CONV_CH = D
POOL_CH = D
POOL_WINDOWS = (2, 4, 8, 16)
POOL_GCH = POOL_CH // len(POOL_WINDOWS)
N_GATE = 3 * D
D_IN = QKV_W + 2 * CONV_CH + POOL_CH + N_GATE
D_FF = 2816
N_MOD = 6
DEPTH = 4
CONV_K = 31
FFN_K = 3
GRID_W = 64
ROPE_THETA = 10000.0
ROPE_PAIRS = HD // 4
ALPHA = (2 * DEPTH) ** 0.25
LN_EPS = 1e-5
RMS_EPS = 1e-6
ATTN_SCALE = HD ** -0.5
LOG2_E = 1.4426950408889634
ADAM_LR, ADAM_B1, ADAM_B2, ADAM_EPS, ADAM_WD, ADAM_STEP = 0.001, 0.9, 0.999, 1e-08, 0.01, 10

N_DEV = 8
TM = 256
GAP = 16
LANES = 1024
LANE = 128
VMEM_MB = 48

NN = (((1,), (0,)), ((), ()))
NT = (((1,), (1,)), ((), ()))
TN = (((0,), (0,)), ((), ()))

_pcall = pl.pallas_call


def _call(body, *, name, grid, in_specs, out_specs, out_shape, scratch=(), aliases=None, vmem=VMEM_MB):
    return _pcall(
        body, name=name, grid=grid, in_specs=in_specs, out_specs=out_specs, out_shape=out_shape,
        scratch_shapes=list(scratch), input_output_aliases=aliases or {},
        compiler_params=pltpu.CompilerParams(dimension_semantics=("arbitrary",) * len(grid), vmem_limit_bytes=vmem * 2 ** 20),
    )


def _sds(shape, dtype):
    return jax.ShapeDtypeStruct(tuple(shape), dtype)


def _pick(n, cap, mult):
    best = None
    for t in range(mult, min(n, cap) + 1, mult):
        if n % t == 0:
            best = t
    return best if best is not None else n


def _dot(a, b, dims):
    return lax.dot_general(a, b, dims, preferred_element_type=F32)


def _sigmoid(x):
    return 1.0 / (1.0 + jnp.exp(-x))


def _matmul(a, b, mode, *, name, bias=None, acc_in=None, into=None, out_dtype=F32, tm=1024, tn=1024, tk=None):
    if mode == "nn":
        (M, K), (K2, N) = a.shape, b.shape
    elif mode == "nt":
        (M, K), (N, K2) = a.shape, b.shape
    else:
        (K, M), (K2, N) = a.shape, b.shape
    assert K == K2, (a.shape, b.shape, mode)
    tm = _pick(M, tm, 16)
    tn = _pick(N, tn, 128)
    tk = K if tk is None else _pick(K, tk, 128 if mode != "tn" else 16)
    gk = K // tk
    dims = {"nn": NN, "nt": NT, "tn": TN}[mode]
    a_spec = pl.BlockSpec((tk, tm), lambda j, i, k: (k, i)) if mode == "tn" else pl.BlockSpec((tm, tk), lambda j, i, k: (i, k))
    b_spec = pl.BlockSpec((tn, tk), lambda j, i, k: (j, k)) if mode == "nt" else pl.BlockSpec((tk, tn), lambda j, i, k: (k, j))
    in_specs, args = [a_spec, b_spec], [a, b]
    if bias is not None:
        in_specs.append(pl.BlockSpec((1, tn), lambda j, i, k: (0, j)))
        args.append(bias)
    aliases = {}
    if acc_in is not None:
        aliases = {len(args): 0}
        in_specs.append(pl.BlockSpec((tm, tn), lambda j, i, k: (i, j)))
        args.append(acc_in)
    n_total, col0, prev = (N, 0, None) if into is None else into
    assert col0 % tn == 0
    jb = col0 // tn
    if prev is not None:
        aliases = {len(args): 0}
        in_specs.append(pl.BlockSpec(memory_space=pl.ANY))
        args.append(prev)
    n_in = len(args)

    def body(*refs):
        a_ref, b_ref = refs[0], refs[1]
        pos = 2
        bias_ref = acc_in_ref = None
        if bias is not None:
            bias_ref = refs[pos]
            pos += 1
        if acc_in is not None:
            acc_in_ref = refs[pos]
        pos = n_in
        o_ref = refs[pos]
        part = _dot(a_ref[...].astype(MXU), b_ref[...].astype(MXU), dims)

        def finish(acc):
            if bias_ref is not None:
                acc = acc + bias_ref[...]
            if acc_in_ref is not None:
                acc = acc + acc_in_ref[...]
            o_ref[...] = acc.astype(out_dtype)

        if gk == 1:
            finish(part)
        else:
            acc_ref = refs[pos + 1]
            k = pl.program_id(2)

            @pl.when(k == 0)
            def _():
                acc_ref[...] = part

            @pl.when(k > 0)
            def _():
                acc_ref[...] += part

            @pl.when(k == gk - 1)
            def _():
                finish(acc_ref[...])

    return _call(
        body, name=name, grid=(N // tn, M // tm, gk), in_specs=in_specs,
        out_specs=pl.BlockSpec((tm, tn), lambda j, i, k: (i, j + jb)), out_shape=_sds((M, n_total), out_dtype),
        scratch=[pltpu.VMEM((tm, tn), F32)] if gk > 1 else [], aliases=aliases,
    )(*args)


def _rt(w, cb=0):
    return pl.BlockSpec((TM, w), lambda i: (i, cb))


def _ct(w):
    return pl.BlockSpec((w, TM), lambda i: (0, i))


def _vec(w):
    return pl.BlockSpec((1, w), lambda i: (0, 0))


def _part(w):
    return pl.BlockSpec((1, 1, w), lambda i: (i, 0, 0))


def _mod(ref, k):
    return ref[0, :, k * D:(k + 1) * D]


def _colsum(x):
    return jnp.sum(x, axis=0, keepdims=True)


def _ln_stats(s):
    mu = jnp.mean(s, axis=1, keepdims=True)
    cen = s - mu
    var = jnp.mean(cen * cen, axis=1, keepdims=True)
    rstd = lax.rsqrt(var + LN_EPS)
    return cen * rstd, rstd


def _modulate_cast(x, modt, k_shift):
    T = x.shape[0]
    nt = T // TM

    def body(x_ref, mod_ref, h_ref, ht_ref):
        h = (x_ref[...] * (1.0 + _mod(mod_ref, k_shift + 1)) + _mod(mod_ref, k_shift)).astype(MXU)
        h_ref[...] = h
        ht_ref[...] = h.T

    return _call(body, name="modulate", grid=(nt,), in_specs=[_rt(D), _part(N_MOD * D)], out_specs=[_rt(D), _ct(D)],
                 out_shape=[_sds((T, D), MXU), _sds((D, T), MXU)])(x, modt)


def _resid_ln(x, br, modt, k_gate, g, b, mod_next=None, k_shift_next=0):
    T = x.shape[0]
    nt = T // TM
    with_h = mod_next is not None

    def body(*refs):
        x_ref, br_ref, mod_ref, g_ref, b_ref = refs[:5]
        s = ALPHA * x_ref[...] + _mod(mod_ref, k_gate) * br_ref[...]
        xhat, _ = _ln_stats(s)
        y = xhat * g_ref[...] + b_ref[...]
        if with_h:
            modn_ref, y_ref, h_ref, ht_ref = refs[5:]
            y_ref[...] = y
            h = (y * (1.0 + _mod(modn_ref, k_shift_next + 1)) + _mod(modn_ref, k_shift_next)).astype(MXU)
            h_ref[...] = h
            ht_ref[...] = h.T
        else:
            refs[5][...] = y

    in_specs = [_rt(D), _rt(D), _part(N_MOD * D), _vec(D), _vec(D)]
    args = [x, br, modt, g, b]
    if with_h:
        in_specs.append(_part(N_MOD * D))
        args.append(mod_next)
        return _call(body, name="resid_ln_mod", grid=(nt,), in_specs=in_specs, out_specs=[_rt(D), _rt(D), _ct(D)],
                     out_shape=[_sds((T, D), F32), _sds((T, D), MXU), _sds((D, T), MXU)])(*args)
    return _call(body, name="resid_ln", grid=(nt,), in_specs=in_specs, out_specs=_rt(D), out_shape=_sds((T, D), F32))(*args)


def _ln_bwd(dy_part, x, br, modt, k_gate, g, dh=None, y=None, mod_next=None, k_shift_next=0):
    T = x.shape[0]
    nt = T // TM
    with_h = dh is not None

    def body(*refs):
        if with_h:
            dyp_ref, x_ref, br_ref, mod_ref, g_ref, dh_ref, y_ref, modn_ref = refs[:8]
            outs = refs[8:]
        else:
            dyp_ref, x_ref, br_ref, mod_ref, g_ref = refs[:5]
            outs = refs[5:]
        dx_ref, dbr_ref, dgate_ref, dlg_ref, dlb_ref, dbsum_ref = outs[:6]
        i = pl.program_id(0)

        @pl.when(i == 0)
        def _():
            dlg_ref[...] = jnp.zeros_like(dlg_ref)
            dlb_ref[...] = jnp.zeros_like(dlb_ref)
            dbsum_ref[...] = jnp.zeros_like(dbsum_ref)

        dy = dyp_ref[...]
        if with_h:
            dshift_ref, dscale_ref = outs[6:]
            dhv = dh_ref[...]
            dy = dy + dhv * (1.0 + _mod(modn_ref, k_shift_next + 1))
            dshift_ref[0] = _colsum(dhv)
            dscale_ref[0] = _colsum(dhv * y_ref[...])
        gate = _mod(mod_ref, k_gate)
        brv = br_ref[...]
        s = ALPHA * x_ref[...] + gate * brv
        xhat, rstd = _ln_stats(s)
        dlg_ref[...] += _colsum(dy * xhat)
        dlb_ref[...] += _colsum(dy)
        dyg = dy * g_ref[...]
        m1 = jnp.mean(dyg, axis=1, keepdims=True)
        m2 = jnp.mean(dyg * xhat, axis=1, keepdims=True)
        ds = rstd * (dyg - m1 - xhat * m2)
        dx_ref[...] = ALPHA * ds
        dbr = gate * ds
        dbr_ref[...] = dbr.astype(MXU)
        dbsum_ref[...] += _colsum(dbr)
        dgate_ref[0] = _colsum(ds * brv)

    in_specs = [_rt(D), _rt(D), _rt(D), _part(N_MOD * D), _vec(D)]
    args = [dy_part, x, br, modt, g]
    out_specs = [_rt(D), _rt(D), _part(D), _vec(D), _vec(D), _vec(D)]
    out_shape = [_sds((T, D), F32), _sds((T, D), MXU), _sds((nt, 1, D), F32), _sds((1, D), F32), _sds((1, D), F32), _sds((1, D), F32)]
    if with_h:
        in_specs += [_rt(D), _rt(D), _part(N_MOD * D)]
        args += [dh, y, mod_next]
        out_specs += [_part(D), _part(D)]
        out_shape += [_sds((nt, 1, D), F32), _sds((nt, 1, D), F32)]
    return _call(body, name="ln_bwd_mod" if with_h else "ln_bwd", grid=(nt,), in_specs=in_specs, out_specs=out_specs,
                 out_shape=out_shape)(*args)


def _mod_bwd(dx_part, dh, x, modt, k_shift):
    T = x.shape[0]
    nt = T // TM

    def body(dxp_ref, dh_ref, x_ref, mod_ref, dx_ref, dshift_ref, dscale_ref):
        dhv = dh_ref[...]
        dx_ref[...] = dxp_ref[...] + dhv * (1.0 + _mod(mod_ref, k_shift + 1))
        dshift_ref[0] = _colsum(dhv)
        dscale_ref[0] = _colsum(dhv * x_ref[...])

    return _call(body, name="mod_bwd", grid=(nt,), in_specs=[_rt(D), _rt(D), _rt(D), _part(N_MOD * D)],
                 out_specs=[_rt(D), _part(D), _part(D)],
                 out_shape=[_sds((T, D), F32), _sds((nt, 1, D), F32), _sds((nt, 1, D), F32)])(dx_part, dh, x, modt)


def _loss_grad(y, target, tpe, ncq):
    T = y.shape[0]
    nt = T // TM
    nl = tpe - ncq

    def body(y_ref, t_ref, dy_ref, loss_ref):
        i = pl.program_id(0)

        @pl.when(i == 0)
        def _():
            loss_ref[...] = jnp.zeros_like(loss_ref)

        @pl.when(i % tpe < ncq)
        def _():
            dy_ref[...] = jnp.zeros_like(dy_ref)

        @pl.when(i % tpe >= ncq)
        def _():
            err = y_ref[...] - t_ref[...]
            dy_ref[...] = err * (1.0 / D)
            loss_ref[...] += (0.5 / D) * jnp.sum(_colsum(err * err), axis=1, keepdims=True)

    tgt_spec = pl.BlockSpec((TM, D), lambda i: ((i // tpe) * nl + jnp.maximum(i % tpe - ncq, 0), 0))
    return _call(body, name="loss_grad", grid=(nt,), in_specs=[_rt(D), tgt_spec], out_specs=[_rt(D), _vec(128)],
                 out_shape=[_sds((T, D), F32), _sds((1, 128), F32)])(y, target)


def _rope_partner(x):
    lane = lax.broadcasted_iota(jnp.int32, x.shape, 1)
    first = (lane % (2 * ROPE_PAIRS)) < ROPE_PAIRS
    return jnp.where(first, pltpu.roll(x, HD - ROPE_PAIRS, 1), pltpu.roll(x, ROPE_PAIRS, 1))


def _qk_prep(z_qkv, cos_t, sin_t, q_gain, k_gain, tpe):
    T = z_qkv.shape[0]
    nt = T // TM

    def body(z_ref, cos_ref, sin_ref, qg_ref, kg_ref, o_ref):
        cos, sin = cos_ref[...], sin_ref[...]
        for h in range(NH + NKV):
            sl = slice(h * HD, (h + 1) * HD)
            t = z_ref[:, sl]
            gain = qg_ref[...] if h < NH else kg_ref[...]
            n = t * lax.rsqrt(jnp.mean(t * t, axis=1, keepdims=True) + RMS_EPS) * gain
            o_ref[:, sl] = (n * cos + _rope_partner(n) * sin).astype(MXU)
        o_ref[:, (NH + NKV) * HD:] = z_ref[:, (NH + NKV) * HD:].astype(MXU)

    tab = pl.BlockSpec((TM, HD), lambda i: (i % tpe, 0))
    return _call(body, name="qk_prep", grid=(nt,), in_specs=[_rt(QKV_W), tab, tab, _vec(HD), _vec(HD)], out_specs=_rt(QKV_W),
                 out_shape=_sds((T, QKV_W), MXU))(z_qkv, cos_t, sin_t, q_gain, k_gain)


def _qk_bwd(dq, dk, dv, z_qkv, cos_t, sin_t, q_gain, k_gain, tpe):
    T = z_qkv.shape[0]
    nt = T // TM

    def body(dq_ref, dk_ref, dv_ref, z_ref, cos_ref, sin_ref, qg_ref, kg_ref, dz_ref, dqg_ref, dkg_ref, bsum_ref):
        i = pl.program_id(0)

        @pl.when(i == 0)
        def _():
            dqg_ref[...] = jnp.zeros_like(dqg_ref)
            dkg_ref[...] = jnp.zeros_like(dkg_ref)
            bsum_ref[...] = jnp.zeros_like(bsum_ref)

        cos, sin = cos_ref[...], sin_ref[...]
        for h in range(NH + NKV):
            sl = slice(h * HD, (h + 1) * HD)
            dr = dq_ref[:, sl] if h < NH else dk_ref[:, (h - NH) * HD:(h - NH + 1) * HD]
            gain = qg_ref[...] if h < NH else kg_ref[...]
            dn = dr * cos + _rope_partner(dr * sin)
            t = z_ref[:, sl]
            rstd = lax.rsqrt(jnp.mean(t * t, axis=1, keepdims=True) + RMS_EPS)
            that = t * rstd
            dgain = _colsum(dn * that)
            if h < NH:
                dqg_ref[...] += dgain
            else:
                dkg_ref[...] += dgain
            dthat = dn * gain
            dt = rstd * (dthat - that * jnp.mean(dthat * that, axis=1, keepdims=True))
            dz_ref[:, sl] = dt.astype(MXU)
            bsum_ref[:, sl] += _colsum(dt)
        dvv = dv_ref[...]
        dz_ref[:, (NH + NKV) * HD:] = dvv.astype(MXU)
        bsum_ref[:, (NH + NKV) * HD:] += _colsum(dvv)

    tab = pl.BlockSpec((TM, HD), lambda i: (i % tpe, 0))
    return _call(body, name="qk_bwd", grid=(nt,),
                 in_specs=[_rt(NH * HD), _rt(KV_W), _rt(KV_W), _rt(QKV_W), tab, tab, _vec(HD), _vec(HD)],
                 out_specs=[_rt(QKV_W), _vec(HD), _vec(HD), _vec(QKV_W)],
                 out_shape=[_sds((T, QKV_W), MXU), _sds((1, HD), F32), _sds((1, HD), F32), _sds((1, QKV_W), F32)],
                 )(dq, dk, dv, z_qkv, cos_t, sin_t, q_gain, k_gain)


def _ln_silu(hc, g, b):
    T = hc.shape[0]

    def body(h_ref, g_ref, b_ref, o_ref, ot_ref):
        xhat, _ = _ln_stats(h_ref[...])
        n = xhat * g_ref[...] + b_ref[...]
        sw = (n * _sigmoid(n)).astype(MXU)
        o_ref[...] = sw
        ot_ref[...] = sw.T

    return _call(body, name="ln_silu", grid=(T // TM,), in_specs=[_rt(D), _vec(D), _vec(D)], out_specs=[_rt(D), _ct(D)],
                 out_shape=[_sds((T, D), MXU), _sds((D, T), MXU)])(hc, g, b)


def _ln_silu_bwd(dsw, hc, g, b):
    T = hc.shape[0]

    def body(d_ref, h_ref, g_ref, b_ref, dh_ref, dg_ref, db_ref, dcb_ref):
        i = pl.program_id(0)

        @pl.when(i == 0)
        def _():
            dg_ref[...] = jnp.zeros_like(dg_ref)
            db_ref[...] = jnp.zeros_like(db_ref)
            dcb_ref[...] = jnp.zeros_like(dcb_ref)

        xhat, rstd = _ln_stats(h_ref[...])
        n = xhat * g_ref[...] + b_ref[...]
        sg = _sigmoid(n)
        dn = d_ref[...] * (sg * (1.0 + n * (1.0 - sg)))
        dg_ref[...] += _colsum(dn * xhat)
        db_ref[...] += _colsum(dn)
        dng = dn * g_ref[...]
        m1 = jnp.mean(dng, axis=1, keepdims=True)
        m2 = jnp.mean(dng * xhat, axis=1, keepdims=True)
        dh = rstd * (dng - m1 - xhat * m2)
        dh_ref[...] = dh
        dcb_ref[...] += _colsum(dh)

    return _call(body, name="ln_silu_bwd", grid=(T // TM,), in_specs=[_rt(D), _rt(D), _vec(D), _vec(D)],
                 out_specs=[_rt(D), _vec(D), _vec(D), _vec(D)],
                 out_shape=[_sds((T, D), F32)] + [_sds((1, D), F32)] * 3)(dsw, hc, g, b)


def _merge(attn, conv_o, pool_o, z_gate):
    T = attn.shape[0]

    def body(a_ref, c_ref, p_ref, zg_ref, m_ref, mt_ref):
        m = (_sigmoid(zg_ref[:, 0:D]) * a_ref[...] + _sigmoid(zg_ref[:, D:2 * D]) * c_ref[...]
             + _sigmoid(zg_ref[:, 2 * D:3 * D]) * p_ref[...]).astype(MXU)
        m_ref[...] = m
        mt_ref[...] = m.T

    return _call(body, name="merge", grid=(T // TM,), in_specs=[_rt(D), _rt(D), _rt(D), _rt(N_GATE)], out_specs=[_rt(D), _ct(D)],
                 out_shape=[_sds((T, D), MXU), _sds((D, T), MXU)])(attn, conv_o, pool_o, z_gate)


def _merge_bwd(dm, attn, conv_o, pool_o, z_gate):
    T = attn.shape[0]

    def body(dm_ref, a_ref, c_ref, p_ref, zg_ref, da_ref, dc_ref, dp_ref, dzg_ref, dcsum_ref, gsum_ref):
        i = pl.program_id(0)

        @pl.when(i == 0)
        def _():
            dcsum_ref[...] = jnp.zeros_like(dcsum_ref)
            gsum_ref[...] = jnp.zeros_like(gsum_ref)

        dmv = dm_ref[...]
        for k, (br_ref, out_ref) in enumerate(((a_ref, da_ref), (c_ref, dc_ref), (p_ref, dp_ref))):
            gk = _sigmoid(zg_ref[:, k * D:(k + 1) * D])
            dbr = dmv * gk
            out_ref[...] = dbr.astype(out_ref.dtype)
            if k == 1:
                dcsum_ref[...] += _colsum(dbr)
            dzg = dmv * br_ref[...] * gk * (1.0 - gk)
            dzg_ref[:, k * D:(k + 1) * D] = dzg.astype(MXU)
            gsum_ref[:, k * D:(k + 1) * D] += _colsum(dzg)

    return _call(body, name="merge_bwd", grid=(T // TM,), in_specs=[_rt(D), _rt(D), _rt(D), _rt(D), _rt(N_GATE)],
                 out_specs=[_rt(D), _rt(D), _rt(D), _rt(N_GATE), _vec(D), _vec(N_GATE)],
                 out_shape=[_sds((T, D), F32), _sds((T, D), MXU), _sds((T, D), F32), _sds((T, N_GATE), MXU),
                            _sds((1, D), F32), _sds((1, N_GATE), F32)])(dm, attn, conv_o, pool_o, z_gate)


def _softmax_parts(q, k):
    s = _dot(q, k, NT)
    p = jnp.exp2((s - jnp.max(s, axis=1, keepdims=True)) * (ATTN_SCALE * LOG2_E))
    return p, 1.0 / jnp.sum(p, axis=1, keepdims=True)


def _attn_specs(nq):
    q_spec = pl.BlockSpec((TM, QG * HD), lambda b, h, q: (b * nq + q, h))
    k_spec = pl.BlockSpec((nq * TM, HD), lambda b, h, q: (b, NH + h))
    v_spec = pl.BlockSpec((nq * TM, HD), lambda b, h, q: (b, NH + NKV + h))
    return q_spec, k_spec, v_spec


class _Carried:
    def __init__(self, ex, n_in, n_out, n_scratch):
        self.ex, self.n_in, self.n_out, self.n_scratch = ex, n_in, n_out, n_scratch
        self.ci = len(ex.inputs) if ex else 0
        self.co = len(ex.out_shape) if ex else 0

    def in_specs(self):
        return [ANY] * self.ci

    def out_specs(self):
        return [ANY] * self.co

    def split(self, refs):
        a = self.n_in
        b = a + self.ci
        c = b + self.n_out
        d = c + self.co
        e = d + self.n_scratch
        return (refs[:a], refs[b:c], refs[d:e]), (refs[a:b], refs[c:d], refs[e:])

    def before(self, step, parts):
        if self.ex:
            pl.when(step == 0)(lambda: self.ex.start(*parts))

    def after(self, step, n_steps, parts):
        if self.ex:
            pl.when(step == (3 * n_steps) // 4)(lambda: self.ex.mid(*parts))
            pl.when(step == n_steps - 1)(lambda: self.ex.finish(*parts))


def _attn_fwd(qkv, B, C, R, carry=None):
    nq, ncq = R // TM, C // TM
    car = _Carried(carry, 3, 1, 0)

    def body(*refs):
        (q_ref, k_ref, v_ref), (o_ref,), _ = car.split(refs)[0]
        parts = car.split(refs)[1]
        qi = pl.program_id(2)
        step = (pl.program_id(0) * NKV + pl.program_id(1)) * nq + qi
        car.before(step, parts)

        def attend(L):
            k, v = k_ref[0:L, :], v_ref[0:L, :]
            for i in range(QG):
                sl = slice(i * HD, (i + 1) * HD)
                p, inv_l = _softmax_parts(q_ref[:, sl], k)
                o_ref[:, sl] = _dot(p.astype(MXU), v, NN) * inv_l

        pl.when(qi < ncq)(functools.partial(attend, C))
        pl.when(qi >= ncq)(functools.partial(attend, R))
        car.after(step, B * NKV * nq, parts)

    q_spec, k_spec, v_spec = _attn_specs(nq)
    res = _call(body, name="attn_fwd", grid=(B, NKV, nq), in_specs=[q_spec, k_spec, v_spec] + car.in_specs(),
                out_specs=[q_spec] + car.out_specs(), out_shape=[_sds((B * R, NH * HD), F32)] + list(carry.out_shape if carry else []),
                scratch=list(carry.scratch) if carry else [])(qkv, qkv, qkv, *(carry.inputs if carry else []))
    return res[0], res[1:]


def _attn_bwd(qkv, o, do, B, C, R, carry=None):
    nq, ncq = R // TM, C // TM
    car = _Carried(carry, 5, 3, 2)

    def body(*refs):
        (q_ref, k_ref, v_ref, o_ref, do_ref), (dq_ref, dk_ref, dv_ref), (dkt, dvt) = car.split(refs)[0]
        parts = car.split(refs)[1]
        qi = pl.program_id(2)
        step = (pl.program_id(0) * NKV + pl.program_id(1)) * nq + qi
        car.before(step, parts)

        @pl.when(qi == 0)
        def _():
            dkt[...] = jnp.zeros_like(dkt)
            dvt[...] = jnp.zeros_like(dvt)

        def bwd(L):
            k, v = k_ref[0:L, :], v_ref[0:L, :]
            for i in range(QG):
                sl = slice(i * HD, (i + 1) * HD)
                q = q_ref[:, sl]
                p, inv_l = _softmax_parts(q, k)
                dov = do_ref[:, sl]
                dp = _dot(dov.astype(MXU), v, NT)
                dl = jnp.sum(dov * o_ref[:, sl], axis=1, keepdims=True)
                ds = (p * ((dp - dl) * (inv_l * ATTN_SCALE))).astype(MXU)
                dq_ref[:, sl] = _dot(ds, k, NN)
                dkt[:, 0:L] += _dot(q, ds, TN)
                dvt[:, 0:L] += _dot((dov * inv_l).astype(MXU), p.astype(MXU), TN)

        pl.when(qi < ncq)(functools.partial(bwd, C))
        pl.when(qi >= ncq)(functools.partial(bwd, R))

        @pl.when(qi == nq - 1)
        def _():
            dk_ref[...] = dkt[...].T
            dv_ref[...] = dvt[...].T

        car.after(step, B * NKV * nq, parts)

    q_spec, k_spec, v_spec = _attn_specs(nq)
    kv_out = pl.BlockSpec((R, HD), lambda b, h, q: (b, h))
    res = _call(body, name="attn_bwd", grid=(B, NKV, nq), in_specs=[q_spec, k_spec, v_spec, q_spec, q_spec] + car.in_specs(),
                out_specs=[q_spec, kv_out, kv_out] + car.out_specs(),
                out_shape=[_sds((B * R, NH * HD), F32), _sds((B * R, KV_W), F32), _sds((B * R, KV_W), F32)]
                + list(carry.out_shape if carry else []),
                scratch=[pltpu.VMEM((HD, R), F32), pltpu.VMEM((HD, R), F32)] + list(carry.scratch if carry else []),
                )(qkv, qkv, qkv, o, do, *(carry.inputs if carry else []))
    return res[0], res[1], res[2], res[3:]


def _segments(C, S):
    return ((0, GAP, C), (C, 2 * GAP + C, S))


def _padded_rows(C, S):
    return 3 * GAP + C + S


def _zero_gaps(pad_ref, C, S):
    for off in (0, GAP + C, 2 * GAP + C + S):
        pad_ref[off:off + GAP, :] = jnp.zeros((GAP, pad_ref.shape[1]), pad_ref.dtype)


def _chunks(n, ch, fn):
    def step(i, carry):
        fn(pl.multiple_of(i * ch, ch))
        return carry

    lax.fori_loop(0, n // ch, step, 0)


class _Window:
    def __init__(self, pad_ref, row, ch):
        self.pad_ref, (self.r, self.po), self.ch = pad_ref, row, ch

    def at(self, off):
        return self.pad_ref[pl.ds(self.r + (self.po + off), self.ch), :]


def _taps(pad_ref, w, row, ch, n_taps, flip=False):
    half = (n_taps - 1) // 2
    win = _Window(pad_ref, row, ch)
    acc = None
    for k in range(n_taps):
        term = w[k:k + 1, :] * win.at((half - k) if flip else (k - half))
        acc = term if acc is None else acc + term
    return acc


def _tap_grads(dw_ref, d, pad_ref, row, ch, n_taps):
    half = (n_taps - 1) // 2
    win = _Window(pad_ref, row, ch)
    for k in range(n_taps):
        prod = d * win.at(k - half)
        dw_ref[k] += jnp.sum(prod.reshape(ch // 8, 8, prod.shape[1]), axis=0)


def _conv_fwd(z_conv, w, bias, B, C, S, cw=128, ch=128):
    R = C + S
    nj = CONV_CH // cw
    segs = _segments(C, S)

    def body(a_ref, g_ref, w_ref, b_ref, o_ref, pad):
        _zero_gaps(pad, C, S)
        wv = w_ref[...]
        for so, po, n in segs:
            def fill(r, so=so, po=po):
                pad[pl.ds(po + r, ch), :] = a_ref[pl.ds(so + r, ch), :] * _sigmoid(g_ref[pl.ds(so + r, ch), :])

            _chunks(n, ch, fill)
        for so, po, n in segs:
            def conv(r, so=so, po=po):
                o_ref[pl.ds(so + r, ch), :] = _taps(pad, wv, (r, po), ch, CONV_K) + b_ref[...]

            _chunks(n, ch, conv)

    return _call(
        body, name="conv_fwd", grid=(nj, B),
        in_specs=[pl.BlockSpec((R, cw), lambda j, b: (b, j)), pl.BlockSpec((R, cw), lambda j, b: (b, nj + j)),
                  pl.BlockSpec((32, cw), lambda j, b: (0, j)), pl.BlockSpec((1, cw), lambda j, b: (0, j))],
        out_specs=pl.BlockSpec((R, cw), lambda j, b: (b, j)), out_shape=_sds((B * R, CONV_CH), F32),
        scratch=[pltpu.VMEM((_padded_rows(C, S), cw), F32)])(z_conv, z_conv, w, bias)


def _conv_bwd(dhc, z_conv, w, B, C, S, cw=128, ch=128):
    R = C + S
    nj = CONV_CH // cw
    segs = _segments(C, S)

    def body(d_ref, a_ref, g_ref, w_ref, da_ref, dg_ref, dw_ref, sa_ref, sg_ref, gpad, dpad, dwacc):
        b = pl.program_id(1)

        @pl.when(b == 0)
        def _():
            dw_ref[...] = jnp.zeros_like(dw_ref)
            sa_ref[...] = jnp.zeros_like(sa_ref)
            sg_ref[...] = jnp.zeros_like(sg_ref)

        _zero_gaps(gpad, C, S)
        _zero_gaps(dpad, C, S)
        dwacc[...] = jnp.zeros_like(dwacc)
        wv = w_ref[...]
        for so, po, n in segs:
            def fill(r, so=so, po=po):
                gpad[pl.ds(po + r, ch), :] = a_ref[pl.ds(so + r, ch), :] * _sigmoid(g_ref[pl.ds(so + r, ch), :])
                dpad[pl.ds(po + r, ch), :] = d_ref[pl.ds(so + r, ch), :]

            _chunks(n, ch, fill)
        for so, po, n in segs:
            def step(r, so=so, po=po):
                _tap_grads(dwacc, dpad[pl.ds(po + r, ch), :], gpad, (r, po), ch, CONV_K)
                dglu = _taps(dpad, wv, (r, po), ch, CONV_K, flip=True)
                av = a_ref[pl.ds(so + r, ch), :]
                sg = _sigmoid(g_ref[pl.ds(so + r, ch), :])
                da = dglu * sg
                dg = dglu * av * sg * (1.0 - sg)
                da_ref[pl.ds(so + r, ch), :] = da.astype(MXU)
                dg_ref[pl.ds(so + r, ch), :] = dg.astype(MXU)
                sa_ref[...] += _colsum(da)
                sg_ref[...] += _colsum(dg)

            _chunks(n, ch, step)
        for k in range(CONV_K):
            dw_ref[k:k + 1, :] += _colsum(dwacc[k])

    blk = pl.BlockSpec((R, cw), lambda j, b: (b, j))
    acc1 = pl.BlockSpec((1, cw), lambda j, b: (0, j))
    return _call(
        body, name="conv_bwd", grid=(nj, B),
        in_specs=[blk, blk, pl.BlockSpec((R, cw), lambda j, b: (b, nj + j)), pl.BlockSpec((32, cw), lambda j, b: (0, j))],
        out_specs=[blk, blk, pl.BlockSpec((32, cw), lambda j, b: (0, j)), acc1, acc1],
        out_shape=[_sds((B * R, CONV_CH), MXU), _sds((B * R, CONV_CH), MXU), _sds((32, CONV_CH), F32),
                   _sds((1, CONV_CH), F32), _sds((1, CONV_CH), F32)],
        scratch=[pltpu.VMEM((_padded_rows(C, S), cw), F32), pltpu.VMEM((_padded_rows(C, S), cw), F32),
                 pltpu.VMEM((32, 8, cw), F32)])(dhc, z_conv, z_conv, w)


def _ffn_mid(up, w, bias, B, C, S, cw=256, ch=64, carry=None):
    R = C + S
    nj = D_FF // cw
    segs = _segments(C, S)
    car = _Carried(carry, 4, 2, 1)

    def body(*refs):
        (a_ref, u_ref, w_ref, b_ref), (f_ref, ft_ref), (pad,) = car.split(refs)[0]
        parts = car.split(refs)[1]
        step = pl.program_id(0) * B + pl.program_id(1)
        car.before(step, parts)
        for h in range(cw // LANE):
            _zero_gaps(pad.at[h], C, S)
        wv = w_ref[...]
        for so, po, n in segs:
            def fill(r, so=so, po=po):
                for h in range(cw // LANE):
                    pad[h, pl.ds(po + r, ch), :] = a_ref[pl.ds(so + r, ch), h * LANE:(h + 1) * LANE]

            _chunks(n, ch, fill)
        for so, po, n in segs:
            def conv(r, so=so, po=po):
                ac = jnp.concatenate([_taps(pad.at[h], wv[:, h * LANE:(h + 1) * LANE], (r, po), ch, FFN_K)
                                      for h in range(cw // LANE)], axis=1) + b_ref[...]
                f_ref[pl.ds(so + r, ch), :] = (ac * _sigmoid(ac) * u_ref[pl.ds(so + r, ch), :]).astype(MXU)

            _chunks(n, ch, conv)
        ft_ref[...] = f_ref[...].T
        car.after(step, nj * B, parts)

    res = _call(
        body, name="ffn_mid", grid=(nj, B),
        in_specs=[pl.BlockSpec((R, cw), lambda j, b: (b, j)), pl.BlockSpec((R, cw), lambda j, b: (b, nj + j)),
                  pl.BlockSpec((8, cw), lambda j, b: (0, j)), pl.BlockSpec((1, cw), lambda j, b: (0, j))] + car.in_specs(),
        out_specs=[pl.BlockSpec((R, cw), lambda j, b: (b, j)), pl.BlockSpec((cw, R), lambda j, b: (j, b))] + car.out_specs(),
        out_shape=[_sds((B * R, D_FF), MXU), _sds((D_FF, B * R), MXU)] + list(carry.out_shape if carry else []),
        scratch=[pltpu.VMEM((cw // LANE, _padded_rows(C, S), LANE), F32)] + list(carry.scratch if carry else []),
    )(up, up, w, bias, *(carry.inputs if carry else []))
    return res[0], res[1], res[2:]


def _ffn_mid_bwd(df, up, w, bias, B, C, S, cw=128, ch=128, carry=None):
    R = C + S
    nj = D_FF // cw
    segs = _segments(C, S)
    car = _Carried(carry, 5, 4, 3)

    def body(*refs):
        (d_ref, a_ref, u_ref, w_ref, b_ref), (da_ref, du_ref, dw_ref, db_ref), (apad, dpad, dwacc) = car.split(refs)[0]
        parts = car.split(refs)[1]
        b = pl.program_id(1)
        step = pl.program_id(0) * B + b
        car.before(step, parts)

        @pl.when(b == 0)
        def _():
            dw_ref[...] = jnp.zeros_like(dw_ref)
            db_ref[...] = jnp.zeros_like(db_ref)

        _zero_gaps(apad, C, S)
        _zero_gaps(dpad, C, S)
        dwacc[...] = jnp.zeros_like(dwacc)
        wv = w_ref[...]
        for so, po, n in segs:
            def fill(r, so=so, po=po):
                apad[pl.ds(po + r, ch), :] = a_ref[pl.ds(so + r, ch), :]

            _chunks(n, ch, fill)
        for so, po, n in segs:
            def first(r, so=so, po=po):
                ac = _taps(apad, wv, (r, po), ch, FFN_K) + b_ref[...]
                sg = _sigmoid(ac)
                dfv = d_ref[pl.ds(so + r, ch), :]
                du_ref[pl.ds(so + r, ch), :] = (dfv * ac * sg).astype(MXU)
                dac = dfv * u_ref[pl.ds(so + r, ch), :] * (sg * (1.0 + ac * (1.0 - sg)))
                dpad[pl.ds(po + r, ch), :] = dac
                db_ref[...] += _colsum(dac)

            _chunks(n, ch, first)
        for so, po, n in segs:
            def second(r, so=so, po=po):
                _tap_grads(dwacc, dpad[pl.ds(po + r, ch), :], apad, (r, po), ch, FFN_K)
                da_ref[pl.ds(so + r, ch), :] = _taps(dpad, wv, (r, po), ch, FFN_K, flip=True).astype(MXU)

            _chunks(n, ch, second)
        for k in range(FFN_K):
            dw_ref[k:k + 1, :] += _colsum(dwacc[k])
        car.after(step, nj * B, parts)

    blk = pl.BlockSpec((R, cw), lambda j, b: (b, j))
    res = _call(
        body, name="ffn_mid_bwd", grid=(nj, B),
        in_specs=[blk, blk, pl.BlockSpec((R, cw), lambda j, b: (b, nj + j)), pl.BlockSpec((8, cw), lambda j, b: (0, j)),
                  pl.BlockSpec((1, cw), lambda j, b: (0, j))] + car.in_specs(),
        out_specs=[blk, blk, pl.BlockSpec((8, cw), lambda j, b: (0, j)), pl.BlockSpec((1, cw), lambda j, b: (0, j))]
        + car.out_specs(),
        out_shape=[_sds((B * R, D_FF), MXU), _sds((B * R, D_FF), MXU), _sds((8, D_FF), F32), _sds((1, D_FF), F32)]
        + list(carry.out_shape if carry else []),
        scratch=[pltpu.VMEM((_padded_rows(C, S), cw), F32), pltpu.VMEM((_padded_rows(C, S), cw), F32),
                 pltpu.VMEM((8, 8, cw), F32)] + list(carry.scratch if carry else []),
    )(df, up, up, w, bias, *(carry.inputs if carry else []))
    return res[0], res[1], res[2], res[3], res[4:]


def _view_sum(win, lo, hi):
    acc = win.at(lo)
    for off in range(lo + 1, hi + 1):
        acc = acc + win.at(off)
    return acc


def _window_count(r, ch, n, w):
    t = r + lax.broadcasted_iota(jnp.int32, (ch, 1), 0)
    return (jnp.minimum(t + w // 2, n) - jnp.maximum(t - w // 2, 0)).astype(F32)


def _pool_fwd(z_pool, pool_w, pool_scale, B, C, S, ch=128):
    R = C + S
    gch = POOL_GCH
    segs = _segments(C, S)

    def body(u_ref, pw_ref, sc_ref, pooled_ref, po_ref, pad):
        g = pl.program_id(1)
        for h in range(gch // LANE):
            _zero_gaps(pad.at[h], C, S)
        for so, po, n in segs:
            def fill(r, so=so, po=po):
                for h in range(gch // LANE):
                    pad[h, pl.ds(po + r, ch), :] = u_ref[pl.ds(so + r, ch), h * LANE:(h + 1) * LANE]

            _chunks(n, ch, fill)
        for gi, w in enumerate(POOL_WINDOWS):
            @pl.when(g == gi)
            def _(w=w):
                for so, po, n in segs:
                    def step(r, so=so, po=po, n=n):
                        cnt = _window_count(r, ch, n, w)
                        halves = []
                        for h in range(gch // LANE):
                            win = _Window(pad.at[h], (r, po), ch)
                            halves.append(_view_sum(win, -(w // 2), w // 2 - 1) / cnt - win.at(0))
                        pooled = jnp.concatenate(halves, axis=1).astype(MXU)
                        pooled_ref[pl.ds(so + r, ch), :] = pooled
                        po_ref[pl.ds(so + r, ch), :] = _dot(pooled, pw_ref[0], NN) * sc_ref[...]

                    _chunks(n, ch, step)

    blk = pl.BlockSpec((R, gch), lambda b, g: (b, g))
    return _call(
        body, name="pool_fwd", grid=(B, len(POOL_WINDOWS)),
        in_specs=[blk, pl.BlockSpec((1, gch, gch), lambda b, g: (g, 0, 0)), pl.BlockSpec((1, gch), lambda b, g: (0, g))],
        out_specs=[blk, blk], out_shape=[_sds((B * R, POOL_CH), MXU), _sds((B * R, POOL_CH), F32)],
        scratch=[pltpu.VMEM((gch // LANE, _padded_rows(C, S), LANE), F32)])(z_pool, pool_w, pool_scale)


def _pool_bwd(dpo, pooled, pool_w, pool_scale, B, C, S, ch=128):
    R = C + S
    gch = POOL_GCH
    segs = _segments(C, S)

    def body(d_ref, p_ref, pw_ref, sc_ref, du_ref, dpw_ref, dsc_ref, su_ref, qpad, dpl):
        g, b = pl.program_id(0), pl.program_id(1)

        @pl.when(b == 0)
        def _():
            dpw_ref[...] = jnp.zeros_like(dpw_ref)
            dsc_ref[...] = jnp.zeros_like(dsc_ref)
            su_ref[...] = jnp.zeros_like(su_ref)

        for h in range(gch // LANE):
            _zero_gaps(qpad.at[h], C, S)
        pw = pw_ref[0]
        for gi, w in enumerate(POOL_WINDOWS):
            @pl.when(g == gi)
            def _(w=w):
                for so, po, n in segs:
                    def first(r, so=so, po=po, n=n):
                        pv = p_ref[pl.ds(so + r, ch), :]
                        dv = d_ref[pl.ds(so + r, ch), :]
                        dsc_ref[...] += _colsum(dv * _dot(pv, pw, NN))
                        dmx = (dv * sc_ref[...]).astype(MXU)
                        dpw_ref[0] += _dot(pv, dmx, TN)
                        dp = _dot(dmx, pw, NT)
                        dpl[pl.ds(so + r, ch), :] = dp
                        q = dp / _window_count(r, ch, n, w)
                        for h in range(gch // LANE):
                            qpad[h, pl.ds(po + r, ch), :] = q[:, h * LANE:(h + 1) * LANE]

                    _chunks(n, ch, first)
                for so, po, n in segs:
                    def second(r, so=so, po=po):
                        acc = jnp.concatenate([_view_sum(_Window(qpad.at[h], (r, po), ch), 1 - w // 2, w // 2)
                                               for h in range(gch // LANE)], axis=1)
                        du = acc - dpl[pl.ds(so + r, ch), :]
                        du_ref[pl.ds(so + r, ch), :] = du.astype(MXU)
                        su_ref[...] += _colsum(du)

                    _chunks(n, ch, second)

    blk = pl.BlockSpec((R, gch), lambda g, b: (b, g))
    vec = pl.BlockSpec((1, gch), lambda g, b: (0, g))
    wblk = pl.BlockSpec((1, gch, gch), lambda g, b: (g, 0, 0))
    return _call(
        body, name="pool_bwd", grid=(len(POOL_WINDOWS), B), in_specs=[blk, blk, wblk, vec], out_specs=[blk, wblk, vec, vec],
        out_shape=[_sds((B * R, POOL_CH), MXU), _sds((len(POOL_WINDOWS), gch, gch), F32), _sds((1, POOL_CH), F32),
                   _sds((1, POOL_CH), F32)],
        scratch=[pltpu.VMEM((gch // LANE, _padded_rows(C, S), LANE), F32), pltpu.VMEM((R, gch), F32)],
    )(dpo, pooled, pool_w, pool_scale)


def _silu_rows(cond):
    def body(c_ref, s_ref, d_ref):
        c = c_ref[...]
        sg = _sigmoid(c)
        s_ref[...] = (c * sg).astype(MXU)
        d_ref[...] = sg * (1.0 + c * (1.0 - sg))

    full = pl.BlockSpec(cond.shape, lambda i: (0, 0))
    return _call(body, name="silu_rows", grid=(1,), in_specs=[full], out_specs=[full, full],
                 out_shape=[_sds(cond.shape, MXU), _sds(cond.shape, F32)])(cond)


def _row_tile(rows, cols, n_bufs):
    cap = max(16, (16 * 2 ** 20) // (4 * n_bufs * max(cols, 128)))
    return rows if rows <= cap else _pick(rows, cap, 16)


def _adamw(parts, w, m, v, layer, prev, *, name):
    n_parts, rows, cols = parts.shape
    layers = w.shape[0]
    c1 = 1.0 - ADAM_B1 ** ADAM_STEP
    c2 = 1.0 - ADAM_B2 ** ADAM_STEP
    tr = _row_tile(rows, cols, n_parts + 7)

    def body(p_ref, w_ref, m_ref, v_ref, *rest):
        g_ref, d_ref, nm_ref, nv_ref = rest[-4:]
        g = p_ref[0].astype(F32)
        for k in range(1, n_parts):
            g = g + p_ref[k].astype(F32)
        nm = ADAM_B1 * m_ref[...] + (1.0 - ADAM_B1) * g
        nv = ADAM_B2 * v_ref[...] + (1.0 - ADAM_B2) * (g * g)
        g_ref[...] = g
        nm_ref[...] = nm
        nv_ref[...] = nv
        d_ref[...] = -ADAM_LR * ((nm / c1) / (jnp.sqrt(nv / c2) + ADAM_EPS) + ADAM_WD * w_ref[...])

    blk = pl.BlockSpec((None, tr, cols), lambda i: (layer, i, 0))
    in_specs = [pl.BlockSpec((n_parts, tr, cols), lambda i: (0, i, 0)), blk, blk, blk]
    args = [parts, w, m, v]
    aliases = {}
    if prev is not None:
        in_specs += [ANY] * 4
        aliases = {4 + k: k for k in range(4)}
        args += list(prev)
    return _call(body, name=name, grid=(rows // tr,), in_specs=in_specs, out_specs=[blk] * 4,
                 out_shape=[_sds((layers, rows, cols), F32)] * 4, aliases=aliases)(*args)


def _pair_sum(g, r1):
    _, rows, cols = g.shape
    c = lax.axis_index("c")
    g4 = g.reshape(4, 2, rows, cols)
    tr = _row_tile(rows, cols, 10)

    def body(c_ref, g_ref, r_ref, o_ref):
        o_ref[...] = (g_ref[...] + r_ref[...]).astype(WIRE)

    return _pcall(
        body, name="pair_sum", out_shape=_sds((4, rows, cols), WIRE),
        grid_spec=pltpu.PrefetchScalarGridSpec(
            num_scalar_prefetch=1, grid=(rows // tr,),
            in_specs=[pl.BlockSpec((4, None, tr, cols), lambda i, c_ref: (0, c_ref[0], i, 0)),
                      pl.BlockSpec((4, tr, cols), lambda i, c_ref: (0, i, 0))],
            out_specs=pl.BlockSpec((4, tr, cols), lambda i, c_ref: (0, i, 0))),
        compiler_params=pltpu.CompilerParams(dimension_semantics=("arbitrary",), vmem_limit_bytes=VMEM_MB * 2 ** 20),
    )(jnp.reshape(c, (1,)).astype(jnp.int32), g4, r1)


MESH = pl.DeviceIdType.MESH
ANY = pl.BlockSpec(memory_space=pl.ANY)


class _Exchange:
    inputs, out_shape, scratch = (), (), ()

    def start(self, ins, outs, sems):
        raise NotImplementedError

    def mid(self, ins, outs, sems):
        pass

    def finish(self, ins, outs, sems):
        raise NotImplementedError


def _run_exchange(ex, *, name):
    n_in, n_out = len(ex.inputs), len(ex.out_shape)

    def body(*refs):
        parts = refs[:n_in], refs[n_in:n_in + n_out], refs[n_in + n_out:]
        ex.start(*parts)
        ex.mid(*parts)
        ex.finish(*parts)

    return _pcall(body, name=name, out_shape=list(ex.out_shape), in_specs=[ANY] * n_in, out_specs=[ANY] * n_out,
                  scratch_shapes=list(ex.scratch))(*ex.inputs)


class _Gather(_Exchange):
    def __init__(self, shards, layer=None):
        self.inputs, self.layer, self.n_t = list(shards), layer, len(shards)
        self.out_shape = [_sds((N_DEV,) + tuple(s.shape if layer is None else s.shape[1:]), s.dtype) for s in shards]
        self.scratch = [pltpu.SemaphoreType.DMA((7 * self.n_t,)), pltpu.SemaphoreType.DMA((7 * self.n_t,)),
                        pltpu.SemaphoreType.DMA((self.n_t,))]

    def _place(self):
        x, y, c = lax.axis_index("x"), lax.axis_index("y"), lax.axis_index("c")
        return (x, y, c), (x, y, 1 - c), [(1 - x, y), (x, 1 - y), (1 - x, 1 - y)]

    def _own(self, ins, n):
        return ins[n] if self.layer is None else ins[n].at[self.layer]

    def _copy(self, ins, outs, sems, n, k, blk, to, own=False):
        dst = outs[n].at[4 * blk[0] + 2 * blk[1] + blk[2]]
        return pltpu.make_async_remote_copy(src_ref=self._own(ins, n) if own else dst, dst_ref=dst, send_sem=sems[0].at[n * 7 + k],
                                            recv_sem=sems[1].at[n * 7 + k], device_id=to, device_id_type=MESH)

    def _mine(self, ins, outs, sems, n):
        (x, y, c), _, _ = self._place()
        return pltpu.make_async_copy(self._own(ins, n), outs[n].at[4 * x + 2 * y + c], sems[2].at[n])

    def start(self, ins, outs, sems):
        me, sibling, chips = self._place()
        for n in range(self.n_t):
            self._mine(ins, outs, sems, n).start()
        for j, chip in enumerate(chips):
            for n in range(self.n_t):
                self._copy(ins, outs, sems, n, 1 + j, me, (*chip, me[2]), own=True).start()
        for n in range(self.n_t):
            self._copy(ins, outs, sems, n, 0, me, sibling, own=True).start()

    def mid(self, ins, outs, sems):
        me, sibling, chips = self._place()
        for j, chip in enumerate(chips):
            for n in range(self.n_t):
                self._copy(ins, outs, sems, n, 1 + j, (*chip, me[2]), me).wait_recv()
                self._copy(ins, outs, sems, n, 4 + j, (*chip, me[2]), sibling).start()

    def finish(self, ins, outs, sems):
        me, sibling, chips = self._place()
        for n in range(self.n_t):
            self._copy(ins, outs, sems, n, 0, sibling, me).wait_recv()
        for j, chip in enumerate(chips):
            for n in range(self.n_t):
                self._copy(ins, outs, sems, n, 4 + j, (*chip, 1 - me[2]), me).wait_recv()
        for j, chip in enumerate(chips):
            for n in range(self.n_t):
                self._copy(ins, outs, sems, n, 1 + j, me, (*chip, me[2]), own=True).wait_send()
                self._copy(ins, outs, sems, n, 4 + j, (*chip, me[2]), sibling).wait_send()
        for n in range(self.n_t):
            self._copy(ins, outs, sems, n, 0, me, sibling, own=True).wait_send()
            self._mine(ins, outs, sems, n).wait()


def _all_gather(shards, *, name, layer=None):
    return _run_exchange(_Gather(shards, layer), name=name)


class _Both(_Exchange):
    def __init__(self, a, b):
        self.a, self.b = a, b
        self.inputs = list(a.inputs) + list(b.inputs)
        self.out_shape = list(a.out_shape) + list(b.out_shape)
        self.scratch = list(a.scratch) + list(b.scratch)

    def _parts(self, ins, outs, sems):
        i, o, s = len(self.a.inputs), len(self.a.out_shape), len(self.a.scratch)
        return (ins[:i], outs[:o], sems[:s]), (ins[i:], outs[o:], sems[s:])

    def start(self, ins, outs, sems):
        pa, pb = self._parts(ins, outs, sems)
        self.a.start(*pa)
        self.b.start(*pb)

    def mid(self, ins, outs, sems):
        pa, pb = self._parts(ins, outs, sems)
        self.a.mid(*pa)
        self.b.mid(*pb)

    def finish(self, ins, outs, sems):
        pa, pb = self._parts(ins, outs, sems)
        self.a.finish(*pa)
        self.b.finish(*pb)


class _SiblingExchange(_Exchange):
    def __init__(self, gs):
        self.n_t = len(gs)
        self.inputs = [g.reshape((4, 2) + g.shape[1:]) for g in gs]
        self.out_shape = [_sds((4,) + g.shape[1:], g.dtype) for g in gs]
        self.scratch = [pltpu.SemaphoreType.DMA((self.n_t,)), pltpu.SemaphoreType.DMA((self.n_t,))]

    def _copy(self, ins, outs, sems, n):
        x, y, c = lax.axis_index("x"), lax.axis_index("y"), lax.axis_index("c")
        return pltpu.make_async_remote_copy(src_ref=ins[n].at[:, 1 - c], dst_ref=outs[n], send_sem=sems[0].at[n],
                                            recv_sem=sems[1].at[n], device_id=(x, y, 1 - c), device_id_type=MESH)

    def start(self, ins, outs, sems):
        for n in range(self.n_t):
            self._copy(ins, outs, sems, n).start()

    def finish(self, ins, outs, sems):
        for n in range(self.n_t):
            self._copy(ins, outs, sems, n).wait_recv()
        for n in range(self.n_t):
            self._copy(ins, outs, sems, n).wait_send()


class _ChipExchange(_Exchange):
    def __init__(self, ps):
        self.n_t, self.inputs = len(ps), list(ps)
        self.out_shape = [_sds(p.shape, p.dtype) for p in ps]
        self.scratch = [pltpu.SemaphoreType.DMA((3 * self.n_t,)), pltpu.SemaphoreType.DMA((3 * self.n_t,)),
                        pltpu.SemaphoreType.DMA((self.n_t,))]

    def _place(self):
        x, y, c = lax.axis_index("x"), lax.axis_index("y"), lax.axis_index("c")
        return 2 * x + y, c, [(1 - x, y), (x, 1 - y), (1 - x, 1 - y)]

    def _copy(self, ins, outs, sems, n, j, src_chip, dst_slot):
        _, c, chips = self._place()
        return pltpu.make_async_remote_copy(
            src_ref=ins[n].at[src_chip], dst_ref=outs[n].at[dst_slot], send_sem=sems[0].at[n * 3 + j],
            recv_sem=sems[1].at[n * 3 + j], device_id=(*chips[j], c), device_id_type=MESH)

    def _own(self, ins, outs, sems, n):
        mine, _, _ = self._place()
        return pltpu.make_async_copy(ins[n].at[mine], outs[n].at[mine], sems[2].at[n])

    def start(self, ins, outs, sems):
        mine, _, chips = self._place()
        for n in range(self.n_t):
            self._own(ins, outs, sems, n).start()
        for j, (px, py) in enumerate(chips):
            for n in range(self.n_t):
                self._copy(ins, outs, sems, n, j, 2 * px + py, mine).start()

    def finish(self, ins, outs, sems):
        mine, _, chips = self._place()
        for j, (px, py) in enumerate(chips):
            for n in range(self.n_t):
                self._copy(ins, outs, sems, n, j, mine, 2 * px + py).wait_recv()
        for j, (px, py) in enumerate(chips):
            for n in range(self.n_t):
                self._copy(ins, outs, sems, n, j, 2 * px + py, mine).wait_send()
        for n in range(self.n_t):
            self._own(ins, outs, sems, n).wait()


BIG = (("w_ada", (D, N_MOD * D // N_DEV), 1), ("w_in", (D, D_IN // N_DEV), 1), ("conv_pw_w", (CONV_CH // N_DEV, D), 0),
       ("pool_w", (len(POOL_WINDOWS), POOL_GCH // N_DEV, POOL_GCH), 1), ("w_out", (D // N_DEV, D), 0),
       ("w_up", (D, 2 * D_FF // N_DEV), 1), ("w_down", (D_FF // N_DEV, D), 0))
TAPS = (("conv_dw_w", (CONV_K, CONV_CH // N_DEV), 1), ("ffn_dw_w", (FFN_K, D_FF // N_DEV), 1))
SHARDED = BIG + TAPS
EARLY = (0, 1)
LATE = (2, 3, 4, 5, 6)
READY_0 = (2, 3, 4, 5, 6, 7, 8)
LAST_0 = (0, 1)
REPLICATED =(("b_ada", N_MOD * D), ("b_in", D_IN), ("q_gain", HD), ("k_gain", HD), ("conv_dw_b", CONV_CH), ("conv_ln_g", CONV_CH),
              ("conv_ln_b", CONV_CH), ("conv_pw_b", D), ("pool_scale", POOL_CH), ("b_out", D), ("ln1_g", D), ("ln1_b", D),
              ("ln2_g", D), ("ln2_b", D), ("ffn_dw_b", D_FF))


def _as_rows(shape):
    return (int(np.prod(shape[:-1])), shape[-1])


def _full_from_blocks(blocks, axis):
    moved = jnp.moveaxis(blocks, 0, axis)
    shape = list(moved.shape)
    shape[axis:axis + 2] = [shape[axis] * shape[axis + 1]]
    return moved.reshape(shape)


def _blocks_from_full(full, axis):
    shape = list(full.shape)
    shape[axis:axis + 1] = [N_DEV, shape[axis] // N_DEV]
    return jnp.moveaxis(full.reshape(shape), axis, 0)


SMALL_N = DEPTH * sum(n for _, n in REPLICATED) + D
SMALL_ROWS = -(-(SMALL_N + 1) // (8 * LANES)) * 8


def _rope_tables(C, S):
    t = np.arange(S)
    inv_freq = ROPE_THETA ** (-np.arange(ROPE_PAIRS, dtype=np.float32) / ROPE_PAIRS)
    row = jnp.asarray((t // GRID_W).astype(np.float32))[:, None] * jnp.asarray(inv_freq, F32)
    col = jnp.asarray((t % GRID_W).astype(np.float32))[:, None] * jnp.asarray(inv_freq, F32)
    cos = jnp.concatenate([jnp.cos(row), jnp.cos(row), jnp.cos(col), jnp.cos(col)], axis=1)
    sin = jnp.concatenate([-jnp.sin(row), jnp.sin(row), -jnp.sin(col), jnp.sin(col)], axis=1)
    cos = jnp.concatenate([jnp.ones((C, HD), F32), cos], axis=0)
    sin = jnp.concatenate([jnp.zeros((C, HD), F32), sin], axis=0)
    return cos, sin


def _segment_sums(parts, B, tpe, ncq):
    p = parts.reshape(B, tpe, D)
    return jnp.concatenate([jnp.sum(p[:, ncq:], axis=1), jnp.sum(p[:, :ncq], axis=(0, 1))[None]], axis=0)


def kernel(x, c, ctx, c_ctx, w_ada, b_ada, w_in, b_in, q_gain, k_gain, conv_dw_w, conv_dw_b, conv_ln_g, conv_ln_b, conv_pw_w, conv_pw_b, pool_w, pool_scale, w_out, b_out, ln1_g, ln1_b, ln2_g, ln2_b, w_up, ffn_dw_w, ffn_dw_b, w_down, loss_target, m_c_ctx, m_w_ada, m_b_ada, m_w_in, m_b_in, m_q_gain, m_k_gain, m_conv_dw_w, m_conv_dw_b, m_conv_ln_g, m_conv_ln_b, m_conv_pw_w, m_conv_pw_b, m_pool_w, m_pool_scale, m_w_out, m_b_out, m_ln1_g, m_ln1_b, m_ln2_g, m_ln2_b, m_w_up, m_ffn_dw_w, m_ffn_dw_b, m_w_down, v_c_ctx, v_w_ada, v_b_ada, v_w_in, v_b_in, v_q_gain, v_k_gain, v_conv_dw_w, v_conv_dw_b, v_conv_ln_g, v_conv_ln_b, v_conv_pw_w, v_conv_pw_b, v_pool_w, v_pool_scale, v_w_out, v_b_out, v_ln1_g, v_ln1_b, v_ln2_g, v_ln2_b, v_w_up, v_ffn_dw_w, v_ffn_dw_b, v_w_down):
    given = dict(locals())
    B, S, _ = x.shape
    C = ctx.shape[1]
    R = C + S
    T = B * R
    tpe, ncq = R // TM, C // TM
    nt = T // TM
    assert S % TM == 0 and C % TM == 0 and B + 1 <= 16

    operands = [given[n].astype(MXU) for n, _, _ in BIG]
    taps = _all_gather([given[n] for n, _, _ in TAPS], name="gather_taps")
    W, modt = [], []

    xu = jnp.concatenate([ctx, x], axis=1).reshape(T, D)
    cond = jnp.concatenate([c, c_ctx[None], jnp.zeros((16 - B - 1, D), F32)], axis=0)
    s_cond, ds_cond = _silu_rows(cond)
    ctx_tile = jnp.asarray((np.arange(tpe) < ncq)[None, :, None])
    cos_t, sin_t = _rope_tables(C, S)
    row = lambda v: v.reshape(1, -1)

    early_ops, late_ops = [operands[k] for k in EARLY], [operands[k] for k in LATE]

    def add_early(blocks):
        l = len(W)
        wl = {BIG[k][0]: _full_from_blocks(blk, BIG[k][2]) for k, blk in zip(EARLY, blocks)}
        for (n, _, a), blk in zip(TAPS, taps):
            wl[n] = _full_from_blocks(blk[:, l], a)
        W.append(wl)
        m = _matmul(s_cond, wl["w_ada"], "nn", name="ada", bias=row(b_ada[l]), tm=16, tn=1024)
        modt.append(jnp.where(ctx_tile, m[B][None, None, :], m[:B][:, None, :]).reshape(nt, 1, N_MOD * D))

    def add_late(l, blocks):
        for k, blk in zip(LATE, blocks):
            W[l][BIG[k][0]] = _full_from_blocks(blk, BIG[k][2])

    add_early(_all_gather(early_ops, name="gather_weights", layer=0))

    saved = []
    h1, h1t = _modulate_cast(xu, modt[0], 0)
    xin = xu
    for l in range(DEPTH):
        wl = W[l]
        w_inl, b_inl = wl["w_in"], b_in[l]
        z_qkv = _matmul(h1, w_inl[:, :QKV_W], "nn", name="z_qkv", bias=row(b_inl[:QKV_W]), tn=768)
        c0, p0, g0 = QKV_W, QKV_W + 2 * CONV_CH, QKV_W + 2 * CONV_CH + POOL_CH
        z_conv = _matmul(h1, w_inl[:, c0:p0], "nn", name="z_conv", bias=row(b_inl[c0:p0]))
        z_pool = _matmul(h1, w_inl[:, p0:g0], "nn", name="z_pool", bias=row(b_inl[p0:g0]))
        z_gate = _matmul(h1, w_inl[:, g0:], "nn", name="z_gate", bias=row(b_inl[g0:]))
        qkv = _qk_prep(z_qkv, cos_t, sin_t, row(q_gain[l]), row(k_gain[l]), tpe)
        if l + 1 < DEPTH:
            if l == 0:
                attn, res = _attn_fwd(qkv, B, C, R, carry=_Both(_Gather(early_ops, 1), _Gather(late_ops, 0)))
                add_late(0, res[len(EARLY):])
            else:
                attn, res = _attn_fwd(qkv, B, C, R, carry=_Gather(early_ops, l + 1))
            add_early(res[:len(EARLY)])
        else:
            attn, _ = _attn_fwd(qkv, B, C, R)
        dw32 =jnp.pad(wl["conv_dw_w"], ((0, 32 - CONV_K), (0, 0)))
        hc = _conv_fwd(z_conv, dw32, row(conv_dw_b[l]), B, C, S)
        sw, swt = _ln_silu(hc, row(conv_ln_g[l]), row(conv_ln_b[l]))
        conv_o = _matmul(sw, wl["conv_pw_w"], "nn", name="conv_pw", bias=row(conv_pw_b[l]))
        pooled, pool_o = _pool_fwd(z_pool, wl["pool_w"], row(pool_scale[l]), B, C, S)
        m, mt = _merge(attn, conv_o, pool_o, z_gate)
        mo = _matmul(m, wl["w_out"], "nn", name="w_out", bias=row(b_out[l]))
        y1, h2, h2t = _resid_ln(xin, mo, modt[l], 2, row(ln1_g[l]), row(ln1_b[l]), modt[l], 3)
        up = _matmul(h2, wl["w_up"], "nn", name="w_up", tn=1408)
        fw8 = jnp.pad(wl["ffn_dw_w"], ((0, 8 - FFN_K), (0, 0)))
        if l + 1 < DEPTH:
            f, ft, late_blocks = _ffn_mid(up, fw8, row(ffn_dw_b[l]), B, C, S, carry=_Gather(late_ops, l + 1))
            add_late(l + 1, late_blocks)
        else:
            f, ft, _ = _ffn_mid(up, fw8, row(ffn_dw_b[l]), B, C, S)
        fo = _matmul(f, wl["w_down"], "nn", name="w_down", tm=512)
        if l + 1 < DEPTH:
            y2, h_next, ht_next = _resid_ln(y1, fo, modt[l], 5, row(ln2_g[l]), row(ln2_b[l]), modt[l + 1], 0)
        else:
            y2, h_next, ht_next = _resid_ln(y1, fo, modt[l], 5, row(ln2_g[l]), row(ln2_b[l])), None, None
        saved.append(dict(xin=xin, h1t=h1t, z_qkv=z_qkv, z_conv=z_conv, z_gate=z_gate, qkv=qkv, attn=attn, hc=hc, swt=swt,
                          conv_o=conv_o, pooled=pooled, pool_o=pool_o, mt=mt, mo=mo, y1=y1, h2t=h2t, up=up, ft=ft, fo=fo,
                          dw32=dw32, fw8=fw8))
        xin, h1, h1t = y2, h_next, ht_next

    dy, loss_part = _loss_grad(xin, loss_target.reshape(B * S, D), tpe, ncq)

    small = {n: [None] * DEPTH for n, _ in REPLICATED}
    d_c_ctx = jnp.zeros((D,), F32)
    dmods_t = [[None] * N_MOD for _ in range(DEPTH)]
    layer_grads = [None] * DEPTH
    kinds = ("grad_", "delta_", "new_m_", "new_v_")
    stacks = {n: [given[pre + n].reshape((DEPTH,) + _as_rows(s)) for pre in ("", "m_", "v_")] for n, s, _ in SHARDED}
    results = {n: None for n, _, _ in SHARDED}

    every = tuple(range(len(SHARDED)))

    def block_major(l, which=every):
        return [_blocks_from_full(layer_grads[l][SHARDED[k][0]], SHARDED[k][2]).reshape((N_DEV,) + _as_rows(SHARDED[k][1]))
                for k in which]

    def reduce_prepare(l, which=every):
        nonlocal d_c_ctx
        dmods = jnp.concatenate([_segment_sums(p, B, tpe, ncq) for p in dmods_t[l]], axis=1)
        small["b_ada"][l] = jnp.sum(dmods, axis=0)
        dm16 = jnp.concatenate([dmods, jnp.zeros((16 - B - 1, N_MOD * D), F32)], axis=0).astype(MXU)
        layer_grads[l]["w_ada"] = _matmul(s_cond, dm16, "tn", name="g_w_ada", tm=1024, tn=1024)
        dcond = _matmul(dm16, W[l]["w_ada"], "nt", name="d_cond", tm=16, tn=1024, tk=2048)
        d_c_ctx = d_c_ctx + dcond[B] * ds_cond[B]
        return block_major(l, which)

    def reduce_end(l, r2, which=every):
        for k, parts in zip(which, r2):
            n = SHARDED[k][0]
            results[n] = _adamw(parts, *stacks[n], l, results[n], name="adamw_sharded")

    dh1 = None
    for l in reversed(range(DEPTH)):
        gs_above = None
        wl, sv = W[l], saved[l]
        dmod = dmods_t[l]
        if dh1 is None:
            dy1p, dfo, dgate2, dg, db, _ = _ln_bwd(dy, sv["y1"], sv["fo"], modt[l], 5, row(ln2_g[l]))
        else:
            dy1p, dfo, dgate2, dg, db, _, dsh, dsc = _ln_bwd(dy, sv["y1"], sv["fo"], modt[l], 5, row(ln2_g[l]), dh=dh1,
                                                             y=saved[l + 1]["xin"], mod_next=modt[l + 1], k_shift_next=0)
            dmods_t[l + 1][0], dmods_t[l + 1][1] = dsh, dsc
            gs_above = reduce_prepare(l + 1)
        small["ln2_g"][l], small["ln2_b"][l] = dg[0], db[0]
        dmod[5] = dgate2
        df = _matmul(dfo, wl["w_down"], "nt", name="d_f", tn=1408)
        g_w_down = _matmul(sv["ft"], dfo, "nn", name="g_w_down", tm=1408, tk=2304)
        if gs_above is None:
            da2, du2, g_fdw, g_fdb, _ = _ffn_mid_bwd(df, sv["up"], sv["fw8"], row(ffn_dw_b[l]), B, C, S)
            ps_above = None
        else:
            da2, du2, g_fdw, g_fdb, r1 = _ffn_mid_bwd(df, sv["up"], sv["fw8"], row(ffn_dw_b[l]), B, C, S,
                                                      carry=_SiblingExchange(gs_above))
            ps_above = [_pair_sum(g, r) for g, r in zip(gs_above, r1)]
        small["ffn_dw_b"][l] = g_fdb[0]
        dh2 = _matmul(da2, wl["w_up"][:, :D_FF], "nt", name="d_h2a", tm=512)
        dh2 = _matmul(du2, wl["w_up"][:, D_FF:], "nt", name="d_h2u", tm=512, acc_in=dh2)
        g_w_up = _matmul(sv["h2t"], da2, "nn", name="g_w_up_a", tn=1408, tk=2304, into=(2 * D_FF, 0, None))
        g_w_up = _matmul(sv["h2t"], du2, "nn", name="g_w_up_u", tn=1408, tk=2304, into=(2 * D_FF, D_FF, g_w_up))
        dxp, dmo, dgate1, dg, db, dbo, dsh, dsc = _ln_bwd(dy1p, sv["xin"], sv["mo"], modt[l], 2, row(ln1_g[l]), dh=dh2,
                                                          y=sv["y1"], mod_next=modt[l], k_shift_next=3)
        small["ln1_g"][l], small["ln1_b"][l], small["b_out"][l] = dg[0], db[0], dbo[0]
        dmod[2], dmod[3], dmod[4] = dgate1, dsh, dsc
        dm = _matmul(dmo, wl["w_out"], "nt", name="d_m")
        g_w_out = _matmul(sv["mt"], dmo, "nn", name="g_w_out", tk=2304)
        dattn, dconv_o, dpool_o, dzg, g_pwb, gsum_gate = _merge_bwd(dm, sv["attn"], sv["conv_o"], sv["pool_o"], sv["z_gate"])
        small["conv_pw_b"][l] = g_pwb[0]
        du, g_pool_w, g_pool_sc, gsum_pool = _pool_bwd(dpool_o, sv["pooled"], wl["pool_w"], row(pool_scale[l]), B, C, S)
        small["pool_scale"][l] = g_pool_sc[0]
        dsw = _matmul(dconv_o, wl["conv_pw_w"], "nt", name="d_sw")
        g_pw = _matmul(sv["swt"], dconv_o, "nn", name="g_conv_pw", tk=2304)
        dhc, g_cg, g_cb, g_cdb = _ln_silu_bwd(dsw, sv["hc"], row(conv_ln_g[l]), row(conv_ln_b[l]))
        small["conv_ln_g"][l], small["conv_ln_b"][l], small["conv_dw_b"][l] = g_cg[0], g_cb[0], g_cdb[0]
        da, dgt, g_cdw, gsum_a, gsum_gt = _conv_bwd(dhc, sv["z_conv"], sv["dw32"], B, C, S)
        layer_grads[l] = {"conv_pw_w": g_pw, "pool_w": g_pool_w, "w_out": g_w_out, "w_up": g_w_up, "w_down": g_w_down,
                          "conv_dw_w": g_cdw[:CONV_K], "ffn_dw_w": g_fdw[:FFN_K]}
        if ps_above is None:
            dq, dk, dv, _ = _attn_bwd(sv["qkv"], sv["attn"], dattn, B, C, R)
        elif l > 0:
            dq, dk, dv, r2 = _attn_bwd(sv["qkv"], sv["attn"], dattn, B, C, R, carry=_ChipExchange(ps_above))
            reduce_end(l + 1, r2)
        else:
            gs = block_major(0, READY_0)
            r1 = _run_exchange(_SiblingExchange(gs), name="sibling_exchange")
            ps_own = [_pair_sum(g, r) for g, r in zip(gs, r1)]
            dq, dk, dv, r2 = _attn_bwd(sv["qkv"], sv["attn"], dattn, B, C, R,
                                       carry=_Both(_ChipExchange(ps_above), _ChipExchange(ps_own)))
            reduce_end(1, r2[:len(SHARDED)])
            reduce_end(0, r2[len(SHARDED):], READY_0)
        dz_qkv, g_qg, g_kg, gsum_qkv = _qk_bwd(dq, dk, dv, sv["z_qkv"], cos_t, sin_t, row(q_gain[l]), row(k_gain[l]), tpe)
        small["q_gain"][l], small["k_gain"][l] = g_qg[0], g_kg[0]
        small["b_in"][l] = jnp.concatenate([gsum_qkv[0], gsum_a[0], gsum_gt[0], gsum_pool[0], gsum_gate[0]])
        w_inl = wl["w_in"]
        g0 = QKV_W + 2 * CONV_CH + POOL_CH
        pieces = ((dzg, g0, N_GATE), (da, QKV_W, CONV_CH), (dgt, QKV_W + CONV_CH, CONV_CH), (du, QKV_W + 2 * CONV_CH, POOL_CH),
                  (dz_qkv, 0, QKV_W))
        dh1 = g_w_in = None
        at = 0
        for k, (dz, c0, wd) in enumerate(pieces):
            dh1 = _matmul(dz, w_inl[:, c0:c0 + wd], "nt", name=f"d_h1_{k}", tm=512, acc_in=dh1)
            g_w_in = _matmul(sv["h1t"], dz, "nn", name=f"g_w_in_{k}", tk=2304, into=(D_IN, at, g_w_in))
            at += wd
        g_w_in = jnp.concatenate([g_w_in[:, N_GATE + 2 * CONV_CH + POOL_CH:], g_w_in[:, N_GATE:N_GATE + 2 * CONV_CH + POOL_CH],
                                  g_w_in[:, :N_GATE]], axis=1)

        layer_grads[l]["w_in"] = g_w_in
        dy = dxp
    gx_u, dmods_t[0][0], dmods_t[0][1] = _mod_bwd(dy, dh1, saved[0]["xin"], modt[0], 0)
    grad_x = gx_u.reshape(B, R, D)[:, C:]

    last = every if DEPTH == 1 else LAST_0
    gs = reduce_prepare(0, last)
    r1 = _run_exchange(_SiblingExchange(gs), name="sibling_exchange")
    reduce_end(0, _run_exchange(_ChipExchange([_pair_sum(g, r) for g, r in zip(gs, r1)]), name="chip_exchange"), last)
    outs = {}
    for n, s, _ in SHARDED:
        for kind, buf in zip(kinds, results[n]):
            outs[kind + n] = buf.reshape((DEPTH,) + tuple(s))

    def small_pack(pieces):
        flat = jnp.concatenate([p.reshape(-1) for p in pieces])
        return jnp.pad(flat, (0, SMALL_ROWS * LANES - flat.shape[0])).reshape(SMALL_ROWS, LANES)

    zero1 = jnp.zeros((1,), F32)
    g_pack = small_pack([small[n][l] for n, _ in REPLICATED for l in range(DEPTH)] + [d_c_ctx, loss_part[0, :1]])
    g_small, = _all_gather([g_pack], name="gather_small")
    wmv = [small_pack([given[pre + n] for n, _ in REPLICATED] + [given[pre + "c_ctx"], zero1])[None] for pre in ("", "m_", "v_")]
    res = _adamw(g_small, *wmv, 0, None, name="adamw_small")
    for kind, buf in zip(kinds, res):
        flat = buf.reshape(-1)
        off = 0
        for n, sz in REPLICATED:
            outs[kind + n] = flat[off:off + DEPTH * sz].reshape(DEPTH, sz)
            off += DEPTH * sz
        outs[kind + "c_ctx"] = flat[off:off + D]
        if kind == "grad_":
            loss = flat[off + D]

    names = ["c_ctx", "w_ada", "b_ada", "w_in", "b_in", "q_gain", "k_gain", "conv_dw_w", "conv_dw_b", "conv_ln_g", "conv_ln_b",
             "conv_pw_w", "conv_pw_b", "pool_w", "pool_scale", "w_out", "b_out", "ln1_g", "ln1_b", "ln2_g", "ln2_b", "w_up",
             "ffn_dw_w", "ffn_dw_b", "w_down"]
    return (loss, grad_x, *[outs[k + n] for k in ("grad_", "delta_", "new_m_", "new_v_") for n in names])
```

```python
import functools

import jax
import jax.numpy as jnp
import numpy as np
from jax import lax
from jax.experimental import pallas as pl
from jax.experimental.pallas import tpu as pltpu

F32 = jnp.float32
MXU = jnp.bfloat16
WIRE = jnp.bfloat16

D = 1024
HD = 128
NH = 8
NKV = 2
QG = NH // NKV
KV_W = NKV * HD
QKV_W = NH * HD + 2 * KV_W
CONV_CH = D
POOL_CH = D
POOL_WINDOWS = (2, 4, 8, 16)
POOL_GCH = POOL_CH // len(POOL_WINDOWS)
N_GATE = 3 * D
D_IN = QKV_W + 2 * CONV_CH + POOL_CH + N_GATE
D_FF = 2816
N_MOD = 6
DEPTH = 4
CONV_K = 31
FFN_K = 3
GRID_W = 64
ROPE_THETA = 10000.0
ROPE_PAIRS = HD // 4
ALPHA = (2 * DEPTH) ** 0.25
LN_EPS = 1e-5
RMS_EPS = 1e-6
ATTN_SCALE = HD ** -0.5
LOG2_E = 1.4426950408889634
ADAM_LR, ADAM_B1, ADAM_B2, ADAM_EPS, ADAM_WD, ADAM_STEP = 0.001, 0.9, 0.999, 1e-08, 0.01, 10

N_DEV = 8
TM = 256
GAP = 16
LANES = 1024
LANE = 128
VMEM_MB = 48

NN = (((1,), (0,)), ((), ()))
NT = (((1,), (1,)), ((), ()))
TN = (((0,), (0,)), ((), ()))

_pcall = pl.pallas_call


def _call(body, *, name, grid, in_specs, out_specs, out_shape, scratch=(), aliases=None, vmem=VMEM_MB):
    return _pcall(
        body, name=name, grid=grid, in_specs=in_specs, out_specs=out_specs, out_shape=out_shape,
        scratch_shapes=list(scratch), input_output_aliases=aliases or {},
        compiler_params=pltpu.CompilerParams(dimension_semantics=("arbitrary",) * len(grid), vmem_limit_bytes=vmem * 2 ** 20),
    )


def _sds(shape, dtype):
    return jax.ShapeDtypeStruct(tuple(shape), dtype)


def _pick(n, cap, mult):
    best = None
    for t in range(mult, min(n, cap) + 1, mult):
        if n % t == 0:
            best = t
    return best if best is not None else n


def _dot(a, b, dims):
    return lax.dot_general(a, b, dims, preferred_element_type=F32)


def _sigmoid(x):
    return 1.0 / (1.0 + jnp.exp(-x))


def _matmul(a, b, mode, *, name, bias=None, acc_in=None, into=None, out_dtype=F32, tm=1024, tn=1024, tk=None):
    if mode == "nn":
        (M, K), (K2, N) = a.shape, b.shape
    elif mode == "nt":
        (M, K), (N, K2) = a.shape, b.shape
    else:
        (K, M), (K2, N) = a.shape, b.shape
    assert K == K2, (a.shape, b.shape, mode)
    tm = _pick(M, tm, 16)
    tn = _pick(N, tn, 128)
    tk = K if tk is None else _pick(K, tk, 128 if mode != "tn" else 16)
    gk = K // tk
    dims = {"nn": NN, "nt": NT, "tn": TN}[mode]
    a_spec = pl.BlockSpec((tk, tm), lambda j, i, k: (k, i)) if mode == "tn" else pl.BlockSpec((tm, tk), lambda j, i, k: (i, k))
    b_spec = pl.BlockSpec((tn, tk), lambda j, i, k: (j, k)) if mode == "nt" else pl.BlockSpec((tk, tn), lambda j, i, k: (k, j))
    in_specs, args = [a_spec, b_spec], [a, b]
    if bias is not None:
        in_specs.append(pl.BlockSpec((1, tn), lambda j, i, k: (0, j)))
        args.append(bias)
    aliases = {}
    if acc_in is not None:
        aliases = {len(args): 0}
        in_specs.append(pl.BlockSpec((tm, tn), lambda j, i, k: (i, j)))
        args.append(acc_in)
    n_total, col0, prev = (N, 0, None) if into is None else into
    assert col0 % tn == 0
    jb = col0 // tn
    if prev is not None:
        aliases = {len(args): 0}
        in_specs.append(pl.BlockSpec(memory_space=pl.ANY))
        args.append(prev)
    n_in = len(args)

    def body(*refs):
        a_ref, b_ref = refs[0], refs[1]
        pos = 2
        bias_ref = acc_in_ref = None
        if bias is not None:
            bias_ref = refs[pos]
            pos += 1
        if acc_in is not None:
            acc_in_ref = refs[pos]
        pos = n_in
        o_ref = refs[pos]
        part = _dot(a_ref[...].astype(MXU), b_ref[...].astype(MXU), dims)

        def finish(acc):
            if bias_ref is not None:
                acc = acc + bias_ref[...]
            if acc_in_ref is not None:
                acc = acc + acc_in_ref[...]
            o_ref[...] = acc.astype(out_dtype)

        if gk == 1:
            finish(part)
        else:
            acc_ref = refs[pos + 1]
            k = pl.program_id(2)

            @pl.when(k == 0)
            def _():
                acc_ref[...] = part

            @pl.when(k > 0)
            def _():
                acc_ref[...] += part

            @pl.when(k == gk - 1)
            def _():
                finish(acc_ref[...])

    return _call(
        body, name=name, grid=(N // tn, M // tm, gk), in_specs=in_specs,
        out_specs=pl.BlockSpec((tm, tn), lambda j, i, k: (i, j + jb)), out_shape=_sds((M, n_total), out_dtype),
        scratch=[pltpu.VMEM((tm, tn), F32)] if gk > 1 else [], aliases=aliases,
    )(*args)


def _rt(w, cb=0):
    return pl.BlockSpec((TM, w), lambda i: (i, cb))


def _ct(w):
    return pl.BlockSpec((w, TM), lambda i: (0, i))


def _vec(w):
    return pl.BlockSpec((1, w), lambda i: (0, 0))


def _part(w):
    return pl.BlockSpec((1, 1, w), lambda i: (i, 0, 0))


def _mod(ref, k):
    return ref[0, :, k * D:(k + 1) * D]


def _colsum(x):
    return jnp.sum(x, axis=0, keepdims=True)


def _ln_stats(s):
    mu = jnp.mean(s, axis=1, keepdims=True)
    cen = s - mu
    var = jnp.mean(cen * cen, axis=1, keepdims=True)
    rstd = lax.rsqrt(var + LN_EPS)
    return cen * rstd, rstd


def _modulate_cast(x, modt, k_shift):
    T = x.shape[0]
    nt = T // TM

    def body(x_ref, mod_ref, h_ref, ht_ref):
        h = (x_ref[...] * (1.0 + _mod(mod_ref, k_shift + 1)) + _mod(mod_ref, k_shift)).astype(MXU)
        h_ref[...] = h
        ht_ref[...] = h.T

    return _call(body, name="modulate", grid=(nt,), in_specs=[_rt(D), _part(N_MOD * D)], out_specs=[_rt(D), _ct(D)],
                 out_shape=[_sds((T, D), MXU), _sds((D, T), MXU)])(x, modt)


def _resid_ln(x, br, modt, k_gate, g, b, mod_next=None, k_shift_next=0):
    T = x.shape[0]
    nt = T // TM
    with_h = mod_next is not None

    def body(*refs):
        x_ref, br_ref, mod_ref, g_ref, b_ref = refs[:5]
        s = ALPHA * x_ref[...] + _mod(mod_ref, k_gate) * br_ref[...]
        xhat, _ = _ln_stats(s)
        y = xhat * g_ref[...] + b_ref[...]
        if with_h:
            modn_ref, y_ref, h_ref, ht_ref = refs[5:]
            y_ref[...] = y
            h = (y * (1.0 + _mod(modn_ref, k_shift_next + 1)) + _mod(modn_ref, k_shift_next)).astype(MXU)
            h_ref[...] = h
            ht_ref[...] = h.T
        else:
            refs[5][...] = y

    in_specs = [_rt(D), _rt(D), _part(N_MOD * D), _vec(D), _vec(D)]
    args = [x, br, modt, g, b]
    if with_h:
        in_specs.append(_part(N_MOD * D))
        args.append(mod_next)
        return _call(body, name="resid_ln_mod", grid=(nt,), in_specs=in_specs, out_specs=[_rt(D), _rt(D), _ct(D)],
                     out_shape=[_sds((T, D), F32), _sds((T, D), MXU), _sds((D, T), MXU)])(*args)
    return _call(body, name="resid_ln", grid=(nt,), in_specs=in_specs, out_specs=_rt(D), out_shape=_sds((T, D), F32))(*args)


def _ln_bwd(dy_part, x, br, modt, k_gate, g, dh=None, y=None, mod_next=None, k_shift_next=0):
    T = x.shape[0]
    nt = T // TM
    with_h = dh is not None

    def body(*refs):
        if with_h:
            dyp_ref, x_ref, br_ref, mod_ref, g_ref, dh_ref, y_ref, modn_ref = refs[:8]
            outs = refs[8:]
        else:
            dyp_ref, x_ref, br_ref, mod_ref, g_ref = refs[:5]
            outs = refs[5:]
        dx_ref, dbr_ref, dgate_ref, dlg_ref, dlb_ref, dbsum_ref = outs[:6]
        i = pl.program_id(0)

        @pl.when(i == 0)
        def _():
            dlg_ref[...] = jnp.zeros_like(dlg_ref)
            dlb_ref[...] = jnp.zeros_like(dlb_ref)
            dbsum_ref[...] = jnp.zeros_like(dbsum_ref)

        dy = dyp_ref[...]
        if with_h:
            dshift_ref, dscale_ref = outs[6:]
            dhv = dh_ref[...]
            dy = dy + dhv * (1.0 + _mod(modn_ref, k_shift_next + 1))
            dshift_ref[0] = _colsum(dhv)
            dscale_ref[0] = _colsum(dhv * y_ref[...])
        gate = _mod(mod_ref, k_gate)
        brv = br_ref[...]
        s = ALPHA * x_ref[...] + gate * brv
        xhat, rstd = _ln_stats(s)
        dlg_ref[...] += _colsum(dy * xhat)
        dlb_ref[...] += _colsum(dy)
        dyg = dy * g_ref[...]
        m1 = jnp.mean(dyg, axis=1, keepdims=True)
        m2 = jnp.mean(dyg * xhat, axis=1, keepdims=True)
        ds = rstd * (dyg - m1 - xhat * m2)
        dx_ref[...] = ALPHA * ds
        dbr = gate * ds
        dbr_ref[...] = dbr.astype(MXU)
        dbsum_ref[...] += _colsum(dbr)
        dgate_ref[0] = _colsum(ds * brv)

    in_specs = [_rt(D), _rt(D), _rt(D), _part(N_MOD * D), _vec(D)]
    args = [dy_part, x, br, modt, g]
    out_specs = [_rt(D), _rt(D), _part(D), _vec(D), _vec(D), _vec(D)]
    out_shape = [_sds((T, D), F32), _sds((T, D), MXU), _sds((nt, 1, D), F32), _sds((1, D), F32), _sds((1, D), F32), _sds((1, D), F32)]
    if with_h:
        in_specs += [_rt(D), _rt(D), _part(N_MOD * D)]
        args += [dh, y, mod_next]
        out_specs += [_part(D), _part(D)]
        out_shape += [_sds((nt, 1, D), F32), _sds((nt, 1, D), F32)]
    return _call(body, name="ln_bwd_mod" if with_h else "ln_bwd", grid=(nt,), in_specs=in_specs, out_specs=out_specs,
                 out_shape=out_shape)(*args)


def _mod_bwd(dx_part, dh, x, modt, k_shift):
    T = x.shape[0]
    nt = T // TM

    def body(dxp_ref, dh_ref, x_ref, mod_ref, dx_ref, dshift_ref, dscale_ref):
        dhv = dh_ref[...]
        dx_ref[...] = dxp_ref[...] + dhv * (1.0 + _mod(mod_ref, k_shift + 1))
        dshift_ref[0] = _colsum(dhv)
        dscale_ref[0] = _colsum(dhv * x_ref[...])

    return _call(body, name="mod_bwd", grid=(nt,), in_specs=[_rt(D), _rt(D), _rt(D), _part(N_MOD * D)],
                 out_specs=[_rt(D), _part(D), _part(D)],
                 out_shape=[_sds((T, D), F32), _sds((nt, 1, D), F32), _sds((nt, 1, D), F32)])(dx_part, dh, x, modt)


def _loss_grad(y, target, tpe, ncq):
    T = y.shape[0]
    nt = T // TM
    nl = tpe - ncq

    def body(y_ref, t_ref, dy_ref, loss_ref):
        i = pl.program_id(0)

        @pl.when(i == 0)
        def _():
            loss_ref[...] = jnp.zeros_like(loss_ref)

        @pl.when(i % tpe < ncq)
        def _():
            dy_ref[...] = jnp.zeros_like(dy_ref)

        @pl.when(i % tpe >= ncq)
        def _():
            err = y_ref[...] - t_ref[...]
            dy_ref[...] = err * (1.0 / D)
            loss_ref[...] += (0.5 / D) * jnp.sum(_colsum(err * err), axis=1, keepdims=True)

    tgt_spec = pl.BlockSpec((TM, D), lambda i: ((i // tpe) * nl + jnp.maximum(i % tpe - ncq, 0), 0))
    return _call(body, name="loss_grad", grid=(nt,), in_specs=[_rt(D), tgt_spec], out_specs=[_rt(D), _vec(128)],
                 out_shape=[_sds((T, D), F32), _sds((1, 128), F32)])(y, target)


def _rope_partner(x):
    lane = lax.broadcasted_iota(jnp.int32, x.shape, 1)
    first = (lane % (2 * ROPE_PAIRS)) < ROPE_PAIRS
    return jnp.where(first, pltpu.roll(x, HD - ROPE_PAIRS, 1), pltpu.roll(x, ROPE_PAIRS, 1))


def _qk_prep(z_qkv, cos_t, sin_t, q_gain, k_gain, tpe):
    T = z_qkv.shape[0]
    nt = T // TM

    def body(z_ref, cos_ref, sin_ref, qg_ref, kg_ref, o_ref):
        cos, sin = cos_ref[...], sin_ref[...]
        for h in range(NH + NKV):
            sl = slice(h * HD, (h + 1) * HD)
            t = z_ref[:, sl]
            gain = qg_ref[...] if h < NH else kg_ref[...]
            n = t * lax.rsqrt(jnp.mean(t * t, axis=1, keepdims=True) + RMS_EPS) * gain
            o_ref[:, sl] = (n * cos + _rope_partner(n) * sin).astype(MXU)
        o_ref[:, (NH + NKV) * HD:] = z_ref[:, (NH + NKV) * HD:].astype(MXU)

    tab = pl.BlockSpec((TM, HD), lambda i: (i % tpe, 0))
    return _call(body, name="qk_prep", grid=(nt,), in_specs=[_rt(QKV_W), tab, tab, _vec(HD), _vec(HD)], out_specs=_rt(QKV_W),
                 out_shape=_sds((T, QKV_W), MXU))(z_qkv, cos_t, sin_t, q_gain, k_gain)


def _qk_bwd(dq, dk, dv, z_qkv, cos_t, sin_t, q_gain, k_gain, tpe):
    T = z_qkv.shape[0]
    nt = T // TM

    def body(dq_ref, dk_ref, dv_ref, z_ref, cos_ref, sin_ref, qg_ref, kg_ref, dz_ref, dqg_ref, dkg_ref, bsum_ref):
        i = pl.program_id(0)

        @pl.when(i == 0)
        def _():
            dqg_ref[...] = jnp.zeros_like(dqg_ref)
            dkg_ref[...] = jnp.zeros_like(dkg_ref)
            bsum_ref[...] = jnp.zeros_like(bsum_ref)

        cos, sin = cos_ref[...], sin_ref[...]
        for h in range(NH + NKV):
            sl = slice(h * HD, (h + 1) * HD)
            dr = dq_ref[:, sl] if h < NH else dk_ref[:, (h - NH) * HD:(h - NH + 1) * HD]
            gain = qg_ref[...] if h < NH else kg_ref[...]
            dn = dr * cos + _rope_partner(dr * sin)
            t = z_ref[:, sl]
            rstd = lax.rsqrt(jnp.mean(t * t, axis=1, keepdims=True) + RMS_EPS)
            that = t * rstd
            dgain = _colsum(dn * that)
            if h < NH:
                dqg_ref[...] += dgain
            else:
                dkg_ref[...] += dgain
            dthat = dn * gain
            dt = rstd * (dthat - that * jnp.mean(dthat * that, axis=1, keepdims=True))
            dz_ref[:, sl] = dt.astype(MXU)
            bsum_ref[:, sl] += _colsum(dt)
        dvv = dv_ref[...]
        dz_ref[:, (NH + NKV) * HD:] = dvv.astype(MXU)
        bsum_ref[:, (NH + NKV) * HD:] += _colsum(dvv)

    tab = pl.BlockSpec((TM, HD), lambda i: (i % tpe, 0))
    return _call(body, name="qk_bwd", grid=(nt,),
                 in_specs=[_rt(NH * HD), _rt(KV_W), _rt(KV_W), _rt(QKV_W), tab, tab, _vec(HD), _vec(HD)],
                 out_specs=[_rt(QKV_W), _vec(HD), _vec(HD), _vec(QKV_W)],
                 out_shape=[_sds((T, QKV_W), MXU), _sds((1, HD), F32), _sds((1, HD), F32), _sds((1, QKV_W), F32)],
                 )(dq, dk, dv, z_qkv, cos_t, sin_t, q_gain, k_gain)


def _ln_silu(hc, g, b):
    T = hc.shape[0]

    def body(h_ref, g_ref, b_ref, o_ref, ot_ref):
        xhat, _ = _ln_stats(h_ref[...])
        n = xhat * g_ref[...] + b_ref[...]
        sw = (n * _sigmoid(n)).astype(MXU)
        o_ref[...] = sw
        ot_ref[...] = sw.T

    return _call(body, name="ln_silu", grid=(T // TM,), in_specs=[_rt(D), _vec(D), _vec(D)], out_specs=[_rt(D), _ct(D)],
                 out_shape=[_sds((T, D), MXU), _sds((D, T), MXU)])(hc, g, b)


def _ln_silu_bwd(dsw, hc, g, b):
    T = hc.shape[0]

    def body(d_ref, h_ref, g_ref, b_ref, dh_ref, dg_ref, db_ref, dcb_ref):
        i = pl.program_id(0)

        @pl.when(i == 0)
        def _():
            dg_ref[...] = jnp.zeros_like(dg_ref)
            db_ref[...] = jnp.zeros_like(db_ref)
            dcb_ref[...] = jnp.zeros_like(dcb_ref)

        xhat, rstd = _ln_stats(h_ref[...])
        n = xhat * g_ref[...] + b_ref[...]
        sg = _sigmoid(n)
        dn = d_ref[...] * (sg * (1.0 + n * (1.0 - sg)))
        dg_ref[...] += _colsum(dn * xhat)
        db_ref[...] += _colsum(dn)
        dng = dn * g_ref[...]
        m1 = jnp.mean(dng, axis=1, keepdims=True)
        m2 = jnp.mean(dng * xhat, axis=1, keepdims=True)
        dh = rstd * (dng - m1 - xhat * m2)
        dh_ref[...] = dh
        dcb_ref[...] += _colsum(dh)

    return _call(body, name="ln_silu_bwd", grid=(T // TM,), in_specs=[_rt(D), _rt(D), _vec(D), _vec(D)],
                 out_specs=[_rt(D), _vec(D), _vec(D), _vec(D)],
                 out_shape=[_sds((T, D), F32)] + [_sds((1, D), F32)] * 3)(dsw, hc, g, b)


def _merge(attn, conv_o, pool_o, z_gate):
    T = attn.shape[0]

    def body(a_ref, c_ref, p_ref, zg_ref, m_ref, mt_ref):
        m = (_sigmoid(zg_ref[:, 0:D]) * a_ref[...] + _sigmoid(zg_ref[:, D:2 * D]) * c_ref[...]
             + _sigmoid(zg_ref[:, 2 * D:3 * D]) * p_ref[...]).astype(MXU)
        m_ref[...] = m
        mt_ref[...] = m.T

    return _call(body, name="merge", grid=(T // TM,), in_specs=[_rt(D), _rt(D), _rt(D), _rt(N_GATE)], out_specs=[_rt(D), _ct(D)],
                 out_shape=[_sds((T, D), MXU), _sds((D, T), MXU)])(attn, conv_o, pool_o, z_gate)


def _merge_bwd(dm, attn, conv_o, pool_o, z_gate):
    T = attn.shape[0]

    def body(dm_ref, a_ref, c_ref, p_ref, zg_ref, da_ref, dc_ref, dp_ref, dzg_ref, dcsum_ref, gsum_ref):
        i = pl.program_id(0)

        @pl.when(i == 0)
        def _():
            dcsum_ref[...] = jnp.zeros_like(dcsum_ref)
            gsum_ref[...] = jnp.zeros_like(gsum_ref)

        dmv = dm_ref[...]
        for k, (br_ref, out_ref) in enumerate(((a_ref, da_ref), (c_ref, dc_ref), (p_ref, dp_ref))):
            gk = _sigmoid(zg_ref[:, k * D:(k + 1) * D])
            dbr = dmv * gk
            out_ref[...] = dbr.astype(out_ref.dtype)
            if k == 1:
                dcsum_ref[...] += _colsum(dbr)
            dzg = dmv * br_ref[...] * gk * (1.0 - gk)
            dzg_ref[:, k * D:(k + 1) * D] = dzg.astype(MXU)
            gsum_ref[:, k * D:(k + 1) * D] += _colsum(dzg)

    return _call(body, name="merge_bwd", grid=(T // TM,), in_specs=[_rt(D), _rt(D), _rt(D), _rt(D), _rt(N_GATE)],
                 out_specs=[_rt(D), _rt(D), _rt(D), _rt(N_GATE), _vec(D), _vec(N_GATE)],
                 out_shape=[_sds((T, D), F32), _sds((T, D), MXU), _sds((T, D), F32), _sds((T, N_GATE), MXU),
                            _sds((1, D), F32), _sds((1, N_GATE), F32)])(dm, attn, conv_o, pool_o, z_gate)


def _softmax_parts(q, k):
    s = _dot(q, k, NT)
    p = jnp.exp2((s - jnp.max(s, axis=1, keepdims=True)) * (ATTN_SCALE * LOG2_E))
    return p, 1.0 / jnp.sum(p, axis=1, keepdims=True)


def _attn_specs(nq):
    q_spec = pl.BlockSpec((TM, QG * HD), lambda b, h, q: (b * nq + q, h))
    k_spec = pl.BlockSpec((nq * TM, HD), lambda b, h, q: (b, NH + h))
    v_spec = pl.BlockSpec((nq * TM, HD), lambda b, h, q: (b, NH + NKV + h))
    return q_spec, k_spec, v_spec


class _Carried:
    def __init__(self, ex, n_in, n_out, n_scratch, mid_at=0.75):
        self.ex, self.n_in, self.n_out, self.n_scratch, self.mid_at = ex, n_in, n_out, n_scratch, mid_at
        self.ci = len(ex.inputs) if ex else 0
        self.co = len(ex.out_shape) if ex else 0

    def in_specs(self):
        return [ANY] * self.ci

    def out_specs(self):
        return [ANY] * self.co

    def split(self, refs):
        a = self.n_in
        b = a + self.ci
        c = b + self.n_out
        d = c + self.co
        e = d + self.n_scratch
        return (refs[:a], refs[b:c], refs[d:e]), (refs[a:b], refs[c:d], refs[e:])

    def before(self, step, parts):
        if self.ex:
            pl.when(step == 0)(lambda: self.ex.start(*parts))

    def after(self, step, n_steps, parts):
        if self.ex:
            pl.when(step == min(n_steps - 1, int(self.mid_at * n_steps)))(lambda: self.ex.mid(*parts))
            pl.when(step == n_steps - 1)(lambda: self.ex.finish(*parts))


def _attn_fwd(qkv, B, C, R, carry=None, mid_at=0.75):
    nq, ncq = R // TM, C // TM
    car = _Carried(carry, 3, 1, 0, mid_at)

    def body(*refs):
        (q_ref, k_ref, v_ref), (o_ref,), _ = car.split(refs)[0]
        parts = car.split(refs)[1]
        qi = pl.program_id(2)
        step = (pl.program_id(0) * NKV + pl.program_id(1)) * nq + qi
        car.before(step, parts)

        def attend(L):
            k, v = k_ref[0:L, :], v_ref[0:L, :]
            for i in range(QG):
                sl = slice(i * HD, (i + 1) * HD)
                p, inv_l = _softmax_parts(q_ref[:, sl], k)
                o_ref[:, sl] = _dot(p.astype(MXU), v, NN) * inv_l

        pl.when(qi < ncq)(functools.partial(attend, C))
        pl.when(qi >= ncq)(functools.partial(attend, R))
        car.after(step, B * NKV * nq, parts)

    q_spec, k_spec, v_spec = _attn_specs(nq)
    res = _call(body, name="attn_fwd", grid=(B, NKV, nq), in_specs=[q_spec, k_spec, v_spec] + car.in_specs(),
                out_specs=[q_spec] + car.out_specs(), out_shape=[_sds((B * R, NH * HD), F32)] + list(carry.out_shape if carry else []),
                scratch=list(carry.scratch) if carry else [])(qkv, qkv, qkv, *(carry.inputs if carry else []))
    return res[0], res[1:]


def _attn_bwd(qkv, o, do, B, C, R, carry=None):
    nq, ncq = R // TM, C // TM
    car = _Carried(carry, 5, 3, 2)

    def body(*refs):
        (q_ref, k_ref, v_ref, o_ref, do_ref), (dq_ref, dk_ref, dv_ref), (dkt, dvt) = car.split(refs)[0]
        parts = car.split(refs)[1]
        qi = pl.program_id(2)
        step = (pl.program_id(0) * NKV + pl.program_id(1)) * nq + qi
        car.before(step, parts)

        @pl.when(qi == 0)
        def _():
            dkt[...] = jnp.zeros_like(dkt)
            dvt[...] = jnp.zeros_like(dvt)

        def bwd(L):
            k, v = k_ref[0:L, :], v_ref[0:L, :]
            for i in range(QG):
                sl = slice(i * HD, (i + 1) * HD)
                q = q_ref[:, sl]
                p, inv_l = _softmax_parts(q, k)
                dov = do_ref[:, sl]
                dp = _dot(dov.astype(MXU), v, NT)
                dl = jnp.sum(dov * o_ref[:, sl], axis=1, keepdims=True)
                ds = (p * ((dp - dl) * (inv_l * ATTN_SCALE))).astype(MXU)
                dq_ref[:, sl] = _dot(ds, k, NN)
                dkt[:, 0:L] += _dot(q, ds, TN)
                dvt[:, 0:L] += _dot((dov * inv_l).astype(MXU), p.astype(MXU), TN)

        pl.when(qi < ncq)(functools.partial(bwd, C))
        pl.when(qi >= ncq)(functools.partial(bwd, R))

        @pl.when(qi == nq - 1)
        def _():
            dk_ref[...] = dkt[...].T
            dv_ref[...] = dvt[...].T

        car.after(step, B * NKV * nq, parts)

    q_spec, k_spec, v_spec = _attn_specs(nq)
    kv_out = pl.BlockSpec((R, HD), lambda b, h, q: (b, h))
    res = _call(body, name="attn_bwd", grid=(B, NKV, nq), in_specs=[q_spec, k_spec, v_spec, q_spec, q_spec] + car.in_specs(),
                out_specs=[q_spec, kv_out, kv_out] + car.out_specs(),
                out_shape=[_sds((B * R, NH * HD), F32), _sds((B * R, KV_W), F32), _sds((B * R, KV_W), F32)]
                + list(carry.out_shape if carry else []),
                scratch=[pltpu.VMEM((HD, R), F32), pltpu.VMEM((HD, R), F32)] + list(carry.scratch if carry else []),
                )(qkv, qkv, qkv, o, do, *(carry.inputs if carry else []))
    return res[0], res[1], res[2], res[3:]


def _segments(C, S):
    return ((0, GAP, C), (C, 2 * GAP + C, S))


def _padded_rows(C, S):
    return 3 * GAP + C + S


def _zero_gaps(pad_ref, C, S):
    for off in (0, GAP + C, 2 * GAP + C + S):
        pad_ref[off:off + GAP, :] = jnp.zeros((GAP, pad_ref.shape[1]), pad_ref.dtype)


def _chunks(n, ch, fn):
    def step(i, carry):
        fn(pl.multiple_of(i * ch, ch))
        return carry

    lax.fori_loop(0, n // ch, step, 0)


class _Window:
    def __init__(self, pad_ref, row, ch):
        self.pad_ref, (self.r, self.po), self.ch = pad_ref, row, ch

    def at(self, off):
        return self.pad_ref[pl.ds(self.r + (self.po + off), self.ch), :]


def _taps(pad_ref, w, row, ch, n_taps, flip=False):
    half = (n_taps - 1) // 2
    win = _Window(pad_ref, row, ch)
    acc = None
    for k in range(n_taps):
        term = w[k:k + 1, :] * win.at((half - k) if flip else (k - half))
        acc = term if acc is None else acc + term
    return acc


def _tap_grads(dw_ref, d, pad_ref, row, ch, n_taps):
    half = (n_taps - 1) // 2
    win = _Window(pad_ref, row, ch)
    for k in range(n_taps):
        prod = d * win.at(k - half)
        dw_ref[k] += jnp.sum(prod.reshape(ch // 8, 8, prod.shape[1]), axis=0)


def _conv_fwd(z_conv, w, bias, B, C, S, cw=128, ch=128):
    R = C + S
    nj = CONV_CH // cw
    segs = _segments(C, S)

    def body(a_ref, g_ref, w_ref, b_ref, o_ref, pad):
        _zero_gaps(pad, C, S)
        wv = w_ref[...]
        for so, po, n in segs:
            def fill(r, so=so, po=po):
                pad[pl.ds(po + r, ch), :] = a_ref[pl.ds(so + r, ch), :] * _sigmoid(g_ref[pl.ds(so + r, ch), :])

            _chunks(n, ch, fill)
        for so, po, n in segs:
            def conv(r, so=so, po=po):
                o_ref[pl.ds(so + r, ch), :] = _taps(pad, wv, (r, po), ch, CONV_K) + b_ref[...]

            _chunks(n, ch, conv)

    return _call(
        body, name="conv_fwd", grid=(nj, B),
        in_specs=[pl.BlockSpec((R, cw), lambda j, b: (b, j)), pl.BlockSpec((R, cw), lambda j, b: (b, nj + j)),
                  pl.BlockSpec((32, cw), lambda j, b: (0, j)), pl.BlockSpec((1, cw), lambda j, b: (0, j))],
        out_specs=pl.BlockSpec((R, cw), lambda j, b: (b, j)), out_shape=_sds((B * R, CONV_CH), F32),
        scratch=[pltpu.VMEM((_padded_rows(C, S), cw), F32)])(z_conv, z_conv, w, bias)


def _conv_bwd(dhc, z_conv, w, B, C, S, cw=128, ch=128):
    R = C + S
    nj = CONV_CH // cw
    segs = _segments(C, S)

    def body(d_ref, a_ref, g_ref, w_ref, da_ref, dg_ref, dw_ref, sa_ref, sg_ref, gpad, dpad, dwacc):
        b = pl.program_id(1)

        @pl.when(b == 0)
        def _():
            dw_ref[...] = jnp.zeros_like(dw_ref)
            sa_ref[...] = jnp.zeros_like(sa_ref)
            sg_ref[...] = jnp.zeros_like(sg_ref)

        _zero_gaps(gpad, C, S)
        _zero_gaps(dpad, C, S)
        dwacc[...] = jnp.zeros_like(dwacc)
        wv = w_ref[...]
        for so, po, n in segs:
            def fill(r, so=so, po=po):
                gpad[pl.ds(po + r, ch), :] = a_ref[pl.ds(so + r, ch), :] * _sigmoid(g_ref[pl.ds(so + r, ch), :])
                dpad[pl.ds(po + r, ch), :] = d_ref[pl.ds(so + r, ch), :]

            _chunks(n, ch, fill)
        for so, po, n in segs:
            def step(r, so=so, po=po):
                _tap_grads(dwacc, dpad[pl.ds(po + r, ch), :], gpad, (r, po), ch, CONV_K)
                dglu = _taps(dpad, wv, (r, po), ch, CONV_K, flip=True)
                av = a_ref[pl.ds(so + r, ch), :]
                sg = _sigmoid(g_ref[pl.ds(so + r, ch), :])
                da = dglu * sg
                dg = dglu * av * sg * (1.0 - sg)
                da_ref[pl.ds(so + r, ch), :] = da.astype(MXU)
                dg_ref[pl.ds(so + r, ch), :] = dg.astype(MXU)
                sa_ref[...] += _colsum(da)
                sg_ref[...] += _colsum(dg)

            _chunks(n, ch, step)
        for k in range(CONV_K):
            dw_ref[k:k + 1, :] += _colsum(dwacc[k])

    blk = pl.BlockSpec((R, cw), lambda j, b: (b, j))
    acc1 = pl.BlockSpec((1, cw), lambda j, b: (0, j))
    return _call(
        body, name="conv_bwd", grid=(nj, B),
        in_specs=[blk, blk, pl.BlockSpec((R, cw), lambda j, b: (b, nj + j)), pl.BlockSpec((32, cw), lambda j, b: (0, j))],
        out_specs=[blk, blk, pl.BlockSpec((32, cw), lambda j, b: (0, j)), acc1, acc1],
        out_shape=[_sds((B * R, CONV_CH), MXU), _sds((B * R, CONV_CH), MXU), _sds((32, CONV_CH), F32),
                   _sds((1, CONV_CH), F32), _sds((1, CONV_CH), F32)],
        scratch=[pltpu.VMEM((_padded_rows(C, S), cw), F32), pltpu.VMEM((_padded_rows(C, S), cw), F32),
                 pltpu.VMEM((32, 8, cw), F32)])(dhc, z_conv, z_conv, w)


def _ffn_mid(up, w, bias, B, C, S, cw=256, ch=64, carry=None):
    R = C + S
    nj = D_FF // cw
    segs = _segments(C, S)
    car = _Carried(carry, 4, 2, 1)

    def body(*refs):
        (a_ref, u_ref, w_ref, b_ref), (f_ref, ft_ref), (pad,) = car.split(refs)[0]
        parts = car.split(refs)[1]
        step = pl.program_id(0) * B + pl.program_id(1)
        car.before(step, parts)
        for h in range(cw // LANE):
            _zero_gaps(pad.at[h], C, S)
        wv = w_ref[...]
        for so, po, n in segs:
            def fill(r, so=so, po=po):
                for h in range(cw // LANE):
                    pad[h, pl.ds(po + r, ch), :] = a_ref[pl.ds(so + r, ch), h * LANE:(h + 1) * LANE]

            _chunks(n, ch, fill)
        for so, po, n in segs:
            def conv(r, so=so, po=po):
                ac = jnp.concatenate([_taps(pad.at[h], wv[:, h * LANE:(h + 1) * LANE], (r, po), ch, FFN_K)
                                      for h in range(cw // LANE)], axis=1) + b_ref[...]
                f_ref[pl.ds(so + r, ch), :] = (ac * _sigmoid(ac) * u_ref[pl.ds(so + r, ch), :]).astype(MXU)

            _chunks(n, ch, conv)
        ft_ref[...] = f_ref[...].T
        car.after(step, nj * B, parts)

    res = _call(
        body, name="ffn_mid", grid=(nj, B),
        in_specs=[pl.BlockSpec((R, cw), lambda j, b: (b, j)), pl.BlockSpec((R, cw), lambda j, b: (b, nj + j)),
                  pl.BlockSpec((8, cw), lambda j, b: (0, j)), pl.BlockSpec((1, cw), lambda j, b: (0, j))] + car.in_specs(),
        out_specs=[pl.BlockSpec((R, cw), lambda j, b: (b, j)), pl.BlockSpec((cw, R), lambda j, b: (j, b))] + car.out_specs(),
        out_shape=[_sds((B * R, D_FF), MXU), _sds((D_FF, B * R), MXU)] + list(carry.out_shape if carry else []),
        scratch=[pltpu.VMEM((cw // LANE, _padded_rows(C, S), LANE), F32)] + list(carry.scratch if carry else []),
    )(up, up, w, bias, *(carry.inputs if carry else []))
    return res[0], res[1], res[2:]


def _ffn_mid_bwd(df, up, w, bias, B, C, S, cw=128, ch=128, carry=None):
    R = C + S
    nj = D_FF // cw
    segs = _segments(C, S)
    car = _Carried(carry, 5, 4, 3)

    def body(*refs):
        (d_ref, a_ref, u_ref, w_ref, b_ref), (da_ref, du_ref, dw_ref, db_ref), (apad, dpad, dwacc) = car.split(refs)[0]
        parts = car.split(refs)[1]
        b = pl.program_id(1)
        step = pl.program_id(0) * B + b
        car.before(step, parts)

        @pl.when(b == 0)
        def _():
            dw_ref[...] = jnp.zeros_like(dw_ref)
            db_ref[...] = jnp.zeros_like(db_ref)

        _zero_gaps(apad, C, S)
        _zero_gaps(dpad, C, S)
        dwacc[...] = jnp.zeros_like(dwacc)
        wv = w_ref[...]
        for so, po, n in segs:
            def fill(r, so=so, po=po):
                apad[pl.ds(po + r, ch), :] = a_ref[pl.ds(so + r, ch), :]

            _chunks(n, ch, fill)
        for so, po, n in segs:
            def first(r, so=so, po=po):
                ac = _taps(apad, wv, (r, po), ch, FFN_K) + b_ref[...]
                sg = _sigmoid(ac)
                dfv = d_ref[pl.ds(so + r, ch), :]
                du_ref[pl.ds(so + r, ch), :] = (dfv * ac * sg).astype(MXU)
                dac = dfv * u_ref[pl.ds(so + r, ch), :] * (sg * (1.0 + ac * (1.0 - sg)))
                dpad[pl.ds(po + r, ch), :] = dac
                db_ref[...] += _colsum(dac)

            _chunks(n, ch, first)
        for so, po, n in segs:
            def second(r, so=so, po=po):
                _tap_grads(dwacc, dpad[pl.ds(po + r, ch), :], apad, (r, po), ch, FFN_K)
                da_ref[pl.ds(so + r, ch), :] = _taps(dpad, wv, (r, po), ch, FFN_K, flip=True).astype(MXU)

            _chunks(n, ch, second)
        for k in range(FFN_K):
            dw_ref[k:k + 1, :] += _colsum(dwacc[k])
        car.after(step, nj * B, parts)

    blk = pl.BlockSpec((R, cw), lambda j, b: (b, j))
    res = _call(
        body, name="ffn_mid_bwd", grid=(nj, B),
        in_specs=[blk, blk, pl.BlockSpec((R, cw), lambda j, b: (b, nj + j)), pl.BlockSpec((8, cw), lambda j, b: (0, j)),
                  pl.BlockSpec((1, cw), lambda j, b: (0, j))] + car.in_specs(),
        out_specs=[blk, blk, pl.BlockSpec((8, cw), lambda j, b: (0, j)), pl.BlockSpec((1, cw), lambda j, b: (0, j))]
        + car.out_specs(),
        out_shape=[_sds((B * R, D_FF), MXU), _sds((B * R, D_FF), MXU), _sds((8, D_FF), F32), _sds((1, D_FF), F32)]
        + list(carry.out_shape if carry else []),
        scratch=[pltpu.VMEM((_padded_rows(C, S), cw), F32), pltpu.VMEM((_padded_rows(C, S), cw), F32),
                 pltpu.VMEM((8, 8, cw), F32)] + list(carry.scratch if carry else []),
    )(df, up, up, w, bias, *(carry.inputs if carry else []))
    return res[0], res[1], res[2], res[3], res[4:]


def _view_sum(win, lo, hi):
    acc = win.at(lo)
    for off in range(lo + 1, hi + 1):
        acc = acc + win.at(off)
    return acc


def _window_count(r, ch, n, w):
    t = r + lax.broadcasted_iota(jnp.int32, (ch, 1), 0)
    return (jnp.minimum(t + w // 2, n) - jnp.maximum(t - w // 2, 0)).astype(F32)


def _pool_fwd(z_pool, pool_w, pool_scale, B, C, S, ch=128):
    R = C + S
    gch = POOL_GCH
    segs = _segments(C, S)

    def body(u_ref, pw_ref, sc_ref, pooled_ref, po_ref, pad):
        g = pl.program_id(1)
        for h in range(gch // LANE):
            _zero_gaps(pad.at[h], C, S)
        for so, po, n in segs:
            def fill(r, so=so, po=po):
                for h in range(gch // LANE):
                    pad[h, pl.ds(po + r, ch), :] = u_ref[pl.ds(so + r, ch), h * LANE:(h + 1) * LANE]

            _chunks(n, ch, fill)
        for gi, w in enumerate(POOL_WINDOWS):
            @pl.when(g == gi)
            def _(w=w):
                for so, po, n in segs:
                    def step(r, so=so, po=po, n=n):
                        cnt = _window_count(r, ch, n, w)
                        halves = []
                        for h in range(gch // LANE):
                            win = _Window(pad.at[h], (r, po), ch)
                            halves.append(_view_sum(win, -(w // 2), w // 2 - 1) / cnt - win.at(0))
                        pooled = jnp.concatenate(halves, axis=1).astype(MXU)
                        pooled_ref[pl.ds(so + r, ch), :] = pooled
                        po_ref[pl.ds(so + r, ch), :] = _dot(pooled, pw_ref[0], NN) * sc_ref[...]

                    _chunks(n, ch, step)

    blk = pl.BlockSpec((R, gch), lambda b, g: (b, g))
    return _call(
        body, name="pool_fwd", grid=(B, len(POOL_WINDOWS)),
        in_specs=[blk, pl.BlockSpec((1, gch, gch), lambda b, g: (g, 0, 0)), pl.BlockSpec((1, gch), lambda b, g: (0, g))],
        out_specs=[blk, blk], out_shape=[_sds((B * R, POOL_CH), MXU), _sds((B * R, POOL_CH), F32)],
        scratch=[pltpu.VMEM((gch // LANE, _padded_rows(C, S), LANE), F32)])(z_pool, pool_w, pool_scale)


def _pool_bwd(dpo, pooled, pool_w, pool_scale, B, C, S, ch=128):
    R = C + S
    gch = POOL_GCH
    segs = _segments(C, S)

    def body(d_ref, p_ref, pw_ref, sc_ref, du_ref, dpw_ref, dsc_ref, su_ref, qpad, dpl):
        g, b = pl.program_id(0), pl.program_id(1)

        @pl.when(b == 0)
        def _():
            dpw_ref[...] = jnp.zeros_like(dpw_ref)
            dsc_ref[...] = jnp.zeros_like(dsc_ref)
            su_ref[...] = jnp.zeros_like(su_ref)

        for h in range(gch // LANE):
            _zero_gaps(qpad.at[h], C, S)
        pw = pw_ref[0]
        for gi, w in enumerate(POOL_WINDOWS):
            @pl.when(g == gi)
            def _(w=w):
                for so, po, n in segs:
                    def first(r, so=so, po=po, n=n):
                        pv = p_ref[pl.ds(so + r, ch), :]
                        dv = d_ref[pl.ds(so + r, ch), :]
                        dsc_ref[...] += _colsum(dv * _dot(pv, pw, NN))
                        dmx = (dv * sc_ref[...]).astype(MXU)
                        dpw_ref[0] += _dot(pv, dmx, TN)
                        dp = _dot(dmx, pw, NT)
                        dpl[pl.ds(so + r, ch), :] = dp
                        q = dp / _window_count(r, ch, n, w)
                        for h in range(gch // LANE):
                            qpad[h, pl.ds(po + r, ch), :] = q[:, h * LANE:(h + 1) * LANE]

                    _chunks(n, ch, first)
                for so, po, n in segs:
                    def second(r, so=so, po=po):
                        acc = jnp.concatenate([_view_sum(_Window(qpad.at[h], (r, po), ch), 1 - w // 2, w // 2)
                                               for h in range(gch // LANE)], axis=1)
                        du = acc - dpl[pl.ds(so + r, ch), :]
                        du_ref[pl.ds(so + r, ch), :] = du.astype(MXU)
                        su_ref[...] += _colsum(du)

                    _chunks(n, ch, second)

    blk = pl.BlockSpec((R, gch), lambda g, b: (b, g))
    vec = pl.BlockSpec((1, gch), lambda g, b: (0, g))
    wblk = pl.BlockSpec((1, gch, gch), lambda g, b: (g, 0, 0))
    return _call(
        body, name="pool_bwd", grid=(len(POOL_WINDOWS), B), in_specs=[blk, blk, wblk, vec], out_specs=[blk, wblk, vec, vec],
        out_shape=[_sds((B * R, POOL_CH), MXU), _sds((len(POOL_WINDOWS), gch, gch), F32), _sds((1, POOL_CH), F32),
                   _sds((1, POOL_CH), F32)],
        scratch=[pltpu.VMEM((gch // LANE, _padded_rows(C, S), LANE), F32), pltpu.VMEM((R, gch), F32)],
    )(dpo, pooled, pool_w, pool_scale)


def _silu_rows(cond):
    def body(c_ref, s_ref, d_ref):
        c = c_ref[...]
        sg = _sigmoid(c)
        s_ref[...] = (c * sg).astype(MXU)
        d_ref[...] = sg * (1.0 + c * (1.0 - sg))

    full = pl.BlockSpec(cond.shape, lambda i: (0, 0))
    return _call(body, name="silu_rows", grid=(1,), in_specs=[full], out_specs=[full, full],
                 out_shape=[_sds(cond.shape, MXU), _sds(cond.shape, F32)])(cond)


def _row_tile(rows, cols, n_bufs):
    cap = max(16, (16 * 2 ** 20) // (4 * n_bufs * max(cols, 128)))
    return rows if rows <= cap else _pick(rows, cap, 16)


def _adamw(parts, w, m, v, layer, prev, *, name):
    n_parts, rows, cols = parts.shape
    layers = w.shape[0]
    c1 = 1.0 - ADAM_B1 ** ADAM_STEP
    c2 = 1.0 - ADAM_B2 ** ADAM_STEP
    tr = _row_tile(rows, cols, n_parts + 7)

    def body(p_ref, w_ref, m_ref, v_ref, *rest):
        g_ref, d_ref, nm_ref, nv_ref = rest[-4:]
        g = p_ref[0].astype(F32)
        for k in range(1, n_parts):
            g = g + p_ref[k].astype(F32)
        nm = ADAM_B1 * m_ref[...] + (1.0 - ADAM_B1) * g
        nv = ADAM_B2 * v_ref[...] + (1.0 - ADAM_B2) * (g * g)
        g_ref[...] = g
        nm_ref[...] = nm
        nv_ref[...] = nv
        d_ref[...] = -ADAM_LR * ((nm / c1) / (jnp.sqrt(nv / c2) + ADAM_EPS) + ADAM_WD * w_ref[...])

    blk = pl.BlockSpec((None, tr, cols), lambda i: (layer, i, 0))
    in_specs = [pl.BlockSpec((n_parts, tr, cols), lambda i: (0, i, 0)), blk, blk, blk]
    args = [parts, w, m, v]
    aliases = {}
    if prev is not None:
        in_specs += [ANY] * 4
        aliases = {4 + k: k for k in range(4)}
        args += list(prev)
    return _call(body, name=name, grid=(rows // tr,), in_specs=in_specs, out_specs=[blk] * 4,
                 out_shape=[_sds((layers, rows, cols), F32)] * 4, aliases=aliases)(*args)


def _pair_sum(g, r1):
    _, rows, cols = g.shape
    c = lax.axis_index("c")
    g4 = g.reshape(4, 2, rows, cols)
    tr = _row_tile(rows, cols, 10)

    def body(c_ref, g_ref, r_ref, o_ref):
        o_ref[...] = (g_ref[...] + r_ref[...]).astype(WIRE)

    return _pcall(
        body, name="pair_sum", out_shape=_sds((4, rows, cols), WIRE),
        grid_spec=pltpu.PrefetchScalarGridSpec(
            num_scalar_prefetch=1, grid=(rows // tr,),
            in_specs=[pl.BlockSpec((4, None, tr, cols), lambda i, c_ref: (0, c_ref[0], i, 0)),
                      pl.BlockSpec((4, tr, cols), lambda i, c_ref: (0, i, 0))],
            out_specs=pl.BlockSpec((4, tr, cols), lambda i, c_ref: (0, i, 0))),
        compiler_params=pltpu.CompilerParams(dimension_semantics=("arbitrary",), vmem_limit_bytes=VMEM_MB * 2 ** 20),
    )(jnp.reshape(c, (1,)).astype(jnp.int32), g4, r1)


MESH = pl.DeviceIdType.MESH
ANY = pl.BlockSpec(memory_space=pl.ANY)


class _Exchange:
    inputs, out_shape, scratch = (), (), ()

    def start(self, ins, outs, sems):
        raise NotImplementedError

    def mid(self, ins, outs, sems):
        pass

    def finish(self, ins, outs, sems):
        raise NotImplementedError


def _run_exchange(ex, *, name):
    n_in, n_out = len(ex.inputs), len(ex.out_shape)

    def body(*refs):
        parts = refs[:n_in], refs[n_in:n_in + n_out], refs[n_in + n_out:]
        ex.start(*parts)
        ex.mid(*parts)
        ex.finish(*parts)

    return _pcall(body, name=name, out_shape=list(ex.out_shape), in_specs=[ANY] * n_in, out_specs=[ANY] * n_out,
                  scratch_shapes=list(ex.scratch))(*ex.inputs)


class _Gather(_Exchange):
    def __init__(self, shards, layer=None):
        self.inputs, self.layer, self.n_t = list(shards), layer, len(shards)
        self.out_shape = [_sds((N_DEV,) + tuple(s.shape if layer is None else s.shape[1:]), s.dtype) for s in shards]
        self.scratch = [pltpu.SemaphoreType.DMA((7 * self.n_t,)), pltpu.SemaphoreType.DMA((7 * self.n_t,)),
                        pltpu.SemaphoreType.DMA((self.n_t,))]

    def _place(self):
        x, y, c = lax.axis_index("x"), lax.axis_index("y"), lax.axis_index("c")
        return (x, y, c), (x, y, 1 - c), [(1 - x, y), (x, 1 - y), (1 - x, 1 - y)]

    def _own(self, ins, n):
        return ins[n] if self.layer is None else ins[n].at[self.layer]

    def _copy(self, ins, outs, sems, n, k, blk, to, own=False):
        dst = outs[n].at[4 * blk[0] + 2 * blk[1] + blk[2]]
        return pltpu.make_async_remote_copy(src_ref=self._own(ins, n) if own else dst, dst_ref=dst, send_sem=sems[0].at[n * 7 + k],
                                            recv_sem=sems[1].at[n * 7 + k], device_id=to, device_id_type=MESH)

    def _mine(self, ins, outs, sems, n):
        (x, y, c), _, _ = self._place()
        return pltpu.make_async_copy(self._own(ins, n), outs[n].at[4 * x + 2 * y + c], sems[2].at[n])

    def start(self, ins, outs, sems):
        me, sibling, chips = self._place()
        for n in range(self.n_t):
            self._mine(ins, outs, sems, n).start()
        for j, chip in enumerate(chips):
            for n in range(self.n_t):
                self._copy(ins, outs, sems, n, 1 + j, me, (*chip, me[2]), own=True).start()
        for n in range(self.n_t):
            self._copy(ins, outs, sems, n, 0, me, sibling, own=True).start()

    def mid(self, ins, outs, sems):
        me, sibling, chips = self._place()
        for j, chip in enumerate(chips):
            for n in range(self.n_t):
                self._copy(ins, outs, sems, n, 1 + j, (*chip, me[2]), me).wait_recv()
                self._copy(ins, outs, sems, n, 4 + j, (*chip, me[2]), sibling).start()

    def finish(self, ins, outs, sems):
        me, sibling, chips = self._place()
        for n in range(self.n_t):
            self._copy(ins, outs, sems, n, 0, sibling, me).wait_recv()
        for j, chip in enumerate(chips):
            for n in range(self.n_t):
                self._copy(ins, outs, sems, n, 4 + j, (*chip, 1 - me[2]), me).wait_recv()
        for j, chip in enumerate(chips):
            for n in range(self.n_t):
                self._copy(ins, outs, sems, n, 1 + j, me, (*chip, me[2]), own=True).wait_send()
                self._copy(ins, outs, sems, n, 4 + j, (*chip, me[2]), sibling).wait_send()
        for n in range(self.n_t):
            self._copy(ins, outs, sems, n, 0, me, sibling, own=True).wait_send()
            self._mine(ins, outs, sems, n).wait()


def _all_gather(shards, *, name, layer=None):
    return _run_exchange(_Gather(shards, layer), name=name)


class _Both(_Exchange):
    def __init__(self, a, b):
        self.a, self.b = a, b
        self.inputs = list(a.inputs) + list(b.inputs)
        self.out_shape = list(a.out_shape) + list(b.out_shape)
        self.scratch = list(a.scratch) + list(b.scratch)

    def _parts(self, ins, outs, sems):
        i, o, s = len(self.a.inputs), len(self.a.out_shape), len(self.a.scratch)
        return (ins[:i], outs[:o], sems[:s]), (ins[i:], outs[o:], sems[s:])

    def start(self, ins, outs, sems):
        pa, pb = self._parts(ins, outs, sems)
        self.a.start(*pa)
        self.b.start(*pb)

    def mid(self, ins, outs, sems):
        pa, pb = self._parts(ins, outs, sems)
        self.a.mid(*pa)
        self.b.mid(*pb)

    def finish(self, ins, outs, sems):
        pa, pb = self._parts(ins, outs, sems)
        self.a.finish(*pa)
        self.b.finish(*pb)


class _SiblingExchange(_Exchange):
    def __init__(self, gs):
        self.n_t = len(gs)
        self.inputs = [g.reshape((4, 2) + g.shape[1:]) for g in gs]
        self.out_shape = [_sds((4,) + g.shape[1:], g.dtype) for g in gs]
        self.scratch = [pltpu.SemaphoreType.DMA((self.n_t,)), pltpu.SemaphoreType.DMA((self.n_t,))]

    def _copy(self, ins, outs, sems, n):
        x, y, c = lax.axis_index("x"), lax.axis_index("y"), lax.axis_index("c")
        return pltpu.make_async_remote_copy(src_ref=ins[n].at[:, 1 - c], dst_ref=outs[n], send_sem=sems[0].at[n],
                                            recv_sem=sems[1].at[n], device_id=(x, y, 1 - c), device_id_type=MESH)

    def start(self, ins, outs, sems):
        for n in range(self.n_t):
            self._copy(ins, outs, sems, n).start()

    def finish(self, ins, outs, sems):
        for n in range(self.n_t):
            self._copy(ins, outs, sems, n).wait_recv()
        for n in range(self.n_t):
            self._copy(ins, outs, sems, n).wait_send()


class _ChipExchange(_Exchange):
    def __init__(self, ps):
        self.n_t, self.inputs = len(ps), list(ps)
        self.out_shape = [_sds(p.shape, p.dtype) for p in ps]
        self.scratch = [pltpu.SemaphoreType.DMA((3 * self.n_t,)), pltpu.SemaphoreType.DMA((3 * self.n_t,)),
                        pltpu.SemaphoreType.DMA((self.n_t,))]

    def _place(self):
        x, y, c = lax.axis_index("x"), lax.axis_index("y"), lax.axis_index("c")
        return 2 * x + y, c, [(1 - x, y), (x, 1 - y), (1 - x, 1 - y)]

    def _copy(self, ins, outs, sems, n, j, src_chip, dst_slot):
        _, c, chips = self._place()
        return pltpu.make_async_remote_copy(
            src_ref=ins[n].at[src_chip], dst_ref=outs[n].at[dst_slot], send_sem=sems[0].at[n * 3 + j],
            recv_sem=sems[1].at[n * 3 + j], device_id=(*chips[j], c), device_id_type=MESH)

    def _own(self, ins, outs, sems, n):
        mine, _, _ = self._place()
        return pltpu.make_async_copy(ins[n].at[mine], outs[n].at[mine], sems[2].at[n])

    def start(self, ins, outs, sems):
        mine, _, chips = self._place()
        for n in range(self.n_t):
            self._own(ins, outs, sems, n).start()
        for j, (px, py) in enumerate(chips):
            for n in range(self.n_t):
                self._copy(ins, outs, sems, n, j, 2 * px + py, mine).start()

    def finish(self, ins, outs, sems):
        mine, _, chips = self._place()
        for j, (px, py) in enumerate(chips):
            for n in range(self.n_t):
                self._copy(ins, outs, sems, n, j, mine, 2 * px + py).wait_recv()
        for j, (px, py) in enumerate(chips):
            for n in range(self.n_t):
                self._copy(ins, outs, sems, n, j, 2 * px + py, mine).wait_send()
        for n in range(self.n_t):
            self._own(ins, outs, sems, n).wait()


BIG = (("w_ada", (D, N_MOD * D // N_DEV), 1), ("w_in", (D, D_IN // N_DEV), 1), ("conv_pw_w", (CONV_CH // N_DEV, D), 0),
       ("pool_w", (len(POOL_WINDOWS), POOL_GCH // N_DEV, POOL_GCH), 1), ("w_out", (D // N_DEV, D), 0),
       ("w_up", (D, 2 * D_FF // N_DEV), 1), ("w_down", (D_FF // N_DEV, D), 0))
TAPS = (("conv_dw_w", (CONV_K, CONV_CH // N_DEV), 1), ("ffn_dw_w", (FFN_K, D_FF // N_DEV), 1))
SHARDED = BIG + TAPS
EARLY = (0, 1)
LATE = (2, 3, 4, 5, 6)
READY_0 = (2, 3, 4, 5, 6, 7, 8)
LAST_0 = (0, 1)
REPLICATED =(("b_ada", N_MOD * D), ("b_in", D_IN), ("q_gain", HD), ("k_gain", HD), ("conv_dw_b", CONV_CH), ("conv_ln_g", CONV_CH),
              ("conv_ln_b", CONV_CH), ("conv_pw_b", D), ("pool_scale", POOL_CH), ("b_out", D), ("ln1_g", D), ("ln1_b", D),
              ("ln2_g", D), ("ln2_b", D), ("ffn_dw_b", D_FF))


def _as_rows(shape):
    return (int(np.prod(shape[:-1])), shape[-1])


def _full_from_blocks(blocks, axis):
    moved = jnp.moveaxis(blocks, 0, axis)
    shape = list(moved.shape)
    shape[axis:axis + 2] = [shape[axis] * shape[axis + 1]]
    return moved.reshape(shape)


def _blocks_from_full(full, axis):
    shape = list(full.shape)
    shape[axis:axis + 1] = [N_DEV, shape[axis] // N_DEV]
    return jnp.moveaxis(full.reshape(shape), axis, 0)


SMALL_N = DEPTH * sum(n for _, n in REPLICATED) + D
SMALL_ROWS = -(-(SMALL_N + 1) // (8 * LANES)) * 8


def _rope_tables(C, S):
    t = np.arange(S)
    inv_freq = ROPE_THETA ** (-np.arange(ROPE_PAIRS, dtype=np.float32) / ROPE_PAIRS)
    row = jnp.asarray((t // GRID_W).astype(np.float32))[:, None] * jnp.asarray(inv_freq, F32)
    col = jnp.asarray((t % GRID_W).astype(np.float32))[:, None] * jnp.asarray(inv_freq, F32)
    cos = jnp.concatenate([jnp.cos(row), jnp.cos(row), jnp.cos(col), jnp.cos(col)], axis=1)
    sin = jnp.concatenate([-jnp.sin(row), jnp.sin(row), -jnp.sin(col), jnp.sin(col)], axis=1)
    cos = jnp.concatenate([jnp.ones((C, HD), F32), cos], axis=0)
    sin = jnp.concatenate([jnp.zeros((C, HD), F32), sin], axis=0)
    return cos, sin


def _segment_sums(parts, B, tpe, ncq):
    p = parts.reshape(B, tpe, D)
    return jnp.concatenate([jnp.sum(p[:, ncq:], axis=1), jnp.sum(p[:, :ncq], axis=(0, 1))[None]], axis=0)


def kernel(x, c, ctx, c_ctx, w_ada, b_ada, w_in, b_in, q_gain, k_gain, conv_dw_w, conv_dw_b, conv_ln_g, conv_ln_b, conv_pw_w, conv_pw_b, pool_w, pool_scale, w_out, b_out, ln1_g, ln1_b, ln2_g, ln2_b, w_up, ffn_dw_w, ffn_dw_b, w_down, loss_target, m_c_ctx, m_w_ada, m_b_ada, m_w_in, m_b_in, m_q_gain, m_k_gain, m_conv_dw_w, m_conv_dw_b, m_conv_ln_g, m_conv_ln_b, m_conv_pw_w, m_conv_pw_b, m_pool_w, m_pool_scale, m_w_out, m_b_out, m_ln1_g, m_ln1_b, m_ln2_g, m_ln2_b, m_w_up, m_ffn_dw_w, m_ffn_dw_b, m_w_down, v_c_ctx, v_w_ada, v_b_ada, v_w_in, v_b_in, v_q_gain, v_k_gain, v_conv_dw_w, v_conv_dw_b, v_conv_ln_g, v_conv_ln_b, v_conv_pw_w, v_conv_pw_b, v_pool_w, v_pool_scale, v_w_out, v_b_out, v_ln1_g, v_ln1_b, v_ln2_g, v_ln2_b, v_w_up, v_ffn_dw_w, v_ffn_dw_b, v_w_down):
    given = dict(locals())
    B, S, _ = x.shape
    C = ctx.shape[1]
    R = C + S
    T = B * R
    tpe, ncq = R // TM, C // TM
    nt = T // TM
    assert S % TM == 0 and C % TM == 0 and B + 1 <= 16

    operands = [given[n].astype(MXU) for n, _, _ in BIG]
    taps = _all_gather([given[n] for n, _, _ in TAPS], name="gather_taps")
    W, modt = [], []

    xu = jnp.concatenate([ctx, x], axis=1).reshape(T, D)
    cond = jnp.concatenate([c, c_ctx[None], jnp.zeros((16 - B - 1, D), F32)], axis=0)
    s_cond, ds_cond = _silu_rows(cond)
    ctx_tile = jnp.asarray((np.arange(tpe) < ncq)[None, :, None])
    cos_t, sin_t = _rope_tables(C, S)
    row = lambda v: v.reshape(1, -1)

    early_ops, late_ops = [operands[k] for k in EARLY], [operands[k] for k in LATE]

    def split_for(t):
        return (EARLY, LATE) if t <= 1 else (EARLY + (5,), tuple(k for k in LATE if k != 5))

    def ops(which):
        return [operands[k] for k in which]

    def add_early(blocks, which=EARLY):
        l = len(W)
        wl = {BIG[k][0]: _full_from_blocks(blk, BIG[k][2]) for k, blk in zip(which, blocks)}
        for (n, _, a), blk in zip(TAPS, taps):
            wl[n] = _full_from_blocks(blk[:, l], a)
        W.append(wl)
        m = _matmul(s_cond, wl["w_ada"], "nn", name="ada", bias=row(b_ada[l]), tm=16, tn=1024)
        modt.append(jnp.where(ctx_tile, m[B][None, None, :], m[:B][:, None, :]).reshape(nt, 1, N_MOD * D))

    def add_late(l, blocks, which=LATE):
        for k, blk in zip(which, blocks):
            W[l][BIG[k][0]] = _full_from_blocks(blk, BIG[k][2])

    add_early(_all_gather(early_ops, name="gather_weights", layer=0))

    saved = []
    h1, h1t = _modulate_cast(xu, modt[0], 0)
    xin = xu
    for l in range(DEPTH):
        wl = W[l]
        w_inl, b_inl = wl["w_in"], b_in[l]
        z_qkv = _matmul(h1, w_inl[:, :QKV_W], "nn", name="z_qkv", bias=row(b_inl[:QKV_W]), tn=768)
        c0, p0, g0 = QKV_W, QKV_W + 2 * CONV_CH, QKV_W + 2 * CONV_CH + POOL_CH
        z_conv = _matmul(h1, w_inl[:, c0:p0], "nn", name="z_conv", bias=row(b_inl[c0:p0]))
        z_pool = _matmul(h1, w_inl[:, p0:g0], "nn", name="z_pool", bias=row(b_inl[p0:g0]))
        z_gate = _matmul(h1, w_inl[:, g0:], "nn", name="z_gate", bias=row(b_inl[g0:]))
        qkv = _qk_prep(z_qkv, cos_t, sin_t, row(q_gain[l]), row(k_gain[l]), tpe)
        if l + 1 < DEPTH:
            first, _ = split_for(l + 1)
            if l == 0:
                attn, res = _attn_fwd(qkv, B, C, R, carry=_Both(_Gather(ops(first), 1), _Gather(late_ops, 0)), mid_at=1.0)
                add_late(0, res[len(first):])
            else:
                attn, res = _attn_fwd(qkv, B, C, R, carry=_Gather(ops(first), l + 1), mid_at=0.85)
            add_early(res[:len(first)], first)
        else:
            attn, _ = _attn_fwd(qkv, B, C, R)
        dw32 =jnp.pad(wl["conv_dw_w"], ((0, 32 - CONV_K), (0, 0)))
        hc = _conv_fwd(z_conv, dw32, row(conv_dw_b[l]), B, C, S)
        sw, swt = _ln_silu(hc, row(conv_ln_g[l]), row(conv_ln_b[l]))
        conv_o = _matmul(sw, wl["conv_pw_w"], "nn", name="conv_pw", bias=row(conv_pw_b[l]))
        pooled, pool_o = _pool_fwd(z_pool, wl["pool_w"], row(pool_scale[l]), B, C, S)
        m, mt = _merge(attn, conv_o, pool_o, z_gate)
        mo = _matmul(m, wl["w_out"], "nn", name="w_out", bias=row(b_out[l]))
        y1, h2, h2t = _resid_ln(xin, mo, modt[l], 2, row(ln1_g[l]), row(ln1_b[l]), modt[l], 3)
        up = _matmul(h2, wl["w_up"], "nn", name="w_up", tn=1408)
        fw8 = jnp.pad(wl["ffn_dw_w"], ((0, 8 - FFN_K), (0, 0)))
        if l + 1 < DEPTH:
            rest = split_for(l + 1)[1]
            f, ft, late_blocks = _ffn_mid(up, fw8, row(ffn_dw_b[l]), B, C, S, carry=_Gather(ops(rest), l + 1))
            add_late(l + 1, late_blocks, rest)
        else:
            f, ft, _ = _ffn_mid(up, fw8, row(ffn_dw_b[l]), B, C, S)
        fo = _matmul(f, wl["w_down"], "nn", name="w_down", tm=512)
        if l + 1 < DEPTH:
            y2, h_next, ht_next = _resid_ln(y1, fo, modt[l], 5, row(ln2_g[l]), row(ln2_b[l]), modt[l + 1], 0)
        else:
            y2, h_next, ht_next = _resid_ln(y1, fo, modt[l], 5, row(ln2_g[l]), row(ln2_b[l])), None, None
        saved.append(dict(xin=xin, h1t=h1t, z_qkv=z_qkv, z_conv=z_conv, z_gate=z_gate, qkv=qkv, attn=attn, hc=hc, swt=swt,
                          conv_o=conv_o, pooled=pooled, pool_o=pool_o, mt=mt, mo=mo, y1=y1, h2t=h2t, up=up, ft=ft, fo=fo,
                          dw32=dw32, fw8=fw8))
        xin, h1, h1t = y2, h_next, ht_next

    dy, loss_part = _loss_grad(xin, loss_target.reshape(B * S, D), tpe, ncq)

    small = {n: [None] * DEPTH for n, _ in REPLICATED}
    d_c_ctx = jnp.zeros((D,), F32)
    dmods_t = [[None] * N_MOD for _ in range(DEPTH)]
    layer_grads = [None] * DEPTH
    kinds = ("grad_", "delta_", "new_m_", "new_v_")
    stacks = {n: [given[pre + n].reshape((DEPTH,) + _as_rows(s)) for pre in ("", "m_", "v_")] for n, s, _ in SHARDED}
    results = {n: None for n, _, _ in SHARDED}

    every = tuple(range(len(SHARDED)))

    def block_major(l, which=every):
        return [_blocks_from_full(layer_grads[l][SHARDED[k][0]], SHARDED[k][2]).reshape((N_DEV,) + _as_rows(SHARDED[k][1]))
                for k in which]

    def reduce_prepare(l, which=every):
        nonlocal d_c_ctx
        dmods = jnp.concatenate([_segment_sums(p, B, tpe, ncq) for p in dmods_t[l]], axis=1)
        small["b_ada"][l] = jnp.sum(dmods, axis=0)
        dm16 = jnp.concatenate([dmods, jnp.zeros((16 - B - 1, N_MOD * D), F32)], axis=0).astype(MXU)
        layer_grads[l]["w_ada"] = _matmul(s_cond, dm16, "tn", name="g_w_ada", tm=1024, tn=1024)
        dcond = _matmul(dm16, W[l]["w_ada"], "nt", name="d_cond", tm=16, tn=1024, tk=2048)
        d_c_ctx = d_c_ctx + dcond[B] * ds_cond[B]
        return block_major(l, which)

    def reduce_end(l, r2, which=every):
        for k, parts in zip(which, r2):
            n = SHARDED[k][0]
            results[n] = _adamw(parts, *stacks[n], l, results[n], name="adamw_sharded")

    dh1 = None
    for l in reversed(range(DEPTH)):
        gs_above = None
        wl, sv = W[l], saved[l]
        dmod = dmods_t[l]
        if dh1 is None:
            dy1p, dfo, dgate2, dg, db, _ = _ln_bwd(dy, sv["y1"], sv["fo"], modt[l], 5, row(ln2_g[l]))
        else:
            dy1p, dfo, dgate2, dg, db, _, dsh, dsc = _ln_bwd(dy, sv["y1"], sv["fo"], modt[l], 5, row(ln2_g[l]), dh=dh1,
                                                             y=saved[l + 1]["xin"], mod_next=modt[l + 1], k_shift_next=0)
            dmods_t[l + 1][0], dmods_t[l + 1][1] = dsh, dsc
            gs_above = reduce_prepare(l + 1)
        small["ln2_g"][l], small["ln2_b"][l] = dg[0], db[0]
        dmod[5] = dgate2
        df = _matmul(dfo, wl["w_down"], "nt", name="d_f", tn=1408)
        g_w_down = _matmul(sv["ft"], dfo, "nn", name="g_w_down", tm=1408, tk=2304)
        if gs_above is None:
            da2, du2, g_fdw, g_fdb, _ = _ffn_mid_bwd(df, sv["up"], sv["fw8"], row(ffn_dw_b[l]), B, C, S)
            ps_above = None
        else:
            da2, du2, g_fdw, g_fdb, r1 = _ffn_mid_bwd(df, sv["up"], sv["fw8"], row(ffn_dw_b[l]), B, C, S,
                                                      carry=_SiblingExchange(gs_above))
            ps_above = [_pair_sum(g, r) for g, r in zip(gs_above, r1)]
        small["ffn_dw_b"][l] = g_fdb[0]
        dh2 = _matmul(da2, wl["w_up"][:, :D_FF], "nt", name="d_h2a", tm=512)
        dh2 = _matmul(du2, wl["w_up"][:, D_FF:], "nt", name="d_h2u", tm=512, acc_in=dh2)
        g_w_up = _matmul(sv["h2t"], da2, "nn", name="g_w_up_a", tn=1408, tk=2304, into=(2 * D_FF, 0, None))
        g_w_up = _matmul(sv["h2t"], du2, "nn", name="g_w_up_u", tn=1408, tk=2304, into=(2 * D_FF, D_FF, g_w_up))
        dxp, dmo, dgate1, dg, db, dbo, dsh, dsc = _ln_bwd(dy1p, sv["xin"], sv["mo"], modt[l], 2, row(ln1_g[l]), dh=dh2,
                                                          y=sv["y1"], mod_next=modt[l], k_shift_next=3)
        small["ln1_g"][l], small["ln1_b"][l], small["b_out"][l] = dg[0], db[0], dbo[0]
        dmod[2], dmod[3], dmod[4] = dgate1, dsh, dsc
        dm = _matmul(dmo, wl["w_out"], "nt", name="d_m")
        g_w_out = _matmul(sv["mt"], dmo, "nn", name="g_w_out", tk=2304)
        dattn, dconv_o, dpool_o, dzg, g_pwb, gsum_gate = _merge_bwd(dm, sv["attn"], sv["conv_o"], sv["pool_o"], sv["z_gate"])
        small["conv_pw_b"][l] = g_pwb[0]
        du, g_pool_w, g_pool_sc, gsum_pool = _pool_bwd(dpool_o, sv["pooled"], wl["pool_w"], row(pool_scale[l]), B, C, S)
        small["pool_scale"][l] = g_pool_sc[0]
        dsw = _matmul(dconv_o, wl["conv_pw_w"], "nt", name="d_sw")
        g_pw = _matmul(sv["swt"], dconv_o, "nn", name="g_conv_pw", tk=2304)
        dhc, g_cg, g_cb, g_cdb = _ln_silu_bwd(dsw, sv["hc"], row(conv_ln_g[l]), row(conv_ln_b[l]))
        small["conv_ln_g"][l], small["conv_ln_b"][l], small["conv_dw_b"][l] = g_cg[0], g_cb[0], g_cdb[0]
        da, dgt, g_cdw, gsum_a, gsum_gt = _conv_bwd(dhc, sv["z_conv"], sv["dw32"], B, C, S)
        layer_grads[l] = {"conv_pw_w": g_pw, "pool_w": g_pool_w, "w_out": g_w_out, "w_up": g_w_up, "w_down": g_w_down,
                          "conv_dw_w": g_cdw[:CONV_K], "ffn_dw_w": g_fdw[:FFN_K]}
        if ps_above is None:
            dq, dk, dv, _ = _attn_bwd(sv["qkv"], sv["attn"], dattn, B, C, R)
        elif l > 0:
            dq, dk, dv, r2 = _attn_bwd(sv["qkv"], sv["attn"], dattn, B, C, R, carry=_ChipExchange(ps_above))
            reduce_end(l + 1, r2)
        else:
            gs = block_major(0, READY_0)
            r1 = _run_exchange(_SiblingExchange(gs), name="sibling_exchange")
            ps_own = [_pair_sum(g, r) for g, r in zip(gs, r1)]
            dq, dk, dv, r2 = _attn_bwd(sv["qkv"], sv["attn"], dattn, B, C, R,
                                       carry=_Both(_ChipExchange(ps_above), _ChipExchange(ps_own)))
            reduce_end(1, r2[:len(SHARDED)])
            reduce_end(0, r2[len(SHARDED):], READY_0)
        dz_qkv, g_qg, g_kg, gsum_qkv = _qk_bwd(dq, dk, dv, sv["z_qkv"], cos_t, sin_t, row(q_gain[l]), row(k_gain[l]), tpe)
        small["q_gain"][l], small["k_gain"][l] = g_qg[0], g_kg[0]
        small["b_in"][l] = jnp.concatenate([gsum_qkv[0], gsum_a[0], gsum_gt[0], gsum_pool[0], gsum_gate[0]])
        w_inl = wl["w_in"]
        g0 = QKV_W + 2 * CONV_CH + POOL_CH
        pieces = ((dzg, g0, N_GATE), (da, QKV_W, CONV_CH), (dgt, QKV_W + CONV_CH, CONV_CH), (du, QKV_W + 2 * CONV_CH, POOL_CH),
                  (dz_qkv, 0, QKV_W))
        dh1 = g_w_in = None
        at = 0
        for k, (dz, c0, wd) in enumerate(pieces):
            dh1 = _matmul(dz, w_inl[:, c0:c0 + wd], "nt", name=f"d_h1_{k}", tm=512, acc_in=dh1)
            g_w_in = _matmul(sv["h1t"], dz, "nn", name=f"g_w_in_{k}", tk=2304, into=(D_IN, at, g_w_in))
            at += wd
        g_w_in = jnp.concatenate([g_w_in[:, N_GATE + 2 * CONV_CH + POOL_CH:], g_w_in[:, N_GATE:N_GATE + 2 * CONV_CH + POOL_CH],
                                  g_w_in[:, :N_GATE]], axis=1)

        layer_grads[l]["w_in"] = g_w_in
        dy = dxp
    gx_u, dmods_t[0][0], dmods_t[0][1] = _mod_bwd(dy, dh1, saved[0]["xin"], modt[0], 0)
    grad_x = gx_u.reshape(B, R, D)[:, C:]

    last = every if DEPTH == 1 else LAST_0
    gs = reduce_prepare(0, last)
    r1 = _run_exchange(_SiblingExchange(gs), name="sibling_exchange")
    reduce_end(0, _run_exchange(_ChipExchange([_pair_sum(g, r) for g, r in zip(gs, r1)]), name="chip_exchange"), last)
    outs = {}
    for n, s, _ in SHARDED:
        for kind, buf in zip(kinds, results[n]):
            outs[kind + n] = buf.reshape((DEPTH,) + tuple(s))

    def small_pack(pieces):
        flat = jnp.concatenate([p.reshape(-1) for p in pieces])
        return jnp.pad(flat, (0, SMALL_ROWS * LANES - flat.shape[0])).reshape(SMALL_ROWS, LANES)

    zero1 = jnp.zeros((1,), F32)
    g_pack = small_pack([small[n][l] for n, _ in REPLICATED for l in range(DEPTH)] + [d_c_ctx, loss_part[0, :1]])
    g_small, = _all_gather([g_pack], name="gather_small")
    wmv = [small_pack([given[pre + n] for n, _ in REPLICATED] + [given[pre + "c_ctx"], zero1])[None] for pre in ("", "m_", "v_")]
    res = _adamw(g_small, *wmv, 0, None, name="adamw_small")
    for kind, buf in zip(kinds, res):
        flat = buf.reshape(-1)
        off = 0
        for n, sz in REPLICATED:
            outs[kind + n] = flat[off:off + DEPTH * sz].reshape(DEPTH, sz)
            off += DEPTH * sz
        outs[kind + "c_ctx"] = flat[off:off + D]
        if kind == "grad_":
            loss = flat[off + D]

    names = ["c_ctx", "w_ada", "b_ada", "w_in", "b_in", "q_gain", "k_gain", "conv_dw_w", "conv_dw_b", "conv_ln_g", "conv_ln_b",
             "conv_pw_w", "conv_pw_b", "pool_w", "pool_scale", "w_out", "b_out", "ln1_g", "ln1_b", "ln2_g", "ln2_b", "w_up",
             "ffn_dw_w", "ffn_dw_b", "w_down"]
    return (loss, grad_x, *[outs[k + n] for k in ("grad_", "delta_", "new_m_", "new_v_") for n in names])
```

```python
import functools

import jax
import jax.numpy as jnp
import numpy as np
from jax import lax
from jax.experimental import pallas as pl
from jax.experimental.pallas import tpu as pltpu

F32 = jnp.float32
MXU = jnp.bfloat16
WIRE = jnp.bfloat16

D = 1024
HD = 128
NH = 8
NKV = 2
QG = NH // NKV
KV_W = NKV * HD
QKV_W = NH * HD + 2 * KV_W
CONV_CH = D
POOL_CH = D
POOL_WINDOWS = (2, 4, 8, 16)
POOL_GCH = POOL_CH // len(POOL_WINDOWS)
N_GATE = 3 * D
D_IN = QKV_W + 2 * CONV_CH + POOL_CH + N_GATE
D_FF = 2816
N_MOD = 6
DEPTH = 4
CONV_K = 31
FFN_K = 3
GRID_W = 64
ROPE_THETA = 10000.0
ROPE_PAIRS = HD // 4
ALPHA = (2 * DEPTH) ** 0.25
LN_EPS = 1e-5
RMS_EPS = 1e-6
ATTN_SCALE = HD ** -0.5
LOG2_E = 1.4426950408889634
ADAM_LR, ADAM_B1, ADAM_B2, ADAM_EPS, ADAM_WD, ADAM_STEP = 0.001, 0.9, 0.999, 1e-08, 0.01, 10

N_DEV = 8
TM = 256
GAP = 16
LANES = 1024
LANE = 128
VMEM_MB = 48

NN = (((1,), (0,)), ((), ()))
NT = (((1,), (1,)), ((), ()))
TN = (((0,), (0,)), ((), ()))

_pcall = pl.pallas_call


def _call(body, *, name, grid, in_specs, out_specs, out_shape, scratch=(), aliases=None, vmem=VMEM_MB):
    return _pcall(
        body, name=name, grid=grid, in_specs=in_specs, out_specs=out_specs, out_shape=out_shape,
        scratch_shapes=list(scratch), input_output_aliases=aliases or {},
        compiler_params=pltpu.CompilerParams(dimension_semantics=("arbitrary",) * len(grid), vmem_limit_bytes=vmem * 2 ** 20),
    )


def _sds(shape, dtype):
    return jax.ShapeDtypeStruct(tuple(shape), dtype)


def _pick(n, cap, mult):
    best = None
    for t in range(mult, min(n, cap) + 1, mult):
        if n % t == 0:
            best = t
    return best if best is not None else n


def _dot(a, b, dims):
    return lax.dot_general(a, b, dims, preferred_element_type=F32)


def _sigmoid(x):
    return 1.0 / (1.0 + jnp.exp(-x))


def _matmul(a, b, mode, *, name, bias=None, acc_in=None, into=None, out_dtype=F32, tm=1024, tn=1024, tk=None):
    if mode == "nn":
        (M, K), (K2, N) = a.shape, b.shape
    elif mode == "nt":
        (M, K), (N, K2) = a.shape, b.shape
    else:
        (K, M), (K2, N) = a.shape, b.shape
    assert K == K2, (a.shape, b.shape, mode)
    tm = _pick(M, tm, 16)
    tn = _pick(N, tn, 128)
    tk = K if tk is None else _pick(K, tk, 128 if mode != "tn" else 16)
    gk = K // tk
    dims = {"nn": NN, "nt": NT, "tn": TN}[mode]
    a_spec = pl.BlockSpec((tk, tm), lambda j, i, k: (k, i)) if mode == "tn" else pl.BlockSpec((tm, tk), lambda j, i, k: (i, k))
    b_spec = pl.BlockSpec((tn, tk), lambda j, i, k: (j, k)) if mode == "nt" else pl.BlockSpec((tk, tn), lambda j, i, k: (k, j))
    in_specs, args = [a_spec, b_spec], [a, b]
    if bias is not None:
        in_specs.append(pl.BlockSpec((1, tn), lambda j, i, k: (0, j)))
        args.append(bias)
    aliases = {}
    if acc_in is not None:
        aliases = {len(args): 0}
        in_specs.append(pl.BlockSpec((tm, tn), lambda j, i, k: (i, j)))
        args.append(acc_in)
    n_total, col0, prev = (N, 0, None) if into is None else into
    assert col0 % tn == 0
    jb = col0 // tn
    if prev is not None:
        aliases = {len(args): 0}
        in_specs.append(pl.BlockSpec(memory_space=pl.ANY))
        args.append(prev)
    n_in = len(args)

    def body(*refs):
        a_ref, b_ref = refs[0], refs[1]
        pos = 2
        bias_ref = acc_in_ref = None
        if bias is not None:
            bias_ref = refs[pos]
            pos += 1
        if acc_in is not None:
            acc_in_ref = refs[pos]
        pos = n_in
        o_ref = refs[pos]
        part = _dot(a_ref[...].astype(MXU), b_ref[...].astype(MXU), dims)

        def finish(acc):
            if bias_ref is not None:
                acc = acc + bias_ref[...]
            if acc_in_ref is not None:
                acc = acc + acc_in_ref[...]
            o_ref[...] = acc.astype(out_dtype)

        if gk == 1:
            finish(part)
        else:
            acc_ref = refs[pos + 1]
            k = pl.program_id(2)

            @pl.when(k == 0)
            def _():
                acc_ref[...] = part

            @pl.when(k > 0)
            def _():
                acc_ref[...] += part

            @pl.when(k == gk - 1)
            def _():
                finish(acc_ref[...])

    return _call(
        body, name=name, grid=(N // tn, M // tm, gk), in_specs=in_specs,
        out_specs=pl.BlockSpec((tm, tn), lambda j, i, k: (i, j + jb)), out_shape=_sds((M, n_total), out_dtype),
        scratch=[pltpu.VMEM((tm, tn), F32)] if gk > 1 else [], aliases=aliases,
    )(*args)


def _rt(w, cb=0):
    return pl.BlockSpec((TM, w), lambda i: (i, cb))


def _ct(w):
    return pl.BlockSpec((w, TM), lambda i: (0, i))


def _vec(w):
    return pl.BlockSpec((1, w), lambda i: (0, 0))


def _part(w):
    return pl.BlockSpec((1, 1, w), lambda i: (i, 0, 0))


def _mod(ref, k):
    return ref[0, :, k * D:(k + 1) * D]


def _colsum(x):
    return jnp.sum(x, axis=0, keepdims=True)


def _ln_stats(s):
    mu = jnp.mean(s, axis=1, keepdims=True)
    cen = s - mu
    var = jnp.mean(cen * cen, axis=1, keepdims=True)
    rstd = lax.rsqrt(var + LN_EPS)
    return cen * rstd, rstd


def _modulate_cast(x, modt, k_shift):
    T = x.shape[0]
    nt = T // TM

    def body(x_ref, mod_ref, h_ref, ht_ref):
        h = (x_ref[...] * (1.0 + _mod(mod_ref, k_shift + 1)) + _mod(mod_ref, k_shift)).astype(MXU)
        h_ref[...] = h
        ht_ref[...] = h.T

    return _call(body, name="modulate", grid=(nt,), in_specs=[_rt(D), _part(N_MOD * D)], out_specs=[_rt(D), _ct(D)],
                 out_shape=[_sds((T, D), MXU), _sds((D, T), MXU)])(x, modt)


def _resid_ln(x, br, modt, k_gate, g, b, mod_next=None, k_shift_next=0):
    T = x.shape[0]
    nt = T // TM
    with_h = mod_next is not None

    def body(*refs):
        x_ref, br_ref, mod_ref, g_ref, b_ref = refs[:5]
        s = ALPHA * x_ref[...] + _mod(mod_ref, k_gate) * br_ref[...]
        xhat, _ = _ln_stats(s)
        y = xhat * g_ref[...] + b_ref[...]
        if with_h:
            modn_ref, y_ref, h_ref, ht_ref = refs[5:]
            y_ref[...] = y
            h = (y * (1.0 + _mod(modn_ref, k_shift_next + 1)) + _mod(modn_ref, k_shift_next)).astype(MXU)
            h_ref[...] = h
            ht_ref[...] = h.T
        else:
            refs[5][...] = y

    in_specs = [_rt(D), _rt(D), _part(N_MOD * D), _vec(D), _vec(D)]
    args = [x, br, modt, g, b]
    if with_h:
        in_specs.append(_part(N_MOD * D))
        args.append(mod_next)
        return _call(body, name="resid_ln_mod", grid=(nt,), in_specs=in_specs, out_specs=[_rt(D), _rt(D), _ct(D)],
                     out_shape=[_sds((T, D), F32), _sds((T, D), MXU), _sds((D, T), MXU)])(*args)
    return _call(body, name="resid_ln", grid=(nt,), in_specs=in_specs, out_specs=_rt(D), out_shape=_sds((T, D), F32))(*args)


def _ln_bwd(dy_part, x, br, modt, k_gate, g, dh=None, y=None, mod_next=None, k_shift_next=0):
    T = x.shape[0]
    nt = T // TM
    with_h = dh is not None

    def body(*refs):
        if with_h:
            dyp_ref, x_ref, br_ref, mod_ref, g_ref, dh_ref, y_ref, modn_ref = refs[:8]
            outs = refs[8:]
        else:
            dyp_ref, x_ref, br_ref, mod_ref, g_ref = refs[:5]
            outs = refs[5:]
        dx_ref, dbr_ref, dgate_ref, dlg_ref, dlb_ref, dbsum_ref = outs[:6]
        i = pl.program_id(0)

        @pl.when(i == 0)
        def _():
            dlg_ref[...] = jnp.zeros_like(dlg_ref)
            dlb_ref[...] = jnp.zeros_like(dlb_ref)
            dbsum_ref[...] = jnp.zeros_like(dbsum_ref)

        dy = dyp_ref[...]
        if with_h:
            dshift_ref, dscale_ref = outs[6:]
            dhv = dh_ref[...]
            dy = dy + dhv * (1.0 + _mod(modn_ref, k_shift_next + 1))
            dshift_ref[0] = _colsum(dhv)
            dscale_ref[0] = _colsum(dhv * y_ref[...])
        gate = _mod(mod_ref, k_gate)
        brv = br_ref[...]
        s = ALPHA * x_ref[...] + gate * brv
        xhat, rstd = _ln_stats(s)
        dlg_ref[...] += _colsum(dy * xhat)
        dlb_ref[...] += _colsum(dy)
        dyg = dy * g_ref[...]
        m1 = jnp.mean(dyg, axis=1, keepdims=True)
        m2 = jnp.mean(dyg * xhat, axis=1, keepdims=True)
        ds = rstd * (dyg - m1 - xhat * m2)
        dx_ref[...] = ALPHA * ds
        dbr = gate * ds
        dbr_ref[...] = dbr.astype(MXU)
        dbsum_ref[...] += _colsum(dbr)
        dgate_ref[0] = _colsum(ds * brv)

    in_specs = [_rt(D), _rt(D), _rt(D), _part(N_MOD * D), _vec(D)]
    args = [dy_part, x, br, modt, g]
    out_specs = [_rt(D), _rt(D), _part(D), _vec(D), _vec(D), _vec(D)]
    out_shape = [_sds((T, D), F32), _sds((T, D), MXU), _sds((nt, 1, D), F32), _sds((1, D), F32), _sds((1, D), F32), _sds((1, D), F32)]
    if with_h:
        in_specs += [_rt(D), _rt(D), _part(N_MOD * D)]
        args += [dh, y, mod_next]
        out_specs += [_part(D), _part(D)]
        out_shape += [_sds((nt, 1, D), F32), _sds((nt, 1, D), F32)]
    return _call(body, name="ln_bwd_mod" if with_h else "ln_bwd", grid=(nt,), in_specs=in_specs, out_specs=out_specs,
                 out_shape=out_shape)(*args)


def _mod_bwd(dx_part, dh, x, modt, k_shift):
    T = x.shape[0]
    nt = T // TM

    def body(dxp_ref, dh_ref, x_ref, mod_ref, dx_ref, dshift_ref, dscale_ref):
        dhv = dh_ref[...]
        dx_ref[...] = dxp_ref[...] + dhv * (1.0 + _mod(mod_ref, k_shift + 1))
        dshift_ref[0] = _colsum(dhv)
        dscale_ref[0] = _colsum(dhv * x_ref[...])

    return _call(body, name="mod_bwd", grid=(nt,), in_specs=[_rt(D), _rt(D), _rt(D), _part(N_MOD * D)],
                 out_specs=[_rt(D), _part(D), _part(D)],
                 out_shape=[_sds((T, D), F32), _sds((nt, 1, D), F32), _sds((nt, 1, D), F32)])(dx_part, dh, x, modt)


def _loss_grad(y, target, tpe, ncq):
    T = y.shape[0]
    nt = T // TM
    nl = tpe - ncq

    def body(y_ref, t_ref, dy_ref, loss_ref):
        i = pl.program_id(0)

        @pl.when(i == 0)
        def _():
            loss_ref[...] = jnp.zeros_like(loss_ref)

        @pl.when(i % tpe < ncq)
        def _():
            dy_ref[...] = jnp.zeros_like(dy_ref)

        @pl.when(i % tpe >= ncq)
        def _():
            err = y_ref[...] - t_ref[...]
            dy_ref[...] = err * (1.0 / D)
            loss_ref[...] += (0.5 / D) * jnp.sum(_colsum(err * err), axis=1, keepdims=True)

    tgt_spec = pl.BlockSpec((TM, D), lambda i: ((i // tpe) * nl + jnp.maximum(i % tpe - ncq, 0), 0))
    return _call(body, name="loss_grad", grid=(nt,), in_specs=[_rt(D), tgt_spec], out_specs=[_rt(D), _vec(128)],
                 out_shape=[_sds((T, D), F32), _sds((1, 128), F32)])(y, target)


def _rope_partner(x):
    lane = lax.broadcasted_iota(jnp.int32, x.shape, 1)
    first = (lane % (2 * ROPE_PAIRS)) < ROPE_PAIRS
    return jnp.where(first, pltpu.roll(x, HD - ROPE_PAIRS, 1), pltpu.roll(x, ROPE_PAIRS, 1))


def _qk_prep(z_qkv, cos_t, sin_t, q_gain, k_gain, tpe):
    T = z_qkv.shape[0]
    nt = T // TM

    def body(z_ref, cos_ref, sin_ref, qg_ref, kg_ref, o_ref):
        cos, sin = cos_ref[...], sin_ref[...]
        for h in range(NH + NKV):
            sl = slice(h * HD, (h + 1) * HD)
            t = z_ref[:, sl]
            gain = qg_ref[...] if h < NH else kg_ref[...]
            n = t * lax.rsqrt(jnp.mean(t * t, axis=1, keepdims=True) + RMS_EPS) * gain
            o_ref[:, sl] = (n * cos + _rope_partner(n) * sin).astype(MXU)
        o_ref[:, (NH + NKV) * HD:] = z_ref[:, (NH + NKV) * HD:].astype(MXU)

    tab = pl.BlockSpec((TM, HD), lambda i: (i % tpe, 0))
    return _call(body, name="qk_prep", grid=(nt,), in_specs=[_rt(QKV_W), tab, tab, _vec(HD), _vec(HD)], out_specs=_rt(QKV_W),
                 out_shape=_sds((T, QKV_W), MXU))(z_qkv, cos_t, sin_t, q_gain, k_gain)


def _qk_bwd(dq, dk, dv, z_qkv, cos_t, sin_t, q_gain, k_gain, tpe):
    T = z_qkv.shape[0]
    nt = T // TM

    def body(dq_ref, dk_ref, dv_ref, z_ref, cos_ref, sin_ref, qg_ref, kg_ref, dz_ref, dqg_ref, dkg_ref, bsum_ref):
        i = pl.program_id(0)

        @pl.when(i == 0)
        def _():
            dqg_ref[...] = jnp.zeros_like(dqg_ref)
            dkg_ref[...] = jnp.zeros_like(dkg_ref)
            bsum_ref[...] = jnp.zeros_like(bsum_ref)

        cos, sin = cos_ref[...], sin_ref[...]
        for h in range(NH + NKV):
            sl = slice(h * HD, (h + 1) * HD)
            dr = dq_ref[:, sl] if h < NH else dk_ref[:, (h - NH) * HD:(h - NH + 1) * HD]
            gain = qg_ref[...] if h < NH else kg_ref[...]
            dn = dr * cos + _rope_partner(dr * sin)
            t = z_ref[:, sl]
            rstd = lax.rsqrt(jnp.mean(t * t, axis=1, keepdims=True) + RMS_EPS)
            that = t * rstd
            dgain = _colsum(dn * that)
            if h < NH:
                dqg_ref[...] += dgain
            else:
                dkg_ref[...] += dgain
            dthat = dn * gain
            dt = rstd * (dthat - that * jnp.mean(dthat * that, axis=1, keepdims=True))
            dz_ref[:, sl] = dt.astype(MXU)
            bsum_ref[:, sl] += _colsum(dt)
        dvv = dv_ref[...]
        dz_ref[:, (NH + NKV) * HD:] = dvv.astype(MXU)
        bsum_ref[:, (NH + NKV) * HD:] += _colsum(dvv)

    tab = pl.BlockSpec((TM, HD), lambda i: (i % tpe, 0))
    return _call(body, name="qk_bwd", grid=(nt,),
                 in_specs=[_rt(NH * HD), _rt(KV_W), _rt(KV_W), _rt(QKV_W), tab, tab, _vec(HD), _vec(HD)],
                 out_specs=[_rt(QKV_W), _vec(HD), _vec(HD), _vec(QKV_W)],
                 out_shape=[_sds((T, QKV_W), MXU), _sds((1, HD), F32), _sds((1, HD), F32), _sds((1, QKV_W), F32)],
                 )(dq, dk, dv, z_qkv, cos_t, sin_t, q_gain, k_gain)


def _ln_silu(hc, g, b):
    T = hc.shape[0]

    def body(h_ref, g_ref, b_ref, o_ref, ot_ref):
        xhat, _ = _ln_stats(h_ref[...])
        n = xhat * g_ref[...] + b_ref[...]
        sw = (n * _sigmoid(n)).astype(MXU)
        o_ref[...] = sw
        ot_ref[...] = sw.T

    return _call(body, name="ln_silu", grid=(T // TM,), in_specs=[_rt(D), _vec(D), _vec(D)], out_specs=[_rt(D), _ct(D)],
                 out_shape=[_sds((T, D), MXU), _sds((D, T), MXU)])(hc, g, b)


def _ln_silu_bwd(dsw, hc, g, b):
    T = hc.shape[0]

    def body(d_ref, h_ref, g_ref, b_ref, dh_ref, dg_ref, db_ref, dcb_ref):
        i = pl.program_id(0)

        @pl.when(i == 0)
        def _():
            dg_ref[...] = jnp.zeros_like(dg_ref)
            db_ref[...] = jnp.zeros_like(db_ref)
            dcb_ref[...] = jnp.zeros_like(dcb_ref)

        xhat, rstd = _ln_stats(h_ref[...])
        n = xhat * g_ref[...] + b_ref[...]
        sg = _sigmoid(n)
        dn = d_ref[...] * (sg * (1.0 + n * (1.0 - sg)))
        dg_ref[...] += _colsum(dn * xhat)
        db_ref[...] += _colsum(dn)
        dng = dn * g_ref[...]
        m1 = jnp.mean(dng, axis=1, keepdims=True)
        m2 = jnp.mean(dng * xhat, axis=1, keepdims=True)
        dh = rstd * (dng - m1 - xhat * m2)
        dh_ref[...] = dh
        dcb_ref[...] += _colsum(dh)

    return _call(body, name="ln_silu_bwd", grid=(T // TM,), in_specs=[_rt(D), _rt(D), _vec(D), _vec(D)],
                 out_specs=[_rt(D), _vec(D), _vec(D), _vec(D)],
                 out_shape=[_sds((T, D), F32)] + [_sds((1, D), F32)] * 3)(dsw, hc, g, b)


def _merge(attn, conv_o, pool_o, z_gate):
    T = attn.shape[0]

    def body(a_ref, c_ref, p_ref, zg_ref, m_ref, mt_ref):
        m = (_sigmoid(zg_ref[:, 0:D]) * a_ref[...] + _sigmoid(zg_ref[:, D:2 * D]) * c_ref[...]
             + _sigmoid(zg_ref[:, 2 * D:3 * D]) * p_ref[...]).astype(MXU)
        m_ref[...] = m
        mt_ref[...] = m.T

    return _call(body, name="merge", grid=(T // TM,), in_specs=[_rt(D), _rt(D), _rt(D), _rt(N_GATE)], out_specs=[_rt(D), _ct(D)],
                 out_shape=[_sds((T, D), MXU), _sds((D, T), MXU)])(attn, conv_o, pool_o, z_gate)


def _merge_bwd(dm, attn, conv_o, pool_o, z_gate):
    T = attn.shape[0]

    def body(dm_ref, a_ref, c_ref, p_ref, zg_ref, da_ref, dc_ref, dp_ref, dzg_ref, dcsum_ref, gsum_ref):
        i = pl.program_id(0)

        @pl.when(i == 0)
        def _():
            dcsum_ref[...] = jnp.zeros_like(dcsum_ref)
            gsum_ref[...] = jnp.zeros_like(gsum_ref)

        dmv = dm_ref[...]
        for k, (br_ref, out_ref) in enumerate(((a_ref, da_ref), (c_ref, dc_ref), (p_ref, dp_ref))):
            gk = _sigmoid(zg_ref[:, k * D:(k + 1) * D])
            dbr = dmv * gk
            out_ref[...] = dbr.astype(out_ref.dtype)
            if k == 1:
                dcsum_ref[...] += _colsum(dbr)
            dzg = dmv * br_ref[...] * gk * (1.0 - gk)
            dzg_ref[:, k * D:(k + 1) * D] = dzg.astype(MXU)
            gsum_ref[:, k * D:(k + 1) * D] += _colsum(dzg)

    return _call(body, name="merge_bwd", grid=(T // TM,), in_specs=[_rt(D), _rt(D), _rt(D), _rt(D), _rt(N_GATE)],
                 out_specs=[_rt(D), _rt(D), _rt(D), _rt(N_GATE), _vec(D), _vec(N_GATE)],
                 out_shape=[_sds((T, D), F32), _sds((T, D), MXU), _sds((T, D), F32), _sds((T, N_GATE), MXU),
                            _sds((1, D), F32), _sds((1, N_GATE), F32)])(dm, attn, conv_o, pool_o, z_gate)


def _softmax_parts(q, k):
    s = _dot(q, k, NT)
    p = jnp.exp2((s - jnp.max(s, axis=1, keepdims=True)) * (ATTN_SCALE * LOG2_E))
    return p, 1.0 / jnp.sum(p, axis=1, keepdims=True)


def _attn_specs(nq):
    q_spec = pl.BlockSpec((TM, QG * HD), lambda b, h, q: (b * nq + q, h))
    k_spec = pl.BlockSpec((nq * TM, HD), lambda b, h, q: (b, NH + h))
    v_spec = pl.BlockSpec((nq * TM, HD), lambda b, h, q: (b, NH + NKV + h))
    return q_spec, k_spec, v_spec


class _Carried:
    def __init__(self, ex, n_in, n_out, n_scratch, mid_at=0.75):
        self.ex, self.n_in, self.n_out, self.n_scratch, self.mid_at = ex, n_in, n_out, n_scratch, mid_at
        self.ci = len(ex.inputs) if ex else 0
        self.co = len(ex.out_shape) if ex else 0

    def in_specs(self):
        return [ANY] * self.ci

    def out_specs(self):
        return [ANY] * self.co

    def split(self, refs):
        a = self.n_in
        b = a + self.ci
        c = b + self.n_out
        d = c + self.co
        e = d + self.n_scratch
        return (refs[:a], refs[b:c], refs[d:e]), (refs[a:b], refs[c:d], refs[e:])

    def before(self, step, parts):
        if self.ex:
            pl.when(step == 0)(lambda: self.ex.start(*parts))

    def after(self, step, n_steps, parts):
        if self.ex:
            pl.when(step == min(n_steps - 1, int(self.mid_at * n_steps)))(lambda: self.ex.mid(*parts))
            pl.when(step == n_steps - 1)(lambda: self.ex.finish(*parts))


def _attn_fwd(qkv, B, C, R, carry=None, mid_at=0.75):
    nq, ncq = R // TM, C // TM
    car = _Carried(carry, 3, 1, 0, mid_at)

    def body(*refs):
        (q_ref, k_ref, v_ref), (o_ref,), _ = car.split(refs)[0]
        parts = car.split(refs)[1]
        qi = pl.program_id(2)
        step = (pl.program_id(0) * NKV + pl.program_id(1)) * nq + qi
        car.before(step, parts)

        def attend(L):
            k, v = k_ref[0:L, :], v_ref[0:L, :]
            for i in range(QG):
                sl = slice(i * HD, (i + 1) * HD)
                p, inv_l = _softmax_parts(q_ref[:, sl], k)
                o_ref[:, sl] = _dot(p.astype(MXU), v, NN) * inv_l

        pl.when(qi < ncq)(functools.partial(attend, C))
        pl.when(qi >= ncq)(functools.partial(attend, R))
        car.after(step, B * NKV * nq, parts)

    q_spec, k_spec, v_spec = _attn_specs(nq)
    res = _call(body, name="attn_fwd", grid=(B, NKV, nq), in_specs=[q_spec, k_spec, v_spec] + car.in_specs(),
                out_specs=[q_spec] + car.out_specs(), out_shape=[_sds((B * R, NH * HD), F32)] + list(carry.out_shape if carry else []),
                scratch=list(carry.scratch) if carry else [])(qkv, qkv, qkv, *(carry.inputs if carry else []))
    return res[0], res[1:]


def _attn_bwd(qkv, o, do, B, C, R, carry=None):
    nq, ncq = R // TM, C // TM
    car = _Carried(carry, 5, 3, 2)

    def body(*refs):
        (q_ref, k_ref, v_ref, o_ref, do_ref), (dq_ref, dk_ref, dv_ref), (dkt, dvt) = car.split(refs)[0]
        parts = car.split(refs)[1]
        qi = pl.program_id(2)
        step = (pl.program_id(0) * NKV + pl.program_id(1)) * nq + qi
        car.before(step, parts)

        @pl.when(qi == 0)
        def _():
            dkt[...] = jnp.zeros_like(dkt)
            dvt[...] = jnp.zeros_like(dvt)

        def bwd(L):
            k, v = k_ref[0:L, :], v_ref[0:L, :]
            for i in range(QG):
                sl = slice(i * HD, (i + 1) * HD)
                q = q_ref[:, sl]
                p, inv_l = _softmax_parts(q, k)
                dov = do_ref[:, sl]
                dp = _dot(dov.astype(MXU), v, NT)
                dl = jnp.sum(dov * o_ref[:, sl], axis=1, keepdims=True)
                ds = (p * ((dp - dl) * (inv_l * ATTN_SCALE))).astype(MXU)
                dq_ref[:, sl] = _dot(ds, k, NN)
                dkt[:, 0:L] += _dot(q, ds, TN)
                dvt[:, 0:L] += _dot((dov * inv_l).astype(MXU), p.astype(MXU), TN)

        pl.when(qi < ncq)(functools.partial(bwd, C))
        pl.when(qi >= ncq)(functools.partial(bwd, R))

        @pl.when(qi == nq - 1)
        def _():
            dk_ref[...] = dkt[...].T
            dv_ref[...] = dvt[...].T

        car.after(step, B * NKV * nq, parts)

    q_spec, k_spec, v_spec = _attn_specs(nq)
    kv_out = pl.BlockSpec((R, HD), lambda b, h, q: (b, h))
    res = _call(body, name="attn_bwd", grid=(B, NKV, nq), in_specs=[q_spec, k_spec, v_spec, q_spec, q_spec] + car.in_specs(),
                out_specs=[q_spec, kv_out, kv_out] + car.out_specs(),
                out_shape=[_sds((B * R, NH * HD), F32), _sds((B * R, KV_W), F32), _sds((B * R, KV_W), F32)]
                + list(carry.out_shape if carry else []),
                scratch=[pltpu.VMEM((HD, R), F32), pltpu.VMEM((HD, R), F32)] + list(carry.scratch if carry else []),
                )(qkv, qkv, qkv, o, do, *(carry.inputs if carry else []))
    return res[0], res[1], res[2], res[3:]


def _segments(C, S):
    return ((0, GAP, C), (C, 2 * GAP + C, S))


def _padded_rows(C, S):
    return 3 * GAP + C + S


def _zero_gaps(pad_ref, C, S):
    for off in (0, GAP + C, 2 * GAP + C + S):
        pad_ref[off:off + GAP, :] = jnp.zeros((GAP, pad_ref.shape[1]), pad_ref.dtype)


def _chunks(n, ch, fn):
    def step(i, carry):
        fn(pl.multiple_of(i * ch, ch))
        return carry

    lax.fori_loop(0, n // ch, step, 0)


class _Window:
    def __init__(self, pad_ref, row, ch):
        self.pad_ref, (self.r, self.po), self.ch = pad_ref, row, ch

    def at(self, off):
        return self.pad_ref[pl.ds(self.r + (self.po + off), self.ch), :]


def _taps(pad_ref, w, row, ch, n_taps, flip=False):
    half = (n_taps - 1) // 2
    win = _Window(pad_ref, row, ch)
    acc = None
    for k in range(n_taps):
        term = w[k:k + 1, :] * win.at((half - k) if flip else (k - half))
        acc = term if acc is None else acc + term
    return acc


def _tap_grads(dw_ref, d, pad_ref, row, ch, n_taps):
    half = (n_taps - 1) // 2
    win = _Window(pad_ref, row, ch)
    for k in range(n_taps):
        prod = d * win.at(k - half)
        dw_ref[k] += jnp.sum(prod.reshape(ch // 8, 8, prod.shape[1]), axis=0)


def _conv_fwd(z_conv, w, bias, B, C, S, cw=128, ch=128):
    R = C + S
    nj = CONV_CH // cw
    segs = _segments(C, S)

    def body(a_ref, g_ref, w_ref, b_ref, o_ref, pad):
        _zero_gaps(pad, C, S)
        wv = w_ref[...]
        for so, po, n in segs:
            def fill(r, so=so, po=po):
                pad[pl.ds(po + r, ch), :] = a_ref[pl.ds(so + r, ch), :] * _sigmoid(g_ref[pl.ds(so + r, ch), :])

            _chunks(n, ch, fill)
        for so, po, n in segs:
            def conv(r, so=so, po=po):
                o_ref[pl.ds(so + r, ch), :] = _taps(pad, wv, (r, po), ch, CONV_K) + b_ref[...]

            _chunks(n, ch, conv)

    return _call(
        body, name="conv_fwd", grid=(nj, B),
        in_specs=[pl.BlockSpec((R, cw), lambda j, b: (b, j)), pl.BlockSpec((R, cw), lambda j, b: (b, nj + j)),
                  pl.BlockSpec((32, cw), lambda j, b: (0, j)), pl.BlockSpec((1, cw), lambda j, b: (0, j))],
        out_specs=pl.BlockSpec((R, cw), lambda j, b: (b, j)), out_shape=_sds((B * R, CONV_CH), F32),
        scratch=[pltpu.VMEM((_padded_rows(C, S), cw), F32)])(z_conv, z_conv, w, bias)


def _conv_bwd(dhc, z_conv, w, B, C, S, cw=128, ch=128):
    R = C + S
    nj = CONV_CH // cw
    segs = _segments(C, S)

    def body(d_ref, a_ref, g_ref, w_ref, da_ref, dg_ref, dw_ref, sa_ref, sg_ref, gpad, dpad, dwacc):
        b = pl.program_id(1)

        @pl.when(b == 0)
        def _():
            dw_ref[...] = jnp.zeros_like(dw_ref)
            sa_ref[...] = jnp.zeros_like(sa_ref)
            sg_ref[...] = jnp.zeros_like(sg_ref)

        _zero_gaps(gpad, C, S)
        _zero_gaps(dpad, C, S)
        dwacc[...] = jnp.zeros_like(dwacc)
        wv = w_ref[...]
        for so, po, n in segs:
            def fill(r, so=so, po=po):
                gpad[pl.ds(po + r, ch), :] = a_ref[pl.ds(so + r, ch), :] * _sigmoid(g_ref[pl.ds(so + r, ch), :])
                dpad[pl.ds(po + r, ch), :] = d_ref[pl.ds(so + r, ch), :]

            _chunks(n, ch, fill)
        for so, po, n in segs:
            def step(r, so=so, po=po):
                _tap_grads(dwacc, dpad[pl.ds(po + r, ch), :], gpad, (r, po), ch, CONV_K)
                dglu = _taps(dpad, wv, (r, po), ch, CONV_K, flip=True)
                av = a_ref[pl.ds(so + r, ch), :]
                sg = _sigmoid(g_ref[pl.ds(so + r, ch), :])
                da = dglu * sg
                dg = dglu * av * sg * (1.0 - sg)
                da_ref[pl.ds(so + r, ch), :] = da.astype(MXU)
                dg_ref[pl.ds(so + r, ch), :] = dg.astype(MXU)
                sa_ref[...] += _colsum(da)
                sg_ref[...] += _colsum(dg)

            _chunks(n, ch, step)
        for k in range(CONV_K):
            dw_ref[k:k + 1, :] += _colsum(dwacc[k])

    blk = pl.BlockSpec((R, cw), lambda j, b: (b, j))
    acc1 = pl.BlockSpec((1, cw), lambda j, b: (0, j))
    return _call(
        body, name="conv_bwd", grid=(nj, B),
        in_specs=[blk, blk, pl.BlockSpec((R, cw), lambda j, b: (b, nj + j)), pl.BlockSpec((32, cw), lambda j, b: (0, j))],
        out_specs=[blk, blk, pl.BlockSpec((32, cw), lambda j, b: (0, j)), acc1, acc1],
        out_shape=[_sds((B * R, CONV_CH), MXU), _sds((B * R, CONV_CH), MXU), _sds((32, CONV_CH), F32),
                   _sds((1, CONV_CH), F32), _sds((1, CONV_CH), F32)],
        scratch=[pltpu.VMEM((_padded_rows(C, S), cw), F32), pltpu.VMEM((_padded_rows(C, S), cw), F32),
                 pltpu.VMEM((32, 8, cw), F32)])(dhc, z_conv, z_conv, w)


def _ffn_mid(up, w, bias, B, C, S, cw=256, ch=64, carry=None):
    R = C + S
    nj = D_FF // cw
    segs = _segments(C, S)
    car = _Carried(carry, 4, 2, 1)

    def body(*refs):
        (a_ref, u_ref, w_ref, b_ref), (f_ref, ft_ref), (pad,) = car.split(refs)[0]
        parts = car.split(refs)[1]
        step = pl.program_id(0) * B + pl.program_id(1)
        car.before(step, parts)
        for h in range(cw // LANE):
            _zero_gaps(pad.at[h], C, S)
        wv = w_ref[...]
        for so, po, n in segs:
            def fill(r, so=so, po=po):
                for h in range(cw // LANE):
                    pad[h, pl.ds(po + r, ch), :] = a_ref[pl.ds(so + r, ch), h * LANE:(h + 1) * LANE]

            _chunks(n, ch, fill)
        for so, po, n in segs:
            def conv(r, so=so, po=po):
                ac = jnp.concatenate([_taps(pad.at[h], wv[:, h * LANE:(h + 1) * LANE], (r, po), ch, FFN_K)
                                      for h in range(cw // LANE)], axis=1) + b_ref[...]
                f_ref[pl.ds(so + r, ch), :] = (ac * _sigmoid(ac) * u_ref[pl.ds(so + r, ch), :]).astype(MXU)

            _chunks(n, ch, conv)
        ft_ref[...] = f_ref[...].T
        car.after(step, nj * B, parts)

    res = _call(
        body, name="ffn_mid", grid=(nj, B),
        in_specs=[pl.BlockSpec((R, cw), lambda j, b: (b, j)), pl.BlockSpec((R, cw), lambda j, b: (b, nj + j)),
                  pl.BlockSpec((8, cw), lambda j, b: (0, j)), pl.BlockSpec((1, cw), lambda j, b: (0, j))] + car.in_specs(),
        out_specs=[pl.BlockSpec((R, cw), lambda j, b: (b, j)), pl.BlockSpec((cw, R), lambda j, b: (j, b))] + car.out_specs(),
        out_shape=[_sds((B * R, D_FF), MXU), _sds((D_FF, B * R), MXU)] + list(carry.out_shape if carry else []),
        scratch=[pltpu.VMEM((cw // LANE, _padded_rows(C, S), LANE), F32)] + list(carry.scratch if carry else []),
    )(up, up, w, bias, *(carry.inputs if carry else []))
    return res[0], res[1], res[2:]


def _ffn_mid_bwd(df, up, w, bias, B, C, S, cw=128, ch=128, carry=None):
    R = C + S
    nj = D_FF // cw
    segs = _segments(C, S)
    car = _Carried(carry, 5, 4, 3)

    def body(*refs):
        (d_ref, a_ref, u_ref, w_ref, b_ref), (da_ref, du_ref, dw_ref, db_ref), (apad, dpad, dwacc) = car.split(refs)[0]
        parts = car.split(refs)[1]
        b = pl.program_id(1)
        step = pl.program_id(0) * B + b
        car.before(step, parts)

        @pl.when(b == 0)
        def _():
            dw_ref[...] = jnp.zeros_like(dw_ref)
            db_ref[...] = jnp.zeros_like(db_ref)

        _zero_gaps(apad, C, S)
        _zero_gaps(dpad, C, S)
        dwacc[...] = jnp.zeros_like(dwacc)
        wv = w_ref[...]
        for so, po, n in segs:
            def fill(r, so=so, po=po):
                apad[pl.ds(po + r, ch), :] = a_ref[pl.ds(so + r, ch), :]

            _chunks(n, ch, fill)
        for so, po, n in segs:
            def first(r, so=so, po=po):
                ac = _taps(apad, wv, (r, po), ch, FFN_K) + b_ref[...]
                sg = _sigmoid(ac)
                dfv = d_ref[pl.ds(so + r, ch), :]
                du_ref[pl.ds(so + r, ch), :] = (dfv * ac * sg).astype(MXU)
                dac = dfv * u_ref[pl.ds(so + r, ch), :] * (sg * (1.0 + ac * (1.0 - sg)))
                dpad[pl.ds(po + r, ch), :] = dac
                db_ref[...] += _colsum(dac)

            _chunks(n, ch, first)
        for so, po, n in segs:
            def second(r, so=so, po=po):
                _tap_grads(dwacc, dpad[pl.ds(po + r, ch), :], apad, (r, po), ch, FFN_K)
                da_ref[pl.ds(so + r, ch), :] = _taps(dpad, wv, (r, po), ch, FFN_K, flip=True).astype(MXU)

            _chunks(n, ch, second)
        for k in range(FFN_K):
            dw_ref[k:k + 1, :] += _colsum(dwacc[k])
        car.after(step, nj * B, parts)

    blk = pl.BlockSpec((R, cw), lambda j, b: (b, j))
    res = _call(
        body, name="ffn_mid_bwd", grid=(nj, B),
        in_specs=[blk, blk, pl.BlockSpec((R, cw), lambda j, b: (b, nj + j)), pl.BlockSpec((8, cw), lambda j, b: (0, j)),
                  pl.BlockSpec((1, cw), lambda j, b: (0, j))] + car.in_specs(),
        out_specs=[blk, blk, pl.BlockSpec((8, cw), lambda j, b: (0, j)), pl.BlockSpec((1, cw), lambda j, b: (0, j))]
        + car.out_specs(),
        out_shape=[_sds((B * R, D_FF), MXU), _sds((B * R, D_FF), MXU), _sds((8, D_FF), F32), _sds((1, D_FF), F32)]
        + list(carry.out_shape if carry else []),
        scratch=[pltpu.VMEM((_padded_rows(C, S), cw), F32), pltpu.VMEM((_padded_rows(C, S), cw), F32),
                 pltpu.VMEM((8, 8, cw), F32)] + list(carry.scratch if carry else []),
    )(df, up, up, w, bias, *(carry.inputs if carry else []))
    return res[0], res[1], res[2], res[3], res[4:]


def _view_sum(win, lo, hi):
    acc = win.at(lo)
    for off in range(lo + 1, hi + 1):
        acc = acc + win.at(off)
    return acc


def _window_count(r, ch, n, w):
    t = r + lax.broadcasted_iota(jnp.int32, (ch, 1), 0)
    return (jnp.minimum(t + w // 2, n) - jnp.maximum(t - w // 2, 0)).astype(F32)


def _pool_fwd(z_pool, pool_w, pool_scale, B, C, S, ch=128):
    R = C + S
    gch = POOL_GCH
    segs = _segments(C, S)

    def body(u_ref, pw_ref, sc_ref, pooled_ref, po_ref, pad):
        g = pl.program_id(1)
        for h in range(gch // LANE):
            _zero_gaps(pad.at[h], C, S)
        for so, po, n in segs:
            def fill(r, so=so, po=po):
                for h in range(gch // LANE):
                    pad[h, pl.ds(po + r, ch), :] = u_ref[pl.ds(so + r, ch), h * LANE:(h + 1) * LANE]

            _chunks(n, ch, fill)
        for gi, w in enumerate(POOL_WINDOWS):
            @pl.when(g == gi)
            def _(w=w):
                for so, po, n in segs:
                    def step(r, so=so, po=po, n=n):
                        cnt = _window_count(r, ch, n, w)
                        halves = []
                        for h in range(gch // LANE):
                            win = _Window(pad.at[h], (r, po), ch)
                            halves.append(_view_sum(win, -(w // 2), w // 2 - 1) / cnt - win.at(0))
                        pooled = jnp.concatenate(halves, axis=1).astype(MXU)
                        pooled_ref[pl.ds(so + r, ch), :] = pooled
                        po_ref[pl.ds(so + r, ch), :] = (_dot(pooled, pw_ref[0], NN) * sc_ref[...]).astype(MXU)

                    _chunks(n, ch, step)

    blk = pl.BlockSpec((R, gch), lambda b, g: (b, g))
    return _call(
        body, name="pool_fwd", grid=(B, len(POOL_WINDOWS)),
        in_specs=[blk, pl.BlockSpec((1, gch, gch), lambda b, g: (g, 0, 0)), pl.BlockSpec((1, gch), lambda b, g: (0, g))],
        out_specs=[blk, blk], out_shape=[_sds((B * R, POOL_CH), MXU), _sds((B * R, POOL_CH), MXU)],
        scratch=[pltpu.VMEM((gch // LANE, _padded_rows(C, S), LANE), F32)])(z_pool, pool_w, pool_scale)


def _pool_bwd(dpo, pooled, pool_w, pool_scale, B, C, S, ch=128):
    R = C + S
    gch = POOL_GCH
    segs = _segments(C, S)

    def body(d_ref, p_ref, pw_ref, sc_ref, du_ref, dpw_ref, dsc_ref, su_ref, qpad, dpl):
        g, b = pl.program_id(0), pl.program_id(1)

        @pl.when(b == 0)
        def _():
            dpw_ref[...] = jnp.zeros_like(dpw_ref)
            dsc_ref[...] = jnp.zeros_like(dsc_ref)
            su_ref[...] = jnp.zeros_like(su_ref)

        for h in range(gch // LANE):
            _zero_gaps(qpad.at[h], C, S)
        pw = pw_ref[0]
        for gi, w in enumerate(POOL_WINDOWS):
            @pl.when(g == gi)
            def _(w=w):
                for so, po, n in segs:
                    def first(r, so=so, po=po, n=n):
                        pv = p_ref[pl.ds(so + r, ch), :]
                        dv = d_ref[pl.ds(so + r, ch), :]
                        dsc_ref[...] += _colsum(dv * _dot(pv, pw, NN))
                        dmx = (dv * sc_ref[...]).astype(MXU)
                        dpw_ref[0] += _dot(pv, dmx, TN)
                        dp = _dot(dmx, pw, NT)
                        dpl[pl.ds(so + r, ch), :] = dp
                        q = dp / _window_count(r, ch, n, w)
                        for h in range(gch // LANE):
                            qpad[h, pl.ds(po + r, ch), :] = q[:, h * LANE:(h + 1) * LANE]

                    _chunks(n, ch, first)
                for so, po, n in segs:
                    def second(r, so=so, po=po):
                        acc = jnp.concatenate([_view_sum(_Window(qpad.at[h], (r, po), ch), 1 - w // 2, w // 2)
                                               for h in range(gch // LANE)], axis=1)
                        du = acc - dpl[pl.ds(so + r, ch), :]
                        du_ref[pl.ds(so + r, ch), :] = du.astype(MXU)
                        su_ref[...] += _colsum(du)

                    _chunks(n, ch, second)

    blk = pl.BlockSpec((R, gch), lambda g, b: (b, g))
    vec = pl.BlockSpec((1, gch), lambda g, b: (0, g))
    wblk = pl.BlockSpec((1, gch, gch), lambda g, b: (g, 0, 0))
    return _call(
        body, name="pool_bwd", grid=(len(POOL_WINDOWS), B), in_specs=[blk, blk, wblk, vec], out_specs=[blk, wblk, vec, vec],
        out_shape=[_sds((B * R, POOL_CH), MXU), _sds((len(POOL_WINDOWS), gch, gch), F32), _sds((1, POOL_CH), F32),
                   _sds((1, POOL_CH), F32)],
        scratch=[pltpu.VMEM((gch // LANE, _padded_rows(C, S), LANE), F32), pltpu.VMEM((R, gch), F32)],
    )(dpo, pooled, pool_w, pool_scale)


def _silu_rows(cond):
    def body(c_ref, s_ref, d_ref):
        c = c_ref[...]
        sg = _sigmoid(c)
        s_ref[...] = (c * sg).astype(MXU)
        d_ref[...] = sg * (1.0 + c * (1.0 - sg))

    full = pl.BlockSpec(cond.shape, lambda i: (0, 0))
    return _call(body, name="silu_rows", grid=(1,), in_specs=[full], out_specs=[full, full],
                 out_shape=[_sds(cond.shape, MXU), _sds(cond.shape, F32)])(cond)


def _row_tile(rows, cols, n_bufs):
    cap = max(16, (16 * 2 ** 20) // (4 * n_bufs * max(cols, 128)))
    return rows if rows <= cap else _pick(rows, cap, 16)


def _adamw(parts, w, m, v, layer, prev, *, name):
    n_parts, rows, cols = parts.shape
    layers = w.shape[0]
    c1 = 1.0 - ADAM_B1 ** ADAM_STEP
    c2 = 1.0 - ADAM_B2 ** ADAM_STEP
    tr = _row_tile(rows, cols, n_parts + 7)

    def body(p_ref, w_ref, m_ref, v_ref, *rest):
        g_ref, d_ref, nm_ref, nv_ref = rest[-4:]
        g = p_ref[0].astype(F32)
        for k in range(1, n_parts):
            g = g + p_ref[k].astype(F32)
        nm = ADAM_B1 * m_ref[...] + (1.0 - ADAM_B1) * g
        nv = ADAM_B2 * v_ref[...] + (1.0 - ADAM_B2) * (g * g)
        g_ref[...] = g
        nm_ref[...] = nm
        nv_ref[...] = nv
        d_ref[...] = -ADAM_LR * ((nm / c1) / (jnp.sqrt(nv / c2) + ADAM_EPS) + ADAM_WD * w_ref[...])

    blk = pl.BlockSpec((None, tr, cols), lambda i: (layer, i, 0))
    in_specs = [pl.BlockSpec((n_parts, tr, cols), lambda i: (0, i, 0)), blk, blk, blk]
    args = [parts, w, m, v]
    aliases = {}
    if prev is not None:
        in_specs += [ANY] * 4
        aliases = {4 + k: k for k in range(4)}
        args += list(prev)
    return _call(body, name=name, grid=(rows // tr,), in_specs=in_specs, out_specs=[blk] * 4,
                 out_shape=[_sds((layers, rows, cols), F32)] * 4, aliases=aliases)(*args)


def _pair_sum(g, r1):
    _, rows, cols = g.shape
    c = lax.axis_index("c")
    g4 = g.reshape(4, 2, rows, cols)
    tr = _row_tile(rows, cols, 10)

    def body(c_ref, g_ref, r_ref, o_ref):
        o_ref[...] = (g_ref[...] + r_ref[...]).astype(WIRE)

    return _pcall(
        body, name="pair_sum", out_shape=_sds((4, rows, cols), WIRE),
        grid_spec=pltpu.PrefetchScalarGridSpec(
            num_scalar_prefetch=1, grid=(rows // tr,),
            in_specs=[pl.BlockSpec((4, None, tr, cols), lambda i, c_ref: (0, c_ref[0], i, 0)),
                      pl.BlockSpec((4, tr, cols), lambda i, c_ref: (0, i, 0))],
            out_specs=pl.BlockSpec((4, tr, cols), lambda i, c_ref: (0, i, 0))),
        compiler_params=pltpu.CompilerParams(dimension_semantics=("arbitrary",), vmem_limit_bytes=VMEM_MB * 2 ** 20),
    )(jnp.reshape(c, (1,)).astype(jnp.int32), g4, r1)


MESH = pl.DeviceIdType.MESH
ANY = pl.BlockSpec(memory_space=pl.ANY)


class _Exchange:
    inputs, out_shape, scratch = (), (), ()

    def start(self, ins, outs, sems):
        raise NotImplementedError

    def mid(self, ins, outs, sems):
        pass

    def finish(self, ins, outs, sems):
        raise NotImplementedError


def _run_exchange(ex, *, name):
    n_in, n_out = len(ex.inputs), len(ex.out_shape)

    def body(*refs):
        parts = refs[:n_in], refs[n_in:n_in + n_out], refs[n_in + n_out:]
        ex.start(*parts)
        ex.mid(*parts)
        ex.finish(*parts)

    return _pcall(body, name=name, out_shape=list(ex.out_shape), in_specs=[ANY] * n_in, out_specs=[ANY] * n_out,
                  scratch_shapes=list(ex.scratch))(*ex.inputs)


class _Gather(_Exchange):
    def __init__(self, shards, layer=None):
        self.inputs, self.layer, self.n_t = list(shards), layer, len(shards)
        self.out_shape = [_sds((N_DEV,) + tuple(s.shape if layer is None else s.shape[1:]), s.dtype) for s in shards]
        self.scratch = [pltpu.SemaphoreType.DMA((7 * self.n_t,)), pltpu.SemaphoreType.DMA((7 * self.n_t,)),
                        pltpu.SemaphoreType.DMA((self.n_t,))]

    def _place(self):
        x, y, c = lax.axis_index("x"), lax.axis_index("y"), lax.axis_index("c")
        return (x, y, c), (x, y, 1 - c), [(1 - x, y), (x, 1 - y), (1 - x, 1 - y)]

    def _own(self, ins, n):
        return ins[n] if self.layer is None else ins[n].at[self.layer]

    def _copy(self, ins, outs, sems, n, k, blk, to, own=False):
        dst = outs[n].at[4 * blk[0] + 2 * blk[1] + blk[2]]
        return pltpu.make_async_remote_copy(src_ref=self._own(ins, n) if own else dst, dst_ref=dst, send_sem=sems[0].at[n * 7 + k],
                                            recv_sem=sems[1].at[n * 7 + k], device_id=to, device_id_type=MESH)

    def _mine(self, ins, outs, sems, n):
        (x, y, c), _, _ = self._place()
        return pltpu.make_async_copy(self._own(ins, n), outs[n].at[4 * x + 2 * y + c], sems[2].at[n])

    def start(self, ins, outs, sems):
        me, sibling, chips = self._place()
        for n in range(self.n_t):
            self._mine(ins, outs, sems, n).start()
        for j, chip in enumerate(chips):
            for n in range(self.n_t):
                self._copy(ins, outs, sems, n, 1 + j, me, (*chip, me[2]), own=True).start()
        for n in range(self.n_t):
            self._copy(ins, outs, sems, n, 0, me, sibling, own=True).start()

    def mid(self, ins, outs, sems):
        me, sibling, chips = self._place()
        for j, chip in enumerate(chips):
            for n in range(self.n_t):
                self._copy(ins, outs, sems, n, 1 + j, (*chip, me[2]), me).wait_recv()
                self._copy(ins, outs, sems, n, 4 + j, (*chip, me[2]), sibling).start()

    def finish(self, ins, outs, sems):
        me, sibling, chips = self._place()
        for n in range(self.n_t):
            self._copy(ins, outs, sems, n, 0, sibling, me).wait_recv()
        for j, chip in enumerate(chips):
            for n in range(self.n_t):
                self._copy(ins, outs, sems, n, 4 + j, (*chip, 1 - me[2]), me).wait_recv()
        for j, chip in enumerate(chips):
            for n in range(self.n_t):
                self._copy(ins, outs, sems, n, 1 + j, me, (*chip, me[2]), own=True).wait_send()
                self._copy(ins, outs, sems, n, 4 + j, (*chip, me[2]), sibling).wait_send()
        for n in range(self.n_t):
            self._copy(ins, outs, sems, n, 0, me, sibling, own=True).wait_send()
            self._mine(ins, outs, sems, n).wait()


def _all_gather(shards, *, name, layer=None):
    return _run_exchange(_Gather(shards, layer), name=name)


class _Both(_Exchange):
    def __init__(self, a, b):
        self.a, self.b = a, b
        self.inputs = list(a.inputs) + list(b.inputs)
        self.out_shape = list(a.out_shape) + list(b.out_shape)
        self.scratch = list(a.scratch) + list(b.scratch)

    def _parts(self, ins, outs, sems):
        i, o, s = len(self.a.inputs), len(self.a.out_shape), len(self.a.scratch)
        return (ins[:i], outs[:o], sems[:s]), (ins[i:], outs[o:], sems[s:])

    def start(self, ins, outs, sems):
        pa, pb = self._parts(ins, outs, sems)
        self.a.start(*pa)
        self.b.start(*pb)

    def mid(self, ins, outs, sems):
        pa, pb = self._parts(ins, outs, sems)
        self.a.mid(*pa)
        self.b.mid(*pb)

    def finish(self, ins, outs, sems):
        pa, pb = self._parts(ins, outs, sems)
        self.a.finish(*pa)
        self.b.finish(*pb)


class _SiblingExchange(_Exchange):
    def __init__(self, gs):
        self.n_t = len(gs)
        self.inputs = [g.reshape((4, 2) + g.shape[1:]) for g in gs]
        self.out_shape = [_sds((4,) + g.shape[1:], g.dtype) for g in gs]
        self.scratch = [pltpu.SemaphoreType.DMA((self.n_t,)), pltpu.SemaphoreType.DMA((self.n_t,))]

    def _copy(self, ins, outs, sems, n):
        x, y, c = lax.axis_index("x"), lax.axis_index("y"), lax.axis_index("c")
        return pltpu.make_async_remote_copy(src_ref=ins[n].at[:, 1 - c], dst_ref=outs[n], send_sem=sems[0].at[n],
                                            recv_sem=sems[1].at[n], device_id=(x, y, 1 - c), device_id_type=MESH)

    def start(self, ins, outs, sems):
        for n in range(self.n_t):
            self._copy(ins, outs, sems, n).start()

    def finish(self, ins, outs, sems):
        for n in range(self.n_t):
            self._copy(ins, outs, sems, n).wait_recv()
        for n in range(self.n_t):
            self._copy(ins, outs, sems, n).wait_send()


class _ChipExchange(_Exchange):
    def __init__(self, ps):
        self.n_t, self.inputs = len(ps), list(ps)
        self.out_shape = [_sds(p.shape, p.dtype) for p in ps]
        self.scratch = [pltpu.SemaphoreType.DMA((3 * self.n_t,)), pltpu.SemaphoreType.DMA((3 * self.n_t,)),
                        pltpu.SemaphoreType.DMA((self.n_t,))]

    def _place(self):
        x, y, c = lax.axis_index("x"), lax.axis_index("y"), lax.axis_index("c")
        return 2 * x + y, c, [(1 - x, y), (x, 1 - y), (1 - x, 1 - y)]

    def _copy(self, ins, outs, sems, n, j, src_chip, dst_slot):
        _, c, chips = self._place()
        return pltpu.make_async_remote_copy(
            src_ref=ins[n].at[src_chip], dst_ref=outs[n].at[dst_slot], send_sem=sems[0].at[n * 3 + j],
            recv_sem=sems[1].at[n * 3 + j], device_id=(*chips[j], c), device_id_type=MESH)

    def _own(self, ins, outs, sems, n):
        mine, _, _ = self._place()
        return pltpu.make_async_copy(ins[n].at[mine], outs[n].at[mine], sems[2].at[n])

    def start(self, ins, outs, sems):
        mine, _, chips = self._place()
        for n in range(self.n_t):
            self._own(ins, outs, sems, n).start()
        for j, (px, py) in enumerate(chips):
            for n in range(self.n_t):
                self._copy(ins, outs, sems, n, j, 2 * px + py, mine).start()

    def finish(self, ins, outs, sems):
        mine, _, chips = self._place()
        for j, (px, py) in enumerate(chips):
            for n in range(self.n_t):
                self._copy(ins, outs, sems, n, j, mine, 2 * px + py).wait_recv()
        for j, (px, py) in enumerate(chips):
            for n in range(self.n_t):
                self._copy(ins, outs, sems, n, j, 2 * px + py, mine).wait_send()
        for n in range(self.n_t):
            self._own(ins, outs, sems, n).wait()


BIG = (("w_ada", (D, N_MOD * D // N_DEV), 1), ("w_in", (D, D_IN // N_DEV), 1), ("conv_pw_w", (CONV_CH // N_DEV, D), 0),
       ("pool_w", (len(POOL_WINDOWS), POOL_GCH // N_DEV, POOL_GCH), 1), ("w_out", (D // N_DEV, D), 0),
       ("w_up", (D, 2 * D_FF // N_DEV), 1), ("w_down", (D_FF // N_DEV, D), 0))
TAPS = (("conv_dw_w", (CONV_K, CONV_CH // N_DEV), 1), ("ffn_dw_w", (FFN_K, D_FF // N_DEV), 1))
SHARDED = BIG + TAPS
EARLY = (0, 1)
LATE = (2, 3, 4, 5, 6)
READY_0 = (2, 3, 4, 5, 6, 7, 8)
LAST_0 = (0, 1)
REPLICATED =(("b_ada", N_MOD * D), ("b_in", D_IN), ("q_gain", HD), ("k_gain", HD), ("conv_dw_b", CONV_CH), ("conv_ln_g", CONV_CH),
              ("conv_ln_b", CONV_CH), ("conv_pw_b", D), ("pool_scale", POOL_CH), ("b_out", D), ("ln1_g", D), ("ln1_b", D),
              ("ln2_g", D), ("ln2_b", D), ("ffn_dw_b", D_FF))


def _as_rows(shape):
    return (int(np.prod(shape[:-1])), shape[-1])


def _full_from_blocks(blocks, axis):
    moved = jnp.moveaxis(blocks, 0, axis)
    shape = list(moved.shape)
    shape[axis:axis + 2] = [shape[axis] * shape[axis + 1]]
    return moved.reshape(shape)


def _blocks_from_full(full, axis):
    shape = list(full.shape)
    shape[axis:axis + 1] = [N_DEV, shape[axis] // N_DEV]
    return jnp.moveaxis(full.reshape(shape), axis, 0)


SMALL_N = DEPTH * sum(n for _, n in REPLICATED) + D
SMALL_ROWS = -(-(SMALL_N + 1) // (8 * LANES)) * 8


def _rope_tables(C, S):
    t = np.arange(S)
    inv_freq = ROPE_THETA ** (-np.arange(ROPE_PAIRS, dtype=np.float32) / ROPE_PAIRS)
    row = jnp.asarray((t // GRID_W).astype(np.float32))[:, None] * jnp.asarray(inv_freq, F32)
    col = jnp.asarray((t % GRID_W).astype(np.float32))[:, None] * jnp.asarray(inv_freq, F32)
    cos = jnp.concatenate([jnp.cos(row), jnp.cos(row), jnp.cos(col), jnp.cos(col)], axis=1)
    sin = jnp.concatenate([-jnp.sin(row), jnp.sin(row), -jnp.sin(col), jnp.sin(col)], axis=1)
    cos = jnp.concatenate([jnp.ones((C, HD), F32), cos], axis=0)
    sin = jnp.concatenate([jnp.zeros((C, HD), F32), sin], axis=0)
    return cos, sin


def _segment_sums(parts, B, tpe, ncq):
    p = parts.reshape(B, tpe, D)
    return jnp.concatenate([jnp.sum(p[:, ncq:], axis=1), jnp.sum(p[:, :ncq], axis=(0, 1))[None]], axis=0)


def kernel(x, c, ctx, c_ctx, w_ada, b_ada, w_in, b_in, q_gain, k_gain, conv_dw_w, conv_dw_b, conv_ln_g, conv_ln_b, conv_pw_w, conv_pw_b, pool_w, pool_scale, w_out, b_out, ln1_g, ln1_b, ln2_g, ln2_b, w_up, ffn_dw_w, ffn_dw_b, w_down, loss_target, m_c_ctx, m_w_ada, m_b_ada, m_w_in, m_b_in, m_q_gain, m_k_gain, m_conv_dw_w, m_conv_dw_b, m_conv_ln_g, m_conv_ln_b, m_conv_pw_w, m_conv_pw_b, m_pool_w, m_pool_scale, m_w_out, m_b_out, m_ln1_g, m_ln1_b, m_ln2_g, m_ln2_b, m_w_up, m_ffn_dw_w, m_ffn_dw_b, m_w_down, v_c_ctx, v_w_ada, v_b_ada, v_w_in, v_b_in, v_q_gain, v_k_gain, v_conv_dw_w, v_conv_dw_b, v_conv_ln_g, v_conv_ln_b, v_conv_pw_w, v_conv_pw_b, v_pool_w, v_pool_scale, v_w_out, v_b_out, v_ln1_g, v_ln1_b, v_ln2_g, v_ln2_b, v_w_up, v_ffn_dw_w, v_ffn_dw_b, v_w_down):
    given = dict(locals())
    B, S, _ = x.shape
    C = ctx.shape[1]
    R = C + S
    T = B * R
    tpe, ncq = R // TM, C // TM
    nt = T // TM
    assert S % TM == 0 and C % TM == 0 and B + 1 <= 16

    operands = [given[n].astype(MXU) for n, _, _ in BIG]
    taps = _all_gather([given[n] for n, _, _ in TAPS], name="gather_taps")
    W, modt = [], []

    xu = jnp.concatenate([ctx, x], axis=1).reshape(T, D)
    cond = jnp.concatenate([c, c_ctx[None], jnp.zeros((16 - B - 1, D), F32)], axis=0)
    s_cond, ds_cond = _silu_rows(cond)
    ctx_tile = jnp.asarray((np.arange(tpe) < ncq)[None, :, None])
    cos_t, sin_t = _rope_tables(C, S)
    row = lambda v: v.reshape(1, -1)

    early_ops, late_ops = [operands[k] for k in EARLY], [operands[k] for k in LATE]

    def split_for(t):
        return (EARLY, LATE) if t <= 1 else (EARLY + (5,), tuple(k for k in LATE if k != 5))

    def ops(which):
        return [operands[k] for k in which]

    def add_early(blocks, which=EARLY):
        l = len(W)
        wl = {BIG[k][0]: _full_from_blocks(blk, BIG[k][2]) for k, blk in zip(which, blocks)}
        for (n, _, a), blk in zip(TAPS, taps):
            wl[n] = _full_from_blocks(blk[:, l], a)
        W.append(wl)
        m = _matmul(s_cond, wl["w_ada"], "nn", name="ada", bias=row(b_ada[l]), tm=16, tn=1024)
        modt.append(jnp.where(ctx_tile, m[B][None, None, :], m[:B][:, None, :]).reshape(nt, 1, N_MOD * D))

    def add_late(l, blocks, which=LATE):
        for k, blk in zip(which, blocks):
            W[l][BIG[k][0]] = _full_from_blocks(blk, BIG[k][2])

    add_early(_all_gather(early_ops, name="gather_weights", layer=0))

    saved = []
    h1, h1t = _modulate_cast(xu, modt[0], 0)
    xin = xu
    for l in range(DEPTH):
        wl = W[l]
        w_inl, b_inl = wl["w_in"], b_in[l]
        z_qkv = _matmul(h1, w_inl[:, :QKV_W], "nn", name="z_qkv", bias=row(b_inl[:QKV_W]), tn=768)
        c0, p0, g0 = QKV_W, QKV_W + 2 * CONV_CH, QKV_W + 2 * CONV_CH + POOL_CH
        z_conv = _matmul(h1, w_inl[:, c0:p0], "nn", name="z_conv", bias=row(b_inl[c0:p0]))
        z_pool = _matmul(h1, w_inl[:, p0:g0], "nn", name="z_pool", bias=row(b_inl[p0:g0]))
        z_gate = _matmul(h1, w_inl[:, g0:], "nn", name="z_gate", bias=row(b_inl[g0:]))
        qkv = _qk_prep(z_qkv, cos_t, sin_t, row(q_gain[l]), row(k_gain[l]), tpe)
        if l + 1 < DEPTH:
            first, _ = split_for(l + 1)
            if l == 0:
                attn, res = _attn_fwd(qkv, B, C, R, carry=_Both(_Gather(ops(first), 1), _Gather(late_ops, 0)), mid_at=1.0)
                add_late(0, res[len(first):])
            else:
                attn, res = _attn_fwd(qkv, B, C, R, carry=_Gather(ops(first), l + 1), mid_at=0.85)
            add_early(res[:len(first)], first)
        else:
            attn, _ = _attn_fwd(qkv, B, C, R)
        dw32 =jnp.pad(wl["conv_dw_w"], ((0, 32 - CONV_K), (0, 0)))
        hc = _conv_fwd(z_conv, dw32, row(conv_dw_b[l]), B, C, S)
        sw, swt = _ln_silu(hc, row(conv_ln_g[l]), row(conv_ln_b[l]))
        conv_o = _matmul(sw, wl["conv_pw_w"], "nn", name="conv_pw", bias=row(conv_pw_b[l]), out_dtype=MXU)
        pooled, pool_o = _pool_fwd(z_pool, wl["pool_w"], row(pool_scale[l]), B, C, S)
        m, mt = _merge(attn, conv_o, pool_o, z_gate)
        mo = _matmul(m, wl["w_out"], "nn", name="w_out", bias=row(b_out[l]))
        y1, h2, h2t = _resid_ln(xin, mo, modt[l], 2, row(ln1_g[l]), row(ln1_b[l]), modt[l], 3)
        up = _matmul(h2, wl["w_up"], "nn", name="w_up", tn=1408)
        fw8 = jnp.pad(wl["ffn_dw_w"], ((0, 8 - FFN_K), (0, 0)))
        if l + 1 < DEPTH:
            rest = split_for(l + 1)[1]
            f, ft, late_blocks = _ffn_mid(up, fw8, row(ffn_dw_b[l]), B, C, S, carry=_Gather(ops(rest), l + 1))
            add_late(l + 1, late_blocks, rest)
        else:
            f, ft, _ = _ffn_mid(up, fw8, row(ffn_dw_b[l]), B, C, S)
        fo = _matmul(f, wl["w_down"], "nn", name="w_down", tm=512)
        if l + 1 < DEPTH:
            y2, h_next, ht_next = _resid_ln(y1, fo, modt[l], 5, row(ln2_g[l]), row(ln2_b[l]), modt[l + 1], 0)
        else:
            y2, h_next, ht_next = _resid_ln(y1, fo, modt[l], 5, row(ln2_g[l]), row(ln2_b[l])), None, None
        saved.append(dict(xin=xin, h1t=h1t, z_qkv=z_qkv, z_conv=z_conv, z_gate=z_gate, qkv=qkv, attn=attn, hc=hc, swt=swt,
                          conv_o=conv_o, pooled=pooled, pool_o=pool_o, mt=mt, mo=mo, y1=y1, h2t=h2t, up=up, ft=ft, fo=fo,
                          dw32=dw32, fw8=fw8))
        xin, h1, h1t = y2, h_next, ht_next

    dy, loss_part = _loss_grad(xin, loss_target.reshape(B * S, D), tpe, ncq)

    small = {n: [None] * DEPTH for n, _ in REPLICATED}
    d_c_ctx = jnp.zeros((D,), F32)
    dmods_t = [[None] * N_MOD for _ in range(DEPTH)]
    layer_grads = [None] * DEPTH
    kinds = ("grad_", "delta_", "new_m_", "new_v_")
    stacks = {n: [given[pre + n].reshape((DEPTH,) + _as_rows(s)) for pre in ("", "m_", "v_")] for n, s, _ in SHARDED}
    results = {n: None for n, _, _ in SHARDED}

    every = tuple(range(len(SHARDED)))

    def block_major(l, which=every):
        return [_blocks_from_full(layer_grads[l][SHARDED[k][0]], SHARDED[k][2]).reshape((N_DEV,) + _as_rows(SHARDED[k][1]))
                for k in which]

    def reduce_prepare(l, which=every):
        nonlocal d_c_ctx
        dmods = jnp.concatenate([_segment_sums(p, B, tpe, ncq) for p in dmods_t[l]], axis=1)
        small["b_ada"][l] = jnp.sum(dmods, axis=0)
        dm16 = jnp.concatenate([dmods, jnp.zeros((16 - B - 1, N_MOD * D), F32)], axis=0).astype(MXU)
        layer_grads[l]["w_ada"] = _matmul(s_cond, dm16, "tn", name="g_w_ada", tm=1024, tn=1024)
        dcond = _matmul(dm16, W[l]["w_ada"], "nt", name="d_cond", tm=16, tn=1024, tk=2048)
        d_c_ctx = d_c_ctx + dcond[B] * ds_cond[B]
        return block_major(l, which)

    def reduce_end(l, r2, which=every):
        for k, parts in zip(which, r2):
            n = SHARDED[k][0]
            results[n] = _adamw(parts, *stacks[n], l, results[n], name="adamw_sharded")

    dh1 = None
    for l in reversed(range(DEPTH)):
        gs_above = None
        wl, sv = W[l], saved[l]
        dmod = dmods_t[l]
        if dh1 is None:
            dy1p, dfo, dgate2, dg, db, _ = _ln_bwd(dy, sv["y1"], sv["fo"], modt[l], 5, row(ln2_g[l]))
        else:
            dy1p, dfo, dgate2, dg, db, _, dsh, dsc = _ln_bwd(dy, sv["y1"], sv["fo"], modt[l], 5, row(ln2_g[l]), dh=dh1,
                                                             y=saved[l + 1]["xin"], mod_next=modt[l + 1], k_shift_next=0)
            dmods_t[l + 1][0], dmods_t[l + 1][1] = dsh, dsc
            gs_above = reduce_prepare(l + 1)
        small["ln2_g"][l], small["ln2_b"][l] = dg[0], db[0]
        dmod[5] = dgate2
        df = _matmul(dfo, wl["w_down"], "nt", name="d_f", tn=1408)
        g_w_down = _matmul(sv["ft"], dfo, "nn", name="g_w_down", tm=1408, tk=2304)
        if gs_above is None:
            da2, du2, g_fdw, g_fdb, _ = _ffn_mid_bwd(df, sv["up"], sv["fw8"], row(ffn_dw_b[l]), B, C, S)
            ps_above = None
        else:
            da2, du2, g_fdw, g_fdb, r1 = _ffn_mid_bwd(df, sv["up"], sv["fw8"], row(ffn_dw_b[l]), B, C, S,
                                                      carry=_SiblingExchange(gs_above))
            ps_above = [_pair_sum(g, r) for g, r in zip(gs_above, r1)]
        small["ffn_dw_b"][l] = g_fdb[0]
        dh2 = _matmul(da2, wl["w_up"][:, :D_FF], "nt", name="d_h2a", tm=512)
        dh2 = _matmul(du2, wl["w_up"][:, D_FF:], "nt", name="d_h2u", tm=512, acc_in=dh2)
        g_w_up = _matmul(sv["h2t"], da2, "nn", name="g_w_up_a", tn=1408, tk=2304, into=(2 * D_FF, 0, None))
        g_w_up = _matmul(sv["h2t"], du2, "nn", name="g_w_up_u", tn=1408, tk=2304, into=(2 * D_FF, D_FF, g_w_up))
        dxp, dmo, dgate1, dg, db, dbo, dsh, dsc = _ln_bwd(dy1p, sv["xin"], sv["mo"], modt[l], 2, row(ln1_g[l]), dh=dh2,
                                                          y=sv["y1"], mod_next=modt[l], k_shift_next=3)
        small["ln1_g"][l], small["ln1_b"][l], small["b_out"][l] = dg[0], db[0], dbo[0]
        dmod[2], dmod[3], dmod[4] = dgate1, dsh, dsc
        dm = _matmul(dmo, wl["w_out"], "nt", name="d_m")
        g_w_out = _matmul(sv["mt"], dmo, "nn", name="g_w_out", tk=2304)
        dattn, dconv_o, dpool_o, dzg, g_pwb, gsum_gate = _merge_bwd(dm, sv["attn"], sv["conv_o"], sv["pool_o"], sv["z_gate"])
        small["conv_pw_b"][l] = g_pwb[0]
        du, g_pool_w, g_pool_sc, gsum_pool = _pool_bwd(dpool_o, sv["pooled"], wl["pool_w"], row(pool_scale[l]), B, C, S)
        small["pool_scale"][l] = g_pool_sc[0]
        dsw = _matmul(dconv_o, wl["conv_pw_w"], "nt", name="d_sw")
        g_pw = _matmul(sv["swt"], dconv_o, "nn", name="g_conv_pw", tk=2304)
        dhc, g_cg, g_cb, g_cdb = _ln_silu_bwd(dsw, sv["hc"], row(conv_ln_g[l]), row(conv_ln_b[l]))
        small["conv_ln_g"][l], small["conv_ln_b"][l], small["conv_dw_b"][l] = g_cg[0], g_cb[0], g_cdb[0]
        da, dgt, g_cdw, gsum_a, gsum_gt = _conv_bwd(dhc, sv["z_conv"], sv["dw32"], B, C, S)
        layer_grads[l] = {"conv_pw_w": g_pw, "pool_w": g_pool_w, "w_out": g_w_out, "w_up": g_w_up, "w_down": g_w_down,
                          "conv_dw_w": g_cdw[:CONV_K], "ffn_dw_w": g_fdw[:FFN_K]}
        if ps_above is None:
            dq, dk, dv, _ = _attn_bwd(sv["qkv"], sv["attn"], dattn, B, C, R)
        elif l > 0:
            dq, dk, dv, r2 = _attn_bwd(sv["qkv"], sv["attn"], dattn, B, C, R, carry=_ChipExchange(ps_above))
            reduce_end(l + 1, r2)
        else:
            gs = block_major(0, READY_0)
            r1 = _run_exchange(_SiblingExchange(gs), name="sibling_exchange")
            ps_own = [_pair_sum(g, r) for g, r in zip(gs, r1)]
            dq, dk, dv, r2 = _attn_bwd(sv["qkv"], sv["attn"], dattn, B, C, R,
                                       carry=_Both(_ChipExchange(ps_above), _ChipExchange(ps_own)))
            reduce_end(1, r2[:len(SHARDED)])
            reduce_end(0, r2[len(SHARDED):], READY_0)
        dz_qkv, g_qg, g_kg, gsum_qkv = _qk_bwd(dq, dk, dv, sv["z_qkv"], cos_t, sin_t, row(q_gain[l]), row(k_gain[l]), tpe)
        small["q_gain"][l], small["k_gain"][l] = g_qg[0], g_kg[0]
        small["b_in"][l] = jnp.concatenate([gsum_qkv[0], gsum_a[0], gsum_gt[0], gsum_pool[0], gsum_gate[0]])
        w_inl = wl["w_in"]
        g0 = QKV_W + 2 * CONV_CH + POOL_CH
        pieces = ((dzg, g0, N_GATE), (da, QKV_W, CONV_CH), (dgt, QKV_W + CONV_CH, CONV_CH), (du, QKV_W + 2 * CONV_CH, POOL_CH),
                  (dz_qkv, 0, QKV_W))
        dh1 = g_w_in = None
        at = 0
        for k, (dz, c0, wd) in enumerate(pieces):
            dh1 = _matmul(dz, w_inl[:, c0:c0 + wd], "nt", name=f"d_h1_{k}", tm=512, acc_in=dh1)
            g_w_in = _matmul(sv["h1t"], dz, "nn", name=f"g_w_in_{k}", tk=2304, into=(D_IN, at, g_w_in))
            at += wd
        g_w_in = jnp.concatenate([g_w_in[:, N_GATE + 2 * CONV_CH + POOL_CH:], g_w_in[:, N_GATE:N_GATE + 2 * CONV_CH + POOL_CH],
                                  g_w_in[:, :N_GATE]], axis=1)

        layer_grads[l]["w_in"] = g_w_in
        dy = dxp
    gx_u, dmods_t[0][0], dmods_t[0][1] = _mod_bwd(dy, dh1, saved[0]["xin"], modt[0], 0)
    grad_x = gx_u.reshape(B, R, D)[:, C:]

    last = every if DEPTH == 1 else LAST_0
    gs = reduce_prepare(0, last)
    r1 = _run_exchange(_SiblingExchange(gs), name="sibling_exchange")
    reduce_end(0, _run_exchange(_ChipExchange([_pair_sum(g, r) for g, r in zip(gs, r1)]), name="chip_exchange"), last)
    outs = {}
    for n, s, _ in SHARDED:
        for kind, buf in zip(kinds, results[n]):
            outs[kind + n] = buf.reshape((DEPTH,) + tuple(s))

    def small_pack(pieces):
        flat = jnp.concatenate([p.reshape(-1) for p in pieces])
        return jnp.pad(flat, (0, SMALL_ROWS * LANES - flat.shape[0])).reshape(SMALL_ROWS, LANES)

    zero1 = jnp.zeros((1,), F32)
    g_pack = small_pack([small[n][l] for n, _ in REPLICATED for l in range(DEPTH)] + [d_c_ctx, loss_part[0, :1]])
    g_small, = _all_gather([g_pack], name="gather_small")
    wmv = [small_pack([given[pre + n] for n, _ in REPLICATED] + [given[pre + "c_ctx"], zero1])[None] for pre in ("", "m_", "v_")]
    res = _adamw(g_small, *wmv, 0, None, name="adamw_small")
    for kind, buf in zip(kinds, res):
        flat = buf.reshape(-1)
        off = 0
        for n, sz in REPLICATED:
            outs[kind + n] = flat[off:off + DEPTH * sz].reshape(DEPTH, sz)
            off += DEPTH * sz
        outs[kind + "c_ctx"] = flat[off:off + D]
        if kind == "grad_":
            loss = flat[off + D]

    names = ["c_ctx", "w_ada", "b_ada", "w_in", "b_in", "q_gain", "k_gain", "conv_dw_w", "conv_dw_b", "conv_ln_g", "conv_ln_b",
             "conv_pw_w", "conv_pw_b", "pool_w", "pool_scale", "w_out", "b_out", "ln1_g", "ln1_b", "ln2_g", "ln2_b", "w_up",
             "ffn_dw_w", "ffn_dw_b", "w_down"]
    return (loss, grad_x, *[outs[k + n] for k in ("grad_", "delta_", "new_m_", "new_v_") for n in names])
```
